```python
import jax, jax.numpy as jnp
from jax import lax
import numpy as np

D_MODEL = 2048
BATCH = 2
SEQ = 4096
DEPTH = 1
DEC_BATCH = 8
DEC_SEQ = 8
PAST_LEN = 16384
PAGE_SIZE = 128

N_HEADS = 8
HEAD_DIM = 128
ATTN_WIDTH = N_HEADS * HEAD_DIM
CONV_CH = D_MODEL - ATTN_WIDTH
MIX_WIDTH = ATTN_WIDTH + CONV_CH
IN_WIDTH = 3 * ATTN_WIDTH + 2 * CONV_CH
CONV_WIDTH = 31
MOBA_BLOCK = 256
MOBA_TOPK = 3
Q_BLOCK = 128
MEM_LEN = 256
X_HEADS = 4
X_HEAD_DIM = 128
X_WIDTH = X_HEADS * X_HEAD_DIM
FFN_HIDDEN = -(-(8 * D_MODEL) // (3 * 256)) * 256
EPS = 1e-6

kernel_name = 'hymba_moba_conformer_decoder_step'


def rmsnorm(x, g):
    xf = x.astype(jnp.float32)
    y = xf * lax.rsqrt(jnp.mean(xf * xf, axis=-1, keepdims=True) + EPS)
    return (y * g.astype(jnp.float32)).astype(x.dtype)


def layernorm(x, g, b):
    xf = x.astype(jnp.float32)
    xc = xf - jnp.mean(xf, axis=-1, keepdims=True)
    y = xc * lax.rsqrt(jnp.mean(xc * xc, axis=-1, keepdims=True) + EPS)
    return (y * g.astype(jnp.float32) + b.astype(jnp.float32)).astype(x.dtype)


def alibi_slopes():
    h = jnp.arange(1, N_HEADS + 1, dtype=jnp.float32)
    return jnp.exp2(-8.0 * h / N_HEADS)


def moba_attention(q, k, v, q_start):
    b, tq, nh, dh = q.shape
    lk = k.shape[1]
    nb = -(-lk // MOBA_BLOCK)
    pad = nb * MOBA_BLOCK - lk

    def to_blocks(t):
        t = jnp.pad(t, ((0, 0), (0, pad), (0, 0), (0, 0)))
        return t.reshape(b, nb, MOBA_BLOCK, nh, dh).transpose(0, 3, 1, 2, 4)

    kb = to_blocks(k)
    vb = to_blocks(v)
    k_mean = jnp.mean(kb.astype(jnp.float32), axis=3)
    slopes = alibi_slopes().reshape(1, nh, 1, 1, 1)
    scale = dh ** -0.5
    qc = Q_BLOCK if tq % Q_BLOCK == 0 else tq
    nc = tq // qc
    q_chunks = q.reshape(b, nc, qc, nh, dh).transpose(1, 0, 3, 2, 4)
    pos_chunks = (q_start + jnp.arange(tq, dtype=jnp.int32)).reshape(nc, qc)
    bi = jnp.arange(b)[:, None, None, None]
    hi = jnp.arange(nh)[None, :, None, None]
    blk_ids = jnp.arange(nb, dtype=jnp.int32)
    offs = jnp.arange(MOBA_BLOCK, dtype=jnp.int32)

    def one_chunk(args):
        qch, pos = args
        own = pos // MOBA_BLOCK
        gate = jnp.einsum('bhqd,bhnd->bhqn', qch.astype(jnp.float32), k_mean)
        gate = jnp.where(blk_ids[None, :] < own[:, None], gate, -jnp.inf)
        if nb < MOBA_TOPK:
            gate = jnp.pad(gate, ((0, 0), (0, 0), (0, 0), (0, MOBA_TOPK - nb)), constant_values=-jnp.inf)
        _, sel = lax.top_k(gate, MOBA_TOPK)
        sel_ok = sel < own[:, None]
        own_b = jnp.broadcast_to(own[:, None], sel.shape[:-1] + (1,))
        blocks = jnp.concatenate([jnp.minimum(sel, nb - 1), own_b], axis=-1)
        blk_ok = jnp.concatenate([sel_ok, jnp.ones(own_b.shape, dtype=bool)], axis=-1)
        kg = kb[bi, hi, blocks]
        vg = vb[bi, hi, blocks]
        kpos = blocks[..., None] * MOBA_BLOCK + offs
        qpos = pos[:, None, None]
        mask = blk_ok[..., None] & (kpos <= qpos)
        s = (jnp.einsum('bhqd,bhqnkd->bhqnk', qch, kg).astype(jnp.float32) * scale
             - slopes * (qpos - kpos).astype(jnp.float32))
        s = jnp.where(mask, s, -jnp.inf)
        p = jax.nn.softmax(s.reshape(s.shape[:3] + (-1,)), axis=-1).reshape(s.shape)
        return jnp.einsum('bhqnk,bhqnkd->bhqd', p.astype(vg.dtype), vg)

    out = lax.map(one_chunk, (q_chunks, pos_chunks))
    return out.transpose(1, 0, 3, 2, 4).reshape(b, tq, nh * dh)


def conformer_conv(u, prev, w_dw, b_dw, g_ln, b_ln):
    a = u[..., :CONV_CH] * jax.nn.sigmoid(u[..., CONV_CH:])
    xp = jnp.concatenate([prev.astype(a.dtype), a], axis=1)
    y = lax.conv_general_dilated(xp, w_dw[:, None, :].astype(xp.dtype), window_strides=(1,),
                                 padding='VALID', dimension_numbers=('NWC', 'WIO', 'NWC'),
                                 feature_group_count=CONV_CH) + b_dw.astype(xp.dtype)
    y = jax.nn.silu(layernorm(y, g_ln, b_ln))
    return y, xp[:, -(CONV_WIDTH - 1):]


def memory_kv(mem, g_mem, w_mem_k, w_mem_v):
    b, m, _ = mem.shape
    mn = rmsnorm(mem, g_mem)
    return ((mn @ w_mem_k).reshape(b, m, X_HEADS, X_HEAD_DIM),
            (mn @ w_mem_v).reshape(b, m, X_HEADS, X_HEAD_DIM))


def layer_forward(h, k_past, v_past, conv_prev, mem_k, mem_v,
                  g_pre_mix, w_in, w_dw, b_dw, g_ln_conv, b_ln_conv, g_attn_grp, g_conv_grp,
                  w_out, g_post_mix, g_pre_x, w_xq, w_xo, g_post_x,
                  g_pre_ffn, w_ffn_in, w_ffn_out, g_post_ffn):
    b, t, _ = h.shape
    q_start = 0 if k_past is None else k_past.shape[1]
    z = rmsnorm(h, g_pre_mix) @ w_in
    q = z[..., :ATTN_WIDTH].reshape(b, t, N_HEADS, HEAD_DIM)
    k_new = z[..., ATTN_WIDTH:2 * ATTN_WIDTH].reshape(b, t, N_HEADS, HEAD_DIM)
    v_new = z[..., 2 * ATTN_WIDTH:3 * ATTN_WIDTH].reshape(b, t, N_HEADS, HEAD_DIM)
    u = z[..., 3 * ATTN_WIDTH:]
    if k_past is None:
        k_all, v_all = k_new, v_new
    else:
        k_all = jnp.concatenate([k_past.astype(k_new.dtype), k_new], axis=1)
        v_all = jnp.concatenate([v_past.astype(v_new.dtype), v_new], axis=1)
    attn = moba_attention(q, k_all, v_all, q_start)
    conv, conv_state = conformer_conv(u, conv_prev, w_dw, b_dw, g_ln_conv, b_ln_conv)
    mixed = jnp.concatenate([rmsnorm(attn, g_attn_grp), rmsnorm(conv, g_conv_grp)], axis=-1) @ w_out
    h = h + rmsnorm(mixed, g_post_mix)
    xq = (rmsnorm(h, g_pre_x) @ w_xq).reshape(b, t, X_HEADS, X_HEAD_DIM)
    s = jnp.einsum('bthd,bmhd->bhtm', xq, mem_k.astype(xq.dtype)).astype(jnp.float32) * (X_HEAD_DIM ** -0.5)
    p = jax.nn.softmax(s, axis=-1)
    o = jnp.einsum('bhtm,bmhd->bthd', p.astype(xq.dtype), mem_v.astype(xq.dtype)).reshape(b, t, X_WIDTH)
    h = h + rmsnorm(o @ w_xo, g_post_x)
    gu = rmsnorm(h, g_pre_ffn) @ w_ffn_in
    y = (jax.nn.silu(gu[..., :FFN_HIDDEN]) * gu[..., FFN_HIDDEN:]) @ w_ffn_out
    h = h + rmsnorm(y, g_post_ffn)
    return h, k_new, v_new, conv_state


def setup_inputs(seed: int = 0) -> dict:
    key = jax.random.key(seed)
    ks = jax.random.split(key, 40)
    f32 = jnp.float32
    n_pages = PAST_LEN // PAGE_SIZE
    n_pool = (DEC_BATCH * n_pages * 5) // 4

    def w(k, shape, fan_in):
        return jax.random.normal(k, shape, f32) * (fan_in ** -0.5)

    def gain(k, shape):
        return 1.0 + 0.05 * jax.random.normal(k, shape, f32)

    def small(k, shape):
        return 0.02 * jax.random.normal(k, shape, f32)

    page_table = jax.random.permutation(ks[0], n_pool)[:DEC_BATCH * n_pages].reshape(DEC_BATCH, n_pages).astype(jnp.int32)
    return {
        'x_prompt': jax.random.normal(ks[1], (BATCH, SEQ, D_MODEL), f32),
        'x_sample': jax.random.normal(ks[2], (DEC_BATCH, DEC_SEQ, D_MODEL), f32),
        'cache_k': jax.random.normal(ks[3], (DEPTH, n_pool, PAGE_SIZE, N_HEADS, HEAD_DIM), f32),
        'cache_v': jax.random.normal(ks[4], (DEPTH, n_pool, PAGE_SIZE, N_HEADS, HEAD_DIM), f32),
        'state_conv': 0.5 * jax.random.normal(ks[5], (DEPTH, DEC_BATCH, CONV_WIDTH - 1, CONV_CH), f32),
        'cache_mem_k': jax.random.normal(ks[6], (DEPTH, DEC_BATCH, MEM_LEN, X_HEADS, X_HEAD_DIM), f32),
        'cache_mem_v': jax.random.normal(ks[7], (DEPTH, DEC_BATCH, MEM_LEN, X_HEADS, X_HEAD_DIM), f32),
        'page_table': page_table,
        'mem_prompt': jax.random.normal(ks[8], (BATCH, MEM_LEN, D_MODEL), f32),
        'g_mem': gain(ks[9], (DEPTH, D_MODEL)),
        'w_mem_k': w(ks[10], (DEPTH, D_MODEL, X_WIDTH), D_MODEL),
        'w_mem_v': w(ks[11], (DEPTH, D_MODEL, X_WIDTH), D_MODEL),
        'g_pre_mix': gain(ks[12], (DEPTH, D_MODEL)),
        'w_in': w(ks[13], (DEPTH, D_MODEL, IN_WIDTH), D_MODEL),
        'w_dw': w(ks[14], (DEPTH, CONV_WIDTH, CONV_CH), CONV_WIDTH),
        'b_dw': small(ks[15], (DEPTH, CONV_CH)),
        'g_ln_conv': gain(ks[16], (DEPTH, CONV_CH)),
        'b_ln_conv': small(ks[17], (DEPTH, CONV_CH)),
        'g_attn_grp': gain(ks[18], (DEPTH, ATTN_WIDTH)),
        'g_conv_grp': gain(ks[19], (DEPTH, CONV_CH)),
        'w_out': w(ks[20], (DEPTH, MIX_WIDTH, D_MODEL), MIX_WIDTH),
        'g_post_mix': gain(ks[21], (DEPTH, D_MODEL)),
        'g_pre_x': gain(ks[22], (DEPTH, D_MODEL)),
        'w_xq': w(ks[23], (DEPTH, D_MODEL, X_WIDTH), D_MODEL),
        'w_xo': w(ks[24], (DEPTH, X_WIDTH, D_MODEL), X_WIDTH),
        'g_post_x': gain(ks[25], (DEPTH, D_MODEL)),
        'g_pre_ffn': gain(ks[26], (DEPTH, D_MODEL)),
        'w_ffn_in': w(ks[27], (DEPTH, D_MODEL, 2 * FFN_HIDDEN), D_MODEL),
        'w_ffn_out': w(ks[28], (DEPTH, FFN_HIDDEN, D_MODEL), FFN_HIDDEN),
        'g_post_ffn': gain(ks[29], (DEPTH, D_MODEL)),
    }


def reference(x_prompt, x_sample, cache_k, cache_v, state_conv, cache_mem_k, cache_mem_v, page_table,
              mem_prompt, g_mem, w_mem_k, w_mem_v, g_pre_mix, w_in, w_dw, b_dw, g_ln_conv, b_ln_conv,
              g_attn_grp, g_conv_grp, w_out, g_post_mix, g_pre_x, w_xq, w_xo, g_post_x,
              g_pre_ffn, w_ffn_in, w_ffn_out, g_post_ffn):
    hp, hs = x_prompt, x_sample
    kp_l, vp_l, cp_l, mkp_l, mvp_l, ks_l, vs_l, cs_l = [], [], [], [], [], [], [], []
    for l in range(DEPTH):
        lw = (g_pre_mix[l], w_in[l], w_dw[l], b_dw[l], g_ln_conv[l], b_ln_conv[l], g_attn_grp[l],
              g_conv_grp[l], w_out[l], g_post_mix[l], g_pre_x[l], w_xq[l], w_xo[l], g_post_x[l],
              g_pre_ffn[l], w_ffn_in[l], w_ffn_out[l], g_post_ffn[l])
        mk_p, mv_p = memory_kv(mem_prompt, g_mem[l], w_mem_k[l], w_mem_v[l])
        conv0 = jnp.zeros((hp.shape[0], CONV_WIDTH - 1, CONV_CH), hp.dtype)
        hp, kp, vp, cp = layer_forward(hp, None, None, conv0, mk_p, mv_p, *lw)
        k_past = cache_k[l][page_table].reshape(hs.shape[0], -1, N_HEADS, HEAD_DIM)
        v_past = cache_v[l][page_table].reshape(hs.shape[0], -1, N_HEADS, HEAD_DIM)
        hs, ks_, vs_, cs = layer_forward(hs, k_past, v_past, state_conv[l], cache_mem_k[l], cache_mem_v[l], *lw)
        kp_l.append(kp); vp_l.append(vp); cp_l.append(cp); mkp_l.append(mk_p); mvp_l.append(mv_p)
        ks_l.append(ks_); vs_l.append(vs_); cs_l.append(cs)
    new_k_prompt = jnp.stack(kp_l, 0)
    new_v_prompt = jnp.stack(vp_l, 0)
    new_conv_prompt = jnp.stack(cp_l, 0)
    new_mem_k_prompt = jnp.stack(mkp_l, 0)
    new_mem_v_prompt = jnp.stack(mvp_l, 0)
    new_k_sample = jnp.stack(ks_l, 0)
    new_v_sample = jnp.stack(vs_l, 0)
    new_conv_sample = jnp.stack(cs_l, 0)
    return (hp, hs, new_k_prompt, new_v_prompt, new_conv_prompt, new_mem_k_prompt, new_mem_v_prompt,
            new_k_sample, new_v_sample, new_conv_sample)
```

```python
import functools

import jax
import jax.numpy as jnp
from jax import lax
from jax.experimental import pallas as pl
from jax.experimental.pallas import tpu as pltpu

EPS = 1e-6
N_HEADS = 8
HEAD_DIM = 128
ATTN_WIDTH = N_HEADS * HEAD_DIM
CONV_WIDTH = 31
MOBA_BLOCK = 256
MOBA_TOPK = 3
X_HEADS = 4
X_HEAD_DIM = 128

CONV_HALO = 32
VMEM_LIMIT = 56 * 1024 * 1024

F32 = jnp.float32
BF16 = jnp.bfloat16
NEG_INF = float("-inf")
_NT = (((1,), (1,)), ((), ()))


def _params(*sem):
    return pltpu.CompilerParams(dimension_semantics=sem, vmem_limit_bytes=VMEM_LIMIT)


def _rms(x, g):
    return x * lax.rsqrt(jnp.mean(x * x, axis=-1, keepdims=True) + EPS) * g


def _top_blocks(gate, n_valid):
    rows, nb = gate.shape
    col = lax.broadcasted_iota(jnp.int32, (rows, nb), 1)
    g = jnp.where(col < n_valid, gate, NEG_INF)
    idxs, oks = [], []
    for _ in range(MOBA_TOPK):
        m = jnp.max(g, axis=1, keepdims=True)
        idx = jnp.min(jnp.where(g == m, col, nb), axis=1, keepdims=True)
        ok = m > NEG_INF
        idxs.append(idx)
        oks.append(ok)
        g = jnp.where((col == idx) & ok, NEG_INF, g)
    return idxs, oks


def _norm_matmul_kernel(x_ref, g_ref, w_ref, *rest, tile_ranges):
    out_refs, xn_ref = rest[:-1], rest[-1]
    j = pl.program_id(1)

    @pl.when(j == 0)
    def _():
        xn_ref[...] = _rms(x_ref[...], g_ref[...]).astype(BF16)

    y = jnp.dot(xn_ref[...], w_ref[...], preferred_element_type=F32)
    for o_ref, (lo, hi) in zip(out_refs, tile_ranges):
        @pl.when((j >= lo) & (j < hi))
        def _(o_ref=o_ref):
            o_ref[...] = y


def _norm_matmul(x, g, w, widths, tn, tm):
    m, d = x.shape
    n = w.shape[1]
    assert sum(widths) == n and all(wd % tn == 0 for wd in widths) and m % tm == 0
    tile_ranges, lo = [], 0
    for wd in widths:
        tile_ranges.append((lo, lo + wd // tn))
        lo += wd // tn

    def out_map(i, j, lo, cnt):
        return (i, jnp.clip(j - lo, 0, cnt - 1))

    out_specs = [pl.BlockSpec((tm, tn), functools.partial(out_map, lo=lo, cnt=hi - lo))
                 for lo, hi in tile_ranges]
    return pl.pallas_call(
        functools.partial(_norm_matmul_kernel, tile_ranges=tuple(tile_ranges)),
        grid=(m // tm, n // tn),
        in_specs=[pl.BlockSpec((tm, d), lambda i, j: (i, 0)),
                  pl.BlockSpec((1, d), lambda i, j: (0, 0)),
                  pl.BlockSpec((d, tn), lambda i, j: (0, j))],
        out_specs=out_specs,
        out_shape=[jax.ShapeDtypeStruct((m, wd), F32) for wd in widths],
        scratch_shapes=[pltpu.VMEM((tm, d), BF16)],
        compiler_params=_params("arbitrary", "arbitrary"),
        name="norm_matmul",
    )(x, g, w)


def _moba_prompt_kernel(q_ref, k_ref, v_ref, o_ref, kb_ref, vb_ref, km_ref, m_ref, l_ref, acc_ref, *, nb):
    blk = MOBA_BLOCK
    h = pl.program_id(1)
    i = pl.program_id(2)

    @pl.when(i == 0)
    def _():
        for n in range(nb):
            kf = k_ref[0, n * blk:(n + 1) * blk, :]
            kb_ref[n * blk:(n + 1) * blk, :] = kf.astype(BF16)
            vb_ref[n * blk:(n + 1) * blk, :] = v_ref[0, n * blk:(n + 1) * blk, :].astype(BF16)
            km_ref[n:n + 1, :] = jnp.mean(kf, axis=0, keepdims=True)

    q = q_ref[0]
    gate = lax.dot_general(q, km_ref[...], _NT, precision=lax.Precision.HIGHEST,
                           preferred_element_type=F32)
    idxs, oks = _top_blocks(gate, i)
    col = lax.broadcasted_iota(jnp.int32, (blk, nb), 1)
    sel = jnp.zeros((blk, nb), F32)
    for idx, ok in zip(idxs, oks):
        sel = jnp.where((col == idx) & ok, 1.0, sel)

    qb = q.astype(BF16)
    scale = HEAD_DIM ** -0.5
    slope = jnp.exp2(jnp.broadcast_to(-8.0 * (h + 1).astype(F32) / N_HEADS, (blk, 1)))
    r = lax.broadcasted_iota(jnp.int32, (blk, blk), 0)
    c = lax.broadcasted_iota(jnp.int32, (blk, blk), 1)
    base = slope * (r - c).astype(F32)

    own = pl.multiple_of(i * blk, blk)
    s = lax.dot_general(qb, kb_ref[pl.ds(own, blk), :], _NT, preferred_element_type=F32) * scale - base
    s = jnp.where(c <= r, s, NEG_INF)
    m0 = jnp.max(s, axis=1, keepdims=True)
    p = jnp.exp(s - m0)
    m_ref[...] = m0
    l_ref[...] = jnp.sum(p, axis=1, keepdims=True)
    acc_ref[...] = jnp.dot(p.astype(BF16), vb_ref[pl.ds(own, blk), :], preferred_element_type=F32)

    def body(n, carry):
        selcol = jnp.sum(jnp.where(col == n, sel, 0.0), axis=1, keepdims=True)
        start = pl.multiple_of(n * blk, blk)
        off = slope * ((i - n) * blk).astype(F32)
        s = lax.dot_general(qb, kb_ref[pl.ds(start, blk), :], _NT, preferred_element_type=F32) * scale - base - off
        s = jnp.where(selcol > 0.0, s, NEG_INF)
        m_old = m_ref[...]
        m_new = jnp.maximum(m_old, jnp.max(s, axis=1, keepdims=True))
        alpha = jnp.exp(m_old - m_new)
        p = jnp.exp(s - m_new)
        l_ref[...] = alpha * l_ref[...] + jnp.sum(p, axis=1, keepdims=True)
        acc_ref[...] = alpha * acc_ref[...] + jnp.dot(p.astype(BF16), vb_ref[pl.ds(start, blk), :],
                                                      preferred_element_type=F32)
        m_ref[...] = m_new
        return carry

    lax.fori_loop(0, i, body, 0)
    o_ref[0] = acc_ref[...] / l_ref[...]


def _moba_prompt(q, k, v):
    b, t, _ = q.shape
    blk = MOBA_BLOCK
    assert t % blk == 0
    nb = t // blk
    return pl.pallas_call(
        functools.partial(_moba_prompt_kernel, nb=nb),
        grid=(b, N_HEADS, nb),
        in_specs=[pl.BlockSpec((1, blk, HEAD_DIM), lambda bi, h, i: (bi, i, h)),
                  pl.BlockSpec((1, t, HEAD_DIM), lambda bi, h, i: (bi, 0, h)),
                  pl.BlockSpec((1, t, HEAD_DIM), lambda bi, h, i: (bi, 0, h))],
        out_specs=pl.BlockSpec((1, blk, HEAD_DIM), lambda bi, h, i: (bi, i, h)),
        out_shape=jax.ShapeDtypeStruct((b, t, ATTN_WIDTH), F32),
        scratch_shapes=[pltpu.VMEM((t, HEAD_DIM), BF16), pltpu.VMEM((t, HEAD_DIM), BF16),
                        pltpu.VMEM((nb, HEAD_DIM), F32), pltpu.VMEM((blk, 1), F32),
                        pltpu.VMEM((blk, 1), F32), pltpu.VMEM((blk, HEAD_DIM), F32)],
        compiler_params=_params("arbitrary", "arbitrary", "arbitrary"),
        name="moba_prompt",
    )(q, k, v)


def _head_page_copy(cache_ref, pt_ref, buf_ref, sem, layer, b, h, page_slot, dst_row, n_pages):
    page = pt_ref[b * n_pages + page_slot]
    ps = cache_ref.shape[2]
    return pltpu.make_async_copy(cache_ref.at[layer, page, :, h, :], buf_ref.at[pl.ds(dst_row, ps), :], sem)


def _sample_scores_kernel(pt_ref, q_ref, kc_ref, s_ref, sel_ref, kbuf_ref, km_ref, sem_ref,
                          *, layer, n_pages, chunk):
    b = pl.program_id(0)
    h = pl.program_id(1)
    nbt, nh = pl.num_programs(0), pl.num_programs(1)
    step = b * nh + h
    slot = step % 2
    ps = kc_ref.shape[2]
    past = n_pages * ps
    blk = MOBA_BLOCK

    def copies(st, sl):
        bb, hh = st // nh, st % nh
        return lambda p: _head_page_copy(kc_ref, pt_ref, kbuf_ref.at[sl], sem_ref.at[sl], layer, bb, hh,
                                         p, pl.multiple_of(p * ps, ps), n_pages)

    def start_all(st, sl):
        mk = copies(st, sl)
        lax.fori_loop(0, n_pages, lambda p, c: (mk(p).start(), c)[1], 0)

    @pl.when(step == 0)
    def _():
        start_all(step, slot)

    @pl.when(step + 1 < nbt * nh)
    def _():
        start_all(step + 1, 1 - slot)

    mk = copies(step, slot)
    lax.fori_loop(0, n_pages, lambda p, c: (mk(p).wait(), c)[1], 0)

    q = q_ref[0]
    qb = q.astype(BF16)
    bpc = chunk // blk

    def body(ci, carry):
        start = pl.multiple_of(ci * chunk, chunk)
        kf = kbuf_ref[slot, pl.ds(start, chunk), :]
        sc = lax.dot_general(qb, kf.astype(BF16), _NT, preferred_element_type=F32)
        for jj in range(bpc):
            s_ref[0, 0, ci * bpc + jj] = sc[:, jj * blk:(jj + 1) * blk]
        km_ref[pl.ds(pl.multiple_of(ci * bpc, bpc), bpc), :] = jnp.mean(
            kf.reshape(bpc, blk, HEAD_DIM), axis=1)
        return carry

    lax.fori_loop(0, past // chunk, body, 0)

    gate = lax.dot_general(q, km_ref[...], _NT, precision=lax.Precision.HIGHEST,
                           preferred_element_type=F32)
    idxs, _ = _top_blocks(gate, past // blk)
    lane = lax.broadcasted_iota(jnp.int32, sel_ref.shape[2:], 1)
    out = jnp.zeros(sel_ref.shape[2:], jnp.int32)
    for k, idx in enumerate(idxs):
        out = jnp.where(lane == k, idx, out)
    sel_ref[0, 0] = out


def _sample_scores(q, cache_k, page_table, layer):
    b, t, _ = q.shape
    n_pages = page_table.shape[1]
    ps = cache_k.shape[2]
    past = n_pages * ps
    chunk = 8 * MOBA_BLOCK
    assert past % chunk == 0 and past // MOBA_BLOCK >= MOBA_TOPK
    grid_spec = pltpu.PrefetchScalarGridSpec(
        num_scalar_prefetch=1,
        grid=(b, N_HEADS),
        in_specs=[pl.BlockSpec((1, t, HEAD_DIM), lambda bi, h, pt: (bi, 0, h)),
                  pl.BlockSpec(memory_space=pl.ANY)],
        out_specs=[pl.BlockSpec((1, 1, past // MOBA_BLOCK, t, MOBA_BLOCK), lambda bi, h, pt: (bi, h, 0, 0, 0)),
                   pl.BlockSpec((1, 1, t, 128), lambda bi, h, pt: (bi, h, 0, 0))],
        scratch_shapes=[pltpu.VMEM((2, past, HEAD_DIM), F32),
                        pltpu.VMEM((past // MOBA_BLOCK, HEAD_DIM), F32),
                        pltpu.SemaphoreType.DMA((2,))],
    )
    return pl.pallas_call(
        functools.partial(_sample_scores_kernel, layer=layer, n_pages=n_pages, chunk=chunk),
        grid_spec=grid_spec,
        out_shape=[jax.ShapeDtypeStruct((b, N_HEADS, past // MOBA_BLOCK, t, MOBA_BLOCK), F32),
                   jax.ShapeDtypeStruct((b, N_HEADS, t, 128), jnp.int32)],
        compiler_params=_params("arbitrary", "arbitrary"),
        name="sample_scores",
    )(page_table.reshape(-1), q, cache_k)


def _sample_attend_kernel(pt_ref, sel_ref, s_ref, q_ref, kn_ref, vn_ref, vc_ref, o_ref,
                          vbuf_ref, ssel_ref, sem_ref, *, layer, n_pages, q_start):
    b = pl.program_id(0)
    h = pl.program_id(1)
    nbt, nh = pl.num_programs(0), pl.num_programs(1)
    step = b * nh + h
    slot = step % 2
    t = q_ref.shape[1]
    ps = vc_ref.shape[2]
    blk = MOBA_BLOCK
    ppb = blk // ps
    nsel = t * MOBA_TOPK

    def sel_block(st, e):
        return sel_ref[st * nsel + e]

    def copies(st, sl):
        bb, hh = st // nh, st % nh

        def mk(e, pg):
            return _head_page_copy(vc_ref, pt_ref, vbuf_ref.at[sl], sem_ref.at[sl], layer, bb, hh,
                                   sel_block(st, e) * ppb + pg, e * blk + pg * ps, n_pages)
        return mk

    def start_all(st, sl):
        mk = copies(st, sl)
        for e in range(nsel):
            for pg in range(ppb):
                mk(e, pg).start()

    @pl.when(step == 0)
    def _():
        start_all(step, slot)

    @pl.when(step + 1 < nbt * nh)
    def _():
        start_all(step + 1, 1 - slot)

    scale = HEAD_DIM ** -0.5
    slope = jnp.exp2(jnp.broadcast_to(-8.0 * (h + 1).astype(F32) / N_HEADS, (1, 1)))
    lane = lax.broadcasted_iota(jnp.int32, (1, blk), 1)

    ssel_ref[...] = jnp.full(ssel_ref.shape, NEG_INF, F32)
    for e in range(nsel):
        ti = e // MOBA_TOPK
        n = sel_block(step, e)
        raw = s_ref[0, 0, n, ti:ti + 1, :]
        dist = (q_start + ti - n * blk - lane).astype(F32)
        ssel_ref[ti:ti + 1, e * blk:(e + 1) * blk] = raw * scale - slope * dist

    q = q_ref[0].astype(BF16)
    r = lax.broadcasted_iota(jnp.int32, (t, t), 0)
    c = lax.broadcasted_iota(jnp.int32, (t, t), 1)
    s_own = lax.dot_general(q, kn_ref[0].astype(BF16), _NT, preferred_element_type=F32) * scale
    s_own = jnp.where(c <= r, s_own - slope * (r - c).astype(F32), NEG_INF)

    s_sel = ssel_ref[...]
    m = jnp.maximum(jnp.max(s_sel, axis=1, keepdims=True), jnp.max(s_own, axis=1, keepdims=True))
    p_sel = jnp.exp(s_sel - m)
    p_own = jnp.exp(s_own - m)
    l = jnp.sum(p_sel, axis=1, keepdims=True) + jnp.sum(p_own, axis=1, keepdims=True)

    mk = copies(step, slot)
    for e in range(nsel):
        for pg in range(ppb):
            mk(e, pg).wait()

    acc = jnp.dot(p_sel.astype(BF16), vbuf_ref[slot].astype(BF16), preferred_element_type=F32)
    acc = acc + jnp.dot(p_own.astype(BF16), vn_ref[0].astype(BF16), preferred_element_type=F32)
    o_ref[0] = acc / l


def _sample_attend(scores, sel, q, k_new, v_new, cache_v, page_table, layer):
    b, t, _ = q.shape
    n_pages = page_table.shape[1]
    ps = cache_v.shape[2]
    past = n_pages * ps
    assert MOBA_BLOCK % ps == 0 and past % MOBA_BLOCK == 0 and t <= MOBA_BLOCK
    nsel = t * MOBA_TOPK
    grid_spec = pltpu.PrefetchScalarGridSpec(
        num_scalar_prefetch=2,
        grid=(b, N_HEADS),
        in_specs=[pl.BlockSpec((1, 1, past // MOBA_BLOCK, t, MOBA_BLOCK), lambda bi, h, pt, sl: (bi, h, 0, 0, 0)),
                  pl.BlockSpec((1, t, HEAD_DIM), lambda bi, h, pt, sl: (bi, 0, h)),
                  pl.BlockSpec((1, t, HEAD_DIM), lambda bi, h, pt, sl: (bi, 0, h)),
                  pl.BlockSpec((1, t, HEAD_DIM), lambda bi, h, pt, sl: (bi, 0, h)),
                  pl.BlockSpec(memory_space=pl.ANY)],
        out_specs=pl.BlockSpec((1, t, HEAD_DIM), lambda bi, h, pt, sl: (bi, 0, h)),
        scratch_shapes=[pltpu.VMEM((2, nsel * MOBA_BLOCK, HEAD_DIM), F32),
                        pltpu.VMEM((t, nsel * MOBA_BLOCK), F32),
                        pltpu.SemaphoreType.DMA((2,))],
    )
    return pl.pallas_call(
        functools.partial(_sample_attend_kernel, layer=layer, n_pages=n_pages, q_start=past),
        grid_spec=grid_spec,
        out_shape=jax.ShapeDtypeStruct((b, t, ATTN_WIDTH), F32),
        compiler_params=_params("arbitrary", "arbitrary"),
        name="sample_attend",
    )(page_table.reshape(-1), sel[..., :MOBA_TOPK].reshape(-1), scores, q, k_new, v_new, cache_v)


def _conv_kernel(u_ref, prev_ref, w_ref, b_ref, g_ref, bl_ref, y_ref, st_ref, buf_ref, *, tt, rows):
    ti = pl.program_id(1)
    ch = y_ref.shape[2]
    hist = CONV_WIDTH - 1
    pad = CONV_HALO - hist

    @pl.when(ti == 0)
    def _():
        buf_ref[0:CONV_HALO, :] = jnp.zeros((CONV_HALO, ch), F32)
        buf_ref[pad:CONV_HALO, :] = prev_ref[0]

    @pl.when(ti > 0)
    def _():
        buf_ref[0:CONV_HALO, :] = buf_ref[tt:tt + CONV_HALO, :]

    def glu(ci, carry):
        r0 = pl.multiple_of(ci * rows, rows)
        u = u_ref[0, pl.ds(r0, rows), :]
        buf_ref[pl.ds(CONV_HALO + r0, rows), :] = u[:, :ch] * jax.nn.sigmoid(u[:, ch:])
        return carry

    lax.fori_loop(0, tt // rows, glu, 0)

    def conv(ci, carry):
        r0 = pl.multiple_of(ci * rows, rows)
        win = buf_ref.at[pl.ds(r0, rows + CONV_HALO), :]
        acc = jnp.broadcast_to(b_ref[...], (rows, ch))
        for j in range(CONV_WIDTH):
            acc = acc + w_ref[j:j + 1, :] * win[pad + j:pad + j + rows, :]
        xc = acc - jnp.mean(acc, axis=-1, keepdims=True)
        y = xc * lax.rsqrt(jnp.mean(xc * xc, axis=-1, keepdims=True) + EPS) * g_ref[...] + bl_ref[...]
        y_ref[0, pl.ds(r0, rows), :] = y * jax.nn.sigmoid(y)
        return carry

    lax.fori_loop(0, tt // rows, conv, 0)

    @pl.when(ti == pl.num_programs(1) - 1)
    def _():
        st_ref[0] = buf_ref[tt + pad:tt + CONV_HALO, :]


def _conformer_conv(u, prev, w_dw, b_dw, g_ln, b_ln, tt):
    b, t, ch2 = u.shape
    ch = ch2 // 2
    hist = CONV_WIDTH - 1
    assert t % tt == 0 and tt % 8 == 0
    rows = 16 if tt % 16 == 0 else 8
    row = lambda a: a.reshape(1, ch)
    return pl.pallas_call(
        functools.partial(_conv_kernel, tt=tt, rows=rows),
        grid=(b, t // tt),
        in_specs=[pl.BlockSpec((1, tt, ch2), lambda bi, ti: (bi, ti, 0)),
                  pl.BlockSpec((1, hist, ch), lambda bi, ti: (bi, 0, 0)),
                  pl.BlockSpec((CONV_WIDTH, ch), lambda bi, ti: (0, 0)),
                  pl.BlockSpec((1, ch), lambda bi, ti: (0, 0)),
                  pl.BlockSpec((1, ch), lambda bi, ti: (0, 0)),
                  pl.BlockSpec((1, ch), lambda bi, ti: (0, 0))],
        out_specs=[pl.BlockSpec((1, tt, ch), lambda bi, ti: (bi, ti, 0)),
                   pl.BlockSpec((1, hist, ch), lambda bi, ti: (bi, 0, 0))],
        out_shape=[jax.ShapeDtypeStruct((b, t, ch), F32), jax.ShapeDtypeStruct((b, hist, ch), F32)],
        scratch_shapes=[pltpu.VMEM((tt + CONV_HALO, ch), F32)],
        compiler_params=_params("arbitrary", "arbitrary"),
        name="conformer_conv",
    )(u, prev, w_dw, row(b_dw), row(g_ln), row(b_ln))


def _mix_out_kernel(attn_ref, conv_ref, h_ref, ga_ref, gc_ref, w_ref, gp_ref, o_ref):
    wa = attn_ref.shape[1]
    a = _rms(attn_ref[...], ga_ref[...]).astype(BF16)
    cv = _rms(conv_ref[...], gc_ref[...]).astype(BF16)
    mixed = (jnp.dot(a, w_ref[0:wa, :], preferred_element_type=F32)
             + jnp.dot(cv, w_ref[wa:, :], preferred_element_type=F32))
    o_ref[...] = h_ref[...] + _rms(mixed, gp_ref[...])


def _mix_out(attn, conv, h, g_attn, g_conv, w_out, g_post, tm):
    m, wa = attn.shape
    wc = conv.shape[1]
    d = h.shape[1]
    assert m % tm == 0
    row = lambda a: a.reshape(1, -1)
    return pl.pallas_call(
        _mix_out_kernel,
        grid=(m // tm,),
        in_specs=[pl.BlockSpec((tm, wa), lambda i: (i, 0)),
                  pl.BlockSpec((tm, wc), lambda i: (i, 0)),
                  pl.BlockSpec((tm, d), lambda i: (i, 0)),
                  pl.BlockSpec((1, wa), lambda i: (0, 0)),
                  pl.BlockSpec((1, wc), lambda i: (0, 0)),
                  pl.BlockSpec((wa + wc, d), lambda i: (0, 0)),
                  pl.BlockSpec((1, d), lambda i: (0, 0))],
        out_specs=pl.BlockSpec((tm, d), lambda i: (i, 0)),
        out_shape=jax.ShapeDtypeStruct((m, d), F32),
        compiler_params=_params("arbitrary"),
        name="mix_out",
    )(attn, conv, h, row(g_attn), row(g_conv), w_out, row(g_post))


def _xattn_kernel(h_ref, mk_ref, mv_ref, gpre_ref, wq_ref, wo_ref, gpost_ref, o_ref):
    h = h_ref[...]
    xn = _rms(h, gpre_ref[...]).astype(BF16)
    xq = jnp.dot(xn, wq_ref[...], preferred_element_type=F32)
    scale = X_HEAD_DIM ** -0.5
    outs = []
    for hh in range(X_HEADS):
        cols = slice(hh * X_HEAD_DIM, (hh + 1) * X_HEAD_DIM)
        s = lax.dot_general(xq[:, cols].astype(BF16), mk_ref[0, :, cols].astype(BF16), _NT,
                            preferred_element_type=F32) * scale
        p = jnp.exp(s - jnp.max(s, axis=1, keepdims=True))
        l = jnp.sum(p, axis=1, keepdims=True)
        outs.append(jnp.dot(p.astype(BF16), mv_ref[0, :, cols].astype(BF16), preferred_element_type=F32) / l)
    o = jnp.concatenate(outs, axis=1).astype(BF16)
    y = jnp.dot(o, wo_ref[...], preferred_element_type=F32)
    o_ref[...] = h + _rms(y, gpost_ref[...])


def _xattn(h, mem_k, mem_v, g_pre, w_xq, w_xo, g_post, rows_per_batch, tm):
    m, d = h.shape
    assert rows_per_batch % tm == 0
    tiles = rows_per_batch // tm
    mem_len, xw = mem_k.shape[1:]
    row = lambda a: a.reshape(1, -1)
    return pl.pallas_call(
        _xattn_kernel,
        grid=(m // tm,),
        in_specs=[pl.BlockSpec((tm, d), lambda i: (i, 0)),
                  pl.BlockSpec((1, mem_len, xw), lambda i: (i // tiles, 0, 0)),
                  pl.BlockSpec((1, mem_len, xw), lambda i: (i // tiles, 0, 0)),
                  pl.BlockSpec((1, d), lambda i: (0, 0)),
                  pl.BlockSpec((d, xw), lambda i: (0, 0)),
                  pl.BlockSpec((xw, d), lambda i: (0, 0)),
                  pl.BlockSpec((1, d), lambda i: (0, 0))],
        out_specs=pl.BlockSpec((tm, d), lambda i: (i, 0)),
        out_shape=jax.ShapeDtypeStruct((m, d), F32),
        compiler_params=_params("arbitrary"),
        name="xattn",
    )(h, mem_k, mem_v, row(g_pre), w_xq, w_xo, row(g_post))


def _ffn_kernel(h_ref, gpre_ref, wg_ref, wu_ref, wo_ref, gpost_ref, o_ref, xn_ref, acc_ref):
    f = pl.program_id(1)

    @pl.when(f == 0)
    def _():
        xn_ref[...] = _rms(h_ref[...], gpre_ref[...]).astype(BF16)
        acc_ref[...] = jnp.zeros(acc_ref.shape, F32)

    xn = xn_ref[...]
    g = jnp.dot(xn, wg_ref[...], preferred_element_type=F32)
    u = jnp.dot(xn, wu_ref[...], preferred_element_type=F32)
    a = (g * jax.nn.sigmoid(g) * u).astype(BF16)
    acc_ref[...] += jnp.dot(a, wo_ref[...], preferred_element_type=F32)

    @pl.when(f == pl.num_programs(1) - 1)
    def _():
        o_ref[...] = h_ref[...] + _rms(acc_ref[...], gpost_ref[...])


def _ffn(h, g_pre, w_in, w_out, g_post, tm, tf):
    m, d = h.shape
    hidden = w_out.shape[0]
    assert m % tm == 0 and hidden % tf == 0 and w_in.shape[1] == 2 * hidden
    nf = hidden // tf
    row = lambda a: a.reshape(1, -1)
    return pl.pallas_call(
        _ffn_kernel,
        grid=(m // tm, nf),
        in_specs=[pl.BlockSpec((tm, d), lambda i, f: (i, 0)),
                  pl.BlockSpec((1, d), lambda i, f: (0, 0)),
                  pl.BlockSpec((d, tf), lambda i, f: (0, f)),
                  pl.BlockSpec((d, tf), lambda i, f: (0, f + nf)),
                  pl.BlockSpec((tf, d), lambda i, f: (f, 0)),
                  pl.BlockSpec((1, d), lambda i, f: (0, 0))],
        out_specs=pl.BlockSpec((tm, d), lambda i, f: (i, 0)),
        out_shape=jax.ShapeDtypeStruct((m, d), F32),
        scratch_shapes=[pltpu.VMEM((tm, d), BF16), pltpu.VMEM((tm, d), F32)],
        compiler_params=_params("arbitrary", "arbitrary"),
        name="ffn",
    )(h, row(g_pre), w_in, w_in, w_out, row(g_post))


def _row_tile(m):
    return 512 if m % 512 == 0 else m


def _layer(x, attn_fn, conv_prev, mem_k, mem_v, wts, conv_tile, x_tile):
    b, t, d = x.shape
    m = b * t
    tm = _row_tile(m)
    x2 = x.reshape(m, d)
    ch = wts["w_dw"].shape[1]
    q, k_new, v_new, u = _norm_matmul(x2, wts["g_pre_mix"].reshape(1, d), wts["w_in"],
                                      (ATTN_WIDTH, ATTN_WIDTH, ATTN_WIDTH, 2 * ch), tn=ATTN_WIDTH, tm=tm)
    q, k_new, v_new = (a.reshape(b, t, ATTN_WIDTH) for a in (q, k_new, v_new))
    attn = attn_fn(q, k_new, v_new)
    conv, conv_state = _conformer_conv(u.reshape(b, t, 2 * ch), conv_prev, wts["w_dw"], wts["b_dw"],
                                       wts["g_ln_conv"], wts["b_ln_conv"], conv_tile)
    h = _mix_out(attn.reshape(m, ATTN_WIDTH), conv.reshape(m, ch), x2, wts["g_attn_grp"], wts["g_conv_grp"],
                 wts["w_out"], wts["g_post_mix"], tm=min(tm, 256))
    h = _xattn(h, mem_k, mem_v, wts["g_pre_x"], wts["w_xq"], wts["w_xo"], wts["g_post_x"],
               rows_per_batch=t, tm=x_tile)
    h = _ffn(h, wts["g_pre_ffn"], wts["w_ffn_in"], wts["w_ffn_out"], wts["g_post_ffn"], tm=tm, tf=512)
    shape4 = (b, t, N_HEADS, HEAD_DIM)
    return h.reshape(b, t, d), k_new.reshape(shape4), v_new.reshape(shape4), conv_state


def kernel(x_prompt, x_sample, cache_k, cache_v, state_conv, cache_mem_k, cache_mem_v, page_table,
           mem_prompt, g_mem, w_mem_k, w_mem_v, g_pre_mix, w_in, w_dw, b_dw, g_ln_conv, b_ln_conv,
           g_attn_grp, g_conv_grp, w_out, g_post_mix, g_pre_x, w_xq, w_xo, g_post_x,
           g_pre_ffn, w_ffn_in, w_ffn_out, g_post_ffn):
    depth = w_in.shape[0]
    bp, tp, d = x_prompt.shape
    bs, ts, _ = x_sample.shape
    mem_len = mem_prompt.shape[1]
    xw = X_HEADS * X_HEAD_DIM
    ch = w_dw.shape[2]
    hp, hs = x_prompt, x_sample
    outs = [[] for _ in range(8)]
    for l in range(depth):
        wts = dict(g_pre_mix=g_pre_mix[l], w_in=w_in[l].astype(BF16), w_dw=w_dw[l], b_dw=b_dw[l],
                   g_ln_conv=g_ln_conv[l], b_ln_conv=b_ln_conv[l], g_attn_grp=g_attn_grp[l],
                   g_conv_grp=g_conv_grp[l], w_out=w_out[l].astype(BF16), g_post_mix=g_post_mix[l],
                   g_pre_x=g_pre_x[l], w_xq=w_xq[l].astype(BF16), w_xo=w_xo[l].astype(BF16),
                   g_post_x=g_post_x[l], g_pre_ffn=g_pre_ffn[l], w_ffn_in=w_ffn_in[l].astype(BF16),
                   w_ffn_out=w_ffn_out[l].astype(BF16), g_post_ffn=g_post_ffn[l])
        w_mem = jnp.concatenate([w_mem_k[l], w_mem_v[l]], axis=1).astype(BF16)
        mk_p, mv_p = _norm_matmul(mem_prompt.reshape(bp * mem_len, d), g_mem[l].reshape(1, d), w_mem,
                                  (xw, xw), tn=xw, tm=_row_tile(bp * mem_len))
        mk_p, mv_p = mk_p.reshape(bp, mem_len, xw), mv_p.reshape(bp, mem_len, xw)
        conv0 = jnp.zeros((bp, CONV_WIDTH - 1, ch), F32)
        hp, kp, vp, cp = _layer(hp, _moba_prompt, conv0, mk_p, mv_p, wts,
                                conv_tile=_row_tile(tp), x_tile=_row_tile(tp))

        def sample_attn(q, k_new, v_new, l=l):
            scores, sel = _sample_scores(q, cache_k, page_table, l)
            return _sample_attend(scores, sel, q, k_new, v_new, cache_v, page_table, l)

        hs, ks, vs, cs = _layer(hs, sample_attn, state_conv[l], cache_mem_k[l].reshape(bs, mem_len, xw),
                                cache_mem_v[l].reshape(bs, mem_len, xw), wts, conv_tile=ts, x_tile=ts)
        mem_shape = (bp, mem_len, X_HEADS, X_HEAD_DIM)
        for lst, a in zip(outs, (kp, vp, cp, mk_p.reshape(mem_shape), mv_p.reshape(mem_shape), ks, vs, cs)):
            lst.append(a)
    return (hp, hs) + tuple(jnp.stack(lst, 0) for lst in outs)
```

```python
import functools

import jax
import jax.numpy as jnp
from jax import lax
from jax.experimental import pallas as pl
from jax.experimental.pallas import tpu as pltpu

EPS = 1e-6
N_HEADS = 8
HEAD_DIM = 128
ATTN_WIDTH = N_HEADS * HEAD_DIM
CONV_WIDTH = 31
MOBA_BLOCK = 256
MOBA_TOPK = 3
X_HEADS = 4
X_HEAD_DIM = 128

CONV_HALO = 32
VMEM_LIMIT = 56 * 1024 * 1024

F32 = jnp.float32
BF16 = jnp.bfloat16
NEG_INF = float("-inf")
LOG2_E = 1.4426950408889634
_NT = (((1,), (1,)), ((), ()))


def _params(*sem):
    return pltpu.CompilerParams(dimension_semantics=sem, vmem_limit_bytes=VMEM_LIMIT)


def _rms(x, g):
    return x * lax.rsqrt(jnp.mean(x * x, axis=-1, keepdims=True) + EPS) * g


def _top_blocks(gate, n_valid, axis):
    nb = gate.shape[axis]
    pos = lax.broadcasted_iota(jnp.int32, gate.shape, axis)
    g = jnp.where(pos < n_valid, gate, NEG_INF)
    idxs, oks = [], []
    for _ in range(MOBA_TOPK):
        m = jnp.max(g, axis=axis, keepdims=True)
        idx = jnp.min(jnp.where(g == m, pos, nb), axis=axis, keepdims=True)
        ok = m > NEG_INF
        idxs.append(idx)
        oks.append(ok)
        g = jnp.where((pos == idx) & ok, NEG_INF, g)
    return idxs, oks


def _norm_matmul_kernel(x_ref, g_ref, w_ref, *rest, tile_ranges):
    out_refs, xn_ref = rest[:-1], rest[-1]
    j = pl.program_id(1)

    @pl.when(j == 0)
    def _():
        xn_ref[...] = _rms(x_ref[...], g_ref[...]).astype(BF16)

    for o_ref, (lo, hi) in zip(out_refs, tile_ranges):
        @pl.when((j >= lo) & (j < hi))
        def _(o_ref=o_ref):
            o_ref[...] = jnp.dot(xn_ref[...], w_ref[...], preferred_element_type=F32)


def _norm_matmul(x, g, w, widths, tn, tm):
    m, d = x.shape
    n = w.shape[1]
    assert sum(widths) == n and all(wd % tn == 0 for wd in widths) and m % tm == 0
    tile_ranges, lo = [], 0
    for wd in widths:
        tile_ranges.append((lo, lo + wd // tn))
        lo += wd // tn

    def out_map(i, j, lo, cnt):
        return (i, jnp.clip(j - lo, 0, cnt - 1))

    out_specs = [pl.BlockSpec((tm, tn), functools.partial(out_map, lo=lo, cnt=hi - lo))
                 for lo, hi in tile_ranges]
    return pl.pallas_call(
        functools.partial(_norm_matmul_kernel, tile_ranges=tuple(tile_ranges)),
        grid=(m // tm, n // tn),
        in_specs=[pl.BlockSpec((tm, d), lambda i, j: (i, 0)),
                  pl.BlockSpec((1, d), lambda i, j: (0, 0)),
                  pl.BlockSpec((d, tn), lambda i, j: (0, j))],
        out_specs=out_specs,
        out_shape=[jax.ShapeDtypeStruct((m, wd), F32) for wd in widths],
        scratch_shapes=[pltpu.VMEM((tm, d), BF16)],
        compiler_params=_params("arbitrary", "arbitrary"),
        name="norm_matmul",
    )(x, g, w)


def _moba_prompt_kernel(q_ref, k_ref, v_ref, o_ref, kb_ref, vt_ref, km_ref, *, nb, sb):
    blk = MOBA_BLOCK
    h = pl.program_id(1)
    i = pl.program_id(2)

    @pl.when(i == 0)
    def _():
        for n in range(nb):
            kf = k_ref[0, n * blk:(n + 1) * blk, :]
            kb_ref[n * blk:(n + 1) * blk, :] = kf.astype(BF16)
            vt_ref[n] = v_ref[0, n * blk:(n + 1) * blk, :].T.astype(BF16)
            km_ref[n:n + 1, :] = jnp.mean(kf, axis=0, keepdims=True)

    q = q_ref[0]
    qb = q.astype(BF16)
    gate = lax.dot_general(km_ref[...], q, _NT, precision=lax.Precision.HIGHEST,
                           preferred_element_type=F32)
    idxs, oks = _top_blocks(gate, i, axis=0)
    row = lax.broadcasted_iota(jnp.int32, (nb, blk), 0)
    sel = jnp.zeros((nb, blk), F32)
    for idx, ok in zip(idxs, oks):
        sel = jnp.where((row == idx) & ok, 1.0, sel)

    scale = HEAD_DIM ** -0.5 * LOG2_E
    slope = jnp.exp2(jnp.broadcast_to(-8.0 * (h + 1).astype(F32) / N_HEADS, (1, blk))) * LOG2_E

    def attend(n_past):
        kr = lax.broadcasted_iota(jnp.int32, (blk, blk), 0)
        qc = lax.broadcasted_iota(jnp.int32, (blk, blk), 1)
        base = slope * (qc - kr).astype(F32)
        own = pl.multiple_of(i * blk, blk)
        s_own = lax.dot_general(kb_ref[pl.ds(own, blk), :], qb, _NT, preferred_element_type=F32)
        ts = [jnp.where(kr <= qc, s_own * scale - base, NEG_INF)]
        if n_past:
            s_past = lax.dot_general(kb_ref[0:n_past * blk, :], qb, _NT, preferred_element_type=F32)
            for n in range(n_past):
                off = slope * ((i - n) * blk).astype(F32)
                neg = jnp.where(sel[n:n + 1, :] > 0.0, -off, NEG_INF)
                ts.append(s_past[n * blk:(n + 1) * blk, :] * scale - base + neg)
        m = jnp.max(functools.reduce(jnp.maximum, ts), axis=0, keepdims=True)
        acc = None
        psum = None
        for n, t in enumerate(ts):
            p = jnp.exp2(t - m)
            psum = p if psum is None else psum + p
            vt = vt_ref[i] if n == 0 else vt_ref[n - 1]
            pv = jnp.dot(vt, p.astype(BF16), preferred_element_type=F32)
            acc = pv if acc is None else acc + pv
        l = jnp.sum(psum, axis=0, keepdims=True)
        o_ref[0] = (acc / l).T

    n_sb = (i + sb - 1) // sb
    for v in range((nb - 1 + sb - 1) // sb + 1):
        @pl.when(n_sb == v)
        def _(v=v):
            attend(min(v * sb, nb - 1))


def _moba_prompt(q, k, v, sb=2):
    b, t, _ = q.shape
    blk = MOBA_BLOCK
    assert t % blk == 0
    nb = t // blk
    return pl.pallas_call(
        functools.partial(_moba_prompt_kernel, nb=nb, sb=sb),
        grid=(b, N_HEADS, nb),
        in_specs=[pl.BlockSpec((1, blk, HEAD_DIM), lambda bi, h, i: (bi, i, h)),
                  pl.BlockSpec((1, t, HEAD_DIM), lambda bi, h, i: (bi, 0, h)),
                  pl.BlockSpec((1, t, HEAD_DIM), lambda bi, h, i: (bi, 0, h))],
        out_specs=pl.BlockSpec((1, blk, HEAD_DIM), lambda bi, h, i: (bi, i, h)),
        out_shape=jax.ShapeDtypeStruct((b, t, ATTN_WIDTH), F32),
        scratch_shapes=[pltpu.VMEM((t, HEAD_DIM), BF16), pltpu.VMEM((nb, HEAD_DIM, blk), BF16),
                        pltpu.VMEM((nb, HEAD_DIM), F32)],
        compiler_params=_params("arbitrary", "arbitrary", "arbitrary"),
        name="moba_prompt",
    )(q, k, v)


def _head_page_copy(cache_ref, pt_ref, buf_ref, sem, layer, b, h, page_slot, dst_row, n_pages):
    page = pt_ref[b * n_pages + page_slot]
    ps = cache_ref.shape[2]
    return pltpu.make_async_copy(cache_ref.at[layer, page, :, h, :], buf_ref.at[pl.ds(dst_row, ps), :], sem)


def _sample_scores_kernel(pt_ref, q_ref, kc_ref, s_ref, sel_ref, kbuf_ref, km_ref, gate_ref, sem_ref,
                          *, layer, n_pages, chunk):
    b = pl.program_id(0)
    h = pl.program_id(1)
    nbt, nh = pl.num_programs(0), pl.num_programs(1)
    step = b * nh + h
    slot = step % 2
    ps = kc_ref.shape[2]
    past = n_pages * ps
    blk = MOBA_BLOCK
    t = q_ref.shape[1]

    def start_all(st, sl):
        bb, hh = st // nh, st % nh

        def issue(p, carry):
            _head_page_copy(kc_ref, pt_ref, kbuf_ref.at[sl], sem_ref.at[sl], layer, bb, hh,
                            p, pl.multiple_of(p * ps, ps), n_pages).start()
            return carry

        lax.fori_loop(0, n_pages, issue, 0, unroll=8)

    @pl.when(step == 0)
    def _():
        start_all(step, slot)

    @pl.when(step + 1 < nbt * nh)
    def _():
        start_all(step + 1, 1 - slot)

    pltpu.make_async_copy(kbuf_ref.at[1 - slot], kbuf_ref.at[slot], sem_ref.at[slot]).wait()

    q = q_ref[0]
    qb = q.astype(BF16)
    bpc = chunk // blk

    def body(ci, carry):
        start = pl.multiple_of(ci * chunk, chunk)
        kf = kbuf_ref[slot, pl.ds(start, chunk), :]
        sc = lax.dot_general(qb, kf.astype(BF16), _NT, preferred_element_type=F32)
        for jj in range(bpc):
            s_ref[0, 0, ci * bpc + jj] = sc[:, jj * blk:(jj + 1) * blk]
        km_ref[pl.ds(pl.multiple_of(ci * bpc, bpc), bpc), :] = jnp.mean(
            kf.reshape(bpc, blk, HEAD_DIM), axis=1)
        return carry

    lax.fori_loop(0, past // chunk, body, 0, unroll=2)

    gate_ref[h] = lax.dot_general(q, km_ref[...], _NT, precision=lax.Precision.HIGHEST,
                                  preferred_element_type=F32)

    @pl.when(h == nh - 1)
    def _():
        gate = gate_ref[...].reshape(nh * t, past // blk)
        idxs, _ = _top_blocks(gate, past // blk, axis=1)
        lane = lax.broadcasted_iota(jnp.int32, (nh * t, 128), 1)
        out = jnp.zeros((nh * t, 128), jnp.int32)
        for k, idx in enumerate(idxs):
            out = jnp.where(lane == k, idx, out)
        sel_ref[0] = out.reshape(nh, t, 128)


def _sample_scores(q, cache_k, page_table, layer):
    b, t, _ = q.shape
    n_pages = page_table.shape[1]
    ps = cache_k.shape[2]
    past = n_pages * ps
    chunk = 8 * MOBA_BLOCK
    assert past % chunk == 0 and past // MOBA_BLOCK >= MOBA_TOPK
    grid_spec = pltpu.PrefetchScalarGridSpec(
        num_scalar_prefetch=1,
        grid=(b, N_HEADS),
        in_specs=[pl.BlockSpec((1, t, HEAD_DIM), lambda bi, h, pt: (bi, 0, h)),
                  pl.BlockSpec(memory_space=pl.ANY)],
        out_specs=[pl.BlockSpec((1, 1, past // MOBA_BLOCK, t, MOBA_BLOCK), lambda bi, h, pt: (bi, h, 0, 0, 0)),
                   pl.BlockSpec((1, N_HEADS, t, 128), lambda bi, h, pt: (bi, 0, 0, 0))],
        scratch_shapes=[pltpu.VMEM((2, past, HEAD_DIM), F32),
                        pltpu.VMEM((past // MOBA_BLOCK, HEAD_DIM), F32),
                        pltpu.VMEM((N_HEADS, t, past // MOBA_BLOCK), F32),
                        pltpu.SemaphoreType.DMA((2,))],
    )
    return pl.pallas_call(
        functools.partial(_sample_scores_kernel, layer=layer, n_pages=n_pages, chunk=chunk),
        grid_spec=grid_spec,
        out_shape=[jax.ShapeDtypeStruct((b, N_HEADS, past // MOBA_BLOCK, t, MOBA_BLOCK), F32),
                   jax.ShapeDtypeStruct((b, N_HEADS, t, 128), jnp.int32)],
        compiler_params=_params("arbitrary", "arbitrary"),
        name="sample_scores",
    )(page_table.reshape(-1), q, cache_k)


def _sample_attend_kernel(pt_ref, sel_ref, s_ref, q_ref, kn_ref, vn_ref, vc_ref, o_ref,
                          vbuf_ref, ssel_ref, sem_ref, *, layer, n_pages, q_start):
    b = pl.program_id(0)
    h = pl.program_id(1)
    nbt, nh = pl.num_programs(0), pl.num_programs(1)
    step = b * nh + h
    slot = step % 2
    t = q_ref.shape[1]
    ps = vc_ref.shape[2]
    blk = MOBA_BLOCK
    ppb = blk // ps
    nsel = t * MOBA_TOPK

    def sel_block(st, e):
        return sel_ref[st * nsel + e]

    def copies(st, sl):
        bb, hh = st // nh, st % nh

        def mk(e, pg):
            return _head_page_copy(vc_ref, pt_ref, vbuf_ref.at[sl], sem_ref.at[sl], layer, bb, hh,
                                   sel_block(st, e) * ppb + pg, e * blk + pg * ps, n_pages)
        return mk

    def start_all(st, sl):
        mk = copies(st, sl)
        for e in range(nsel):
            for pg in range(ppb):
                mk(e, pg).start()

    @pl.when(step == 0)
    def _():
        start_all(step, slot)

    @pl.when(step + 1 < nbt * nh)
    def _():
        start_all(step + 1, 1 - slot)

    scale = HEAD_DIM ** -0.5
    slope = jnp.exp2(jnp.broadcast_to(-8.0 * (h + 1).astype(F32) / N_HEADS, (1, 1)))
    lane = lax.broadcasted_iota(jnp.int32, (1, blk), 1)

    ssel_ref[...] = jnp.full(ssel_ref.shape, NEG_INF, F32)
    for e in range(nsel):
        ti = e // MOBA_TOPK
        n = sel_block(step, e)
        raw = s_ref[0, 0, n, ti:ti + 1, :]
        dist = (q_start + ti - n * blk - lane).astype(F32)
        ssel_ref[ti:ti + 1, e * blk:(e + 1) * blk] = raw * scale - slope * dist

    q = q_ref[0].astype(BF16)
    r = lax.broadcasted_iota(jnp.int32, (t, t), 0)
    c = lax.broadcasted_iota(jnp.int32, (t, t), 1)
    s_own = lax.dot_general(q, kn_ref[0].astype(BF16), _NT, preferred_element_type=F32) * scale
    s_own = jnp.where(c <= r, s_own - slope * (r - c).astype(F32), NEG_INF)

    s_sel = ssel_ref[...]
    m = jnp.maximum(jnp.max(s_sel, axis=1, keepdims=True), jnp.max(s_own, axis=1, keepdims=True))
    p_sel = jnp.exp(s_sel - m)
    p_own = jnp.exp(s_own - m)
    l = jnp.sum(p_sel, axis=1, keepdims=True) + jnp.sum(p_own, axis=1, keepdims=True)

    mk = copies(step, slot)
    for e in range(nsel):
        for pg in range(ppb):
            mk(e, pg).wait()

    acc = jnp.dot(p_sel.astype(BF16), vbuf_ref[slot].astype(BF16), preferred_element_type=F32)
    acc = acc + jnp.dot(p_own.astype(BF16), vn_ref[0].astype(BF16), preferred_element_type=F32)
    o_ref[0] = acc / l


def _sample_attend(scores, sel, q, k_new, v_new, cache_v, page_table, layer):
    b, t, _ = q.shape
    n_pages = page_table.shape[1]
    ps = cache_v.shape[2]
    past = n_pages * ps
    assert MOBA_BLOCK % ps == 0 and past % MOBA_BLOCK == 0 and t <= MOBA_BLOCK
    nsel = t * MOBA_TOPK
    grid_spec = pltpu.PrefetchScalarGridSpec(
        num_scalar_prefetch=2,
        grid=(b, N_HEADS),
        in_specs=[pl.BlockSpec((1, 1, past // MOBA_BLOCK, t, MOBA_BLOCK), lambda bi, h, pt, sl: (bi, h, 0, 0, 0)),
                  pl.BlockSpec((1, t, HEAD_DIM), lambda bi, h, pt, sl: (bi, 0, h)),
                  pl.BlockSpec((1, t, HEAD_DIM), lambda bi, h, pt, sl: (bi, 0, h)),
                  pl.BlockSpec((1, t, HEAD_DIM), lambda bi, h, pt, sl: (bi, 0, h)),
                  pl.BlockSpec(memory_space=pl.ANY)],
        out_specs=pl.BlockSpec((1, t, HEAD_DIM), lambda bi, h, pt, sl: (bi, 0, h)),
        scratch_shapes=[pltpu.VMEM((2, nsel * MOBA_BLOCK, HEAD_DIM), F32),
                        pltpu.VMEM((t, nsel * MOBA_BLOCK), F32),
                        pltpu.SemaphoreType.DMA((2,))],
    )
    return pl.pallas_call(
        functools.partial(_sample_attend_kernel, layer=layer, n_pages=n_pages, q_start=past),
        grid_spec=grid_spec,
        out_shape=jax.ShapeDtypeStruct((b, t, ATTN_WIDTH), F32),
        compiler_params=_params("arbitrary", "arbitrary"),
        name="sample_attend",
    )(page_table.reshape(-1), sel[..., :MOBA_TOPK].reshape(-1), scores, q, k_new, v_new, cache_v)


def _conv_taps():
    pad = CONV_HALO - (CONV_WIDTH - 1)
    return [(ph, [(j, (pad + j) // 8) for j in range(CONV_WIDTH) if (pad + j) % 8 == ph]) for ph in range(8)]


def _conv_kernel(u_ref, prev_ref, w_ref, b_ref, g_ref, bl_ref, y_ref, st_ref, buf_ref, par_ref, z_ref,
                 *, tt, rows, nrows, lanes):
    ti = pl.program_id(1)
    ch = y_ref.shape[2]
    hist = CONV_WIDTH - 1
    pad = CONV_HALO - hist
    i_bias, i_gain, i_beta = CONV_WIDTH, CONV_WIDTH + 1, CONV_WIDTH + 2

    @pl.when(ti == 0)
    def _():
        buf_ref[0:CONV_HALO, :] = jnp.zeros((CONV_HALO, ch), F32)
        buf_ref[pad:CONV_HALO, :] = prev_ref[0]
        for j in range(CONV_WIDTH):
            par_ref[j] = jnp.broadcast_to(w_ref[j:j + 1, :], (8, ch))
        par_ref[i_bias] = jnp.broadcast_to(b_ref[...], (8, ch))
        par_ref[i_gain] = jnp.broadcast_to(g_ref[...], (8, ch))
        par_ref[i_beta] = jnp.broadcast_to(bl_ref[...], (8, ch))

    @pl.when(ti > 0)
    def _():
        buf_ref[0:CONV_HALO, :] = buf_ref[tt:tt + CONV_HALO, :]

    def glu(ci, carry):
        r0 = pl.multiple_of(ci * rows, rows)
        u = u_ref[0, pl.ds(r0, rows), :]
        buf_ref[pl.ds(CONV_HALO + r0, rows), :] = u[:, :ch] * jax.nn.sigmoid(u[:, ch:])
        return carry

    lax.fori_loop(0, tt // rows, glu, 0)

    def conv(ci, carry):
        r0 = pl.multiple_of(ci * rows, rows)
        for c0 in range(0, ch, lanes):
            cs = slice(c0, c0 + lanes)
            win = buf_ref.at[pl.ds(r0, rows + CONV_HALO), pl.ds(c0, lanes)]
            acc = jnp.broadcast_to(par_ref[i_bias, :, cs][None], (rows // 8, 8, lanes))
            for ph, taps in _conv_taps():
                a_lo, a_hi = taps[0][1], taps[-1][1]
                x = win[8 * a_lo + ph:8 * a_hi + ph + rows, :]
                for j, a in taps:
                    xs = x[8 * (a - a_lo):8 * (a - a_lo) + rows, :].reshape(rows // 8, 8, lanes)
                    acc = acc + par_ref[j, :, cs][None] * xs
            z_ref[pl.ds(r0, rows), cs] = acc.reshape(rows, lanes)
        return carry

    lax.fori_loop(0, tt // rows, conv, 0)

    def norm(ci, carry):
        r0 = pl.multiple_of(ci * nrows, nrows)
        z = z_ref[pl.ds(r0, nrows), :]
        xc = z - jnp.mean(z, axis=-1, keepdims=True)
        xn = (xc * lax.rsqrt(jnp.mean(xc * xc, axis=-1, keepdims=True) + EPS)).reshape(nrows // 8, 8, ch)
        y = (xn * par_ref[i_gain][None] + par_ref[i_beta][None]).reshape(nrows, ch)
        y_ref[0, pl.ds(r0, nrows), :] = y * jax.nn.sigmoid(y)
        return carry

    lax.fori_loop(0, tt // nrows, norm, 0, unroll=min(8, tt // nrows))

    @pl.when(ti == pl.num_programs(1) - 1)
    def _():
        st_ref[0] = buf_ref[tt + pad:tt + CONV_HALO, :]


def _conformer_conv(u, prev, w_dw, b_dw, g_ln, b_ln, tt):
    b, t, ch2 = u.shape
    ch = ch2 // 2
    hist = CONV_WIDTH - 1
    assert t % tt == 0 and tt % 8 == 0
    rows = 32 if tt % 32 == 0 else 8
    nrows = 16 if tt % 16 == 0 else 8
    lanes = 512 if ch % 512 == 0 else ch
    row = lambda a: a.reshape(1, ch)
    return pl.pallas_call(
        functools.partial(_conv_kernel, tt=tt, rows=rows, nrows=nrows, lanes=lanes),
        grid=(b, t // tt),
        in_specs=[pl.BlockSpec((1, tt, ch2), lambda bi, ti: (bi, ti, 0)),
                  pl.BlockSpec((1, hist, ch), lambda bi, ti: (bi, 0, 0)),
                  pl.BlockSpec((CONV_WIDTH, ch), lambda bi, ti: (0, 0)),
                  pl.BlockSpec((1, ch), lambda bi, ti: (0, 0)),
                  pl.BlockSpec((1, ch), lambda bi, ti: (0, 0)),
                  pl.BlockSpec((1, ch), lambda bi, ti: (0, 0))],
        out_specs=[pl.BlockSpec((1, tt, ch), lambda bi, ti: (bi, ti, 0)),
                   pl.BlockSpec((1, hist, ch), lambda bi, ti: (bi, 0, 0))],
        out_shape=[jax.ShapeDtypeStruct((b, t, ch), F32), jax.ShapeDtypeStruct((b, hist, ch), F32)],
        scratch_shapes=[pltpu.VMEM((tt + CONV_HALO, ch), F32), pltpu.VMEM((CONV_WIDTH + 3, 8, ch), F32),
                        pltpu.VMEM((tt, ch), F32)],
        compiler_params=_params("arbitrary", "arbitrary"),
        name="conformer_conv",
    )(u, prev, w_dw, row(b_dw), row(g_ln), row(b_ln))


def _mix_out_kernel(attn_ref, conv_ref, h_ref, ga_ref, gc_ref, w_ref, gp_ref, o_ref):
    wa = attn_ref.shape[1]
    a = _rms(attn_ref[...], ga_ref[...]).astype(BF16)
    cv = _rms(conv_ref[...], gc_ref[...]).astype(BF16)
    mixed = (jnp.dot(a, w_ref[0:wa, :], preferred_element_type=F32)
             + jnp.dot(cv, w_ref[wa:, :], preferred_element_type=F32))
    o_ref[...] = h_ref[...] + _rms(mixed, gp_ref[...])


def _mix_out(attn, conv, h, g_attn, g_conv, w_out, g_post, tm):
    m, wa = attn.shape
    wc = conv.shape[1]
    d = h.shape[1]
    assert m % tm == 0
    row = lambda a: a.reshape(1, -1)
    return pl.pallas_call(
        _mix_out_kernel,
        grid=(m // tm,),
        in_specs=[pl.BlockSpec((tm, wa), lambda i: (i, 0)),
                  pl.BlockSpec((tm, wc), lambda i: (i, 0)),
                  pl.BlockSpec((tm, d), lambda i: (i, 0)),
                  pl.BlockSpec((1, wa), lambda i: (0, 0)),
                  pl.BlockSpec((1, wc), lambda i: (0, 0)),
                  pl.BlockSpec((wa + wc, d), lambda i: (0, 0)),
                  pl.BlockSpec((1, d), lambda i: (0, 0))],
        out_specs=pl.BlockSpec((tm, d), lambda i: (i, 0)),
        out_shape=jax.ShapeDtypeStruct((m, d), F32),
        compiler_params=_params("arbitrary"),
        name="mix_out",
    )(attn, conv, h, row(g_attn), row(g_conv), w_out, row(g_post))


def _xattn_kernel(h_ref, mk_ref, mv_ref, gpre_ref, wq_ref, wo_ref, gpost_ref, o_ref):
    h = h_ref[...]
    xn = _rms(h, gpre_ref[...]).astype(BF16)
    xq = jnp.dot(xn, wq_ref[...], preferred_element_type=F32)
    scale = X_HEAD_DIM ** -0.5
    outs = []
    for hh in range(X_HEADS):
        cols = slice(hh * X_HEAD_DIM, (hh + 1) * X_HEAD_DIM)
        s = lax.dot_general(xq[:, cols].astype(BF16), mk_ref[0, :, cols].astype(BF16), _NT,
                            preferred_element_type=F32) * scale
        p = jnp.exp(s - jnp.max(s, axis=1, keepdims=True))
        l = jnp.sum(p, axis=1, keepdims=True)
        outs.append(jnp.dot(p.astype(BF16), mv_ref[0, :, cols].astype(BF16), preferred_element_type=F32) / l)
    o = jnp.concatenate(outs, axis=1).astype(BF16)
    y = jnp.dot(o, wo_ref[...], preferred_element_type=F32)
    o_ref[...] = h + _rms(y, gpost_ref[...])


def _xattn(h, mem_k, mem_v, g_pre, w_xq, w_xo, g_post, rows_per_batch, tm):
    m, d = h.shape
    assert rows_per_batch % tm == 0
    tiles = rows_per_batch // tm
    mem_len, xw = mem_k.shape[1:]
    row = lambda a: a.reshape(1, -1)
    return pl.pallas_call(
        _xattn_kernel,
        grid=(m // tm,),
        in_specs=[pl.BlockSpec((tm, d), lambda i: (i, 0)),
                  pl.BlockSpec((1, mem_len, xw), lambda i: (i // tiles, 0, 0)),
                  pl.BlockSpec((1, mem_len, xw), lambda i: (i // tiles, 0, 0)),
                  pl.BlockSpec((1, d), lambda i: (0, 0)),
                  pl.BlockSpec((d, xw), lambda i: (0, 0)),
                  pl.BlockSpec((xw, d), lambda i: (0, 0)),
                  pl.BlockSpec((1, d), lambda i: (0, 0))],
        out_specs=pl.BlockSpec((tm, d), lambda i: (i, 0)),
        out_shape=jax.ShapeDtypeStruct((m, d), F32),
        compiler_params=_params("arbitrary"),
        name="xattn",
    )(h, mem_k, mem_v, row(g_pre), w_xq, w_xo, row(g_post))


def _ffn_kernel(h_ref, gpre_ref, wg_ref, wu_ref, wo_ref, gpost_ref, o_ref, xn_ref, acc_ref):
    f = pl.program_id(1)

    @pl.when(f == 0)
    def _():
        xn_ref[...] = _rms(h_ref[...], gpre_ref[...]).astype(BF16)
        acc_ref[...] = jnp.zeros(acc_ref.shape, F32)

    xn = xn_ref[...]
    g = jnp.dot(xn, wg_ref[...], preferred_element_type=F32)
    u = jnp.dot(xn, wu_ref[...], preferred_element_type=F32)
    a = (g * jax.nn.sigmoid(g) * u).astype(BF16)
    acc_ref[...] += jnp.dot(a, wo_ref[...], preferred_element_type=F32)

    @pl.when(f == pl.num_programs(1) - 1)
    def _():
        o_ref[...] = h_ref[...] + _rms(acc_ref[...], gpost_ref[...])


def _ffn(h, g_pre, w_in, w_out, g_post, tm, tf):
    m, d = h.shape
    hidden = w_out.shape[0]
    assert m % tm == 0 and hidden % tf == 0 and w_in.shape[1] == 2 * hidden
    nf = hidden // tf
    row = lambda a: a.reshape(1, -1)
    return pl.pallas_call(
        _ffn_kernel,
        grid=(m // tm, nf),
        in_specs=[pl.BlockSpec((tm, d), lambda i, f: (i, 0)),
                  pl.BlockSpec((1, d), lambda i, f: (0, 0)),
                  pl.BlockSpec((d, tf), lambda i, f: (0, f)),
                  pl.BlockSpec((d, tf), lambda i, f: (0, f + nf)),
                  pl.BlockSpec((tf, d), lambda i, f: (f, 0)),
                  pl.BlockSpec((1, d), lambda i, f: (0, 0))],
        out_specs=pl.BlockSpec((tm, d), lambda i, f: (i, 0)),
        out_shape=jax.ShapeDtypeStruct((m, d), F32),
        scratch_shapes=[pltpu.VMEM((tm, d), BF16), pltpu.VMEM((tm, d), F32)],
        compiler_params=_params("arbitrary", "arbitrary"),
        name="ffn",
    )(h, row(g_pre), w_in, w_in, w_out, row(g_post))


def _row_tile(m):
    return 512 if m % 512 == 0 else m


def _layer(x, attn_fn, conv_prev, mem_k, mem_v, wts, conv_tile, x_tile):
    b, t, d = x.shape
    m = b * t
    tm = _row_tile(m)
    x2 = x.reshape(m, d)
    ch = wts["w_dw"].shape[1]
    q, k_new, v_new, u = _norm_matmul(x2, wts["g_pre_mix"].reshape(1, d), wts["w_in"],
                                      (ATTN_WIDTH, ATTN_WIDTH, ATTN_WIDTH, 2 * ch), tn=512,
                                      tm=1024 if m % 1024 == 0 else tm)
    q, k_new, v_new = (a.reshape(b, t, ATTN_WIDTH) for a in (q, k_new, v_new))
    attn = attn_fn(q, k_new, v_new)
    conv, conv_state = _conformer_conv(u.reshape(b, t, 2 * ch), conv_prev, wts["w_dw"], wts["b_dw"],
                                       wts["g_ln_conv"], wts["b_ln_conv"], conv_tile)
    h = _mix_out(attn.reshape(m, ATTN_WIDTH), conv.reshape(m, ch), x2, wts["g_attn_grp"], wts["g_conv_grp"],
                 wts["w_out"], wts["g_post_mix"], tm=tm)
    h = _xattn(h, mem_k, mem_v, wts["g_pre_x"], wts["w_xq"], wts["w_xo"], wts["g_post_x"],
               rows_per_batch=t, tm=x_tile)
    h = _ffn(h, wts["g_pre_ffn"], wts["w_ffn_in"], wts["w_ffn_out"], wts["g_post_ffn"], tm=tm, tf=512)
    shape4 = (b, t, N_HEADS, HEAD_DIM)
    return h.reshape(b, t, d), k_new.reshape(shape4), v_new.reshape(shape4), conv_state


def kernel(x_prompt, x_sample, cache_k, cache_v, state_conv, cache_mem_k, cache_mem_v, page_table,
           mem_prompt, g_mem, w_mem_k, w_mem_v, g_pre_mix, w_in, w_dw, b_dw, g_ln_conv, b_ln_conv,
           g_attn_grp, g_conv_grp, w_out, g_post_mix, g_pre_x, w_xq, w_xo, g_post_x,
           g_pre_ffn, w_ffn_in, w_ffn_out, g_post_ffn):
    depth = w_in.shape[0]
    bp, tp, d = x_prompt.shape
    bs, ts, _ = x_sample.shape
    mem_len = mem_prompt.shape[1]
    xw = X_HEADS * X_HEAD_DIM
    ch = w_dw.shape[2]
    hp, hs = x_prompt, x_sample
    outs = [[] for _ in range(8)]
    for l in range(depth):
        wts = dict(g_pre_mix=g_pre_mix[l], w_in=w_in[l].astype(BF16), w_dw=w_dw[l], b_dw=b_dw[l],
                   g_ln_conv=g_ln_conv[l], b_ln_conv=b_ln_conv[l], g_attn_grp=g_attn_grp[l],
                   g_conv_grp=g_conv_grp[l], w_out=w_out[l].astype(BF16), g_post_mix=g_post_mix[l],
                   g_pre_x=g_pre_x[l], w_xq=w_xq[l].astype(BF16), w_xo=w_xo[l].astype(BF16),
                   g_post_x=g_post_x[l], g_pre_ffn=g_pre_ffn[l], w_ffn_in=w_ffn_in[l].astype(BF16),
                   w_ffn_out=w_ffn_out[l].astype(BF16), g_post_ffn=g_post_ffn[l])
        w_mem = jnp.concatenate([w_mem_k[l], w_mem_v[l]], axis=1).astype(BF16)
        mk_p, mv_p = _norm_matmul(mem_prompt.reshape(bp * mem_len, d), g_mem[l].reshape(1, d), w_mem,
                                  (xw, xw), tn=xw, tm=_row_tile(bp * mem_len))
        mk_p, mv_p = mk_p.reshape(bp, mem_len, xw), mv_p.reshape(bp, mem_len, xw)
        conv0 = jnp.zeros((bp, CONV_WIDTH - 1, ch), F32)
        hp, kp, vp, cp = _layer(hp, _moba_prompt, conv0, mk_p, mv_p, wts,
                                conv_tile=_row_tile(tp), x_tile=_row_tile(tp))

        def sample_attn(q, k_new, v_new, l=l):
            scores, sel = _sample_scores(q, cache_k, page_table, l)
            return _sample_attend(scores, sel, q, k_new, v_new, cache_v, page_table, l)

        hs, ks, vs, cs = _layer(hs, sample_attn, state_conv[l], cache_mem_k[l].reshape(bs, mem_len, xw),
                                cache_mem_v[l].reshape(bs, mem_len, xw), wts, conv_tile=ts, x_tile=ts)
        mem_shape = (bp, mem_len, X_HEADS, X_HEAD_DIM)
        for lst, a in zip(outs, (kp, vp, cp, mk_p.reshape(mem_shape), mv_p.reshape(mem_shape), ks, vs, cs)):
            lst.append(a)
    return (hp, hs) + tuple(jnp.stack(lst, 0) for lst in outs)
```

```python
import functools

import jax
import jax.numpy as jnp
from jax import lax
from jax.experimental import pallas as pl
from jax.experimental.pallas import tpu as pltpu

EPS = 1e-6
N_HEADS = 8
HEAD_DIM = 128
ATTN_WIDTH = N_HEADS * HEAD_DIM
CONV_WIDTH = 31
MOBA_BLOCK = 256
MOBA_TOPK = 3
X_HEADS = 4
X_HEAD_DIM = 128

CONV_HALO = 32
DMA_PRIORITIES = 2
VMEM_LIMIT = 56 * 1024 * 1024

F32 = jnp.float32
BF16 = jnp.bfloat16
NEG_INF = float("-inf")
LOG2_E = 1.4426950408889634
_NT = (((1,), (1,)), ((), ()))


def _params(*sem):
    return pltpu.CompilerParams(dimension_semantics=sem, vmem_limit_bytes=VMEM_LIMIT)


def _rms(x, g):
    return x * lax.rsqrt(jnp.mean(x * x, axis=-1, keepdims=True) + EPS) * g


def _top_blocks(gate, n_valid, axis):
    nb = gate.shape[axis]
    pos = lax.broadcasted_iota(jnp.int32, gate.shape, axis)
    g = jnp.where(pos < n_valid, gate, NEG_INF)
    idxs, oks = [], []
    for _ in range(MOBA_TOPK):
        m = jnp.max(g, axis=axis, keepdims=True)
        idx = jnp.min(jnp.where(g == m, pos, nb), axis=axis, keepdims=True)
        ok = m > NEG_INF
        idxs.append(idx)
        oks.append(ok)
        g = jnp.where((pos == idx) & ok, NEG_INF, g)
    return idxs, oks


def _norm_matmul_kernel(x_ref, g_ref, w_ref, *rest, tile_ranges):
    out_refs, xn_ref = rest[:-1], rest[-1]
    j = pl.program_id(1)

    @pl.when(j == 0)
    def _():
        xn_ref[...] = _rms(x_ref[...], g_ref[...]).astype(BF16)

    for o_ref, (lo, hi) in zip(out_refs, tile_ranges):
        @pl.when((j >= lo) & (j < hi))
        def _(o_ref=o_ref):
            o_ref[...] = jnp.dot(xn_ref[...], w_ref[...], preferred_element_type=F32)


def _norm_matmul(x, g, w, widths, tn, tm):
    m, d = x.shape
    n = w.shape[1]
    assert sum(widths) == n and all(wd % tn == 0 for wd in widths) and m % tm == 0
    tile_ranges, lo = [], 0
    for wd in widths:
        tile_ranges.append((lo, lo + wd // tn))
        lo += wd // tn

    def out_map(i, j, lo, cnt):
        return (i, jnp.clip(j - lo, 0, cnt - 1))

    out_specs = [pl.BlockSpec((tm, tn), functools.partial(out_map, lo=lo, cnt=hi - lo))
                 for lo, hi in tile_ranges]
    return pl.pallas_call(
        functools.partial(_norm_matmul_kernel, tile_ranges=tuple(tile_ranges)),
        grid=(m // tm, n // tn),
        in_specs=[pl.BlockSpec((tm, d), lambda i, j: (i, 0)),
                  pl.BlockSpec((1, d), lambda i, j: (0, 0)),
                  pl.BlockSpec((d, tn), lambda i, j: (0, j))],
        out_specs=out_specs,
        out_shape=[jax.ShapeDtypeStruct((m, wd), F32) for wd in widths],
        scratch_shapes=[pltpu.VMEM((tm, d), BF16)],
        compiler_params=_params("arbitrary", "arbitrary"),
        name="norm_matmul",
    )(x, g, w)


def _moba_prompt_kernel(q_ref, k_ref, v_ref, o_ref, kb_ref, vt_ref, km_ref, *, nb, tps):
    blk = MOBA_BLOCK
    h = pl.program_id(1)
    g = pl.program_id(2)

    @pl.when(g == 0)
    def _():
        for n in range(nb):
            kf = k_ref[0, n * blk:(n + 1) * blk, :]
            kb_ref[n * blk:(n + 1) * blk, :] = kf.astype(BF16)
            vt_ref[n] = v_ref[0, n * blk:(n + 1) * blk, :].T.astype(BF16)
            km_ref[n:n + 1, :] = jnp.mean(kf, axis=0, keepdims=True)

    scale = HEAD_DIM ** -0.5 * LOG2_E
    slope = jnp.exp2(jnp.broadcast_to(-8.0 * (h + 1).astype(F32) / N_HEADS, (1, blk))) * LOG2_E

    def attend(i, qi):
        q = q_ref[0, qi * blk:(qi + 1) * blk, :]
        qb = q.astype(BF16)
        kr = lax.broadcasted_iota(jnp.int32, (blk, blk), 0)
        qc = lax.broadcasted_iota(jnp.int32, (blk, blk), 1)
        base = slope * (qc - kr).astype(F32)
        s_own = lax.dot_general(kb_ref[i * blk:(i + 1) * blk, :], qb, _NT, preferred_element_type=F32)
        ts = [jnp.where(kr <= qc, s_own * scale - base, NEG_INF)]
        if i:
            gate = lax.dot_general(km_ref[0:i, :], q, _NT, precision=lax.Precision.HIGHEST,
                                   preferred_element_type=F32)
            idxs, oks = _top_blocks(gate, i, axis=0)
            row = lax.broadcasted_iota(jnp.int32, (i, blk), 0)
            sel = jnp.zeros((i, blk), F32)
            for idx, ok in zip(idxs, oks):
                sel = jnp.where((row == idx) & ok, 1.0, sel)
            for n in range(i):
                s = lax.dot_general(kb_ref[n * blk:(n + 1) * blk, :], qb, _NT, preferred_element_type=F32)
                neg = jnp.where(sel[n:n + 1, :] > 0.0, -slope * float((i - n) * blk), NEG_INF)
                ts.append(s * scale - base + neg)
        m = jnp.max(functools.reduce(jnp.maximum, ts), axis=0, keepdims=True)
        acc = None
        psum = None
        for n, t in enumerate(ts):
            p = jnp.exp2(t - m)
            psum = p if psum is None else psum + p
            vt = vt_ref[i] if n == 0 else vt_ref[n - 1]
            pv = jnp.dot(vt, p.astype(BF16), preferred_element_type=F32)
            acc = pv if acc is None else acc + pv
        l = jnp.sum(psum, axis=0, keepdims=True)
        o_ref[0, qi * blk:(qi + 1) * blk, :] = (acc / l).T

    for v in range(nb // tps):
        @pl.when(g == v)
        def _(v=v):
            for qi in range(tps):
                attend(v * tps + qi, qi)


def _moba_prompt(q, k, v, tps=2):
    b, t, _ = q.shape
    blk = MOBA_BLOCK
    assert t % (blk * tps) == 0
    nb = t // blk
    return pl.pallas_call(
        functools.partial(_moba_prompt_kernel, nb=nb, tps=tps),
        grid=(b, N_HEADS, nb // tps),
        in_specs=[pl.BlockSpec((1, blk * tps, HEAD_DIM), lambda bi, h, g: (bi, g, h)),
                  pl.BlockSpec((1, t, HEAD_DIM), lambda bi, h, g: (bi, 0, h)),
                  pl.BlockSpec((1, t, HEAD_DIM), lambda bi, h, g: (bi, 0, h))],
        out_specs=pl.BlockSpec((1, blk * tps, HEAD_DIM), lambda bi, h, g: (bi, g, h)),
        out_shape=jax.ShapeDtypeStruct((b, t, ATTN_WIDTH), F32),
        scratch_shapes=[pltpu.VMEM((t, HEAD_DIM), BF16), pltpu.VMEM((nb, HEAD_DIM, blk), BF16),
                        pltpu.VMEM((nb, HEAD_DIM), F32)],
        compiler_params=_params("arbitrary", "arbitrary", "arbitrary"),
        name="moba_prompt",
    )(q, k, v)


def _head_page_copy(cache_ref, pt_ref, buf_ref, sem, layer, b, h, page_slot, dst_row, n_pages):
    page = pt_ref[b * n_pages + page_slot]
    ps = cache_ref.shape[2]
    return pltpu.make_async_copy(cache_ref.at[layer, page, :, h, :], buf_ref.at[pl.ds(dst_row, ps), :], sem)


def _sample_scores_kernel(pt_ref, q_ref, kc_ref, s_ref, sel_ref, kbuf_ref, km_ref, gate_ref, sem_ref,
                          *, layer, n_pages, chunk):
    b = pl.program_id(0)
    h = pl.program_id(1)
    nbt, nh = pl.num_programs(0), pl.num_programs(1)
    step = b * nh + h
    slot = step % 2
    ps = kc_ref.shape[2]
    past = n_pages * ps
    blk = MOBA_BLOCK
    t = q_ref.shape[1]

    def start_all(st, sl):
        bb, hh = st // nh, st % nh

        def issue(pair, carry):
            for k in range(DMA_PRIORITIES):
                p = pair * DMA_PRIORITIES + k
                _head_page_copy(kc_ref, pt_ref, kbuf_ref.at[sl], sem_ref.at[sl], layer, bb, hh,
                                p, pl.multiple_of(p * ps, ps), n_pages).start(priority=k)
            return carry

        lax.fori_loop(0, n_pages // DMA_PRIORITIES, issue, 0, unroll=4)

    @pl.when(step == 0)
    def _():
        start_all(step, slot)

    @pl.when(step + 1 < nbt * nh)
    def _():
        start_all(step + 1, 1 - slot)

    pltpu.make_async_copy(kbuf_ref.at[1 - slot], kbuf_ref.at[slot], sem_ref.at[slot]).wait()

    q = q_ref[0]
    qb = q.astype(BF16)
    bpc = chunk // blk

    def body(ci, carry):
        start = pl.multiple_of(ci * chunk, chunk)
        kf = kbuf_ref[slot, pl.ds(start, chunk), :]
        sc = lax.dot_general(qb, kf.astype(BF16), _NT, preferred_element_type=F32)
        for jj in range(bpc):
            s_ref[0, 0, ci * bpc + jj] = sc[:, jj * blk:(jj + 1) * blk]
        km_ref[pl.ds(pl.multiple_of(ci * bpc, bpc), bpc), :] = jnp.mean(
            kf.reshape(bpc, blk, HEAD_DIM), axis=1)
        return carry

    lax.fori_loop(0, past // chunk, body, 0, unroll=2)

    gate_ref[h] = lax.dot_general(q, km_ref[...], _NT, precision=lax.Precision.HIGHEST,
                                  preferred_element_type=F32)

    @pl.when(h == nh - 1)
    def _():
        gate = gate_ref[...].reshape(nh * t, past // blk)
        idxs, _ = _top_blocks(gate, past // blk, axis=1)
        lane = lax.broadcasted_iota(jnp.int32, (nh * t, 128), 1)
        out = jnp.zeros((nh * t, 128), jnp.int32)
        for k, idx in enumerate(idxs):
            out = jnp.where(lane == k, idx, out)
        sel_ref[0] = out.reshape(nh, t, 128)


def _sample_scores(q, cache_k, page_table, layer):
    b, t, _ = q.shape
    n_pages = page_table.shape[1]
    ps = cache_k.shape[2]
    past = n_pages * ps
    chunk = 8 * MOBA_BLOCK
    assert past % chunk == 0 and past // MOBA_BLOCK >= MOBA_TOPK and n_pages % DMA_PRIORITIES == 0
    grid_spec = pltpu.PrefetchScalarGridSpec(
        num_scalar_prefetch=1,
        grid=(b, N_HEADS),
        in_specs=[pl.BlockSpec((1, t, HEAD_DIM), lambda bi, h, pt: (bi, 0, h)),
                  pl.BlockSpec(memory_space=pl.ANY)],
        out_specs=[pl.BlockSpec((1, 1, past // MOBA_BLOCK, t, MOBA_BLOCK), lambda bi, h, pt: (bi, h, 0, 0, 0)),
                   pl.BlockSpec((1, N_HEADS, t, 128), lambda bi, h, pt: (bi, 0, 0, 0))],
        scratch_shapes=[pltpu.VMEM((2, past, HEAD_DIM), F32),
                        pltpu.VMEM((past // MOBA_BLOCK, HEAD_DIM), F32),
                        pltpu.VMEM((N_HEADS, t, past // MOBA_BLOCK), F32),
                        pltpu.SemaphoreType.DMA((2,))],
    )
    return pl.pallas_call(
        functools.partial(_sample_scores_kernel, layer=layer, n_pages=n_pages, chunk=chunk),
        grid_spec=grid_spec,
        out_shape=[jax.ShapeDtypeStruct((b, N_HEADS, past // MOBA_BLOCK, t, MOBA_BLOCK), F32),
                   jax.ShapeDtypeStruct((b, N_HEADS, t, 128), jnp.int32)],
        compiler_params=_params("arbitrary", "arbitrary"),
        name="sample_scores",
    )(page_table.reshape(-1), q, cache_k)


def _sample_attend_kernel(pt_ref, sel_ref, s_ref, q_ref, kn_ref, vn_ref, vc_ref, o_ref,
                          vbuf_ref, ssel_ref, sem_ref, *, layer, n_pages, q_start):
    b = pl.program_id(0)
    h = pl.program_id(1)
    nbt, nh = pl.num_programs(0), pl.num_programs(1)
    step = b * nh + h
    slot = step % 2
    t = q_ref.shape[1]
    ps = vc_ref.shape[2]
    blk = MOBA_BLOCK
    ppb = blk // ps
    nsel = t * MOBA_TOPK

    def sel_block(st, e):
        return sel_ref[st * nsel + e]

    def copies(st, sl):
        bb, hh = st // nh, st % nh

        def mk(e, pg):
            return _head_page_copy(vc_ref, pt_ref, vbuf_ref.at[sl], sem_ref.at[sl], layer, bb, hh,
                                   sel_block(st, e) * ppb + pg, e * blk + pg * ps, n_pages)
        return mk

    def start_all(st, sl):
        mk = copies(st, sl)
        for e in range(nsel):
            for pg in range(ppb):
                mk(e, pg).start(priority=(e * ppb + pg) % DMA_PRIORITIES)

    @pl.when(step == 0)
    def _():
        start_all(step, slot)

    @pl.when(step + 1 < nbt * nh)
    def _():
        start_all(step + 1, 1 - slot)

    scale = HEAD_DIM ** -0.5
    slope = jnp.exp2(jnp.broadcast_to(-8.0 * (h + 1).astype(F32) / N_HEADS, (1, 1)))
    lane = lax.broadcasted_iota(jnp.int32, (1, blk), 1)

    ssel_ref[...] = jnp.full(ssel_ref.shape, NEG_INF, F32)
    for e in range(nsel):
        ti = e // MOBA_TOPK
        n = sel_block(step, e)
        raw = s_ref[0, 0, n, ti:ti + 1, :]
        dist = (q_start + ti - n * blk - lane).astype(F32)
        ssel_ref[ti:ti + 1, e * blk:(e + 1) * blk] = raw * scale - slope * dist

    q = q_ref[0].astype(BF16)
    r = lax.broadcasted_iota(jnp.int32, (t, t), 0)
    c = lax.broadcasted_iota(jnp.int32, (t, t), 1)
    s_own = lax.dot_general(q, kn_ref[0].astype(BF16), _NT, preferred_element_type=F32) * scale
    s_own = jnp.where(c <= r, s_own - slope * (r - c).astype(F32), NEG_INF)

    s_sel = ssel_ref[...]
    m = jnp.maximum(jnp.max(s_sel, axis=1, keepdims=True), jnp.max(s_own, axis=1, keepdims=True))
    p_sel = jnp.exp(s_sel - m)
    p_own = jnp.exp(s_own - m)
    l = jnp.sum(p_sel, axis=1, keepdims=True) + jnp.sum(p_own, axis=1, keepdims=True)

    mk = copies(step, slot)
    for e in range(nsel):
        for pg in range(ppb):
            mk(e, pg).wait()

    acc = jnp.dot(p_sel.astype(BF16), vbuf_ref[slot].astype(BF16), preferred_element_type=F32)
    acc = acc + jnp.dot(p_own.astype(BF16), vn_ref[0].astype(BF16), preferred_element_type=F32)
    o_ref[0] = acc / l


def _sample_attend(scores, sel, q, k_new, v_new, cache_v, page_table, layer):
    b, t, _ = q.shape
    n_pages = page_table.shape[1]
    ps = cache_v.shape[2]
    past = n_pages * ps
    assert MOBA_BLOCK % ps == 0 and past % MOBA_BLOCK == 0 and t <= MOBA_BLOCK
    nsel = t * MOBA_TOPK
    grid_spec = pltpu.PrefetchScalarGridSpec(
        num_scalar_prefetch=2,
        grid=(b, N_HEADS),
        in_specs=[pl.BlockSpec((1, 1, past // MOBA_BLOCK, t, MOBA_BLOCK), lambda bi, h, pt, sl: (bi, h, 0, 0, 0)),
                  pl.BlockSpec((1, t, HEAD_DIM), lambda bi, h, pt, sl: (bi, 0, h)),
                  pl.BlockSpec((1, t, HEAD_DIM), lambda bi, h, pt, sl: (bi, 0, h)),
                  pl.BlockSpec((1, t, HEAD_DIM), lambda bi, h, pt, sl: (bi, 0, h)),
                  pl.BlockSpec(memory_space=pl.ANY)],
        out_specs=pl.BlockSpec((1, t, HEAD_DIM), lambda bi, h, pt, sl: (bi, 0, h)),
        scratch_shapes=[pltpu.VMEM((2, nsel * MOBA_BLOCK, HEAD_DIM), F32),
                        pltpu.VMEM((t, nsel * MOBA_BLOCK), F32),
                        pltpu.SemaphoreType.DMA((2,))],
    )
    return pl.pallas_call(
        functools.partial(_sample_attend_kernel, layer=layer, n_pages=n_pages, q_start=past),
        grid_spec=grid_spec,
        out_shape=jax.ShapeDtypeStruct((b, t, ATTN_WIDTH), F32),
        compiler_params=_params("arbitrary", "arbitrary"),
        name="sample_attend",
    )(page_table.reshape(-1), sel[..., :MOBA_TOPK].reshape(-1), scores, q, k_new, v_new, cache_v)


def _conv_taps():
    pad = CONV_HALO - (CONV_WIDTH - 1)
    return [(ph, [(j, (pad + j) // 8) for j in range(CONV_WIDTH) if (pad + j) % 8 == ph]) for ph in range(8)]


def _conv_kernel(u_ref, prev_ref, w_ref, b_ref, g_ref, bl_ref, y_ref, st_ref, buf_ref, par_ref, z_ref,
                 *, tt, rows, nrows, lanes):
    ti = pl.program_id(1)
    ch = y_ref.shape[2]
    hist = CONV_WIDTH - 1
    pad = CONV_HALO - hist
    i_bias, i_gain, i_beta = CONV_WIDTH, CONV_WIDTH + 1, CONV_WIDTH + 2

    @pl.when(ti == 0)
    def _():
        buf_ref[0:CONV_HALO, :] = jnp.zeros((CONV_HALO, ch), F32)
        buf_ref[pad:CONV_HALO, :] = prev_ref[0]
        for j in range(CONV_WIDTH):
            par_ref[j] = jnp.broadcast_to(w_ref[j:j + 1, :], (8, ch))
        par_ref[i_bias] = jnp.broadcast_to(b_ref[...], (8, ch))
        par_ref[i_gain] = jnp.broadcast_to(g_ref[...], (8, ch))
        par_ref[i_beta] = jnp.broadcast_to(bl_ref[...], (8, ch))

    @pl.when(ti > 0)
    def _():
        buf_ref[0:CONV_HALO, :] = buf_ref[tt:tt + CONV_HALO, :]

    def glu(ci, carry):
        r0 = pl.multiple_of(ci * rows, rows)
        u = u_ref[0, pl.ds(r0, rows), :]
        buf_ref[pl.ds(CONV_HALO + r0, rows), :] = u[:, :ch] * jax.nn.sigmoid(u[:, ch:])
        return carry

    lax.fori_loop(0, tt // rows, glu, 0)

    def conv(ci, carry):
        r0 = pl.multiple_of(ci * rows, rows)
        for c0 in range(0, ch, lanes):
            cs = slice(c0, c0 + lanes)
            win = buf_ref.at[pl.ds(r0, rows + CONV_HALO), pl.ds(c0, lanes)]
            acc = jnp.broadcast_to(par_ref[i_bias, :, cs][None], (rows // 8, 8, lanes))
            for ph, taps in _conv_taps():
                a_lo, a_hi = taps[0][1], taps[-1][1]
                x = win[8 * a_lo + ph:8 * a_hi + ph + rows, :]
                for j, a in taps:
                    xs = x[8 * (a - a_lo):8 * (a - a_lo) + rows, :].reshape(rows // 8, 8, lanes)
                    acc = acc + par_ref[j, :, cs][None] * xs
            z_ref[pl.ds(r0, rows), cs] = acc.reshape(rows, lanes)
        return carry

    lax.fori_loop(0, tt // rows, conv, 0)

    def norm(ci, carry):
        r0 = pl.multiple_of(ci * nrows, nrows)
        z = z_ref[pl.ds(r0, nrows), :]
        xc = z - jnp.mean(z, axis=-1, keepdims=True)
        xn = (xc * lax.rsqrt(jnp.mean(xc * xc, axis=-1, keepdims=True) + EPS)).reshape(nrows // 8, 8, ch)
        y = (xn * par_ref[i_gain][None] + par_ref[i_beta][None]).reshape(nrows, ch)
        y_ref[0, pl.ds(r0, nrows), :] = y * jax.nn.sigmoid(y)
        return carry

    lax.fori_loop(0, tt // nrows, norm, 0, unroll=min(8, tt // nrows))

    @pl.when(ti == pl.num_programs(1) - 1)
    def _():
        st_ref[0] = buf_ref[tt + pad:tt + CONV_HALO, :]


def _conformer_conv(u, prev, w_dw, b_dw, g_ln, b_ln, tt):
    b, t, ch2 = u.shape
    ch = ch2 // 2
    hist = CONV_WIDTH - 1
    assert t % tt == 0 and tt % 8 == 0
    rows = 32 if tt % 32 == 0 else 8
    nrows = 16 if tt % 16 == 0 else 8
    lanes = 512 if ch % 512 == 0 else ch
    row = lambda a: a.reshape(1, ch)
    return pl.pallas_call(
        functools.partial(_conv_kernel, tt=tt, rows=rows, nrows=nrows, lanes=lanes),
        grid=(b, t // tt),
        in_specs=[pl.BlockSpec((1, tt, ch2), lambda bi, ti: (bi, ti, 0)),
                  pl.BlockSpec((1, hist, ch), lambda bi, ti: (bi, 0, 0)),
                  pl.BlockSpec((CONV_WIDTH, ch), lambda bi, ti: (0, 0)),
                  pl.BlockSpec((1, ch), lambda bi, ti: (0, 0)),
                  pl.BlockSpec((1, ch), lambda bi, ti: (0, 0)),
                  pl.BlockSpec((1, ch), lambda bi, ti: (0, 0))],
        out_specs=[pl.BlockSpec((1, tt, ch), lambda bi, ti: (bi, ti, 0)),
                   pl.BlockSpec((1, hist, ch), lambda bi, ti: (bi, 0, 0))],
        out_shape=[jax.ShapeDtypeStruct((b, t, ch), F32), jax.ShapeDtypeStruct((b, hist, ch), F32)],
        scratch_shapes=[pltpu.VMEM((tt + CONV_HALO, ch), F32), pltpu.VMEM((CONV_WIDTH + 3, 8, ch), F32),
                        pltpu.VMEM((tt, ch), F32)],
        compiler_params=_params("arbitrary", "arbitrary"),
        name="conformer_conv",
    )(u, prev, w_dw, row(b_dw), row(g_ln), row(b_ln))


def _mix_out_kernel(attn_ref, conv_ref, h_ref, ga_ref, gc_ref, w_ref, gp_ref, o_ref):
    wa = attn_ref.shape[1]
    a = _rms(attn_ref[...], ga_ref[...]).astype(BF16)
    cv = _rms(conv_ref[...], gc_ref[...]).astype(BF16)
    mixed = (jnp.dot(a, w_ref[0:wa, :], preferred_element_type=F32)
             + jnp.dot(cv, w_ref[wa:, :], preferred_element_type=F32))
    o_ref[...] = h_ref[...] + _rms(mixed, gp_ref[...])


def _mix_out(attn, conv, h, g_attn, g_conv, w_out, g_post, tm):
    m, wa = attn.shape
    wc = conv.shape[1]
    d = h.shape[1]
    assert m % tm == 0
    row = lambda a: a.reshape(1, -1)
    return pl.pallas_call(
        _mix_out_kernel,
        grid=(m // tm,),
        in_specs=[pl.BlockSpec((tm, wa), lambda i: (i, 0)),
                  pl.BlockSpec((tm, wc), lambda i: (i, 0)),
                  pl.BlockSpec((tm, d), lambda i: (i, 0)),
                  pl.BlockSpec((1, wa), lambda i: (0, 0)),
                  pl.BlockSpec((1, wc), lambda i: (0, 0)),
                  pl.BlockSpec((wa + wc, d), lambda i: (0, 0)),
                  pl.BlockSpec((1, d), lambda i: (0, 0))],
        out_specs=pl.BlockSpec((tm, d), lambda i: (i, 0)),
        out_shape=jax.ShapeDtypeStruct((m, d), F32),
        compiler_params=_params("arbitrary"),
        name="mix_out",
    )(attn, conv, h, row(g_attn), row(g_conv), w_out, row(g_post))


def _xattn_kernel(h_ref, mk_ref, mv_ref, gpre_ref, wq_ref, wo_ref, gpost_ref, o_ref):
    h = h_ref[...]
    xn = _rms(h, gpre_ref[...]).astype(BF16)
    xq = jnp.dot(xn, wq_ref[...], preferred_element_type=F32)
    scale = X_HEAD_DIM ** -0.5
    outs = []
    for hh in range(X_HEADS):
        cols = slice(hh * X_HEAD_DIM, (hh + 1) * X_HEAD_DIM)
        s = lax.dot_general(xq[:, cols].astype(BF16), mk_ref[0, :, cols].astype(BF16), _NT,
                            preferred_element_type=F32) * scale
        p = jnp.exp(s - jnp.max(s, axis=1, keepdims=True))
        l = jnp.sum(p, axis=1, keepdims=True)
        outs.append(jnp.dot(p.astype(BF16), mv_ref[0, :, cols].astype(BF16), preferred_element_type=F32) / l)
    o = jnp.concatenate(outs, axis=1).astype(BF16)
    y = jnp.dot(o, wo_ref[...], preferred_element_type=F32)
    o_ref[...] = h + _rms(y, gpost_ref[...])


def _xattn(h, mem_k, mem_v, g_pre, w_xq, w_xo, g_post, rows_per_batch, tm):
    m, d = h.shape
    assert rows_per_batch % tm == 0
    tiles = rows_per_batch // tm
    mem_len, xw = mem_k.shape[1:]
    row = lambda a: a.reshape(1, -1)
    return pl.pallas_call(
        _xattn_kernel,
        grid=(m // tm,),
        in_specs=[pl.BlockSpec((tm, d), lambda i: (i, 0)),
                  pl.BlockSpec((1, mem_len, xw), lambda i: (i // tiles, 0, 0)),
                  pl.BlockSpec((1, mem_len, xw), lambda i: (i // tiles, 0, 0)),
                  pl.BlockSpec((1, d), lambda i: (0, 0)),
                  pl.BlockSpec((d, xw), lambda i: (0, 0)),
                  pl.BlockSpec((xw, d), lambda i: (0, 0)),
                  pl.BlockSpec((1, d), lambda i: (0, 0))],
        out_specs=pl.BlockSpec((tm, d), lambda i: (i, 0)),
        out_shape=jax.ShapeDtypeStruct((m, d), F32),
        compiler_params=_params("arbitrary"),
        name="xattn",
    )(h, mem_k, mem_v, row(g_pre), w_xq, w_xo, row(g_post))


def _ffn_kernel(h_ref, gpre_ref, wg_ref, wu_ref, wo_ref, gpost_ref, o_ref, xn_ref, acc_ref):
    f = pl.program_id(1)

    @pl.when(f == 0)
    def _():
        xn_ref[...] = _rms(h_ref[...], gpre_ref[...]).astype(BF16)
        acc_ref[...] = jnp.zeros(acc_ref.shape, F32)

    xn = xn_ref[...]
    g = jnp.dot(xn, wg_ref[...], preferred_element_type=F32)
    u = jnp.dot(xn, wu_ref[...], preferred_element_type=F32)
    a = (g * jax.nn.sigmoid(g) * u).astype(BF16)
    acc_ref[...] += jnp.dot(a, wo_ref[...], preferred_element_type=F32)

    @pl.when(f == pl.num_programs(1) - 1)
    def _():
        o_ref[...] = h_ref[...] + _rms(acc_ref[...], gpost_ref[...])


def _ffn(h, g_pre, w_in, w_out, g_post, tm, tf):
    m, d = h.shape
    hidden = w_out.shape[0]
    assert m % tm == 0 and hidden % tf == 0 and w_in.shape[1] == 2 * hidden
    nf = hidden // tf
    row = lambda a: a.reshape(1, -1)
    return pl.pallas_call(
        _ffn_kernel,
        grid=(m // tm, nf),
        in_specs=[pl.BlockSpec((tm, d), lambda i, f: (i, 0)),
                  pl.BlockSpec((1, d), lambda i, f: (0, 0)),
                  pl.BlockSpec((d, tf), lambda i, f: (0, f)),
                  pl.BlockSpec((d, tf), lambda i, f: (0, f + nf)),
                  pl.BlockSpec((tf, d), lambda i, f: (f, 0)),
                  pl.BlockSpec((1, d), lambda i, f: (0, 0))],
        out_specs=pl.BlockSpec((tm, d), lambda i, f: (i, 0)),
        out_shape=jax.ShapeDtypeStruct((m, d), F32),
        scratch_shapes=[pltpu.VMEM((tm, d), BF16), pltpu.VMEM((tm, d), F32)],
        compiler_params=_params("arbitrary", "arbitrary"),
        name="ffn",
    )(h, row(g_pre), w_in, w_in, w_out, row(g_post))


def _row_tile(m):
    return 512 if m % 512 == 0 else m


def _layer(x, attn_fn, conv_prev, mem_k, mem_v, wts, conv_tile, x_tile):
    b, t, d = x.shape
    m = b * t
    tm = _row_tile(m)
    x2 = x.reshape(m, d)
    ch = wts["w_dw"].shape[1]
    q, k_new, v_new, u = _norm_matmul(x2, wts["g_pre_mix"].reshape(1, d), wts["w_in"],
                                      (ATTN_WIDTH, ATTN_WIDTH, ATTN_WIDTH, 2 * ch), tn=512,
                                      tm=1024 if m % 1024 == 0 else tm)
    q, k_new, v_new = (a.reshape(b, t, ATTN_WIDTH) for a in (q, k_new, v_new))
    attn = attn_fn(q, k_new, v_new)
    conv, conv_state = _conformer_conv(u.reshape(b, t, 2 * ch), conv_prev, wts["w_dw"], wts["b_dw"],
                                       wts["g_ln_conv"], wts["b_ln_conv"], conv_tile)
    h = _mix_out(attn.reshape(m, ATTN_WIDTH), conv.reshape(m, ch), x2, wts["g_attn_grp"], wts["g_conv_grp"],
                 wts["w_out"], wts["g_post_mix"], tm=tm)
    h = _xattn(h, mem_k, mem_v, wts["g_pre_x"], wts["w_xq"], wts["w_xo"], wts["g_post_x"],
               rows_per_batch=t, tm=x_tile)
    h = _ffn(h, wts["g_pre_ffn"], wts["w_ffn_in"], wts["w_ffn_out"], wts["g_post_ffn"], tm=tm, tf=512)
    shape4 = (b, t, N_HEADS, HEAD_DIM)
    return h.reshape(b, t, d), k_new.reshape(shape4), v_new.reshape(shape4), conv_state


def kernel(x_prompt, x_sample, cache_k, cache_v, state_conv, cache_mem_k, cache_mem_v, page_table,
           mem_prompt, g_mem, w_mem_k, w_mem_v, g_pre_mix, w_in, w_dw, b_dw, g_ln_conv, b_ln_conv,
           g_attn_grp, g_conv_grp, w_out, g_post_mix, g_pre_x, w_xq, w_xo, g_post_x,
           g_pre_ffn, w_ffn_in, w_ffn_out, g_post_ffn):
    depth = w_in.shape[0]
    bp, tp, d = x_prompt.shape
    bs, ts, _ = x_sample.shape
    mem_len = mem_prompt.shape[1]
    xw = X_HEADS * X_HEAD_DIM
    ch = w_dw.shape[2]
    hp, hs = x_prompt, x_sample
    outs = [[] for _ in range(8)]
    for l in range(depth):
        wts = dict(g_pre_mix=g_pre_mix[l], w_in=w_in[l].astype(BF16), w_dw=w_dw[l], b_dw=b_dw[l],
                   g_ln_conv=g_ln_conv[l], b_ln_conv=b_ln_conv[l], g_attn_grp=g_attn_grp[l],
                   g_conv_grp=g_conv_grp[l], w_out=w_out[l].astype(BF16), g_post_mix=g_post_mix[l],
                   g_pre_x=g_pre_x[l], w_xq=w_xq[l].astype(BF16), w_xo=w_xo[l].astype(BF16),
                   g_post_x=g_post_x[l], g_pre_ffn=g_pre_ffn[l], w_ffn_in=w_ffn_in[l].astype(BF16),
                   w_ffn_out=w_ffn_out[l].astype(BF16), g_post_ffn=g_post_ffn[l])
        w_mem = jnp.concatenate([w_mem_k[l], w_mem_v[l]], axis=1).astype(BF16)
        mk_p, mv_p = _norm_matmul(mem_prompt.reshape(bp * mem_len, d), g_mem[l].reshape(1, d), w_mem,
                                  (xw, xw), tn=xw, tm=_row_tile(bp * mem_len))
        mk_p, mv_p = mk_p.reshape(bp, mem_len, xw), mv_p.reshape(bp, mem_len, xw)
        conv0 = jnp.zeros((bp, CONV_WIDTH - 1, ch), F32)
        hp, kp, vp, cp = _layer(hp, _moba_prompt, conv0, mk_p, mv_p, wts,
                                conv_tile=_row_tile(tp), x_tile=_row_tile(tp))

        def sample_attn(q, k_new, v_new, l=l):
            scores, sel = _sample_scores(q, cache_k, page_table, l)
            return _sample_attend(scores, sel, q, k_new, v_new, cache_v, page_table, l)

        hs, ks, vs, cs = _layer(hs, sample_attn, state_conv[l], cache_mem_k[l].reshape(bs, mem_len, xw),
                                cache_mem_v[l].reshape(bs, mem_len, xw), wts, conv_tile=ts, x_tile=ts)
        mem_shape = (bp, mem_len, X_HEADS, X_HEAD_DIM)
        for lst, a in zip(outs, (kp, vp, cp, mk_p.reshape(mem_shape), mv_p.reshape(mem_shape), ks, vs, cs)):
            lst.append(a)
    return (hp, hs) + tuple(jnp.stack(lst, 0) for lst in outs)
```

```python
import functools

import jax
import jax.numpy as jnp
from jax import lax
from jax.experimental import pallas as pl
from jax.experimental.pallas import tpu as pltpu

EPS = 1e-6
N_HEADS = 8
HEAD_DIM = 128
ATTN_WIDTH = N_HEADS * HEAD_DIM
CONV_WIDTH = 31
MOBA_BLOCK = 256
MOBA_TOPK = 3
X_HEADS = 4
X_HEAD_DIM = 128

CONV_HALO = 32
DMA_PRIORITIES = 2
VMEM_LIMIT = 56 * 1024 * 1024

F32 = jnp.float32
BF16 = jnp.bfloat16
NEG_INF = float("-inf")
LOG2_E = 1.4426950408889634
_NT = (((1,), (1,)), ((), ()))


def _params(*sem):
    return pltpu.CompilerParams(dimension_semantics=sem, vmem_limit_bytes=VMEM_LIMIT)


def _rms(x, g):
    return x * lax.rsqrt(jnp.mean(x * x, axis=-1, keepdims=True) + EPS) * g


def _top_blocks(gate, n_valid, axis):
    nb = gate.shape[axis]
    pos = lax.broadcasted_iota(jnp.int32, gate.shape, axis)
    g = jnp.where(pos < n_valid, gate, NEG_INF)
    idxs, oks = [], []
    for _ in range(MOBA_TOPK):
        m = jnp.max(g, axis=axis, keepdims=True)
        idx = jnp.min(jnp.where(g == m, pos, nb), axis=axis, keepdims=True)
        ok = m > NEG_INF
        idxs.append(idx)
        oks.append(ok)
        g = jnp.where((pos == idx) & ok, NEG_INF, g)
    return idxs, oks


def _norm_matmul_kernel(x_ref, g_ref, w_ref, *rest, tile_ranges):
    out_refs, xn_ref = rest[:-1], rest[-1]
    j = pl.program_id(1)

    @pl.when(j == 0)
    def _():
        xn_ref[...] = _rms(x_ref[...], g_ref[...]).astype(BF16)

    for o_ref, (lo, hi) in zip(out_refs, tile_ranges):
        @pl.when((j >= lo) & (j < hi))
        def _(o_ref=o_ref):
            o_ref[...] = jnp.dot(xn_ref[...], w_ref[...], preferred_element_type=F32)


def _norm_matmul(x, g, w, widths, tn, tm):
    m, d = x.shape
    n = w.shape[1]
    assert sum(widths) == n and all(wd % tn == 0 for wd in widths) and m % tm == 0
    tile_ranges, lo = [], 0
    for wd in widths:
        tile_ranges.append((lo, lo + wd // tn))
        lo += wd // tn

    def out_map(i, j, lo, cnt):
        return (i, jnp.clip(j - lo, 0, cnt - 1))

    out_specs = [pl.BlockSpec((tm, tn), functools.partial(out_map, lo=lo, cnt=hi - lo))
                 for lo, hi in tile_ranges]
    return pl.pallas_call(
        functools.partial(_norm_matmul_kernel, tile_ranges=tuple(tile_ranges)),
        grid=(m // tm, n // tn),
        in_specs=[pl.BlockSpec((tm, d), lambda i, j: (i, 0)),
                  pl.BlockSpec((1, d), lambda i, j: (0, 0)),
                  pl.BlockSpec((d, tn), lambda i, j: (0, j))],
        out_specs=out_specs,
        out_shape=[jax.ShapeDtypeStruct((m, wd), F32) for wd in widths],
        scratch_shapes=[pltpu.VMEM((tm, d), BF16)],
        compiler_params=_params("arbitrary", "arbitrary"),
        name="norm_matmul",
    )(x, g, w)


def _in_proj_src_tile(j):
    return jnp.where(j < 3, (j + 1) % 3, j)


def _in_proj_kernel(x_ref, g_ref, w_ref, *rest, emit):
    if emit:
        k_ref, v_ref, qu_ref, wt_ref, xn_ref = rest
    else:
        k_ref, v_ref, qu_ref, xn_ref = rest
    j = pl.program_id(1)

    @pl.when(j == 0)
    def _():
        xn_ref[...] = _rms(x_ref[...], g_ref[...]).astype(BF16)

    if emit:
        wt_ref[...] = w_ref[...].astype(BF16)
        w_ref = wt_ref

    for o_ref, cond in ((k_ref, j == 0), (v_ref, j == 1), (qu_ref, j >= 2)):
        @pl.when(cond)
        def _(o_ref=o_ref):
            o_ref[...] = jnp.dot(xn_ref[...], w_ref[...], preferred_element_type=F32)


def _in_proj(x, g, w, emit, tm):
    m, d = x.shape
    tn = ATTN_WIDTH
    nt = (w.shape[1] // tn) if emit else w.shape[0]
    assert m % tm == 0 and nt > 3 and (not emit or (m == tm and w.shape[1] % tn == 0))
    if emit:
        w_spec = pl.BlockSpec((d, tn), lambda i, j: (0, _in_proj_src_tile(j)))
    else:
        w_spec = pl.BlockSpec((None, d, tn), lambda i, j: (j, 0, 0))
    out_specs = [pl.BlockSpec((tm, tn), lambda i, j: (i, 0)),
                 pl.BlockSpec((tm, tn), lambda i, j: (i, 0)),
                 pl.BlockSpec((None, tm, tn), lambda i, j: (jnp.maximum(j - 2, 0), i, 0))]
    out_shape = [jax.ShapeDtypeStruct((m, tn), F32), jax.ShapeDtypeStruct((m, tn), F32),
                 jax.ShapeDtypeStruct((nt - 2, m, tn), F32)]
    if emit:
        out_specs.append(pl.BlockSpec((None, d, tn), lambda i, j: (j, 0, 0)))
        out_shape.append(jax.ShapeDtypeStruct((nt, d, tn), BF16))
    return pl.pallas_call(
        functools.partial(_in_proj_kernel, emit=emit),
        grid=(m // tm, nt),
        in_specs=[pl.BlockSpec((tm, d), lambda i, j: (i, 0)),
                  pl.BlockSpec((1, d), lambda i, j: (0, 0)),
                  w_spec],
        out_specs=out_specs,
        out_shape=out_shape,
        scratch_shapes=[pltpu.VMEM((tm, d), BF16)],
        compiler_params=_params("arbitrary", "arbitrary"),
        name="in_proj",
    )(x, g, w)


def _moba_prompt_kernel(q_ref, k_ref, v_ref, o_ref, kb_ref, vt_ref, km_ref, *, nb, tps):
    blk = MOBA_BLOCK
    h = pl.program_id(1)
    g = pl.program_id(2)

    @pl.when(g == 0)
    def _():
        for n in range(nb):
            kf = k_ref[0, n * blk:(n + 1) * blk, :]
            kb_ref[n * blk:(n + 1) * blk, :] = kf.astype(BF16)
            vt_ref[n] = v_ref[0, n * blk:(n + 1) * blk, :].T.astype(BF16)
            km_ref[n:n + 1, :] = jnp.mean(kf, axis=0, keepdims=True)

    scale = HEAD_DIM ** -0.5 * LOG2_E
    slope = jnp.exp2(jnp.broadcast_to(-8.0 * (h + 1).astype(F32) / N_HEADS, (1, blk))) * LOG2_E

    def attend(i, qi):
        q = q_ref[0, qi * blk:(qi + 1) * blk, :]
        qb = q.astype(BF16)
        kr = lax.broadcasted_iota(jnp.int32, (blk, blk), 0)
        qc = lax.broadcasted_iota(jnp.int32, (blk, blk), 1)
        base = slope * (qc - kr).astype(F32)
        s_own = lax.dot_general(kb_ref[i * blk:(i + 1) * blk, :], qb, _NT, preferred_element_type=F32)
        ts = [jnp.where(kr <= qc, s_own * scale - base, NEG_INF)]
        if i:
            gate = lax.dot_general(km_ref[0:i, :], q, _NT, precision=lax.Precision.HIGHEST,
                                   preferred_element_type=F32)
            idxs, oks = _top_blocks(gate, i, axis=0)
            row = lax.broadcasted_iota(jnp.int32, (i, blk), 0)
            sel = jnp.zeros((i, blk), F32)
            for idx, ok in zip(idxs, oks):
                sel = jnp.where((row == idx) & ok, 1.0, sel)
            for n in range(i):
                s = lax.dot_general(kb_ref[n * blk:(n + 1) * blk, :], qb, _NT, preferred_element_type=F32)
                neg = jnp.where(sel[n:n + 1, :] > 0.0, -slope * float((i - n) * blk), NEG_INF)
                ts.append(s * scale - base + neg)
        m = jnp.max(functools.reduce(jnp.maximum, ts), axis=0, keepdims=True)
        acc = None
        psum = None
        for n, t in enumerate(ts):
            p = jnp.exp2(t - m)
            psum = p if psum is None else psum + p
            vt = vt_ref[i] if n == 0 else vt_ref[n - 1]
            pv = jnp.dot(vt, p.astype(BF16), preferred_element_type=F32)
            acc = pv if acc is None else acc + pv
        l = jnp.sum(psum, axis=0, keepdims=True)
        o_ref[0, qi * blk:(qi + 1) * blk, :] = (acc / l).T

    for v in range(nb // tps):
        @pl.when(g == v)
        def _(v=v):
            for qi in range(tps):
                attend(v * tps + qi, qi)


def _moba_prompt(qu, k, v, tps=2):
    b, t, _ = k.shape
    blk = MOBA_BLOCK
    assert t % (blk * tps) == 0
    nb = t // blk
    return pl.pallas_call(
        functools.partial(_moba_prompt_kernel, nb=nb, tps=tps),
        grid=(b, N_HEADS, nb // tps),
        in_specs=[pl.BlockSpec((None, 1, blk * tps, HEAD_DIM), lambda bi, h, g: (0, bi, g, h)),
                  pl.BlockSpec((1, t, HEAD_DIM), lambda bi, h, g: (bi, 0, h)),
                  pl.BlockSpec((1, t, HEAD_DIM), lambda bi, h, g: (bi, 0, h))],
        out_specs=pl.BlockSpec((1, blk * tps, HEAD_DIM), lambda bi, h, g: (bi, g, h)),
        out_shape=jax.ShapeDtypeStruct((b, t, ATTN_WIDTH), F32),
        scratch_shapes=[pltpu.VMEM((t, HEAD_DIM), BF16), pltpu.VMEM((nb, HEAD_DIM, blk), BF16),
                        pltpu.VMEM((nb, HEAD_DIM), F32)],
        compiler_params=_params("arbitrary", "arbitrary", "arbitrary"),
        name="moba_prompt",
    )(qu, k, v)


def _head_page_copy(cache_ref, pt_ref, buf_ref, sem, layer, b, h, page_slot, dst_row, n_pages):
    page = pt_ref[b * n_pages + page_slot]
    ps = cache_ref.shape[2]
    return pltpu.make_async_copy(cache_ref.at[layer, page, :, h, :], buf_ref.at[pl.ds(dst_row, ps), :], sem)


def _sample_scores_kernel(pt_ref, q_ref, kc_ref, s_ref, sel_ref, kbuf_ref, km_ref, gate_ref, sem_ref,
                          *, layer, n_pages, chunk):
    b = pl.program_id(0)
    h = pl.program_id(1)
    nbt, nh = pl.num_programs(0), pl.num_programs(1)
    step = b * nh + h
    slot = step % 2
    ps = kc_ref.shape[2]
    past = n_pages * ps
    blk = MOBA_BLOCK
    t = q_ref.shape[1]

    def start_all(st, sl):
        bb, hh = st // nh, st % nh

        def issue(pair, carry):
            for k in range(DMA_PRIORITIES):
                p = pair * DMA_PRIORITIES + k
                _head_page_copy(kc_ref, pt_ref, kbuf_ref.at[sl], sem_ref.at[sl], layer, bb, hh,
                                p, pl.multiple_of(p * ps, ps), n_pages).start(priority=k)
            return carry

        lax.fori_loop(0, n_pages // DMA_PRIORITIES, issue, 0, unroll=4)

    @pl.when(step == 0)
    def _():
        start_all(step, slot)

    @pl.when(step + 1 < nbt * nh)
    def _():
        start_all(step + 1, 1 - slot)

    pltpu.make_async_copy(kbuf_ref.at[1 - slot], kbuf_ref.at[slot], sem_ref.at[slot]).wait()

    q = q_ref[0]
    qb = q.astype(BF16)
    bpc = chunk // blk

    def body(ci, carry):
        start = pl.multiple_of(ci * chunk, chunk)
        kf = kbuf_ref[slot, pl.ds(start, chunk), :]
        sc = lax.dot_general(qb, kf.astype(BF16), _NT, preferred_element_type=F32)
        for jj in range(bpc):
            s_ref[0, 0, ci * bpc + jj] = sc[:, jj * blk:(jj + 1) * blk]
        km_ref[pl.ds(pl.multiple_of(ci * bpc, bpc), bpc), :] = jnp.mean(
            kf.reshape(bpc, blk, HEAD_DIM), axis=1)
        return carry

    lax.fori_loop(0, past // chunk, body, 0, unroll=2)

    gate_ref[h] = lax.dot_general(q, km_ref[...], _NT, precision=lax.Precision.HIGHEST,
                                  preferred_element_type=F32)

    @pl.when(h == nh - 1)
    def _():
        gate = gate_ref[...].reshape(nh * t, past // blk)
        idxs, _ = _top_blocks(gate, past // blk, axis=1)
        lane = lax.broadcasted_iota(jnp.int32, (nh * t, 128), 1)
        out = jnp.zeros((nh * t, 128), jnp.int32)
        for k, idx in enumerate(idxs):
            out = jnp.where(lane == k, idx, out)
        sel_ref[0] = out.reshape(nh, t, 128)


def _sample_scores(qu, cache_k, page_table, layer):
    _, b, t, _ = qu.shape
    n_pages = page_table.shape[1]
    ps = cache_k.shape[2]
    past = n_pages * ps
    chunk = 8 * MOBA_BLOCK
    assert past % chunk == 0 and past // MOBA_BLOCK >= MOBA_TOPK and n_pages % DMA_PRIORITIES == 0
    grid_spec = pltpu.PrefetchScalarGridSpec(
        num_scalar_prefetch=1,
        grid=(b, N_HEADS),
        in_specs=[pl.BlockSpec((None, 1, t, HEAD_DIM), lambda bi, h, pt: (0, bi, 0, h)),
                  pl.BlockSpec(memory_space=pl.ANY)],
        out_specs=[pl.BlockSpec((1, 1, past // MOBA_BLOCK, t, MOBA_BLOCK), lambda bi, h, pt: (bi, h, 0, 0, 0)),
                   pl.BlockSpec((1, N_HEADS, t, 128), lambda bi, h, pt: (bi, 0, 0, 0))],
        scratch_shapes=[pltpu.VMEM((2, past, HEAD_DIM), F32),
                        pltpu.VMEM((past // MOBA_BLOCK, HEAD_DIM), F32),
                        pltpu.VMEM((N_HEADS, t, past // MOBA_BLOCK), F32),
                        pltpu.SemaphoreType.DMA((2,))],
    )
    return pl.pallas_call(
        functools.partial(_sample_scores_kernel, layer=layer, n_pages=n_pages, chunk=chunk),
        grid_spec=grid_spec,
        out_shape=[jax.ShapeDtypeStruct((b, N_HEADS, past // MOBA_BLOCK, t, MOBA_BLOCK), F32),
                   jax.ShapeDtypeStruct((b, N_HEADS, t, 128), jnp.int32)],
        compiler_params=_params("arbitrary", "arbitrary"),
        name="sample_scores",
    )(page_table.reshape(-1), qu, cache_k)


def _sample_attend_kernel(pt_ref, sel_ref, s_ref, q_ref, kn_ref, vn_ref, vc_ref, o_ref,
                          vbuf_ref, ssel_ref, sem_ref, *, layer, n_pages, q_start):
    b = pl.program_id(0)
    h = pl.program_id(1)
    nbt, nh = pl.num_programs(0), pl.num_programs(1)
    step = b * nh + h
    slot = step % 2
    t = q_ref.shape[1]
    ps = vc_ref.shape[2]
    blk = MOBA_BLOCK
    ppb = blk // ps
    nsel = t * MOBA_TOPK

    def sel_block(st, e):
        return sel_ref[st * nsel + e]

    def copies(st, sl):
        bb, hh = st // nh, st % nh

        def mk(e, pg):
            return _head_page_copy(vc_ref, pt_ref, vbuf_ref.at[sl], sem_ref.at[sl], layer, bb, hh,
                                   sel_block(st, e) * ppb + pg, e * blk + pg * ps, n_pages)
        return mk

    def start_all(st, sl):
        mk = copies(st, sl)
        for e in range(nsel):
            for pg in range(ppb):
                mk(e, pg).start(priority=(e * ppb + pg) % DMA_PRIORITIES)

    @pl.when(step == 0)
    def _():
        start_all(step, slot)

    @pl.when(step + 1 < nbt * nh)
    def _():
        start_all(step + 1, 1 - slot)

    scale = HEAD_DIM ** -0.5
    slope = jnp.exp2(jnp.broadcast_to(-8.0 * (h + 1).astype(F32) / N_HEADS, (1, 1)))
    lane = lax.broadcasted_iota(jnp.int32, (1, blk), 1)

    ssel_ref[...] = jnp.full(ssel_ref.shape, NEG_INF, F32)
    for e in range(nsel):
        ti = e // MOBA_TOPK
        n = sel_block(step, e)
        raw = s_ref[0, 0, n, ti:ti + 1, :]
        dist = (q_start + ti - n * blk - lane).astype(F32)
        ssel_ref[ti:ti + 1, e * blk:(e + 1) * blk] = raw * scale - slope * dist

    q = q_ref[0].astype(BF16)
    r = lax.broadcasted_iota(jnp.int32, (t, t), 0)
    c = lax.broadcasted_iota(jnp.int32, (t, t), 1)
    s_own = lax.dot_general(q, kn_ref[0].astype(BF16), _NT, preferred_element_type=F32) * scale
    s_own = jnp.where(c <= r, s_own - slope * (r - c).astype(F32), NEG_INF)

    s_sel = ssel_ref[...]
    m = jnp.maximum(jnp.max(s_sel, axis=1, keepdims=True), jnp.max(s_own, axis=1, keepdims=True))
    p_sel = jnp.exp(s_sel - m)
    p_own = jnp.exp(s_own - m)
    l = jnp.sum(p_sel, axis=1, keepdims=True) + jnp.sum(p_own, axis=1, keepdims=True)

    mk = copies(step, slot)
    for e in range(nsel):
        for pg in range(ppb):
            mk(e, pg).wait()

    acc = jnp.dot(p_sel.astype(BF16), vbuf_ref[slot].astype(BF16), preferred_element_type=F32)
    acc = acc + jnp.dot(p_own.astype(BF16), vn_ref[0].astype(BF16), preferred_element_type=F32)
    o_ref[0] = acc / l


def _sample_attend(scores, sel, qu, k_new, v_new, cache_v, page_table, layer):
    b, t, _ = k_new.shape
    n_pages = page_table.shape[1]
    ps = cache_v.shape[2]
    past = n_pages * ps
    assert MOBA_BLOCK % ps == 0 and past % MOBA_BLOCK == 0 and t <= MOBA_BLOCK
    nsel = t * MOBA_TOPK
    grid_spec = pltpu.PrefetchScalarGridSpec(
        num_scalar_prefetch=2,
        grid=(b, N_HEADS),
        in_specs=[pl.BlockSpec((1, 1, past // MOBA_BLOCK, t, MOBA_BLOCK), lambda bi, h, pt, sl: (bi, h, 0, 0, 0)),
                  pl.BlockSpec((None, 1, t, HEAD_DIM), lambda bi, h, pt, sl: (0, bi, 0, h)),
                  pl.BlockSpec((1, t, HEAD_DIM), lambda bi, h, pt, sl: (bi, 0, h)),
                  pl.BlockSpec((1, t, HEAD_DIM), lambda bi, h, pt, sl: (bi, 0, h)),
                  pl.BlockSpec(memory_space=pl.ANY)],
        out_specs=pl.BlockSpec((1, t, HEAD_DIM), lambda bi, h, pt, sl: (bi, 0, h)),
        scratch_shapes=[pltpu.VMEM((2, nsel * MOBA_BLOCK, HEAD_DIM), F32),
                        pltpu.VMEM((t, nsel * MOBA_BLOCK), F32),
                        pltpu.SemaphoreType.DMA((2,))],
    )
    return pl.pallas_call(
        functools.partial(_sample_attend_kernel, layer=layer, n_pages=n_pages, q_start=past),
        grid_spec=grid_spec,
        out_shape=jax.ShapeDtypeStruct((b, t, ATTN_WIDTH), F32),
        compiler_params=_params("arbitrary", "arbitrary"),
        name="sample_attend",
    )(page_table.reshape(-1), sel[..., :MOBA_TOPK].reshape(-1), scores, qu, k_new, v_new, cache_v)


def _conv_taps():
    pad = CONV_HALO - (CONV_WIDTH - 1)
    return [(ph, [(j, (pad + j) // 8) for j in range(CONV_WIDTH) if (pad + j) % 8 == ph]) for ph in range(8)]


def _conv_kernel(ua_ref, ug_ref, prev_ref, w_ref, b_ref, g_ref, bl_ref, y_ref, st_ref, buf_ref, par_ref, z_ref,
                 *, tt, rows, nrows, lanes):
    ti = pl.program_id(1)
    ch = y_ref.shape[2]
    hist = CONV_WIDTH - 1
    pad = CONV_HALO - hist
    i_bias, i_gain, i_beta = CONV_WIDTH, CONV_WIDTH + 1, CONV_WIDTH + 2

    @pl.when(ti == 0)
    def _():
        buf_ref[0:CONV_HALO, :] = jnp.zeros((CONV_HALO, ch), F32)
        buf_ref[pad:CONV_HALO, :] = prev_ref[0]
        for j in range(CONV_WIDTH):
            par_ref[j] = jnp.broadcast_to(w_ref[j:j + 1, :], (8, ch))
        par_ref[i_bias] = jnp.broadcast_to(b_ref[...], (8, ch))
        par_ref[i_gain] = jnp.broadcast_to(g_ref[...], (8, ch))
        par_ref[i_beta] = jnp.broadcast_to(bl_ref[...], (8, ch))

    @pl.when(ti > 0)
    def _():
        buf_ref[0:CONV_HALO, :] = buf_ref[tt:tt + CONV_HALO, :]

    def glu(ci, carry):
        r0 = pl.multiple_of(ci * rows, rows)
        buf_ref[pl.ds(CONV_HALO + r0, rows), :] = (ua_ref[0, pl.ds(r0, rows), :]
                                                   * jax.nn.sigmoid(ug_ref[0, pl.ds(r0, rows), :]))
        return carry

    lax.fori_loop(0, tt // rows, glu, 0)

    def conv(ci, carry):
        r0 = pl.multiple_of(ci * rows, rows)
        for c0 in range(0, ch, lanes):
            cs = slice(c0, c0 + lanes)
            win = buf_ref.at[pl.ds(r0, rows + CONV_HALO), pl.ds(c0, lanes)]
            acc = jnp.broadcast_to(par_ref[i_bias, :, cs][None], (rows // 8, 8, lanes))
            for ph, taps in _conv_taps():
                a_lo, a_hi = taps[0][1], taps[-1][1]
                x = win[8 * a_lo + ph:8 * a_hi + ph + rows, :]
                for j, a in taps:
                    xs = x[8 * (a - a_lo):8 * (a - a_lo) + rows, :].reshape(rows // 8, 8, lanes)
                    acc = acc + par_ref[j, :, cs][None] * xs
            z_ref[pl.ds(r0, rows), cs] = acc.reshape(rows, lanes)
        return carry

    lax.fori_loop(0, tt // rows, conv, 0)

    def norm(ci, carry):
        r0 = pl.multiple_of(ci * nrows, nrows)
        z = z_ref[pl.ds(r0, nrows), :]
        xc = z - jnp.mean(z, axis=-1, keepdims=True)
        xn = (xc * lax.rsqrt(jnp.mean(xc * xc, axis=-1, keepdims=True) + EPS)).reshape(nrows // 8, 8, ch)
        y = (xn * par_ref[i_gain][None] + par_ref[i_beta][None]).reshape(nrows, ch)
        y_ref[0, pl.ds(r0, nrows), :] = y * jax.nn.sigmoid(y)
        return carry

    lax.fori_loop(0, tt // nrows, norm, 0, unroll=min(8, tt // nrows))

    @pl.when(ti == pl.num_programs(1) - 1)
    def _():
        st_ref[0] = buf_ref[tt + pad:tt + CONV_HALO, :]


def _conformer_conv(qu, prev, w_dw, b_dw, g_ln, b_ln, tt):
    _, b, t, ch = qu.shape
    assert ch == w_dw.shape[1]
    hist = CONV_WIDTH - 1
    assert t % tt == 0 and tt % 8 == 0
    rows = 32 if tt % 32 == 0 else 8
    nrows = 16 if tt % 16 == 0 else 8
    lanes = 512 if ch % 512 == 0 else ch
    row = lambda a: a.reshape(1, ch)
    return pl.pallas_call(
        functools.partial(_conv_kernel, tt=tt, rows=rows, nrows=nrows, lanes=lanes),
        grid=(b, t // tt),
        in_specs=[pl.BlockSpec((None, 1, tt, ch), lambda bi, ti: (1, bi, ti, 0)),
                  pl.BlockSpec((None, 1, tt, ch), lambda bi, ti: (2, bi, ti, 0)),
                  pl.BlockSpec((1, hist, ch), lambda bi, ti: (bi, 0, 0)),
                  pl.BlockSpec((CONV_WIDTH, ch), lambda bi, ti: (0, 0)),
                  pl.BlockSpec((1, ch), lambda bi, ti: (0, 0)),
                  pl.BlockSpec((1, ch), lambda bi, ti: (0, 0)),
                  pl.BlockSpec((1, ch), lambda bi, ti: (0, 0))],
        out_specs=[pl.BlockSpec((1, tt, ch), lambda bi, ti: (bi, ti, 0)),
                   pl.BlockSpec((1, hist, ch), lambda bi, ti: (bi, 0, 0))],
        out_shape=[jax.ShapeDtypeStruct((b, t, ch), F32), jax.ShapeDtypeStruct((b, hist, ch), F32)],
        scratch_shapes=[pltpu.VMEM((tt + CONV_HALO, ch), F32), pltpu.VMEM((CONV_WIDTH + 3, 8, ch), F32),
                        pltpu.VMEM((tt, ch), F32)],
        compiler_params=_params("arbitrary", "arbitrary"),
        name="conformer_conv",
    )(qu, qu, prev, w_dw, row(b_dw), row(g_ln), row(b_ln))


def _mix_out_kernel(attn_ref, conv_ref, h_ref, ga_ref, gc_ref, w_ref, gp_ref, o_ref):
    wa = attn_ref.shape[1]
    a = _rms(attn_ref[...], ga_ref[...]).astype(BF16)
    cv = _rms(conv_ref[...], gc_ref[...]).astype(BF16)
    mixed = (jnp.dot(a, w_ref[0:wa, :], preferred_element_type=F32)
             + jnp.dot(cv, w_ref[wa:, :], preferred_element_type=F32))
    o_ref[...] = h_ref[...] + _rms(mixed, gp_ref[...])


def _mix_out(attn, conv, h, g_attn, g_conv, w_out, g_post, tm):
    m, wa = attn.shape
    wc = conv.shape[1]
    d = h.shape[1]
    assert m % tm == 0
    row = lambda a: a.reshape(1, -1)
    return pl.pallas_call(
        _mix_out_kernel,
        grid=(m // tm,),
        in_specs=[pl.BlockSpec((tm, wa), lambda i: (i, 0)),
                  pl.BlockSpec((tm, wc), lambda i: (i, 0)),
                  pl.BlockSpec((tm, d), lambda i: (i, 0)),
                  pl.BlockSpec((1, wa), lambda i: (0, 0)),
                  pl.BlockSpec((1, wc), lambda i: (0, 0)),
                  pl.BlockSpec((wa + wc, d), lambda i: (0, 0)),
                  pl.BlockSpec((1, d), lambda i: (0, 0))],
        out_specs=pl.BlockSpec((tm, d), lambda i: (i, 0)),
        out_shape=jax.ShapeDtypeStruct((m, d), F32),
        compiler_params=_params("arbitrary"),
        name="mix_out",
    )(attn, conv, h, row(g_attn), row(g_conv), w_out, row(g_post))


def _xattn_kernel(h_ref, mk_ref, mv_ref, gpre_ref, wq_ref, wo_ref, gpost_ref, o_ref):
    h = h_ref[...]
    xn = _rms(h, gpre_ref[...]).astype(BF16)
    xq = jnp.dot(xn, wq_ref[...], preferred_element_type=F32)
    scale = X_HEAD_DIM ** -0.5
    outs = []
    for hh in range(X_HEADS):
        cols = slice(hh * X_HEAD_DIM, (hh + 1) * X_HEAD_DIM)
        s = lax.dot_general(xq[:, cols].astype(BF16), mk_ref[0, :, cols].astype(BF16), _NT,
                            preferred_element_type=F32) * scale
        p = jnp.exp(s - jnp.max(s, axis=1, keepdims=True))
        l = jnp.sum(p, axis=1, keepdims=True)
        outs.append(jnp.dot(p.astype(BF16), mv_ref[0, :, cols].astype(BF16), preferred_element_type=F32) / l)
    o = jnp.concatenate(outs, axis=1).astype(BF16)
    y = jnp.dot(o, wo_ref[...], preferred_element_type=F32)
    o_ref[...] = h + _rms(y, gpost_ref[...])


def _xattn(h, mem_k, mem_v, g_pre, w_xq, w_xo, g_post, rows_per_batch, tm):
    m, d = h.shape
    assert rows_per_batch % tm == 0
    tiles = rows_per_batch // tm
    mem_len, xw = mem_k.shape[1:]
    row = lambda a: a.reshape(1, -1)
    return pl.pallas_call(
        _xattn_kernel,
        grid=(m // tm,),
        in_specs=[pl.BlockSpec((tm, d), lambda i: (i, 0)),
                  pl.BlockSpec((1, mem_len, xw), lambda i: (i // tiles, 0, 0)),
                  pl.BlockSpec((1, mem_len, xw), lambda i: (i // tiles, 0, 0)),
                  pl.BlockSpec((1, d), lambda i: (0, 0)),
                  pl.BlockSpec((d, xw), lambda i: (0, 0)),
                  pl.BlockSpec((xw, d), lambda i: (0, 0)),
                  pl.BlockSpec((1, d), lambda i: (0, 0))],
        out_specs=pl.BlockSpec((tm, d), lambda i: (i, 0)),
        out_shape=jax.ShapeDtypeStruct((m, d), F32),
        compiler_params=_params("arbitrary"),
        name="xattn",
    )(h, mem_k, mem_v, row(g_pre), w_xq, w_xo, row(g_post))


def _ffn_kernel(h_ref, gpre_ref, wg_ref, wu_ref, wo_ref, gpost_ref, o_ref, *rest, emit):
    if emit:
        wg_out, wu_out, wo_out, xn_ref, acc_ref = rest
        wg_out[...] = wg_ref[...].astype(BF16)
        wu_out[...] = wu_ref[...].astype(BF16)
        wo_out[...] = wo_ref[...].astype(BF16)
        wg_ref, wu_ref, wo_ref = wg_out, wu_out, wo_out
    else:
        xn_ref, acc_ref = rest
    f = pl.program_id(1)

    @pl.when(f == 0)
    def _():
        xn_ref[...] = _rms(h_ref[...], gpre_ref[...]).astype(BF16)
        acc_ref[...] = jnp.zeros(acc_ref.shape, F32)

    xn = xn_ref[...]
    g = jnp.dot(xn, wg_ref[...], preferred_element_type=F32)
    u = jnp.dot(xn, wu_ref[...], preferred_element_type=F32)
    a = (g * jax.nn.sigmoid(g) * u).astype(BF16)
    acc_ref[...] += jnp.dot(a, wo_ref[...], preferred_element_type=F32)

    @pl.when(f == pl.num_programs(1) - 1)
    def _():
        o_ref[...] = h_ref[...] + _rms(acc_ref[...], gpost_ref[...])


def _ffn(h, g_pre, w, g_post, tm, tf, emit):
    m, d = h.shape
    row = lambda a: a.reshape(1, -1)
    if emit:
        w_in, w_out = w
        hidden = w_out.shape[0]
        nf = hidden // tf
        assert m == tm and hidden % tf == 0 and w_in.shape[1] == 2 * hidden
        w_args = (w_in, w_in, w_out)
        w_specs = [pl.BlockSpec((d, tf), lambda i, f: (0, f)),
                   pl.BlockSpec((d, tf), lambda i, f: (0, f + nf)),
                   pl.BlockSpec((tf, d), lambda i, f: (f, 0))]
    else:
        w_args = w
        nf, hidden = w[0].shape[0], w[2].shape[0]
        assert m % tm == 0 and w[0].shape == w[1].shape == (nf, d, tf) and hidden == nf * tf
        w_specs = [pl.BlockSpec((None, d, tf), lambda i, f: (f, 0, 0)),
                   pl.BlockSpec((None, d, tf), lambda i, f: (f, 0, 0)),
                   pl.BlockSpec((tf, d), lambda i, f: (f, 0))]
    out_specs = [pl.BlockSpec((tm, d), lambda i, f: (i, 0))]
    out_shape = [jax.ShapeDtypeStruct((m, d), F32)]
    if emit:
        out_specs += [pl.BlockSpec((None, d, tf), lambda i, f: (f, 0, 0)),
                      pl.BlockSpec((None, d, tf), lambda i, f: (f, 0, 0)),
                      pl.BlockSpec((tf, d), lambda i, f: (f, 0))]
        out_shape += [jax.ShapeDtypeStruct((nf, d, tf), BF16), jax.ShapeDtypeStruct((nf, d, tf), BF16),
                      jax.ShapeDtypeStruct((hidden, d), BF16)]
    outs = pl.pallas_call(
        functools.partial(_ffn_kernel, emit=emit),
        grid=(m // tm, nf),
        in_specs=[pl.BlockSpec((tm, d), lambda i, f: (i, 0)),
                  pl.BlockSpec((1, d), lambda i, f: (0, 0)),
                  *w_specs,
                  pl.BlockSpec((1, d), lambda i, f: (0, 0))],
        out_specs=out_specs,
        out_shape=out_shape,
        scratch_shapes=[pltpu.VMEM((tm, d), BF16), pltpu.VMEM((tm, d), F32)],
        compiler_params=_params("arbitrary", "arbitrary"),
        name="ffn",
    )(h, row(g_pre), *w_args, row(g_post))
    return outs[0], tuple(outs[1:])


def _row_tile(m):
    return 512 if m % 512 == 0 else m


def _layer(x, attn_fn, conv_prev, mem_k, mem_v, wts, big, emit, conv_tile, x_tile):
    b, t, d = x.shape
    m = b * t
    tm = _row_tile(m)
    x2 = x.reshape(m, d)
    ch = wts["w_dw"].shape[1]
    assert ch == ATTN_WIDTH
    w_in, w_ffn = big
    k_new, v_new, qu, *w_in_b = _in_proj(x2, wts["g_pre_mix"].reshape(1, d), w_in, emit,
                                         tm=1024 if m % 1024 == 0 else tm)
    k_new, v_new = k_new.reshape(b, t, ATTN_WIDTH), v_new.reshape(b, t, ATTN_WIDTH)
    qu = qu.reshape(qu.shape[0], b, t, ATTN_WIDTH)
    attn = attn_fn(qu, k_new, v_new)
    conv, conv_state = _conformer_conv(qu, conv_prev, wts["w_dw"], wts["b_dw"],
                                       wts["g_ln_conv"], wts["b_ln_conv"], conv_tile)
    h = _mix_out(attn.reshape(m, ATTN_WIDTH), conv.reshape(m, ch), x2, wts["g_attn_grp"], wts["g_conv_grp"],
                 wts["w_out"], wts["g_post_mix"], tm=tm)
    h = _xattn(h, mem_k, mem_v, wts["g_pre_x"], wts["w_xq"], wts["w_xo"], wts["g_post_x"],
               rows_per_batch=t, tm=x_tile)
    h, w_ffn_b = _ffn(h, wts["g_pre_ffn"], w_ffn, wts["g_post_ffn"], tm=tm, tf=512, emit=emit)
    shape4 = (b, t, N_HEADS, HEAD_DIM)
    copies = (w_in_b[0], w_ffn_b) if emit else None
    return h.reshape(b, t, d), k_new.reshape(shape4), v_new.reshape(shape4), conv_state, copies


def kernel(x_prompt, x_sample, cache_k, cache_v, state_conv, cache_mem_k, cache_mem_v, page_table,
           mem_prompt, g_mem, w_mem_k, w_mem_v, g_pre_mix, w_in, w_dw, b_dw, g_ln_conv, b_ln_conv,
           g_attn_grp, g_conv_grp, w_out, g_post_mix, g_pre_x, w_xq, w_xo, g_post_x,
           g_pre_ffn, w_ffn_in, w_ffn_out, g_post_ffn):
    depth = w_in.shape[0]
    bp, tp, d = x_prompt.shape
    bs, ts, _ = x_sample.shape
    mem_len = mem_prompt.shape[1]
    xw = X_HEADS * X_HEAD_DIM
    ch = w_dw.shape[2]
    hp, hs = x_prompt, x_sample
    outs = [[] for _ in range(8)]
    for l in range(depth):
        wts = dict(g_pre_mix=g_pre_mix[l], w_dw=w_dw[l], b_dw=b_dw[l],
                   g_ln_conv=g_ln_conv[l], b_ln_conv=b_ln_conv[l], g_attn_grp=g_attn_grp[l],
                   g_conv_grp=g_conv_grp[l], w_out=w_out[l].astype(BF16), g_post_mix=g_post_mix[l],
                   g_pre_x=g_pre_x[l], w_xq=w_xq[l].astype(BF16), w_xo=w_xo[l].astype(BF16),
                   g_post_x=g_post_x[l], g_pre_ffn=g_pre_ffn[l], g_post_ffn=g_post_ffn[l])

        def sample_attn(qu, k_new, v_new, l=l):
            scores, sel = _sample_scores(qu, cache_k, page_table, l)
            return _sample_attend(scores, sel, qu, k_new, v_new, cache_v, page_table, l)

        hs, ks, vs, cs, big_b = _layer(hs, sample_attn, state_conv[l], cache_mem_k[l].reshape(bs, mem_len, xw),
                                       cache_mem_v[l].reshape(bs, mem_len, xw), wts,
                                       (w_in[l], (w_ffn_in[l], w_ffn_out[l])), True, conv_tile=ts, x_tile=ts)

        w_mem = jnp.concatenate([w_mem_k[l], w_mem_v[l]], axis=1).astype(BF16)
        mk_p, mv_p = _norm_matmul(mem_prompt.reshape(bp * mem_len, d), g_mem[l].reshape(1, d), w_mem,
                                  (xw, xw), tn=xw, tm=_row_tile(bp * mem_len))
        mk_p, mv_p = mk_p.reshape(bp, mem_len, xw), mv_p.reshape(bp, mem_len, xw)
        conv0 = jnp.zeros((bp, CONV_WIDTH - 1, ch), F32)
        hp, kp, vp, cp, _ = _layer(hp, _moba_prompt, conv0, mk_p, mv_p, wts, big_b, False,
                                   conv_tile=_row_tile(tp), x_tile=_row_tile(tp))
        mem_shape = (bp, mem_len, X_HEADS, X_HEAD_DIM)
        for lst, a in zip(outs, (kp, vp, cp, mk_p.reshape(mem_shape), mv_p.reshape(mem_shape), ks, vs, cs)):
            lst.append(a)
    return (hp, hs) + tuple(jnp.stack(lst, 0) for lst in outs)
```

```python
import functools

import jax
import jax.numpy as jnp
from jax import lax
from jax.experimental import pallas as pl
from jax.experimental.pallas import tpu as pltpu

EPS = 1e-6
N_HEADS = 8
HEAD_DIM = 128
ATTN_WIDTH = N_HEADS * HEAD_DIM
CONV_WIDTH = 31
MOBA_BLOCK = 256
MOBA_TOPK = 3
X_HEADS = 4
X_HEAD_DIM = 128

CONV_HALO = 32
DMA_PRIORITIES = 2
VMEM_LIMIT = 56 * 1024 * 1024

F32 = jnp.float32
BF16 = jnp.bfloat16
NEG_INF = float("-inf")
LOG2_E = 1.4426950408889634
_NT = (((1,), (1,)), ((), ()))


def _params(*sem):
    return pltpu.CompilerParams(dimension_semantics=sem, vmem_limit_bytes=VMEM_LIMIT)


def _rms(x, g):
    return x * lax.rsqrt(jnp.mean(x * x, axis=-1, keepdims=True) + EPS) * g


def _top_blocks(gate, n_valid, axis):
    nb = gate.shape[axis]
    pos = lax.broadcasted_iota(jnp.int32, gate.shape, axis)
    g = jnp.where(pos < n_valid, gate, NEG_INF)
    idxs, oks = [], []
    for _ in range(MOBA_TOPK):
        m = jnp.max(g, axis=axis, keepdims=True)
        idx = jnp.min(jnp.where(g == m, pos, nb), axis=axis, keepdims=True)
        ok = m > NEG_INF
        idxs.append(idx)
        oks.append(ok)
        g = jnp.where((pos == idx) & ok, NEG_INF, g)
    return idxs, oks


def _norm_matmul_kernel(x_ref, g_ref, w_ref, *rest, tile_ranges):
    out_refs, xn_ref = rest[:-1], rest[-1]
    j = pl.program_id(1)

    @pl.when(j == 0)
    def _():
        xn_ref[...] = _rms(x_ref[...], g_ref[...]).astype(BF16)

    for o_ref, (lo, hi) in zip(out_refs, tile_ranges):
        @pl.when((j >= lo) & (j < hi))
        def _(o_ref=o_ref):
            o_ref[...] = jnp.dot(xn_ref[...], w_ref[...], preferred_element_type=F32)


def _norm_matmul(x, g, w, widths, tn, tm):
    m, d = x.shape
    n = w.shape[1]
    assert sum(widths) == n and all(wd % tn == 0 for wd in widths) and m % tm == 0
    tile_ranges, lo = [], 0
    for wd in widths:
        tile_ranges.append((lo, lo + wd // tn))
        lo += wd // tn

    def out_map(i, j, lo, cnt):
        return (i, jnp.clip(j - lo, 0, cnt - 1))

    out_specs = [pl.BlockSpec((tm, tn), functools.partial(out_map, lo=lo, cnt=hi - lo))
                 for lo, hi in tile_ranges]
    return pl.pallas_call(
        functools.partial(_norm_matmul_kernel, tile_ranges=tuple(tile_ranges)),
        grid=(m // tm, n // tn),
        in_specs=[pl.BlockSpec((tm, d), lambda i, j: (i, 0)),
                  pl.BlockSpec((1, d), lambda i, j: (0, 0)),
                  pl.BlockSpec((d, tn), lambda i, j: (0, j))],
        out_specs=out_specs,
        out_shape=[jax.ShapeDtypeStruct((m, wd), F32) for wd in widths],
        scratch_shapes=[pltpu.VMEM((tm, d), BF16)],
        compiler_params=_params("arbitrary", "arbitrary"),
        name="norm_matmul",
    )(x, g, w)


def _in_proj_src_tile(j):
    return jnp.where(j < 3, (j + 1) % 3, j)


def _in_proj_kernel(x_ref, g_ref, w_ref, *rest, emit):
    if emit:
        k_ref, v_ref, qu_ref, wt_ref, xn_ref = rest
    else:
        k_ref, v_ref, qu_ref, xn_ref = rest
    j = pl.program_id(1)

    @pl.when(j == 0)
    def _():
        xn_ref[...] = _rms(x_ref[...], g_ref[...]).astype(BF16)

    if emit:
        wt_ref[...] = w_ref[...].astype(BF16)
        w_ref = wt_ref

    for o_ref, cond in ((k_ref, j == 0), (v_ref, j == 1), (qu_ref, j >= 2)):
        @pl.when(cond)
        def _(o_ref=o_ref):
            o_ref[...] = jnp.dot(xn_ref[...], w_ref[...], preferred_element_type=F32)


def _in_proj(x, g, w, emit, tm):
    m, d = x.shape
    tn = ATTN_WIDTH
    nt = (w.shape[1] // tn) if emit else w.shape[0]
    assert m % tm == 0 and nt > 3 and (not emit or (m == tm and w.shape[1] % tn == 0))
    if emit:
        w_spec = pl.BlockSpec((d, tn), lambda i, j: (0, _in_proj_src_tile(j)))
    else:
        w_spec = pl.BlockSpec((None, d, tn), lambda i, j: (j, 0, 0))
    out_specs = [pl.BlockSpec((tm, tn), lambda i, j: (i, 0)),
                 pl.BlockSpec((tm, tn), lambda i, j: (i, 0)),
                 pl.BlockSpec((None, tm, tn), lambda i, j: (jnp.maximum(j - 2, 0), i, 0))]
    out_shape = [jax.ShapeDtypeStruct((m, tn), F32), jax.ShapeDtypeStruct((m, tn), F32),
                 jax.ShapeDtypeStruct((nt - 2, m, tn), F32)]
    if emit:
        out_specs.append(pl.BlockSpec((None, d, tn), lambda i, j: (j, 0, 0)))
        out_shape.append(jax.ShapeDtypeStruct((nt, d, tn), BF16))
    return pl.pallas_call(
        functools.partial(_in_proj_kernel, emit=emit),
        grid=(m // tm, nt),
        in_specs=[pl.BlockSpec((tm, d), lambda i, j: (i, 0)),
                  pl.BlockSpec((1, d), lambda i, j: (0, 0)),
                  w_spec],
        out_specs=out_specs,
        out_shape=out_shape,
        scratch_shapes=[pltpu.VMEM((tm, d), BF16)],
        compiler_params=_params("arbitrary", "arbitrary"),
        name="in_proj",
    )(x, g, w)


def _moba_prompt_kernel(q_ref, k_ref, v_ref, o_ref, kb_ref, vt_ref, km_ref, *, nb, tps):
    blk = MOBA_BLOCK
    h = pl.program_id(1)
    g = pl.program_id(2)

    @pl.when(g == 0)
    def _():
        for n in range(nb):
            kf = k_ref[0, n * blk:(n + 1) * blk, :]
            kb_ref[n * blk:(n + 1) * blk, :] = kf.astype(BF16)
            vt_ref[n] = v_ref[0, n * blk:(n + 1) * blk, :].T.astype(BF16)
            km_ref[n:n + 1, :] = jnp.mean(kf, axis=0, keepdims=True)

    scale = HEAD_DIM ** -0.5 * LOG2_E
    slope = jnp.exp2(jnp.broadcast_to(-8.0 * (h + 1).astype(F32) / N_HEADS, (1, blk))) * LOG2_E

    def attend(i, qi):
        q = q_ref[0, qi * blk:(qi + 1) * blk, :]
        qb = q.astype(BF16)
        kr = lax.broadcasted_iota(jnp.int32, (blk, blk), 0)
        qc = lax.broadcasted_iota(jnp.int32, (blk, blk), 1)
        base = slope * (qc - kr).astype(F32)
        s_own = lax.dot_general(kb_ref[i * blk:(i + 1) * blk, :], qb, _NT, preferred_element_type=F32)
        ts = [jnp.where(kr <= qc, s_own * scale - base, NEG_INF)]
        if i:
            gate = lax.dot_general(km_ref[0:i, :], q, _NT, precision=lax.Precision.HIGHEST,
                                   preferred_element_type=F32)
            idxs, oks = _top_blocks(gate, i, axis=0)
            row = lax.broadcasted_iota(jnp.int32, (i, blk), 0)
            sel = jnp.zeros((i, blk), F32)
            for idx, ok in zip(idxs, oks):
                sel = jnp.where((row == idx) & ok, 1.0, sel)
            for n in range(i):
                s = lax.dot_general(kb_ref[n * blk:(n + 1) * blk, :], qb, _NT, preferred_element_type=F32)
                neg = jnp.where(sel[n:n + 1, :] > 0.0, -slope * float((i - n) * blk), NEG_INF)
                ts.append(s * scale - base + neg)
        m = jnp.max(functools.reduce(jnp.maximum, ts), axis=0, keepdims=True)
        acc = None
        psum = None
        for n, t in enumerate(ts):
            p = jnp.exp2(t - m)
            psum = p if psum is None else psum + p
            vt = vt_ref[i] if n == 0 else vt_ref[n - 1]
            pv = jnp.dot(vt, p.astype(BF16), preferred_element_type=F32)
            acc = pv if acc is None else acc + pv
        l = jnp.sum(psum, axis=0, keepdims=True)
        o_ref[0, qi * blk:(qi + 1) * blk, :] = (acc / l).T

    for v in range(nb // tps):
        @pl.when(g == v)
        def _(v=v):
            for qi in range(tps):
                attend(v * tps + qi, qi)


def _moba_prompt(qu, k, v, tps=8):
    b, t, _ = k.shape
    blk = MOBA_BLOCK
    tps = min(tps, t // blk)
    assert t % (blk * tps) == 0
    nb = t // blk
    return pl.pallas_call(
        functools.partial(_moba_prompt_kernel, nb=nb, tps=tps),
        grid=(b, N_HEADS, nb // tps),
        in_specs=[pl.BlockSpec((None, 1, blk * tps, HEAD_DIM), lambda bi, h, g: (0, bi, g, h)),
                  pl.BlockSpec((1, t, HEAD_DIM), lambda bi, h, g: (bi, 0, h)),
                  pl.BlockSpec((1, t, HEAD_DIM), lambda bi, h, g: (bi, 0, h))],
        out_specs=pl.BlockSpec((1, blk * tps, HEAD_DIM), lambda bi, h, g: (bi, g, h)),
        out_shape=jax.ShapeDtypeStruct((b, t, ATTN_WIDTH), F32),
        scratch_shapes=[pltpu.VMEM((t, HEAD_DIM), BF16), pltpu.VMEM((nb, HEAD_DIM, blk), BF16),
                        pltpu.VMEM((nb, HEAD_DIM), F32)],
        compiler_params=_params("arbitrary", "arbitrary", "arbitrary"),
        name="moba_prompt",
    )(qu, k, v)


def _head_page_copy(cache_ref, pt_ref, buf_ref, sem, layer, b, h, page_slot, dst_row, n_pages):
    page = pt_ref[b * n_pages + page_slot]
    ps = cache_ref.shape[2]
    return pltpu.make_async_copy(cache_ref.at[layer, page, :, h, :], buf_ref.at[pl.ds(dst_row, ps), :], sem)


def _sample_scores_kernel(pt_ref, q_ref, kc_ref, s_ref, sel_ref, kbuf_ref, km_ref, gate_ref, sem_ref,
                          *, layer, n_pages, chunk):
    b = pl.program_id(0)
    h = pl.program_id(1)
    nbt, nh = pl.num_programs(0), pl.num_programs(1)
    step = b * nh + h
    slot = step % 2
    ps = kc_ref.shape[2]
    past = n_pages * ps
    blk = MOBA_BLOCK
    t = q_ref.shape[1]

    def start_all(st, sl):
        bb, hh = st // nh, st % nh

        def issue(pair, carry):
            for k in range(DMA_PRIORITIES):
                p = pair * DMA_PRIORITIES + k
                _head_page_copy(kc_ref, pt_ref, kbuf_ref.at[sl], sem_ref.at[sl], layer, bb, hh,
                                p, pl.multiple_of(p * ps, ps), n_pages).start(priority=k)
            return carry

        lax.fori_loop(0, n_pages // DMA_PRIORITIES, issue, 0, unroll=4)

    @pl.when(step == 0)
    def _():
        start_all(step, slot)

    @pl.when(step + 1 < nbt * nh)
    def _():
        start_all(step + 1, 1 - slot)

    pltpu.make_async_copy(kbuf_ref.at[1 - slot], kbuf_ref.at[slot], sem_ref.at[slot]).wait()

    q = q_ref[0]
    qb = q.astype(BF16)
    bpc = chunk // blk

    def body(ci, carry):
        start = pl.multiple_of(ci * chunk, chunk)
        kf = kbuf_ref[slot, pl.ds(start, chunk), :]
        sc = lax.dot_general(qb, kf.astype(BF16), _NT, preferred_element_type=F32)
        for jj in range(bpc):
            s_ref[0, 0, ci * bpc + jj] = sc[:, jj * blk:(jj + 1) * blk]
        km_ref[pl.ds(pl.multiple_of(ci * bpc, bpc), bpc), :] = jnp.mean(
            kf.reshape(bpc, blk, HEAD_DIM), axis=1)
        return carry

    lax.fori_loop(0, past // chunk, body, 0, unroll=2)

    gate_ref[h] = lax.dot_general(q, km_ref[...], _NT, precision=lax.Precision.HIGHEST,
                                  preferred_element_type=F32)

    @pl.when(h == nh - 1)
    def _():
        gate = gate_ref[...].reshape(nh * t, past // blk)
        idxs, _ = _top_blocks(gate, past // blk, axis=1)
        lane = lax.broadcasted_iota(jnp.int32, (nh * t, 128), 1)
        out = jnp.zeros((nh * t, 128), jnp.int32)
        for k, idx in enumerate(idxs):
            out = jnp.where(lane == k, idx, out)
        sel_ref[0] = out.reshape(nh, t, 128)


def _sample_scores(qu, cache_k, page_table, layer):
    _, b, t, _ = qu.shape
    n_pages = page_table.shape[1]
    ps = cache_k.shape[2]
    past = n_pages * ps
    chunk = 8 * MOBA_BLOCK
    assert past % chunk == 0 and past // MOBA_BLOCK >= MOBA_TOPK and n_pages % DMA_PRIORITIES == 0
    grid_spec = pltpu.PrefetchScalarGridSpec(
        num_scalar_prefetch=1,
        grid=(b, N_HEADS),
        in_specs=[pl.BlockSpec((None, 1, t, HEAD_DIM), lambda bi, h, pt: (0, bi, 0, h)),
                  pl.BlockSpec(memory_space=pl.ANY)],
        out_specs=[pl.BlockSpec((1, 1, past // MOBA_BLOCK, t, MOBA_BLOCK), lambda bi, h, pt: (bi, h, 0, 0, 0)),
                   pl.BlockSpec((1, N_HEADS, t, 128), lambda bi, h, pt: (bi, 0, 0, 0))],
        scratch_shapes=[pltpu.VMEM((2, past, HEAD_DIM), F32),
                        pltpu.VMEM((past // MOBA_BLOCK, HEAD_DIM), F32),
                        pltpu.VMEM((N_HEADS, t, past // MOBA_BLOCK), F32),
                        pltpu.SemaphoreType.DMA((2,))],
    )
    return pl.pallas_call(
        functools.partial(_sample_scores_kernel, layer=layer, n_pages=n_pages, chunk=chunk),
        grid_spec=grid_spec,
        out_shape=[jax.ShapeDtypeStruct((b, N_HEADS, past // MOBA_BLOCK, t, MOBA_BLOCK), F32),
                   jax.ShapeDtypeStruct((b, N_HEADS, t, 128), jnp.int32)],
        compiler_params=_params("arbitrary", "arbitrary"),
        name="sample_scores",
    )(page_table.reshape(-1), qu, cache_k)


def _sample_attend_kernel(pt_ref, sel_ref, s_ref, q_ref, kn_ref, vn_ref, vc_ref, o_ref,
                          vbuf_ref, ssel_ref, sem_ref, *, layer, n_pages, q_start):
    b = pl.program_id(0)
    h = pl.program_id(1)
    nbt, nh = pl.num_programs(0), pl.num_programs(1)
    step = b * nh + h
    slot = step % 2
    t = q_ref.shape[1]
    ps = vc_ref.shape[2]
    blk = MOBA_BLOCK
    ppb = blk // ps
    nsel = t * MOBA_TOPK

    def sel_block(st, e):
        return sel_ref[st * nsel + e]

    def copies(st, sl):
        bb, hh = st // nh, st % nh

        def mk(e, pg):
            return _head_page_copy(vc_ref, pt_ref, vbuf_ref.at[sl], sem_ref.at[sl], layer, bb, hh,
                                   sel_block(st, e) * ppb + pg, e * blk + pg * ps, n_pages)
        return mk

    def start_all(st, sl):
        mk = copies(st, sl)
        for e in range(nsel):
            for pg in range(ppb):
                mk(e, pg).start(priority=(e * ppb + pg) % DMA_PRIORITIES)

    @pl.when(step == 0)
    def _():
        start_all(step, slot)

    @pl.when(step + 1 < nbt * nh)
    def _():
        start_all(step + 1, 1 - slot)

    scale = HEAD_DIM ** -0.5
    slope = jnp.exp2(jnp.broadcast_to(-8.0 * (h + 1).astype(F32) / N_HEADS, (1, 1)))
    lane = lax.broadcasted_iota(jnp.int32, (1, blk), 1)

    ssel_ref[...] = jnp.full(ssel_ref.shape, NEG_INF, F32)
    for e in range(nsel):
        ti = e // MOBA_TOPK
        n = sel_block(step, e)
        raw = s_ref[0, 0, n, ti:ti + 1, :]
        dist = (q_start + ti - n * blk - lane).astype(F32)
        ssel_ref[ti:ti + 1, e * blk:(e + 1) * blk] = raw * scale - slope * dist

    q = q_ref[0].astype(BF16)
    r = lax.broadcasted_iota(jnp.int32, (t, t), 0)
    c = lax.broadcasted_iota(jnp.int32, (t, t), 1)
    s_own = lax.dot_general(q, kn_ref[0].astype(BF16), _NT, preferred_element_type=F32) * scale
    s_own = jnp.where(c <= r, s_own - slope * (r - c).astype(F32), NEG_INF)

    s_sel = ssel_ref[...]
    m = jnp.maximum(jnp.max(s_sel, axis=1, keepdims=True), jnp.max(s_own, axis=1, keepdims=True))
    p_sel = jnp.exp(s_sel - m)
    p_own = jnp.exp(s_own - m)
    l = jnp.sum(p_sel, axis=1, keepdims=True) + jnp.sum(p_own, axis=1, keepdims=True)

    mk = copies(step, slot)
    for e in range(nsel):
        for pg in range(ppb):
            mk(e, pg).wait()

    acc = jnp.dot(p_sel.astype(BF16), vbuf_ref[slot].astype(BF16), preferred_element_type=F32)
    acc = acc + jnp.dot(p_own.astype(BF16), vn_ref[0].astype(BF16), preferred_element_type=F32)
    o_ref[0] = acc / l


def _sample_attend(scores, sel, qu, k_new, v_new, cache_v, page_table, layer):
    b, t, _ = k_new.shape
    n_pages = page_table.shape[1]
    ps = cache_v.shape[2]
    past = n_pages * ps
    assert MOBA_BLOCK % ps == 0 and past % MOBA_BLOCK == 0 and t <= MOBA_BLOCK
    nsel = t * MOBA_TOPK
    grid_spec = pltpu.PrefetchScalarGridSpec(
        num_scalar_prefetch=2,
        grid=(b, N_HEADS),
        in_specs=[pl.BlockSpec((1, 1, past // MOBA_BLOCK, t, MOBA_BLOCK), lambda bi, h, pt, sl: (bi, h, 0, 0, 0)),
                  pl.BlockSpec((None, 1, t, HEAD_DIM), lambda bi, h, pt, sl: (0, bi, 0, h)),
                  pl.BlockSpec((1, t, HEAD_DIM), lambda bi, h, pt, sl: (bi, 0, h)),
                  pl.BlockSpec((1, t, HEAD_DIM), lambda bi, h, pt, sl: (bi, 0, h)),
                  pl.BlockSpec(memory_space=pl.ANY)],
        out_specs=pl.BlockSpec((1, t, HEAD_DIM), lambda bi, h, pt, sl: (bi, 0, h)),
        scratch_shapes=[pltpu.VMEM((2, nsel * MOBA_BLOCK, HEAD_DIM), F32),
                        pltpu.VMEM((t, nsel * MOBA_BLOCK), F32),
                        pltpu.SemaphoreType.DMA((2,))],
    )
    return pl.pallas_call(
        functools.partial(_sample_attend_kernel, layer=layer, n_pages=n_pages, q_start=past),
        grid_spec=grid_spec,
        out_shape=jax.ShapeDtypeStruct((b, t, ATTN_WIDTH), F32),
        compiler_params=_params("arbitrary", "arbitrary"),
        name="sample_attend",
    )(page_table.reshape(-1), sel[..., :MOBA_TOPK].reshape(-1), scores, qu, k_new, v_new, cache_v)


def _conv_taps():
    pad = CONV_HALO - (CONV_WIDTH - 1)
    return [(ph, [(j, (pad + j) // 8) for j in range(CONV_WIDTH) if (pad + j) % 8 == ph]) for ph in range(8)]


def _conv_kernel(ua_ref, ug_ref, prev_ref, w_ref, b_ref, g_ref, bl_ref, y_ref, st_ref, buf_ref, par_ref, z_ref,
                 *, tt, rows, nrows, lanes):
    ti = pl.program_id(1)
    ch = y_ref.shape[2]
    hist = CONV_WIDTH - 1
    pad = CONV_HALO - hist
    i_bias, i_gain, i_beta = CONV_WIDTH, CONV_WIDTH + 1, CONV_WIDTH + 2

    @pl.when(ti == 0)
    def _():
        buf_ref[0:CONV_HALO, :] = jnp.zeros((CONV_HALO, ch), F32)
        buf_ref[pad:CONV_HALO, :] = prev_ref[0]
        for j in range(CONV_WIDTH):
            par_ref[j] = jnp.broadcast_to(w_ref[j:j + 1, :], (8, ch))
        par_ref[i_bias] = jnp.broadcast_to(b_ref[...], (8, ch))
        par_ref[i_gain] = jnp.broadcast_to(g_ref[...], (8, ch))
        par_ref[i_beta] = jnp.broadcast_to(bl_ref[...], (8, ch))

    @pl.when(ti > 0)
    def _():
        buf_ref[0:CONV_HALO, :] = buf_ref[tt:tt + CONV_HALO, :]

    def glu(ci, carry):
        r0 = pl.multiple_of(ci * rows, rows)
        buf_ref[pl.ds(CONV_HALO + r0, rows), :] = (ua_ref[0, pl.ds(r0, rows), :]
                                                   * jax.nn.sigmoid(ug_ref[0, pl.ds(r0, rows), :]))
        return carry

    lax.fori_loop(0, tt // rows, glu, 0)

    def conv(ci, carry):
        r0 = pl.multiple_of(ci * rows, rows)
        for c0 in range(0, ch, lanes):
            cs = slice(c0, c0 + lanes)
            win = buf_ref.at[pl.ds(r0, rows + CONV_HALO), pl.ds(c0, lanes)]
            acc = jnp.broadcast_to(par_ref[i_bias, :, cs][None], (rows // 8, 8, lanes))
            for ph, taps in _conv_taps():
                a_lo, a_hi = taps[0][1], taps[-1][1]
                x = win[8 * a_lo + ph:8 * a_hi + ph + rows, :]
                for j, a in taps:
                    xs = x[8 * (a - a_lo):8 * (a - a_lo) + rows, :].reshape(rows // 8, 8, lanes)
                    acc = acc + par_ref[j, :, cs][None] * xs
            z_ref[pl.ds(r0, rows), cs] = acc.reshape(rows, lanes)
        return carry

    lax.fori_loop(0, tt // rows, conv, 0)

    def norm(ci, carry):
        r0 = pl.multiple_of(ci * nrows, nrows)
        z = z_ref[pl.ds(r0, nrows), :]
        xc = z - jnp.mean(z, axis=-1, keepdims=True)
        xn = (xc * lax.rsqrt(jnp.mean(xc * xc, axis=-1, keepdims=True) + EPS)).reshape(nrows // 8, 8, ch)
        y = (xn * par_ref[i_gain][None] + par_ref[i_beta][None]).reshape(nrows, ch)
        y_ref[0, pl.ds(r0, nrows), :] = y * jax.nn.sigmoid(y)
        return carry

    lax.fori_loop(0, tt // nrows, norm, 0, unroll=min(8, tt // nrows))

    @pl.when(ti == pl.num_programs(1) - 1)
    def _():
        st_ref[0] = buf_ref[tt + pad:tt + CONV_HALO, :]


def _conformer_conv(qu, prev, layer, w_dw, b_dw, g_ln, b_ln, tt):
    _, b, t, ch = qu.shape
    assert ch == w_dw.shape[1]
    hist = CONV_WIDTH - 1
    assert t % tt == 0 and tt % 8 == 0
    rows = 64 if tt % 64 == 0 else 8
    nrows = 16 if tt % 16 == 0 else 8
    lanes = 256 if ch % 256 == 0 else ch
    row = lambda a: a.reshape(1, ch)
    return pl.pallas_call(
        functools.partial(_conv_kernel, tt=tt, rows=rows, nrows=nrows, lanes=lanes),
        grid=(b, t // tt),
        in_specs=[pl.BlockSpec((None, 1, tt, ch), lambda bi, ti: (1, bi, ti, 0)),
                  pl.BlockSpec((None, 1, tt, ch), lambda bi, ti: (2, bi, ti, 0)),
                  pl.BlockSpec((None, 1, hist, ch), lambda bi, ti: (layer, bi, 0, 0)),
                  pl.BlockSpec((CONV_WIDTH, ch), lambda bi, ti: (0, 0)),
                  pl.BlockSpec((1, ch), lambda bi, ti: (0, 0)),
                  pl.BlockSpec((1, ch), lambda bi, ti: (0, 0)),
                  pl.BlockSpec((1, ch), lambda bi, ti: (0, 0))],
        out_specs=[pl.BlockSpec((1, tt, ch), lambda bi, ti: (bi, ti, 0)),
                   pl.BlockSpec((1, hist, ch), lambda bi, ti: (bi, 0, 0))],
        out_shape=[jax.ShapeDtypeStruct((b, t, ch), F32), jax.ShapeDtypeStruct((b, hist, ch), F32)],
        scratch_shapes=[pltpu.VMEM((tt + CONV_HALO, ch), F32), pltpu.VMEM((CONV_WIDTH + 3, 8, ch), F32),
                        pltpu.VMEM((tt, ch), F32)],
        compiler_params=_params("arbitrary", "arbitrary"),
        name="conformer_conv",
    )(qu, qu, prev, w_dw, row(b_dw), row(g_ln), row(b_ln))


def _mix_out_kernel(attn_ref, conv_ref, h_ref, ga_ref, gc_ref, w_ref, gp_ref, o_ref):
    wa = attn_ref.shape[1]
    a = _rms(attn_ref[...], ga_ref[...]).astype(BF16)
    cv = _rms(conv_ref[...], gc_ref[...]).astype(BF16)
    mixed = (jnp.dot(a, w_ref[0:wa, :], preferred_element_type=F32)
             + jnp.dot(cv, w_ref[wa:, :], preferred_element_type=F32))
    o_ref[...] = h_ref[...] + _rms(mixed, gp_ref[...])


def _mix_out(attn, conv, h, g_attn, g_conv, w_out, g_post, tm):
    m, wa = attn.shape
    wc = conv.shape[1]
    d = h.shape[1]
    assert m % tm == 0
    row = lambda a: a.reshape(1, -1)
    return pl.pallas_call(
        _mix_out_kernel,
        grid=(m // tm,),
        in_specs=[pl.BlockSpec((tm, wa), lambda i: (i, 0)),
                  pl.BlockSpec((tm, wc), lambda i: (i, 0)),
                  pl.BlockSpec((tm, d), lambda i: (i, 0)),
                  pl.BlockSpec((1, wa), lambda i: (0, 0)),
                  pl.BlockSpec((1, wc), lambda i: (0, 0)),
                  pl.BlockSpec((wa + wc, d), lambda i: (0, 0)),
                  pl.BlockSpec((1, d), lambda i: (0, 0))],
        out_specs=pl.BlockSpec((tm, d), lambda i: (i, 0)),
        out_shape=jax.ShapeDtypeStruct((m, d), F32),
        compiler_params=_params("arbitrary"),
        name="mix_out",
    )(attn, conv, h, row(g_attn), row(g_conv), w_out, row(g_post))


def _xattn_kernel(h_ref, mk_ref, mv_ref, gpre_ref, wq_ref, wo_ref, gpost_ref, o_ref):
    h = h_ref[...]
    xn = _rms(h, gpre_ref[...]).astype(BF16)
    xq = jnp.dot(xn, wq_ref[...], preferred_element_type=F32)
    scale = X_HEAD_DIM ** -0.5
    outs = []
    for hh in range(X_HEADS):
        cols = slice(hh * X_HEAD_DIM, (hh + 1) * X_HEAD_DIM)
        s = lax.dot_general(xq[:, cols].astype(BF16), mk_ref[0, :, cols].astype(BF16), _NT,
                            preferred_element_type=F32) * scale
        p = jnp.exp(s - jnp.max(s, axis=1, keepdims=True))
        l = jnp.sum(p, axis=1, keepdims=True)
        outs.append(jnp.dot(p.astype(BF16), mv_ref[0, :, cols].astype(BF16), preferred_element_type=F32) / l)
    o = jnp.concatenate(outs, axis=1).astype(BF16)
    y = jnp.dot(o, wo_ref[...], preferred_element_type=F32)
    o_ref[...] = h + _rms(y, gpost_ref[...])


def _xattn(h, mem_k, mem_v, layer, g_pre, w_xq, w_xo, g_post, rows_per_batch, tm):
    m, d = h.shape
    assert rows_per_batch % tm == 0
    tiles = rows_per_batch // tm
    mem_len, xw = mem_k.shape[2:]
    row = lambda a: a.reshape(1, -1)
    return pl.pallas_call(
        _xattn_kernel,
        grid=(m // tm,),
        in_specs=[pl.BlockSpec((tm, d), lambda i: (i, 0)),
                  pl.BlockSpec((None, 1, mem_len, xw), lambda i: (layer, i // tiles, 0, 0)),
                  pl.BlockSpec((None, 1, mem_len, xw), lambda i: (layer, i // tiles, 0, 0)),
                  pl.BlockSpec((1, d), lambda i: (0, 0)),
                  pl.BlockSpec((d, xw), lambda i: (0, 0)),
                  pl.BlockSpec((xw, d), lambda i: (0, 0)),
                  pl.BlockSpec((1, d), lambda i: (0, 0))],
        out_specs=pl.BlockSpec((tm, d), lambda i: (i, 0)),
        out_shape=jax.ShapeDtypeStruct((m, d), F32),
        compiler_params=_params("arbitrary"),
        name="xattn",
    )(h, mem_k, mem_v, row(g_pre), w_xq, w_xo, row(g_post))


def _ffn_kernel(h_ref, gpre_ref, wg_ref, wu_ref, wo_ref, gpost_ref, o_ref, *rest, emit):
    if emit:
        wg_out, wu_out, wo_out, xn_ref, acc_ref = rest
        wg_out[...] = wg_ref[...].astype(BF16)
        wu_out[...] = wu_ref[...].astype(BF16)
        wo_out[...] = wo_ref[...].astype(BF16)
        wg_ref, wu_ref, wo_ref = wg_out, wu_out, wo_out
    else:
        xn_ref, acc_ref = rest
    f = pl.program_id(1)

    @pl.when(f == 0)
    def _():
        xn_ref[...] = _rms(h_ref[...], gpre_ref[...]).astype(BF16)
        acc_ref[...] = jnp.zeros(acc_ref.shape, F32)

    xn = xn_ref[...]
    g = jnp.dot(xn, wg_ref[...], preferred_element_type=F32)
    u = jnp.dot(xn, wu_ref[...], preferred_element_type=F32)
    a = (g * jax.nn.sigmoid(g) * u).astype(BF16)
    acc_ref[...] += jnp.dot(a, wo_ref[...], preferred_element_type=F32)

    @pl.when(f == pl.num_programs(1) - 1)
    def _():
        o_ref[...] = h_ref[...] + _rms(acc_ref[...], gpost_ref[...])


def _ffn(h, g_pre, w, g_post, tm, tf, emit):
    m, d = h.shape
    row = lambda a: a.reshape(1, -1)
    if emit:
        w_in, w_out = w
        hidden = w_out.shape[0]
        nf = hidden // tf
        assert m == tm and hidden % tf == 0 and w_in.shape[1] == 2 * hidden
        w_args = (w_in, w_in, w_out)
        w_specs = [pl.BlockSpec((d, tf), lambda i, f: (0, f)),
                   pl.BlockSpec((d, tf), lambda i, f: (0, f + nf)),
                   pl.BlockSpec((tf, d), lambda i, f: (f, 0))]
    else:
        w_args = w
        nf, hidden = w[0].shape[0], w[2].shape[0]
        assert m % tm == 0 and w[0].shape == w[1].shape == (nf, d, tf) and hidden == nf * tf
        w_specs = [pl.BlockSpec((None, d, tf), lambda i, f: (f, 0, 0)),
                   pl.BlockSpec((None, d, tf), lambda i, f: (f, 0, 0)),
                   pl.BlockSpec((tf, d), lambda i, f: (f, 0))]
    out_specs = [pl.BlockSpec((tm, d), lambda i, f: (i, 0))]
    out_shape = [jax.ShapeDtypeStruct((m, d), F32)]
    if emit:
        out_specs += [pl.BlockSpec((None, d, tf), lambda i, f: (f, 0, 0)),
                      pl.BlockSpec((None, d, tf), lambda i, f: (f, 0, 0)),
                      pl.BlockSpec((tf, d), lambda i, f: (f, 0))]
        out_shape += [jax.ShapeDtypeStruct((nf, d, tf), BF16), jax.ShapeDtypeStruct((nf, d, tf), BF16),
                      jax.ShapeDtypeStruct((hidden, d), BF16)]
    outs = pl.pallas_call(
        functools.partial(_ffn_kernel, emit=emit),
        grid=(m // tm, nf),
        in_specs=[pl.BlockSpec((tm, d), lambda i, f: (i, 0)),
                  pl.BlockSpec((1, d), lambda i, f: (0, 0)),
                  *w_specs,
                  pl.BlockSpec((1, d), lambda i, f: (0, 0))],
        out_specs=out_specs,
        out_shape=out_shape,
        scratch_shapes=[pltpu.VMEM((tm, d), BF16), pltpu.VMEM((tm, d), F32)],
        compiler_params=_params("arbitrary", "arbitrary"),
        name="ffn",
    )(h, row(g_pre), *w_args, row(g_post))
    return outs[0], tuple(outs[1:])


def _row_tile(m):
    return 512 if m % 512 == 0 else m


def _layer(x, attn_fn, state, state_layer, wts, big, emit, conv_tile, x_tile):
    conv_prev, mem_k, mem_v = state
    b, t, d = x.shape
    m = b * t
    tm = _row_tile(m)
    x2 = x.reshape(m, d)
    ch = wts["w_dw"].shape[1]
    assert ch == ATTN_WIDTH
    w_in, w_ffn = big
    k_new, v_new, qu, *w_in_b = _in_proj(x2, wts["g_pre_mix"].reshape(1, d), w_in, emit,
                                         tm=1024 if m % 1024 == 0 else tm)
    k_new, v_new = k_new.reshape(b, t, ATTN_WIDTH), v_new.reshape(b, t, ATTN_WIDTH)
    qu = qu.reshape(qu.shape[0], b, t, ATTN_WIDTH)
    attn = attn_fn(qu, k_new, v_new)
    conv, conv_state = _conformer_conv(qu, conv_prev, state_layer, wts["w_dw"], wts["b_dw"],
                                       wts["g_ln_conv"], wts["b_ln_conv"], conv_tile)
    h = _mix_out(attn.reshape(m, ATTN_WIDTH), conv.reshape(m, ch), x2, wts["g_attn_grp"], wts["g_conv_grp"],
                 wts["w_out"], wts["g_post_mix"], tm=tm)
    h = _xattn(h, mem_k, mem_v, state_layer, wts["g_pre_x"], wts["w_xq"], wts["w_xo"], wts["g_post_x"],
               rows_per_batch=t, tm=x_tile)
    h, w_ffn_b = _ffn(h, wts["g_pre_ffn"], w_ffn, wts["g_post_ffn"], tm=tm, tf=512, emit=emit)
    shape4 = (b, t, N_HEADS, HEAD_DIM)
    copies = (w_in_b[0], w_ffn_b) if emit else None
    return h.reshape(b, t, d), k_new.reshape(shape4), v_new.reshape(shape4), conv_state, copies


def kernel(x_prompt, x_sample, cache_k, cache_v, state_conv, cache_mem_k, cache_mem_v, page_table,
           mem_prompt, g_mem, w_mem_k, w_mem_v, g_pre_mix, w_in, w_dw, b_dw, g_ln_conv, b_ln_conv,
           g_attn_grp, g_conv_grp, w_out, g_post_mix, g_pre_x, w_xq, w_xo, g_post_x,
           g_pre_ffn, w_ffn_in, w_ffn_out, g_post_ffn):
    depth = w_in.shape[0]
    bp, tp, d = x_prompt.shape
    bs, ts, _ = x_sample.shape
    mem_len = mem_prompt.shape[1]
    xw = X_HEADS * X_HEAD_DIM
    ch = w_dw.shape[2]
    hp, hs = x_prompt, x_sample
    outs = [[] for _ in range(8)]
    for l in range(depth):
        wts = dict(g_pre_mix=g_pre_mix[l], w_dw=w_dw[l], b_dw=b_dw[l],
                   g_ln_conv=g_ln_conv[l], b_ln_conv=b_ln_conv[l], g_attn_grp=g_attn_grp[l],
                   g_conv_grp=g_conv_grp[l], w_out=w_out[l].astype(BF16), g_post_mix=g_post_mix[l],
                   g_pre_x=g_pre_x[l], w_xq=w_xq[l].astype(BF16), w_xo=w_xo[l].astype(BF16),
                   g_post_x=g_post_x[l], g_pre_ffn=g_pre_ffn[l], g_post_ffn=g_post_ffn[l])

        def sample_attn(qu, k_new, v_new, l=l):
            scores, sel = _sample_scores(qu, cache_k, page_table, l)
            return _sample_attend(scores, sel, qu, k_new, v_new, cache_v, page_table, l)

        sample_state = (state_conv, cache_mem_k.reshape(depth, bs, mem_len, xw),
                        cache_mem_v.reshape(depth, bs, mem_len, xw))
        hs, ks, vs, cs, big_b = _layer(hs, sample_attn, sample_state, l, wts,
                                       (w_in[l], (w_ffn_in[l], w_ffn_out[l])), True, conv_tile=ts, x_tile=ts)

        w_mem = jnp.concatenate([w_mem_k[l], w_mem_v[l]], axis=1).astype(BF16)
        mk_p, mv_p = _norm_matmul(mem_prompt.reshape(bp * mem_len, d), g_mem[l].reshape(1, d), w_mem,
                                  (xw, xw), tn=xw, tm=_row_tile(bp * mem_len))
        mk_p, mv_p = mk_p.reshape(bp, mem_len, xw), mv_p.reshape(bp, mem_len, xw)
        prompt_state = (jnp.zeros((1, bp, CONV_WIDTH - 1, ch), F32), mk_p[None], mv_p[None])
        hp, kp, vp, cp, _ = _layer(hp, _moba_prompt, prompt_state, 0, wts, big_b, False,
                                   conv_tile=_row_tile(tp), x_tile=_row_tile(tp))
        mem_shape = (bp, mem_len, X_HEADS, X_HEAD_DIM)
        for lst, a in zip(outs, (kp, vp, cp, mk_p.reshape(mem_shape), mv_p.reshape(mem_shape), ks, vs, cs)):
            lst.append(a)
    return (hp, hs) + tuple(jnp.stack(lst, 0) for lst in outs)
```

```python
import functools

import jax
import jax.numpy as jnp
from jax import lax
from jax.experimental import pallas as pl
from jax.experimental.pallas import tpu as pltpu

EPS = 1e-6
N_HEADS = 8
HEAD_DIM = 128
ATTN_WIDTH = N_HEADS * HEAD_DIM
CONV_WIDTH = 31
MOBA_BLOCK = 256
MOBA_TOPK = 3
X_HEADS = 4
X_HEAD_DIM = 128

CONV_HALO = 32
DMA_PRIORITIES = 2
VMEM_LIMIT = 56 * 1024 * 1024

F32 = jnp.float32
BF16 = jnp.bfloat16
NEG_INF = float("-inf")
LOG2_E = 1.4426950408889634
_NT = (((1,), (1,)), ((), ()))


def _params(*sem):
    return pltpu.CompilerParams(dimension_semantics=sem, vmem_limit_bytes=VMEM_LIMIT)


def _rms(x, g):
    return x * lax.rsqrt(jnp.mean(x * x, axis=-1, keepdims=True) + EPS) * g


def _top_blocks(gate, n_valid, axis):
    nb = gate.shape[axis]
    pos = lax.broadcasted_iota(jnp.int32, gate.shape, axis)
    g = jnp.where(pos < n_valid, gate, NEG_INF)
    idxs, oks = [], []
    for _ in range(MOBA_TOPK):
        m = jnp.max(g, axis=axis, keepdims=True)
        idx = jnp.min(jnp.where(g == m, pos, nb), axis=axis, keepdims=True)
        ok = m > NEG_INF
        idxs.append(idx)
        oks.append(ok)
        g = jnp.where((pos == idx) & ok, NEG_INF, g)
    return idxs, oks


def _norm_matmul_kernel(x_ref, g_ref, w_ref, *rest, tile_ranges):
    out_refs, xn_ref = rest[:-1], rest[-1]
    j = pl.program_id(1)

    @pl.when(j == 0)
    def _():
        xn_ref[...] = _rms(x_ref[...], g_ref[...]).astype(BF16)

    for o_ref, (lo, hi) in zip(out_refs, tile_ranges):
        @pl.when((j >= lo) & (j < hi))
        def _(o_ref=o_ref):
            o_ref[...] = jnp.dot(xn_ref[...], w_ref[...], preferred_element_type=F32)


def _norm_matmul(x, g, w, widths, tn, tm):
    m, d = x.shape
    n = w.shape[1]
    assert sum(widths) == n and all(wd % tn == 0 for wd in widths) and m % tm == 0
    tile_ranges, lo = [], 0
    for wd in widths:
        tile_ranges.append((lo, lo + wd // tn))
        lo += wd // tn

    def out_map(i, j, lo, cnt):
        return (i, jnp.clip(j - lo, 0, cnt - 1))

    out_specs = [pl.BlockSpec((tm, tn), functools.partial(out_map, lo=lo, cnt=hi - lo))
                 for lo, hi in tile_ranges]
    return pl.pallas_call(
        functools.partial(_norm_matmul_kernel, tile_ranges=tuple(tile_ranges)),
        grid=(m // tm, n // tn),
        in_specs=[pl.BlockSpec((tm, d), lambda i, j: (i, 0)),
                  pl.BlockSpec((1, d), lambda i, j: (0, 0)),
                  pl.BlockSpec((d, tn), lambda i, j: (0, j))],
        out_specs=out_specs,
        out_shape=[jax.ShapeDtypeStruct((m, wd), F32) for wd in widths],
        scratch_shapes=[pltpu.VMEM((tm, d), BF16)],
        compiler_params=_params("arbitrary", "arbitrary"),
        name="norm_matmul",
    )(x, g, w)


def _in_proj_src_tile(j):
    return jnp.where(j < 3, (j + 1) % 3, j)


def _in_proj_kernel(x_ref, g_ref, w_ref, *rest, emit):
    if emit:
        k_ref, v_ref, qu_ref, wt_ref, xn_ref = rest
    else:
        k_ref, v_ref, qu_ref, xn_ref = rest
    j = pl.program_id(1)

    @pl.when(j == 0)
    def _():
        xn_ref[...] = _rms(x_ref[...], g_ref[...]).astype(BF16)

    if emit:
        wt_ref[...] = w_ref[...].astype(BF16)
        w_ref = wt_ref

    for o_ref, cond in ((k_ref, j == 0), (v_ref, j == 1), (qu_ref, j >= 2)):
        @pl.when(cond)
        def _(o_ref=o_ref):
            o_ref[...] = jnp.dot(xn_ref[...], w_ref[...], preferred_element_type=F32)


def _in_proj(x, g, w, emit, tm):
    m, d = x.shape
    tn = ATTN_WIDTH
    nt = (w.shape[1] // tn) if emit else w.shape[0]
    assert m % tm == 0 and nt > 3 and (not emit or (m == tm and w.shape[1] % tn == 0))
    if emit:
        w_spec = pl.BlockSpec((d, tn), lambda i, j: (0, _in_proj_src_tile(j)))
    else:
        w_spec = pl.BlockSpec((None, d, tn), lambda i, j: (j, 0, 0))
    out_specs = [pl.BlockSpec((tm, tn), lambda i, j: (i, 0)),
                 pl.BlockSpec((tm, tn), lambda i, j: (i, 0)),
                 pl.BlockSpec((None, tm, tn), lambda i, j: (jnp.maximum(j - 2, 0), i, 0))]
    out_shape = [jax.ShapeDtypeStruct((m, tn), F32), jax.ShapeDtypeStruct((m, tn), F32),
                 jax.ShapeDtypeStruct((nt - 2, m, tn), F32)]
    if emit:
        out_specs.append(pl.BlockSpec((None, d, tn), lambda i, j: (j, 0, 0)))
        out_shape.append(jax.ShapeDtypeStruct((nt, d, tn), BF16))
    return pl.pallas_call(
        functools.partial(_in_proj_kernel, emit=emit),
        grid=(m // tm, nt),
        in_specs=[pl.BlockSpec((tm, d), lambda i, j: (i, 0)),
                  pl.BlockSpec((1, d), lambda i, j: (0, 0)),
                  w_spec],
        out_specs=out_specs,
        out_shape=out_shape,
        scratch_shapes=[pltpu.VMEM((tm, d), BF16)],
        compiler_params=_params("arbitrary", "arbitrary"),
        name="in_proj",
    )(x, g, w)


def _moba_prompt_body(h, g, q_ref, k_ref, v_ref, o_ref, kb_ref, vt_ref, km_ref, *, nb, tps):
    blk = MOBA_BLOCK

    @pl.when(g == 0)
    def _():
        for n in range(nb):
            kf = k_ref[0, n * blk:(n + 1) * blk, :]
            kb_ref[n * blk:(n + 1) * blk, :] = kf.astype(BF16)
            vt_ref[n] = v_ref[0, n * blk:(n + 1) * blk, :].T.astype(BF16)
            km_ref[n:n + 1, :] = jnp.mean(kf, axis=0, keepdims=True)

    scale = HEAD_DIM ** -0.5 * LOG2_E
    slope = jnp.exp2(jnp.broadcast_to(-8.0 * (h + 1).astype(F32) / N_HEADS, (1, blk))) * LOG2_E

    def attend(i, qi):
        q = q_ref[0, qi * blk:(qi + 1) * blk, :]
        qb = q.astype(BF16)
        kr = lax.broadcasted_iota(jnp.int32, (blk, blk), 0)
        qc = lax.broadcasted_iota(jnp.int32, (blk, blk), 1)
        base = slope * (qc - kr).astype(F32)
        s_own = lax.dot_general(kb_ref[i * blk:(i + 1) * blk, :], qb, _NT, preferred_element_type=F32)
        ts = [jnp.where(kr <= qc, s_own * scale - base, NEG_INF)]
        if i:
            gate = lax.dot_general(km_ref[0:i, :], q, _NT, precision=lax.Precision.HIGHEST,
                                   preferred_element_type=F32)
            idxs, oks = _top_blocks(gate, i, axis=0)
            row = lax.broadcasted_iota(jnp.int32, (i, blk), 0)
            sel = jnp.zeros((i, blk), F32)
            for idx, ok in zip(idxs, oks):
                sel = jnp.where((row == idx) & ok, 1.0, sel)
            for n in range(i):
                s = lax.dot_general(kb_ref[n * blk:(n + 1) * blk, :], qb, _NT, preferred_element_type=F32)
                neg = jnp.where(sel[n:n + 1, :] > 0.0, -slope * float((i - n) * blk), NEG_INF)
                ts.append(s * scale - base + neg)
        m = jnp.max(functools.reduce(jnp.maximum, ts), axis=0, keepdims=True)
        acc = None
        psum = None
        for n, t in enumerate(ts):
            p = jnp.exp2(t - m)
            psum = p if psum is None else psum + p
            vt = vt_ref[i] if n == 0 else vt_ref[n - 1]
            pv = jnp.dot(vt, p.astype(BF16), preferred_element_type=F32)
            acc = pv if acc is None else acc + pv
        l = jnp.sum(psum, axis=0, keepdims=True)
        o_ref[0, qi * blk:(qi + 1) * blk, :] = (acc / l).T

    for v in range(nb // tps):
        @pl.when(g == v)
        def _(v=v):
            for qi in range(tps):
                attend(v * tps + qi, qi)


def _head_page_copy(cache_ref, pt_ref, buf_ref, sem, layer, b, h, page_slot, dst_row, n_pages):
    page = pt_ref[b * n_pages + page_slot]
    ps = cache_ref.shape[2]
    return pltpu.make_async_copy(cache_ref.at[layer, page, :, h, :], buf_ref.at[pl.ds(dst_row, ps), :], sem)


def _sample_scores_body(step, n_steps, pt_ref, q_ref, kc_ref, s_ref, sel_ref, kbuf_ref, km_ref, gate_ref, sem_ref,
                        *, layer, n_pages, chunk):
    nh = N_HEADS
    b, h = step // nh, step % nh
    slot = step % 2
    ps = kc_ref.shape[2]
    past = n_pages * ps
    blk = MOBA_BLOCK
    t = q_ref.shape[1]

    def start_all(st, sl):
        bb, hh = st // nh, st % nh

        def issue(pair, carry):
            for k in range(DMA_PRIORITIES):
                p = pair * DMA_PRIORITIES + k
                _head_page_copy(kc_ref, pt_ref, kbuf_ref.at[sl], sem_ref.at[sl], layer, bb, hh,
                                p, pl.multiple_of(p * ps, ps), n_pages).start(priority=k)
            return carry

        lax.fori_loop(0, n_pages // DMA_PRIORITIES, issue, 0, unroll=4)

    @pl.when(step == 0)
    def _():
        start_all(step, slot)

    @pl.when(step + 1 < n_steps)
    def _():
        start_all(step + 1, 1 - slot)

    pltpu.make_async_copy(kbuf_ref.at[1 - slot], kbuf_ref.at[slot], sem_ref.at[slot]).wait()

    q = q_ref[0]
    qb = q.astype(BF16)
    bpc = chunk // blk

    def body(ci, carry):
        start = pl.multiple_of(ci * chunk, chunk)
        kf = kbuf_ref[slot, pl.ds(start, chunk), :]
        sc = lax.dot_general(qb, kf.astype(BF16), _NT, preferred_element_type=F32)
        for jj in range(bpc):
            s_ref[0, 0, ci * bpc + jj] = sc[:, jj * blk:(jj + 1) * blk]
        km_ref[pl.ds(pl.multiple_of(ci * bpc, bpc), bpc), :] = jnp.mean(
            kf.reshape(bpc, blk, HEAD_DIM), axis=1)
        return carry

    lax.fori_loop(0, past // chunk, body, 0, unroll=2)

    gate_ref[h] = lax.dot_general(q, km_ref[...], _NT, precision=lax.Precision.HIGHEST,
                                  preferred_element_type=F32)

    @pl.when(h == nh - 1)
    def _():
        gate = gate_ref[...].reshape(nh * t, past // blk)
        idxs, _ = _top_blocks(gate, past // blk, axis=1)
        lane = lax.broadcasted_iota(jnp.int32, (nh * t, 128), 1)
        out = jnp.zeros((nh * t, 128), jnp.int32)
        for k, idx in enumerate(idxs):
            out = jnp.where(lane == k, idx, out)
        sel_ref[0] = out.reshape(nh, t, 128)


def _moba_scores_kernel(pt_ref, qs_ref, kc_ref, qp_ref, kp_ref, vp_ref, s_ref, sel_ref, o_ref,
                        kbuf_ref, km_ref, gate_ref, sem_ref, kb_ref, vt_ref, kmp_ref,
                        *, layer, n_pages, chunk, nb, tps, every):
    step = pl.program_id(0)
    _sample_scores_body(step, pl.num_programs(0), pt_ref, qs_ref, kc_ref, s_ref, sel_ref, kbuf_ref, km_ref,
                        gate_ref, sem_ref, layer=layer, n_pages=n_pages, chunk=chunk)

    @pl.when(step % every == 0)
    def _():
        item = step // every
        groups = nb // tps
        _moba_prompt_body((item // groups) % N_HEADS, item % groups, qp_ref, kp_ref, vp_ref, o_ref,
                          kb_ref, vt_ref, kmp_ref, nb=nb, tps=tps)


def _moba_and_scores(qu_p, k_p, v_p, qu_s, cache_k, page_table, layer, tps=8):
    _, bs, ts, _ = qu_s.shape
    bp, tp, _ = k_p.shape
    blk = MOBA_BLOCK
    n_pages = page_table.shape[1]
    ps = cache_k.shape[2]
    past = n_pages * ps
    chunk = 8 * blk
    assert past % chunk == 0 and past // blk >= MOBA_TOPK and n_pages % DMA_PRIORITIES == 0
    tps = min(tps, tp // blk)
    assert tp % (blk * tps) == 0
    nb = tp // blk
    groups = nb // tps
    n_steps, items = bs * N_HEADS, bp * N_HEADS * groups
    assert n_steps % items == 0
    every = n_steps // items

    def prompt_idx(step):
        item = step // every
        return item // (N_HEADS * groups), (item // groups) % N_HEADS, item % groups

    def qp_map(step, pt):
        bi, h, g = prompt_idx(step)
        return (0, bi, g, h)

    def kv_map(step, pt):
        bi, h, g = prompt_idx(step)
        return (bi, 0, h)

    def o_map(step, pt):
        bi, h, g = prompt_idx(step)
        return (bi, g, h)

    grid_spec = pltpu.PrefetchScalarGridSpec(
        num_scalar_prefetch=1,
        grid=(n_steps,),
        in_specs=[pl.BlockSpec((None, 1, ts, HEAD_DIM), lambda s, pt: (0, s // N_HEADS, 0, s % N_HEADS)),
                  pl.BlockSpec(memory_space=pl.ANY),
                  pl.BlockSpec((None, 1, blk * tps, HEAD_DIM), qp_map),
                  pl.BlockSpec((1, tp, HEAD_DIM), kv_map),
                  pl.BlockSpec((1, tp, HEAD_DIM), kv_map)],
        out_specs=[pl.BlockSpec((1, 1, past // blk, ts, blk), lambda s, pt: (s // N_HEADS, s % N_HEADS, 0, 0, 0)),
                   pl.BlockSpec((1, N_HEADS, ts, 128), lambda s, pt: (s // N_HEADS, 0, 0, 0)),
                   pl.BlockSpec((1, blk * tps, HEAD_DIM), o_map)],
        scratch_shapes=[pltpu.VMEM((2, past, HEAD_DIM), F32),
                        pltpu.VMEM((past // blk, HEAD_DIM), F32),
                        pltpu.VMEM((N_HEADS, ts, past // blk), F32),
                        pltpu.SemaphoreType.DMA((2,)),
                        pltpu.VMEM((tp, HEAD_DIM), BF16), pltpu.VMEM((nb, HEAD_DIM, blk), BF16),
                        pltpu.VMEM((nb, HEAD_DIM), F32)],
    )
    scores, sel, attn = pl.pallas_call(
        functools.partial(_moba_scores_kernel, layer=layer, n_pages=n_pages, chunk=chunk, nb=nb, tps=tps,
                          every=every),
        grid_spec=grid_spec,
        out_shape=[jax.ShapeDtypeStruct((bs, N_HEADS, past // blk, ts, blk), F32),
                   jax.ShapeDtypeStruct((bs, N_HEADS, ts, 128), jnp.int32),
                   jax.ShapeDtypeStruct((bp, tp, ATTN_WIDTH), F32)],
        compiler_params=_params("arbitrary"),
        name="moba_and_scores",
    )(page_table.reshape(-1), qu_s, cache_k, qu_p, k_p, v_p)
    return attn, scores, sel


def _sample_attend_kernel(pt_ref, sel_ref, s_ref, q_ref, kn_ref, vn_ref, vc_ref, o_ref,
                          vbuf_ref, ssel_ref, sem_ref, *, layer, n_pages, q_start):
    b = pl.program_id(0)
    h = pl.program_id(1)
    nbt, nh = pl.num_programs(0), pl.num_programs(1)
    step = b * nh + h
    slot = step % 2
    t = q_ref.shape[1]
    ps = vc_ref.shape[2]
    blk = MOBA_BLOCK
    ppb = blk // ps
    nsel = t * MOBA_TOPK

    def sel_block(st, e):
        return sel_ref[st * nsel + e]

    def copies(st, sl):
        bb, hh = st // nh, st % nh

        def mk(e, pg):
            return _head_page_copy(vc_ref, pt_ref, vbuf_ref.at[sl], sem_ref.at[sl], layer, bb, hh,
                                   sel_block(st, e) * ppb + pg, e * blk + pg * ps, n_pages)
        return mk

    def start_all(st, sl):
        mk = copies(st, sl)
        for e in range(nsel):
            for pg in range(ppb):
                mk(e, pg).start(priority=(e * ppb + pg) % DMA_PRIORITIES)

    @pl.when(step == 0)
    def _():
        start_all(step, slot)

    @pl.when(step + 1 < nbt * nh)
    def _():
        start_all(step + 1, 1 - slot)

    scale = HEAD_DIM ** -0.5
    slope = jnp.exp2(jnp.broadcast_to(-8.0 * (h + 1).astype(F32) / N_HEADS, (1, 1)))
    lane = lax.broadcasted_iota(jnp.int32, (1, blk), 1)

    ssel_ref[...] = jnp.full(ssel_ref.shape, NEG_INF, F32)
    for e in range(nsel):
        ti = e // MOBA_TOPK
        n = sel_block(step, e)
        raw = s_ref[0, 0, n, ti:ti + 1, :]
        dist = (q_start + ti - n * blk - lane).astype(F32)
        ssel_ref[ti:ti + 1, e * blk:(e + 1) * blk] = raw * scale - slope * dist

    q = q_ref[0].astype(BF16)
    r = lax.broadcasted_iota(jnp.int32, (t, t), 0)
    c = lax.broadcasted_iota(jnp.int32, (t, t), 1)
    s_own = lax.dot_general(q, kn_ref[0].astype(BF16), _NT, preferred_element_type=F32) * scale
    s_own = jnp.where(c <= r, s_own - slope * (r - c).astype(F32), NEG_INF)

    s_sel = ssel_ref[...]
    m = jnp.maximum(jnp.max(s_sel, axis=1, keepdims=True), jnp.max(s_own, axis=1, keepdims=True))
    p_sel = jnp.exp(s_sel - m)
    p_own = jnp.exp(s_own - m)
    l = jnp.sum(p_sel, axis=1, keepdims=True) + jnp.sum(p_own, axis=1, keepdims=True)

    mk = copies(step, slot)
    for e in range(nsel):
        for pg in range(ppb):
            mk(e, pg).wait()

    acc = jnp.dot(p_sel.astype(BF16), vbuf_ref[slot].astype(BF16), preferred_element_type=F32)
    acc = acc + jnp.dot(p_own.astype(BF16), vn_ref[0].astype(BF16), preferred_element_type=F32)
    o_ref[0] = acc / l


def _sample_attend(scores, sel, qu, k_new, v_new, cache_v, page_table, layer):
    b, t, _ = k_new.shape
    n_pages = page_table.shape[1]
    ps = cache_v.shape[2]
    past = n_pages * ps
    assert MOBA_BLOCK % ps == 0 and past % MOBA_BLOCK == 0 and t <= MOBA_BLOCK
    nsel = t * MOBA_TOPK
    grid_spec = pltpu.PrefetchScalarGridSpec(
        num_scalar_prefetch=2,
        grid=(b, N_HEADS),
        in_specs=[pl.BlockSpec((1, 1, past // MOBA_BLOCK, t, MOBA_BLOCK), lambda bi, h, pt, sl: (bi, h, 0, 0, 0)),
                  pl.BlockSpec((None, 1, t, HEAD_DIM), lambda bi, h, pt, sl: (0, bi, 0, h)),
                  pl.BlockSpec((1, t, HEAD_DIM), lambda bi, h, pt, sl: (bi, 0, h)),
                  pl.BlockSpec((1, t, HEAD_DIM), lambda bi, h, pt, sl: (bi, 0, h)),
                  pl.BlockSpec(memory_space=pl.ANY)],
        out_specs=pl.BlockSpec((1, t, HEAD_DIM), lambda bi, h, pt, sl: (bi, 0, h)),
        scratch_shapes=[pltpu.VMEM((2, nsel * MOBA_BLOCK, HEAD_DIM), F32),
                        pltpu.VMEM((t, nsel * MOBA_BLOCK), F32),
                        pltpu.SemaphoreType.DMA((2,))],
    )
    return pl.pallas_call(
        functools.partial(_sample_attend_kernel, layer=layer, n_pages=n_pages, q_start=past),
        grid_spec=grid_spec,
        out_shape=jax.ShapeDtypeStruct((b, t, ATTN_WIDTH), F32),
        compiler_params=_params("arbitrary", "arbitrary"),
        name="sample_attend",
    )(page_table.reshape(-1), sel[..., :MOBA_TOPK].reshape(-1), scores, qu, k_new, v_new, cache_v)


def _conv_taps():
    pad = CONV_HALO - (CONV_WIDTH - 1)
    return [(ph, [(j, (pad + j) // 8) for j in range(CONV_WIDTH) if (pad + j) % 8 == ph]) for ph in range(8)]


def _conv_kernel(ua_ref, ug_ref, prev_ref, w_ref, b_ref, g_ref, bl_ref, y_ref, st_ref, buf_ref, par_ref, z_ref,
                 *, tt, rows, nrows, lanes):
    ti = pl.program_id(1)
    ch = y_ref.shape[2]
    hist = CONV_WIDTH - 1
    pad = CONV_HALO - hist
    i_bias, i_gain, i_beta = CONV_WIDTH, CONV_WIDTH + 1, CONV_WIDTH + 2

    @pl.when(ti == 0)
    def _():
        buf_ref[0:CONV_HALO, :] = jnp.zeros((CONV_HALO, ch), F32)
        buf_ref[pad:CONV_HALO, :] = prev_ref[0]
        for j in range(CONV_WIDTH):
            par_ref[j] = jnp.broadcast_to(w_ref[j:j + 1, :], (8, ch))
        par_ref[i_bias] = jnp.broadcast_to(b_ref[...], (8, ch))
        par_ref[i_gain] = jnp.broadcast_to(g_ref[...], (8, ch))
        par_ref[i_beta] = jnp.broadcast_to(bl_ref[...], (8, ch))

    @pl.when(ti > 0)
    def _():
        buf_ref[0:CONV_HALO, :] = buf_ref[tt:tt + CONV_HALO, :]

    def glu(ci, carry):
        r0 = pl.multiple_of(ci * rows, rows)
        buf_ref[pl.ds(CONV_HALO + r0, rows), :] = (ua_ref[0, pl.ds(r0, rows), :]
                                                   * jax.nn.sigmoid(ug_ref[0, pl.ds(r0, rows), :]))
        return carry

    lax.fori_loop(0, tt // rows, glu, 0)

    def conv(ci, carry):
        r0 = pl.multiple_of(ci * rows, rows)
        for c0 in range(0, ch, lanes):
            cs = slice(c0, c0 + lanes)
            win = buf_ref.at[pl.ds(r0, rows + CONV_HALO), pl.ds(c0, lanes)]
            acc = jnp.broadcast_to(par_ref[i_bias, :, cs][None], (rows // 8, 8, lanes))
            for ph, taps in _conv_taps():
                a_lo, a_hi = taps[0][1], taps[-1][1]
                x = win[8 * a_lo + ph:8 * a_hi + ph + rows, :]
                for j, a in taps:
                    xs = x[8 * (a - a_lo):8 * (a - a_lo) + rows, :].reshape(rows // 8, 8, lanes)
                    acc = acc + par_ref[j, :, cs][None] * xs
            z_ref[pl.ds(r0, rows), cs] = acc.reshape(rows, lanes)
        return carry

    lax.fori_loop(0, tt // rows, conv, 0)

    def norm(ci, carry):
        r0 = pl.multiple_of(ci * nrows, nrows)
        z = z_ref[pl.ds(r0, nrows), :]
        xc = z - jnp.mean(z, axis=-1, keepdims=True)
        xn = (xc * lax.rsqrt(jnp.mean(xc * xc, axis=-1, keepdims=True) + EPS)).reshape(nrows // 8, 8, ch)
        y = (xn * par_ref[i_gain][None] + par_ref[i_beta][None]).reshape(nrows, ch)
        y_ref[0, pl.ds(r0, nrows), :] = y * jax.nn.sigmoid(y)
        return carry

    lax.fori_loop(0, tt // nrows, norm, 0, unroll=min(8, tt // nrows))

    @pl.when(ti == pl.num_programs(1) - 1)
    def _():
        st_ref[0] = buf_ref[tt + pad:tt + CONV_HALO, :]


def _conformer_conv(qu, prev, layer, w_dw, b_dw, g_ln, b_ln, tt):
    _, b, t, ch = qu.shape
    assert ch == w_dw.shape[1]
    hist = CONV_WIDTH - 1
    assert t % tt == 0 and tt % 8 == 0
    rows = 64 if tt % 64 == 0 else 8
    nrows = 16 if tt % 16 == 0 else 8
    lanes = 256 if ch % 256 == 0 else ch
    row = lambda a: a.reshape(1, ch)
    return pl.pallas_call(
        functools.partial(_conv_kernel, tt=tt, rows=rows, nrows=nrows, lanes=lanes),
        grid=(b, t // tt),
        in_specs=[pl.BlockSpec((None, 1, tt, ch), lambda bi, ti: (1, bi, ti, 0)),
                  pl.BlockSpec((None, 1, tt, ch), lambda bi, ti: (2, bi, ti, 0)),
                  pl.BlockSpec((None, 1, hist, ch), lambda bi, ti: (layer, bi, 0, 0)),
                  pl.BlockSpec((CONV_WIDTH, ch), lambda bi, ti: (0, 0)),
                  pl.BlockSpec((1, ch), lambda bi, ti: (0, 0)),
                  pl.BlockSpec((1, ch), lambda bi, ti: (0, 0)),
                  pl.BlockSpec((1, ch), lambda bi, ti: (0, 0))],
        out_specs=[pl.BlockSpec((1, tt, ch), lambda bi, ti: (bi, ti, 0)),
                   pl.BlockSpec((1, hist, ch), lambda bi, ti: (bi, 0, 0))],
        out_shape=[jax.ShapeDtypeStruct((b, t, ch), F32), jax.ShapeDtypeStruct((b, hist, ch), F32)],
        scratch_shapes=[pltpu.VMEM((tt + CONV_HALO, ch), F32), pltpu.VMEM((CONV_WIDTH + 3, 8, ch), F32),
                        pltpu.VMEM((tt, ch), F32)],
        compiler_params=_params("arbitrary", "arbitrary"),
        name="conformer_conv",
    )(qu, qu, prev, w_dw, row(b_dw), row(g_ln), row(b_ln))


def _mix_out_kernel(attn_ref, conv_ref, h_ref, ga_ref, gc_ref, w_ref, gp_ref, o_ref):
    wa = attn_ref.shape[1]
    a = _rms(attn_ref[...], ga_ref[...]).astype(BF16)
    cv = _rms(conv_ref[...], gc_ref[...]).astype(BF16)
    mixed = (jnp.dot(a, w_ref[0:wa, :], preferred_element_type=F32)
             + jnp.dot(cv, w_ref[wa:, :], preferred_element_type=F32))
    o_ref[...] = h_ref[...] + _rms(mixed, gp_ref[...])


def _mix_out(attn, conv, h, g_attn, g_conv, w_out, g_post, tm):
    m, wa = attn.shape
    wc = conv.shape[1]
    d = h.shape[1]
    assert m % tm == 0
    row = lambda a: a.reshape(1, -1)
    return pl.pallas_call(
        _mix_out_kernel,
        grid=(m // tm,),
        in_specs=[pl.BlockSpec((tm, wa), lambda i: (i, 0)),
                  pl.BlockSpec((tm, wc), lambda i: (i, 0)),
                  pl.BlockSpec((tm, d), lambda i: (i, 0)),
                  pl.BlockSpec((1, wa), lambda i: (0, 0)),
                  pl.BlockSpec((1, wc), lambda i: (0, 0)),
                  pl.BlockSpec((wa + wc, d), lambda i: (0, 0)),
                  pl.BlockSpec((1, d), lambda i: (0, 0))],
        out_specs=pl.BlockSpec((tm, d), lambda i: (i, 0)),
        out_shape=jax.ShapeDtypeStruct((m, d), F32),
        compiler_params=_params("arbitrary"),
        name="mix_out",
    )(attn, conv, h, row(g_attn), row(g_conv), w_out, row(g_post))


def _xattn_kernel(h_ref, mk_ref, mv_ref, gpre_ref, wq_ref, wo_ref, gpost_ref, o_ref):
    h = h_ref[...]
    xn = _rms(h, gpre_ref[...]).astype(BF16)
    xq = jnp.dot(xn, wq_ref[...], preferred_element_type=F32)
    scale = X_HEAD_DIM ** -0.5
    outs = []
    for hh in range(X_HEADS):
        cols = slice(hh * X_HEAD_DIM, (hh + 1) * X_HEAD_DIM)
        s = lax.dot_general(xq[:, cols].astype(BF16), mk_ref[0, :, cols].astype(BF16), _NT,
                            preferred_element_type=F32) * scale
        p = jnp.exp(s - jnp.max(s, axis=1, keepdims=True))
        l = jnp.sum(p, axis=1, keepdims=True)
        outs.append(jnp.dot(p.astype(BF16), mv_ref[0, :, cols].astype(BF16), preferred_element_type=F32) / l)
    o = jnp.concatenate(outs, axis=1).astype(BF16)
    y = jnp.dot(o, wo_ref[...], preferred_element_type=F32)
    o_ref[...] = h + _rms(y, gpost_ref[...])


def _xattn(h, mem_k, mem_v, layer, g_pre, w_xq, w_xo, g_post, rows_per_batch, tm):
    m, d = h.shape
    assert rows_per_batch % tm == 0
    tiles = rows_per_batch // tm
    mem_len, xw = mem_k.shape[2:]
    row = lambda a: a.reshape(1, -1)
    return pl.pallas_call(
        _xattn_kernel,
        grid=(m // tm,),
        in_specs=[pl.BlockSpec((tm, d), lambda i: (i, 0)),
                  pl.BlockSpec((None, 1, mem_len, xw), lambda i: (layer, i // tiles, 0, 0)),
                  pl.BlockSpec((None, 1, mem_len, xw), lambda i: (layer, i // tiles, 0, 0)),
                  pl.BlockSpec((1, d), lambda i: (0, 0)),
                  pl.BlockSpec((d, xw), lambda i: (0, 0)),
                  pl.BlockSpec((xw, d), lambda i: (0, 0)),
                  pl.BlockSpec((1, d), lambda i: (0, 0))],
        out_specs=pl.BlockSpec((tm, d), lambda i: (i, 0)),
        out_shape=jax.ShapeDtypeStruct((m, d), F32),
        compiler_params=_params("arbitrary"),
        name="xattn",
    )(h, mem_k, mem_v, row(g_pre), w_xq, w_xo, row(g_post))


def _ffn_kernel(h_ref, gpre_ref, wg_ref, wu_ref, wo_ref, gpost_ref, o_ref, *rest, emit):
    if emit:
        wg_out, wu_out, wo_out, xn_ref, acc_ref = rest
        wg_out[...] = wg_ref[...].astype(BF16)
        wu_out[...] = wu_ref[...].astype(BF16)
        wo_out[...] = wo_ref[...].astype(BF16)
        wg_ref, wu_ref, wo_ref = wg_out, wu_out, wo_out
    else:
        xn_ref, acc_ref = rest
    f = pl.program_id(1)

    @pl.when(f == 0)
    def _():
        xn_ref[...] = _rms(h_ref[...], gpre_ref[...]).astype(BF16)
        acc_ref[...] = jnp.zeros(acc_ref.shape, F32)

    xn = xn_ref[...]
    g = jnp.dot(xn, wg_ref[...], preferred_element_type=F32)
    u = jnp.dot(xn, wu_ref[...], preferred_element_type=F32)
    a = (g * jax.nn.sigmoid(g) * u).astype(BF16)
    acc_ref[...] += jnp.dot(a, wo_ref[...], preferred_element_type=F32)

    @pl.when(f == pl.num_programs(1) - 1)
    def _():
        o_ref[...] = h_ref[...] + _rms(acc_ref[...], gpost_ref[...])


def _ffn(h, g_pre, w, g_post, tm, tf, emit):
    m, d = h.shape
    row = lambda a: a.reshape(1, -1)
    if emit:
        w_in, w_out = w
        hidden = w_out.shape[0]
        nf = hidden // tf
        assert m == tm and hidden % tf == 0 and w_in.shape[1] == 2 * hidden
        w_args = (w_in, w_in, w_out)
        w_specs = [pl.BlockSpec((d, tf), lambda i, f: (0, f)),
                   pl.BlockSpec((d, tf), lambda i, f: (0, f + nf)),
                   pl.BlockSpec((tf, d), lambda i, f: (f, 0))]
    else:
        w_args = w
        nf, hidden = w[0].shape[0], w[2].shape[0]
        assert m % tm == 0 and w[0].shape == w[1].shape == (nf, d, tf) and hidden == nf * tf
        w_specs = [pl.BlockSpec((None, d, tf), lambda i, f: (f, 0, 0)),
                   pl.BlockSpec((None, d, tf), lambda i, f: (f, 0, 0)),
                   pl.BlockSpec((tf, d), lambda i, f: (f, 0))]
    out_specs = [pl.BlockSpec((tm, d), lambda i, f: (i, 0))]
    out_shape = [jax.ShapeDtypeStruct((m, d), F32)]
    if emit:
        out_specs += [pl.BlockSpec((None, d, tf), lambda i, f: (f, 0, 0)),
                      pl.BlockSpec((None, d, tf), lambda i, f: (f, 0, 0)),
                      pl.BlockSpec((tf, d), lambda i, f: (f, 0))]
        out_shape += [jax.ShapeDtypeStruct((nf, d, tf), BF16), jax.ShapeDtypeStruct((nf, d, tf), BF16),
                      jax.ShapeDtypeStruct((hidden, d), BF16)]
    outs = pl.pallas_call(
        functools.partial(_ffn_kernel, emit=emit),
        grid=(m // tm, nf),
        in_specs=[pl.BlockSpec((tm, d), lambda i, f: (i, 0)),
                  pl.BlockSpec((1, d), lambda i, f: (0, 0)),
                  *w_specs,
                  pl.BlockSpec((1, d), lambda i, f: (0, 0))],
        out_specs=out_specs,
        out_shape=out_shape,
        scratch_shapes=[pltpu.VMEM((tm, d), BF16), pltpu.VMEM((tm, d), F32)],
        compiler_params=_params("arbitrary", "arbitrary"),
        name="ffn",
    )(h, row(g_pre), *w_args, row(g_post))
    return outs[0], tuple(outs[1:])


def _row_tile(m):
    return 512 if m % 512 == 0 else m


def _project_in(x, wts, w_in, emit):
    b, t, d = x.shape
    m = b * t
    assert wts["w_dw"].shape[1] == ATTN_WIDTH
    k_new, v_new, qu, *w_in_b = _in_proj(x.reshape(m, d), wts["g_pre_mix"].reshape(1, d), w_in, emit,
                                         tm=1024 if m % 1024 == 0 else _row_tile(m))
    k_new, v_new = k_new.reshape(b, t, ATTN_WIDTH), v_new.reshape(b, t, ATTN_WIDTH)
    return k_new, v_new, qu.reshape(qu.shape[0], b, t, ATTN_WIDTH), (w_in_b[0] if emit else None)


def _after_attention(x, attn, qu, state, state_layer, wts, w_ffn, emit, conv_tile, x_tile):
    conv_prev, mem_k, mem_v = state
    b, t, d = x.shape
    m = b * t
    tm = _row_tile(m)
    ch = wts["w_dw"].shape[1]
    conv, conv_state = _conformer_conv(qu, conv_prev, state_layer, wts["w_dw"], wts["b_dw"],
                                       wts["g_ln_conv"], wts["b_ln_conv"], conv_tile)
    h = _mix_out(attn.reshape(m, ATTN_WIDTH), conv.reshape(m, ch), x.reshape(m, d), wts["g_attn_grp"],
                 wts["g_conv_grp"], wts["w_out"], wts["g_post_mix"], tm=tm)
    h = _xattn(h, mem_k, mem_v, state_layer, wts["g_pre_x"], wts["w_xq"], wts["w_xo"], wts["g_post_x"],
               rows_per_batch=t, tm=x_tile)
    h, w_ffn_b = _ffn(h, wts["g_pre_ffn"], w_ffn, wts["g_post_ffn"], tm=tm, tf=512, emit=emit)
    return h.reshape(b, t, d), conv_state, (w_ffn_b if emit else None)


def kernel(x_prompt, x_sample, cache_k, cache_v, state_conv, cache_mem_k, cache_mem_v, page_table,
           mem_prompt, g_mem, w_mem_k, w_mem_v, g_pre_mix, w_in, w_dw, b_dw, g_ln_conv, b_ln_conv,
           g_attn_grp, g_conv_grp, w_out, g_post_mix, g_pre_x, w_xq, w_xo, g_post_x,
           g_pre_ffn, w_ffn_in, w_ffn_out, g_post_ffn):
    depth = w_in.shape[0]
    bp, tp, d = x_prompt.shape
    bs, ts, _ = x_sample.shape
    mem_len = mem_prompt.shape[1]
    xw = X_HEADS * X_HEAD_DIM
    ch = w_dw.shape[2]
    hp, hs = x_prompt, x_sample
    outs = [[] for _ in range(8)]
    for l in range(depth):
        wts = dict(g_pre_mix=g_pre_mix[l], w_dw=w_dw[l], b_dw=b_dw[l],
                   g_ln_conv=g_ln_conv[l], b_ln_conv=b_ln_conv[l], g_attn_grp=g_attn_grp[l],
                   g_conv_grp=g_conv_grp[l], w_out=w_out[l].astype(BF16), g_post_mix=g_post_mix[l],
                   g_pre_x=g_pre_x[l], w_xq=w_xq[l].astype(BF16), w_xo=w_xo[l].astype(BF16),
                   g_post_x=g_post_x[l], g_pre_ffn=g_pre_ffn[l], g_post_ffn=g_post_ffn[l])

        ks, vs, qu_s, w_in_b = _project_in(hs, wts, w_in[l], True)
        kp, vp, qu_p, _ = _project_in(hp, wts, w_in_b, False)
        attn_p, scores, sel = _moba_and_scores(qu_p, kp, vp, qu_s, cache_k, page_table, l)
        attn_s = _sample_attend(scores, sel, qu_s, ks, vs, cache_v, page_table, l)

        sample_state = (state_conv, cache_mem_k.reshape(depth, bs, mem_len, xw),
                        cache_mem_v.reshape(depth, bs, mem_len, xw))
        hs, cs, w_ffn_b = _after_attention(hs, attn_s, qu_s, sample_state, l, wts, (w_ffn_in[l], w_ffn_out[l]),
                                           True, conv_tile=ts, x_tile=ts)

        w_mem = jnp.concatenate([w_mem_k[l], w_mem_v[l]], axis=1).astype(BF16)
        mk_p, mv_p = _norm_matmul(mem_prompt.reshape(bp * mem_len, d), g_mem[l].reshape(1, d), w_mem,
                                  (xw, xw), tn=xw, tm=_row_tile(bp * mem_len))
        mk_p, mv_p = mk_p.reshape(bp, mem_len, xw), mv_p.reshape(bp, mem_len, xw)
        prompt_state = (jnp.zeros((1, bp, CONV_WIDTH - 1, ch), F32), mk_p[None], mv_p[None])
        hp, cp, _ = _after_attention(hp, attn_p, qu_p, prompt_state, 0, wts, w_ffn_b, False,
                                     conv_tile=_row_tile(tp), x_tile=_row_tile(tp))
        mem_shape = (bp, mem_len, X_HEADS, X_HEAD_DIM)
        heads = lambda a: a.reshape(a.shape[:2] + (N_HEADS, HEAD_DIM))
        for lst, a in zip(outs, (heads(kp), heads(vp), cp, mk_p.reshape(mem_shape), mv_p.reshape(mem_shape),
                                 heads(ks), heads(vs), cs)):
            lst.append(a)
    return (hp, hs) + tuple(jnp.stack(lst, 0) for lst in outs)
```

```python
import functools

import jax
import jax.numpy as jnp
from jax import lax
from jax.experimental import pallas as pl
from jax.experimental.pallas import tpu as pltpu

EPS = 1e-6
N_HEADS = 8
HEAD_DIM = 128
ATTN_WIDTH = N_HEADS * HEAD_DIM
CONV_WIDTH = 31
MOBA_BLOCK = 256
MOBA_TOPK = 3
X_HEADS = 4
X_HEAD_DIM = 128

CONV_HALO = 32
DMA_PRIORITIES = 2
VMEM_LIMIT = 56 * 1024 * 1024

F32 = jnp.float32
BF16 = jnp.bfloat16
NEG_INF = float("-inf")
LOG2_E = 1.4426950408889634
_NT = (((1,), (1,)), ((), ()))


def _params(*sem):
    return pltpu.CompilerParams(dimension_semantics=sem, vmem_limit_bytes=VMEM_LIMIT)


def _rms(x, g):
    return x * lax.rsqrt(jnp.mean(x * x, axis=-1, keepdims=True) + EPS) * g


def _top_blocks(gate, n_valid, axis):
    nb = gate.shape[axis]
    pos = lax.broadcasted_iota(jnp.int32, gate.shape, axis)
    g = jnp.where(pos < n_valid, gate, NEG_INF)
    idxs, oks = [], []
    for _ in range(MOBA_TOPK):
        m = jnp.max(g, axis=axis, keepdims=True)
        idx = jnp.min(jnp.where(g == m, pos, nb), axis=axis, keepdims=True)
        ok = m > NEG_INF
        idxs.append(idx)
        oks.append(ok)
        g = jnp.where((pos == idx) & ok, NEG_INF, g)
    return idxs, oks


def _norm_matmul_kernel(x_ref, g_ref, w_ref, *rest, tile_ranges):
    out_refs, xn_ref = rest[:-1], rest[-1]
    j = pl.program_id(1)

    @pl.when(j == 0)
    def _():
        xn_ref[...] = _rms(x_ref[...], g_ref[...]).astype(BF16)

    for o_ref, (lo, hi) in zip(out_refs, tile_ranges):
        @pl.when((j >= lo) & (j < hi))
        def _(o_ref=o_ref):
            o_ref[...] = jnp.dot(xn_ref[...], w_ref[...], preferred_element_type=F32)


def _norm_matmul(x, g, w, widths, tn, tm):
    m, d = x.shape
    n = w.shape[1]
    assert sum(widths) == n and all(wd % tn == 0 for wd in widths) and m % tm == 0
    tile_ranges, lo = [], 0
    for wd in widths:
        tile_ranges.append((lo, lo + wd // tn))
        lo += wd // tn

    def out_map(i, j, lo, cnt):
        return (i, jnp.clip(j - lo, 0, cnt - 1))

    out_specs = [pl.BlockSpec((tm, tn), functools.partial(out_map, lo=lo, cnt=hi - lo))
                 for lo, hi in tile_ranges]
    return pl.pallas_call(
        functools.partial(_norm_matmul_kernel, tile_ranges=tuple(tile_ranges)),
        grid=(m // tm, n // tn),
        in_specs=[pl.BlockSpec((tm, d), lambda i, j: (i, 0)),
                  pl.BlockSpec((1, d), lambda i, j: (0, 0)),
                  pl.BlockSpec((d, tn), lambda i, j: (0, j))],
        out_specs=out_specs,
        out_shape=[jax.ShapeDtypeStruct((m, wd), F32) for wd in widths],
        scratch_shapes=[pltpu.VMEM((tm, d), BF16)],
        compiler_params=_params("arbitrary", "arbitrary"),
        name="norm_matmul",
    )(x, g, w)


def _in_proj_src_tile(j):
    return jnp.where(j < 3, (j + 1) % 3, j)


def _in_proj_kernel(x_ref, g_ref, w_ref, *rest, emit):
    if emit:
        k_ref, v_ref, qu_ref, wt_ref, xn_ref = rest
    else:
        k_ref, v_ref, qu_ref, xn_ref = rest
    j = pl.program_id(1)

    @pl.when(j == 0)
    def _():
        xn_ref[...] = _rms(x_ref[...], g_ref[...]).astype(BF16)

    if emit:
        wt_ref[...] = w_ref[...].astype(BF16)
        w_ref = wt_ref

    for o_ref, cond in ((k_ref, j == 0), (v_ref, j == 1), (qu_ref, j >= 2)):
        @pl.when(cond)
        def _(o_ref=o_ref):
            o_ref[...] = jnp.dot(xn_ref[...], w_ref[...], preferred_element_type=F32)


def _in_proj(x, g, w, emit, tm):
    m, d = x.shape
    tn = ATTN_WIDTH
    nt = (w.shape[1] // tn) if emit else w.shape[0]
    assert m % tm == 0 and nt > 3 and (not emit or (m == tm and w.shape[1] % tn == 0))
    if emit:
        w_spec = pl.BlockSpec((d, tn), lambda i, j: (0, _in_proj_src_tile(j)))
    else:
        w_spec = pl.BlockSpec((None, d, tn), lambda i, j: (j, 0, 0))
    out_specs = [pl.BlockSpec((tm, tn), lambda i, j: (i, 0)),
                 pl.BlockSpec((tm, tn), lambda i, j: (i, 0)),
                 pl.BlockSpec((None, tm, tn), lambda i, j: (jnp.maximum(j - 2, 0), i, 0))]
    out_shape = [jax.ShapeDtypeStruct((m, tn), F32), jax.ShapeDtypeStruct((m, tn), F32),
                 jax.ShapeDtypeStruct((nt - 2, m, tn), F32)]
    if emit:
        out_specs.append(pl.BlockSpec((None, d, tn), lambda i, j: (j, 0, 0)))
        out_shape.append(jax.ShapeDtypeStruct((nt, d, tn), BF16))
    return pl.pallas_call(
        functools.partial(_in_proj_kernel, emit=emit),
        grid=(m // tm, nt),
        in_specs=[pl.BlockSpec((tm, d), lambda i, j: (i, 0)),
                  pl.BlockSpec((1, d), lambda i, j: (0, 0)),
                  w_spec],
        out_specs=out_specs,
        out_shape=out_shape,
        scratch_shapes=[pltpu.VMEM((tm, d), BF16)],
        compiler_params=_params("arbitrary", "arbitrary"),
        name="in_proj",
    )(x, g, w)


def _moba_prompt_body(h, g, q_ref, k_ref, v_ref, o_ref, kb_ref, vt_ref, km_ref, *, nb, tps):
    blk = MOBA_BLOCK

    @pl.when(g == 0)
    def _():
        for n in range(nb):
            kf = k_ref[0, n * blk:(n + 1) * blk, :]
            kb_ref[n * blk:(n + 1) * blk, :] = kf.astype(BF16)
            vt_ref[n] = v_ref[0, n * blk:(n + 1) * blk, :].T.astype(BF16)
            km_ref[n:n + 1, :] = jnp.mean(kf, axis=0, keepdims=True)

    scale = HEAD_DIM ** -0.5 * LOG2_E
    slope = jnp.exp2(jnp.broadcast_to(-8.0 * (h + 1).astype(F32) / N_HEADS, (1, blk))) * LOG2_E

    def attend(i, qi):
        q = q_ref[0, qi * blk:(qi + 1) * blk, :]
        qb = q.astype(BF16)
        kr = lax.broadcasted_iota(jnp.int32, (blk, blk), 0)
        qc = lax.broadcasted_iota(jnp.int32, (blk, blk), 1)
        base = slope * (qc - kr).astype(F32)
        s_own = lax.dot_general(kb_ref[i * blk:(i + 1) * blk, :], qb, _NT, preferred_element_type=F32)
        ts = [jnp.where(kr <= qc, s_own * scale - base, NEG_INF)]
        if i:
            gate = lax.dot_general(km_ref[0:i, :], q, _NT, precision=lax.Precision.HIGHEST,
                                   preferred_element_type=F32)
            idxs, oks = _top_blocks(gate, i, axis=0)
            row = lax.broadcasted_iota(jnp.int32, (i, blk), 0)
            sel = jnp.zeros((i, blk), F32)
            for idx, ok in zip(idxs, oks):
                sel = jnp.where((row == idx) & ok, 1.0, sel)
            for n in range(i):
                s = lax.dot_general(kb_ref[n * blk:(n + 1) * blk, :], qb, _NT, preferred_element_type=F32)
                neg = jnp.where(sel[n:n + 1, :] > 0.0, -slope * float((i - n) * blk), NEG_INF)
                ts.append(s * scale - base + neg)
        m = jnp.max(functools.reduce(jnp.maximum, ts), axis=0, keepdims=True)
        acc = None
        psum = None
        for n, t in enumerate(ts):
            p = jnp.exp2(t - m)
            psum = p if psum is None else psum + p
            vt = vt_ref[i] if n == 0 else vt_ref[n - 1]
            pv = jnp.dot(vt, p.astype(BF16), preferred_element_type=F32)
            acc = pv if acc is None else acc + pv
        l = jnp.sum(psum, axis=0, keepdims=True)
        o_ref[0, qi * blk:(qi + 1) * blk, :] = (acc / l).T

    for v in range(nb // tps):
        @pl.when(g == v)
        def _(v=v):
            for qi in range(tps):
                attend(v * tps + qi, qi)


def _head_page_copy(cache_ref, pt_ref, buf_ref, sem, layer, b, h, page_slot, dst_row, n_pages):
    page = pt_ref[b * n_pages + page_slot]
    ps = cache_ref.shape[2]
    return pltpu.make_async_copy(cache_ref.at[layer, page, :, h, :], buf_ref.at[pl.ds(dst_row, ps), :], sem)


def _start_k_pages(st, first, count, pt_ref, kc_ref, kbuf_ref, sem_ref, *, layer, n_pages):
    bb, hh = st // N_HEADS, st % N_HEADS
    sl = st % 2
    ps = kc_ref.shape[2]

    def issue(pair, carry):
        for k in range(DMA_PRIORITIES):
            p = first + pair * DMA_PRIORITIES + k
            _head_page_copy(kc_ref, pt_ref, kbuf_ref.at[sl], sem_ref.at[sl], layer, bb, hh,
                            p, pl.multiple_of(p * ps, ps), n_pages).start(priority=k)
        return carry

    lax.fori_loop(0, count // DMA_PRIORITIES, issue, 0, unroll=4)


def _sample_scores_body(step, n_steps, pt_ref, q_ref, kc_ref, s_ref, sel_ref, kbuf_ref, km_ref, gate_ref, sem_ref,
                        *, layer, n_pages, chunk):
    nh = N_HEADS
    b, h = step // nh, step % nh
    slot = step % 2
    ps = kc_ref.shape[2]
    past = n_pages * ps
    blk = MOBA_BLOCK
    t = q_ref.shape[1]
    start_pages = functools.partial(_start_k_pages, pt_ref=pt_ref, kc_ref=kc_ref, kbuf_ref=kbuf_ref,
                                    sem_ref=sem_ref, layer=layer, n_pages=n_pages)

    @pl.when(step == 0)
    def _():
        start_pages(step, 0, n_pages)

    @pl.when(step + 1 < n_steps)
    def _():
        start_pages(step + 1, 0, n_pages)

    pltpu.make_async_copy(kbuf_ref.at[1 - slot], kbuf_ref.at[slot], sem_ref.at[slot]).wait()

    q = q_ref[0]
    qb = q.astype(BF16)
    bpc = chunk // blk

    def body(ci, carry):
        start = pl.multiple_of(ci * chunk, chunk)
        kf = kbuf_ref[slot, pl.ds(start, chunk), :]
        sc = lax.dot_general(qb, kf.astype(BF16), _NT, preferred_element_type=F32)
        for jj in range(bpc):
            s_ref[0, 0, ci * bpc + jj] = sc[:, jj * blk:(jj + 1) * blk]
        km_ref[pl.ds(pl.multiple_of(ci * bpc, bpc), bpc), :] = jnp.mean(
            kf.reshape(bpc, blk, HEAD_DIM), axis=1)
        return carry

    lax.fori_loop(0, past // chunk, body, 0, unroll=2)

    gate_ref[h] = lax.dot_general(q, km_ref[...], _NT, precision=lax.Precision.HIGHEST,
                                  preferred_element_type=F32)

    @pl.when(h == nh - 1)
    def _():
        gate = gate_ref[...].reshape(nh * t, past // blk)
        idxs, _ = _top_blocks(gate, past // blk, axis=1)
        lane = lax.broadcasted_iota(jnp.int32, (nh * t, 128), 1)
        out = jnp.zeros((nh * t, 128), jnp.int32)
        for k, idx in enumerate(idxs):
            out = jnp.where(lane == k, idx, out)
        sel_ref[0] = out.reshape(nh, t, 128)


def _moba_scores_kernel(pt_ref, qs_ref, kc_ref, qp_ref, kp_ref, vp_ref, s_ref, sel_ref, o_ref,
                        kbuf_ref, km_ref, gate_ref, sem_ref, kb_ref, vt_ref, kmp_ref,
                        *, layer, n_pages, chunk, nb, tps, every):
    step = pl.program_id(0)
    n_steps = pl.num_programs(0)
    _sample_scores_body(step, n_steps, pt_ref, qs_ref, kc_ref, s_ref, sel_ref, kbuf_ref, km_ref,
                        gate_ref, sem_ref, layer=layer, n_pages=n_pages, chunk=chunk)

    @pl.when(step % every == 0)
    def _():
        item = step // every
        groups = nb // tps
        _moba_prompt_body((item // groups) % N_HEADS, item % groups, qp_ref, kp_ref, vp_ref, o_ref,
                          kb_ref, vt_ref, kmp_ref, nb=nb, tps=tps)


def _moba_and_scores(qu_p, k_p, v_p, qu_s, cache_k, page_table, layer, tps=4):
    _, bs, ts, _ = qu_s.shape
    bp, tp, _ = k_p.shape
    blk = MOBA_BLOCK
    n_pages = page_table.shape[1]
    ps = cache_k.shape[2]
    past = n_pages * ps
    chunk = 8 * blk
    assert past % chunk == 0 and past // blk >= MOBA_TOPK and n_pages % DMA_PRIORITIES == 0
    tps = min(tps, tp // blk)
    assert tp % (blk * tps) == 0 and n_pages % (tps * DMA_PRIORITIES) == 0
    nb = tp // blk
    groups = nb // tps
    n_steps, items = bs * N_HEADS, bp * N_HEADS * groups
    assert n_steps % items == 0
    every = n_steps // items

    def prompt_idx(step):
        item = step // every
        return item // (N_HEADS * groups), (item // groups) % N_HEADS, item % groups

    def qp_map(step, pt):
        bi, h, g = prompt_idx(step)
        return (0, bi, g, h)

    def kv_map(step, pt):
        bi, h, g = prompt_idx(step)
        return (bi, 0, h)

    def o_map(step, pt):
        bi, h, g = prompt_idx(step)
        return (bi, g, h)

    grid_spec = pltpu.PrefetchScalarGridSpec(
        num_scalar_prefetch=1,
        grid=(n_steps,),
        in_specs=[pl.BlockSpec((None, 1, ts, HEAD_DIM), lambda s, pt: (0, s // N_HEADS, 0, s % N_HEADS)),
                  pl.BlockSpec(memory_space=pl.ANY),
                  pl.BlockSpec((None, 1, blk * tps, HEAD_DIM), qp_map),
                  pl.BlockSpec((1, tp, HEAD_DIM), kv_map),
                  pl.BlockSpec((1, tp, HEAD_DIM), kv_map)],
        out_specs=[pl.BlockSpec((1, 1, past // blk, ts, blk), lambda s, pt: (s // N_HEADS, s % N_HEADS, 0, 0, 0)),
                   pl.BlockSpec((1, N_HEADS, ts, 128), lambda s, pt: (s // N_HEADS, 0, 0, 0)),
                   pl.BlockSpec((1, blk * tps, HEAD_DIM), o_map)],
        scratch_shapes=[pltpu.VMEM((2, past, HEAD_DIM), F32),
                        pltpu.VMEM((past // blk, HEAD_DIM), F32),
                        pltpu.VMEM((N_HEADS, ts, past // blk), F32),
                        pltpu.SemaphoreType.DMA((2,)),
                        pltpu.VMEM((tp, HEAD_DIM), BF16), pltpu.VMEM((nb, HEAD_DIM, blk), BF16),
                        pltpu.VMEM((nb, HEAD_DIM), F32)],
    )
    scores, sel, attn = pl.pallas_call(
        functools.partial(_moba_scores_kernel, layer=layer, n_pages=n_pages, chunk=chunk, nb=nb, tps=tps,
                          every=every),
        grid_spec=grid_spec,
        out_shape=[jax.ShapeDtypeStruct((bs, N_HEADS, past // blk, ts, blk), F32),
                   jax.ShapeDtypeStruct((bs, N_HEADS, ts, 128), jnp.int32),
                   jax.ShapeDtypeStruct((bp, tp, ATTN_WIDTH), F32)],
        compiler_params=_params("arbitrary"),
        name="moba_and_scores",
    )(page_table.reshape(-1), qu_s, cache_k, qu_p, k_p, v_p)
    return attn, scores, sel


def _sample_attend_kernel(pt_ref, sel_ref, s_ref, q_ref, kn_ref, vn_ref, vc_ref, o_ref,
                          vbuf_ref, ssel_ref, sem_ref, *, layer, n_pages, q_start):
    b = pl.program_id(0)
    h = pl.program_id(1)
    nbt, nh = pl.num_programs(0), pl.num_programs(1)
    step = b * nh + h
    slot = step % 2
    t = q_ref.shape[1]
    ps = vc_ref.shape[2]
    blk = MOBA_BLOCK
    ppb = blk // ps
    nsel = t * MOBA_TOPK

    def sel_block(st, e):
        return sel_ref[st * nsel + e]

    def copies(st, sl):
        bb, hh = st // nh, st % nh

        def mk(e, pg):
            return _head_page_copy(vc_ref, pt_ref, vbuf_ref.at[sl], sem_ref.at[sl], layer, bb, hh,
                                   sel_block(st, e) * ppb + pg, e * blk + pg * ps, n_pages)
        return mk

    def start_all(st, sl):
        mk = copies(st, sl)
        for e in range(nsel):
            for pg in range(ppb):
                mk(e, pg).start(priority=(e * ppb + pg) % DMA_PRIORITIES)

    @pl.when(step == 0)
    def _():
        start_all(step, slot)

    @pl.when(step + 1 < nbt * nh)
    def _():
        start_all(step + 1, 1 - slot)

    scale = HEAD_DIM ** -0.5
    slope = jnp.exp2(jnp.broadcast_to(-8.0 * (h + 1).astype(F32) / N_HEADS, (1, 1)))
    lane = lax.broadcasted_iota(jnp.int32, (1, blk), 1)

    ssel_ref[...] = jnp.full(ssel_ref.shape, NEG_INF, F32)
    for e in range(nsel):
        ti = e // MOBA_TOPK
        n = sel_block(step, e)
        raw = s_ref[0, 0, n, ti:ti + 1, :]
        dist = (q_start + ti - n * blk - lane).astype(F32)
        ssel_ref[ti:ti + 1, e * blk:(e + 1) * blk] = raw * scale - slope * dist

    q = q_ref[0].astype(BF16)
    r = lax.broadcasted_iota(jnp.int32, (t, t), 0)
    c = lax.broadcasted_iota(jnp.int32, (t, t), 1)
    s_own = lax.dot_general(q, kn_ref[0].astype(BF16), _NT, preferred_element_type=F32) * scale
    s_own = jnp.where(c <= r, s_own - slope * (r - c).astype(F32), NEG_INF)

    s_sel = ssel_ref[...]
    m = jnp.maximum(jnp.max(s_sel, axis=1, keepdims=True), jnp.max(s_own, axis=1, keepdims=True))
    p_sel = jnp.exp(s_sel - m)
    p_own = jnp.exp(s_own - m)
    l = jnp.sum(p_sel, axis=1, keepdims=True) + jnp.sum(p_own, axis=1, keepdims=True)

    mk = copies(step, slot)
    for e in range(nsel):
        for pg in range(ppb):
            mk(e, pg).wait()

    acc = jnp.dot(p_sel.astype(BF16), vbuf_ref[slot].astype(BF16), preferred_element_type=F32)
    acc = acc + jnp.dot(p_own.astype(BF16), vn_ref[0].astype(BF16), preferred_element_type=F32)
    o_ref[0] = acc / l


def _sample_attend(scores, sel, qu, k_new, v_new, cache_v, page_table, layer):
    b, t, _ = k_new.shape
    n_pages = page_table.shape[1]
    ps = cache_v.shape[2]
    past = n_pages * ps
    assert MOBA_BLOCK % ps == 0 and past % MOBA_BLOCK == 0 and t <= MOBA_BLOCK
    nsel = t * MOBA_TOPK
    grid_spec = pltpu.PrefetchScalarGridSpec(
        num_scalar_prefetch=2,
        grid=(b, N_HEADS),
        in_specs=[pl.BlockSpec((1, 1, past // MOBA_BLOCK, t, MOBA_BLOCK), lambda bi, h, pt, sl: (bi, h, 0, 0, 0)),
                  pl.BlockSpec((None, 1, t, HEAD_DIM), lambda bi, h, pt, sl: (0, bi, 0, h)),
                  pl.BlockSpec((1, t, HEAD_DIM), lambda bi, h, pt, sl: (bi, 0, h)),
                  pl.BlockSpec((1, t, HEAD_DIM), lambda bi, h, pt, sl: (bi, 0, h)),
                  pl.BlockSpec(memory_space=pl.ANY)],
        out_specs=pl.BlockSpec((1, t, HEAD_DIM), lambda bi, h, pt, sl: (bi, 0, h)),
        scratch_shapes=[pltpu.VMEM((2, nsel * MOBA_BLOCK, HEAD_DIM), F32),
                        pltpu.VMEM((t, nsel * MOBA_BLOCK), F32),
                        pltpu.SemaphoreType.DMA((2,))],
    )
    return pl.pallas_call(
        functools.partial(_sample_attend_kernel, layer=layer, n_pages=n_pages, q_start=past),
        grid_spec=grid_spec,
        out_shape=jax.ShapeDtypeStruct((b, t, ATTN_WIDTH), F32),
        compiler_params=_params("arbitrary", "arbitrary"),
        name="sample_attend",
    )(page_table.reshape(-1), sel[..., :MOBA_TOPK].reshape(-1), scores, qu, k_new, v_new, cache_v)


def _conv_taps():
    pad = CONV_HALO - (CONV_WIDTH - 1)
    return [(ph, [(j, (pad + j) // 8) for j in range(CONV_WIDTH) if (pad + j) % 8 == ph]) for ph in range(8)]


def _conv_kernel(ua_ref, ug_ref, prev_ref, w_ref, b_ref, g_ref, bl_ref, y_ref, st_ref, buf_ref, par_ref, z_ref,
                 *, tt, rows, nrows, lanes):
    ti = pl.program_id(1)
    ch = y_ref.shape[2]
    hist = CONV_WIDTH - 1
    pad = CONV_HALO - hist
    i_bias, i_gain, i_beta = CONV_WIDTH, CONV_WIDTH + 1, CONV_WIDTH + 2

    @pl.when(ti == 0)
    def _():
        buf_ref[0:CONV_HALO, :] = jnp.zeros((CONV_HALO, ch), F32)
        buf_ref[pad:CONV_HALO, :] = prev_ref[0]
        for j in range(CONV_WIDTH):
            par_ref[j] = jnp.broadcast_to(w_ref[j:j + 1, :], (8, ch))
        par_ref[i_bias] = jnp.broadcast_to(b_ref[...], (8, ch))
        par_ref[i_gain] = jnp.broadcast_to(g_ref[...], (8, ch))
        par_ref[i_beta] = jnp.broadcast_to(bl_ref[...], (8, ch))

    @pl.when(ti > 0)
    def _():
        buf_ref[0:CONV_HALO, :] = buf_ref[tt:tt + CONV_HALO, :]

    def glu(ci, carry):
        r0 = pl.multiple_of(ci * rows, rows)
        buf_ref[pl.ds(CONV_HALO + r0, rows), :] = (ua_ref[0, pl.ds(r0, rows), :]
                                                   * jax.nn.sigmoid(ug_ref[0, pl.ds(r0, rows), :]))
        return carry

    lax.fori_loop(0, tt // rows, glu, 0)

    def conv(ci, carry):
        r0 = pl.multiple_of(ci * rows, rows)
        for c0 in range(0, ch, lanes):
            cs = slice(c0, c0 + lanes)
            win = buf_ref.at[pl.ds(r0, rows + CONV_HALO), pl.ds(c0, lanes)]
            acc = jnp.broadcast_to(par_ref[i_bias, :, cs][None], (rows // 8, 8, lanes))
            for ph, taps in _conv_taps():
                a_lo, a_hi = taps[0][1], taps[-1][1]
                x = win[8 * a_lo + ph:8 * a_hi + ph + rows, :]
                for j, a in taps:
                    xs = x[8 * (a - a_lo):8 * (a - a_lo) + rows, :].reshape(rows // 8, 8, lanes)
                    acc = acc + par_ref[j, :, cs][None] * xs
            z_ref[pl.ds(r0, rows), cs] = acc.reshape(rows, lanes)
        return carry

    lax.fori_loop(0, tt // rows, conv, 0)

    def norm(ci, carry):
        r0 = pl.multiple_of(ci * nrows, nrows)
        z = z_ref[pl.ds(r0, nrows), :]
        xc = z - jnp.mean(z, axis=-1, keepdims=True)
        xn = (xc * lax.rsqrt(jnp.mean(xc * xc, axis=-1, keepdims=True) + EPS)).reshape(nrows // 8, 8, ch)
        y = (xn * par_ref[i_gain][None] + par_ref[i_beta][None]).reshape(nrows, ch)
        y_ref[0, pl.ds(r0, nrows), :] = y * jax.nn.sigmoid(y)
        return carry

    lax.fori_loop(0, tt // nrows, norm, 0, unroll=min(8, tt // nrows))

    @pl.when(ti == pl.num_programs(1) - 1)
    def _():
        st_ref[0] = buf_ref[tt + pad:tt + CONV_HALO, :]


def _conformer_conv(qu, prev, layer, w_dw, b_dw, g_ln, b_ln, tt):
    _, b, t, ch = qu.shape
    assert ch == w_dw.shape[1]
    hist = CONV_WIDTH - 1
    assert t % tt == 0 and tt % 8 == 0
    rows = 64 if tt % 64 == 0 else 8
    nrows = 16 if tt % 16 == 0 else 8
    lanes = 256 if ch % 256 == 0 else ch
    row = lambda a: a.reshape(1, ch)
    return pl.pallas_call(
        functools.partial(_conv_kernel, tt=tt, rows=rows, nrows=nrows, lanes=lanes),
        grid=(b, t // tt),
        in_specs=[pl.BlockSpec((None, 1, tt, ch), lambda bi, ti: (1, bi, ti, 0)),
                  pl.BlockSpec((None, 1, tt, ch), lambda bi, ti: (2, bi, ti, 0)),
                  pl.BlockSpec((None, 1, hist, ch), lambda bi, ti: (layer, bi, 0, 0)),
                  pl.BlockSpec((CONV_WIDTH, ch), lambda bi, ti: (0, 0)),
                  pl.BlockSpec((1, ch), lambda bi, ti: (0, 0)),
                  pl.BlockSpec((1, ch), lambda bi, ti: (0, 0)),
                  pl.BlockSpec((1, ch), lambda bi, ti: (0, 0))],
        out_specs=[pl.BlockSpec((1, tt, ch), lambda bi, ti: (bi, ti, 0)),
                   pl.BlockSpec((1, hist, ch), lambda bi, ti: (bi, 0, 0))],
        out_shape=[jax.ShapeDtypeStruct((b, t, ch), F32), jax.ShapeDtypeStruct((b, hist, ch), F32)],
        scratch_shapes=[pltpu.VMEM((tt + CONV_HALO, ch), F32), pltpu.VMEM((CONV_WIDTH + 3, 8, ch), F32),
                        pltpu.VMEM((tt, ch), F32)],
        compiler_params=_params("arbitrary", "arbitrary"),
        name="conformer_conv",
    )(qu, qu, prev, w_dw, row(b_dw), row(g_ln), row(b_ln))


def _mix_out_kernel(attn_ref, conv_ref, h_ref, ga_ref, gc_ref, w_ref, gp_ref, o_ref):
    wa = attn_ref.shape[1]
    a = _rms(attn_ref[...], ga_ref[...]).astype(BF16)
    cv = _rms(conv_ref[...], gc_ref[...]).astype(BF16)
    mixed = (jnp.dot(a, w_ref[0:wa, :], preferred_element_type=F32)
             + jnp.dot(cv, w_ref[wa:, :], preferred_element_type=F32))
    o_ref[...] = h_ref[...] + _rms(mixed, gp_ref[...])


def _mix_out(attn, conv, h, g_attn, g_conv, w_out, g_post, tm):
    m, wa = attn.shape
    wc = conv.shape[1]
    d = h.shape[1]
    assert m % tm == 0
    row = lambda a: a.reshape(1, -1)
    return pl.pallas_call(
        _mix_out_kernel,
        grid=(m // tm,),
        in_specs=[pl.BlockSpec((tm, wa), lambda i: (i, 0)),
                  pl.BlockSpec((tm, wc), lambda i: (i, 0)),
                  pl.BlockSpec((tm, d), lambda i: (i, 0)),
                  pl.BlockSpec((1, wa), lambda i: (0, 0)),
                  pl.BlockSpec((1, wc), lambda i: (0, 0)),
                  pl.BlockSpec((wa + wc, d), lambda i: (0, 0)),
                  pl.BlockSpec((1, d), lambda i: (0, 0))],
        out_specs=pl.BlockSpec((tm, d), lambda i: (i, 0)),
        out_shape=jax.ShapeDtypeStruct((m, d), F32),
        compiler_params=_params("arbitrary"),
        name="mix_out",
    )(attn, conv, h, row(g_attn), row(g_conv), w_out, row(g_post))


def _xattn_kernel(h_ref, mk_ref, mv_ref, gpre_ref, wq_ref, wo_ref, gpost_ref, o_ref):
    h = h_ref[...]
    xn = _rms(h, gpre_ref[...]).astype(BF16)
    xq = jnp.dot(xn, wq_ref[...], preferred_element_type=F32)
    scale = X_HEAD_DIM ** -0.5
    outs = []
    for hh in range(X_HEADS):
        cols = slice(hh * X_HEAD_DIM, (hh + 1) * X_HEAD_DIM)
        s = lax.dot_general(xq[:, cols].astype(BF16), mk_ref[0, :, cols].astype(BF16), _NT,
                            preferred_element_type=F32) * scale
        p = jnp.exp(s - jnp.max(s, axis=1, keepdims=True))
        l = jnp.sum(p, axis=1, keepdims=True)
        outs.append(jnp.dot(p.astype(BF16), mv_ref[0, :, cols].astype(BF16), preferred_element_type=F32) / l)
    o = jnp.concatenate(outs, axis=1).astype(BF16)
    y = jnp.dot(o, wo_ref[...], preferred_element_type=F32)
    o_ref[...] = h + _rms(y, gpost_ref[...])


def _xattn(h, mem_k, mem_v, layer, g_pre, w_xq, w_xo, g_post, rows_per_batch, tm):
    m, d = h.shape
    assert rows_per_batch % tm == 0
    tiles = rows_per_batch // tm
    mem_len, xw = mem_k.shape[2:]
    row = lambda a: a.reshape(1, -1)
    return pl.pallas_call(
        _xattn_kernel,
        grid=(m // tm,),
        in_specs=[pl.BlockSpec((tm, d), lambda i: (i, 0)),
                  pl.BlockSpec((None, 1, mem_len, xw), lambda i: (layer, i // tiles, 0, 0)),
                  pl.BlockSpec((None, 1, mem_len, xw), lambda i: (layer, i // tiles, 0, 0)),
                  pl.BlockSpec((1, d), lambda i: (0, 0)),
                  pl.BlockSpec((d, xw), lambda i: (0, 0)),
                  pl.BlockSpec((xw, d), lambda i: (0, 0)),
                  pl.BlockSpec((1, d), lambda i: (0, 0))],
        out_specs=pl.BlockSpec((tm, d), lambda i: (i, 0)),
        out_shape=jax.ShapeDtypeStruct((m, d), F32),
        compiler_params=_params("arbitrary"),
        name="xattn",
    )(h, mem_k, mem_v, row(g_pre), w_xq, w_xo, row(g_post))


def _ffn_kernel(h_ref, gpre_ref, wg_ref, wu_ref, wo_ref, gpost_ref, o_ref, *rest, emit):
    if emit:
        wg_out, wu_out, wo_out, xn_ref, acc_ref = rest
        wg_out[...] = wg_ref[...].astype(BF16)
        wu_out[...] = wu_ref[...].astype(BF16)
        wo_out[...] = wo_ref[...].astype(BF16)
        wg_ref, wu_ref, wo_ref = wg_out, wu_out, wo_out
    else:
        xn_ref, acc_ref = rest
    f = pl.program_id(1)

    @pl.when(f == 0)
    def _():
        xn_ref[...] = _rms(h_ref[...], gpre_ref[...]).astype(BF16)
        acc_ref[...] = jnp.zeros(acc_ref.shape, F32)

    xn = xn_ref[...]
    g = jnp.dot(xn, wg_ref[...], preferred_element_type=F32)
    u = jnp.dot(xn, wu_ref[...], preferred_element_type=F32)
    a = (g * jax.nn.sigmoid(g) * u).astype(BF16)
    acc_ref[...] += jnp.dot(a, wo_ref[...], preferred_element_type=F32)

    @pl.when(f == pl.num_programs(1) - 1)
    def _():
        o_ref[...] = h_ref[...] + _rms(acc_ref[...], gpost_ref[...])


def _ffn(h, g_pre, w, g_post, tm, tf, emit):
    m, d = h.shape
    row = lambda a: a.reshape(1, -1)
    if emit:
        w_in, w_out = w
        hidden = w_out.shape[0]
        nf = hidden // tf
        assert m == tm and hidden % tf == 0 and w_in.shape[1] == 2 * hidden
        w_args = (w_in, w_in, w_out)
        w_specs = [pl.BlockSpec((d, tf), lambda i, f: (0, f)),
                   pl.BlockSpec((d, tf), lambda i, f: (0, f + nf)),
                   pl.BlockSpec((tf, d), lambda i, f: (f, 0))]
    else:
        w_args = w
        nf, hidden = w[0].shape[0], w[2].shape[0]
        assert m % tm == 0 and w[0].shape == w[1].shape == (nf, d, tf) and hidden == nf * tf
        w_specs = [pl.BlockSpec((None, d, tf), lambda i, f: (f, 0, 0)),
                   pl.BlockSpec((None, d, tf), lambda i, f: (f, 0, 0)),
                   pl.BlockSpec((tf, d), lambda i, f: (f, 0))]
    out_specs = [pl.BlockSpec((tm, d), lambda i, f: (i, 0))]
    out_shape = [jax.ShapeDtypeStruct((m, d), F32)]
    if emit:
        out_specs += [pl.BlockSpec((None, d, tf), lambda i, f: (f, 0, 0)),
                      pl.BlockSpec((None, d, tf), lambda i, f: (f, 0, 0)),
                      pl.BlockSpec((tf, d), lambda i, f: (f, 0))]
        out_shape += [jax.ShapeDtypeStruct((nf, d, tf), BF16), jax.ShapeDtypeStruct((nf, d, tf), BF16),
                      jax.ShapeDtypeStruct((hidden, d), BF16)]
    outs = pl.pallas_call(
        functools.partial(_ffn_kernel, emit=emit),
        grid=(m // tm, nf),
        in_specs=[pl.BlockSpec((tm, d), lambda i, f: (i, 0)),
                  pl.BlockSpec((1, d), lambda i, f: (0, 0)),
                  *w_specs,
                  pl.BlockSpec((1, d), lambda i, f: (0, 0))],
        out_specs=out_specs,
        out_shape=out_shape,
        scratch_shapes=[pltpu.VMEM((tm, d), BF16), pltpu.VMEM((tm, d), F32)],
        compiler_params=_params("arbitrary", "arbitrary"),
        name="ffn",
    )(h, row(g_pre), *w_args, row(g_post))
    return outs[0], tuple(outs[1:])


def _row_tile(m):
    return 512 if m % 512 == 0 else m


def _project_in(x, wts, w_in, emit):
    b, t, d = x.shape
    m = b * t
    assert wts["w_dw"].shape[1] == ATTN_WIDTH
    k_new, v_new, qu, *w_in_b = _in_proj(x.reshape(m, d), wts["g_pre_mix"].reshape(1, d), w_in, emit,
                                         tm=1024 if m % 1024 == 0 else _row_tile(m))
    k_new, v_new = k_new.reshape(b, t, ATTN_WIDTH), v_new.reshape(b, t, ATTN_WIDTH)
    return k_new, v_new, qu.reshape(qu.shape[0], b, t, ATTN_WIDTH), (w_in_b[0] if emit else None)


def _after_attention(x, attn, qu, state, state_layer, wts, w_ffn, emit, conv_tile, x_tile):
    conv_prev, mem_k, mem_v = state
    b, t, d = x.shape
    m = b * t
    tm = _row_tile(m)
    ch = wts["w_dw"].shape[1]
    conv, conv_state = _conformer_conv(qu, conv_prev, state_layer, wts["w_dw"], wts["b_dw"],
                                       wts["g_ln_conv"], wts["b_ln_conv"], conv_tile)
    h = _mix_out(attn.reshape(m, ATTN_WIDTH), conv.reshape(m, ch), x.reshape(m, d), wts["g_attn_grp"],
                 wts["g_conv_grp"], wts["w_out"], wts["g_post_mix"], tm=tm)
    h = _xattn(h, mem_k, mem_v, state_layer, wts["g_pre_x"], wts["w_xq"], wts["w_xo"], wts["g_post_x"],
               rows_per_batch=t, tm=x_tile)
    h, w_ffn_b = _ffn(h, wts["g_pre_ffn"], w_ffn, wts["g_post_ffn"], tm=tm, tf=512, emit=emit)
    return h.reshape(b, t, d), conv_state, (w_ffn_b if emit else None)


def kernel(x_prompt, x_sample, cache_k, cache_v, state_conv, cache_mem_k, cache_mem_v, page_table,
           mem_prompt, g_mem, w_mem_k, w_mem_v, g_pre_mix, w_in, w_dw, b_dw, g_ln_conv, b_ln_conv,
           g_attn_grp, g_conv_grp, w_out, g_post_mix, g_pre_x, w_xq, w_xo, g_post_x,
           g_pre_ffn, w_ffn_in, w_ffn_out, g_post_ffn):
    depth = w_in.shape[0]
    bp, tp, d = x_prompt.shape
    bs, ts, _ = x_sample.shape
    mem_len = mem_prompt.shape[1]
    xw = X_HEADS * X_HEAD_DIM
    ch = w_dw.shape[2]
    hp, hs = x_prompt, x_sample
    outs = [[] for _ in range(8)]
    for l in range(depth):
        wts = dict(g_pre_mix=g_pre_mix[l], w_dw=w_dw[l], b_dw=b_dw[l],
                   g_ln_conv=g_ln_conv[l], b_ln_conv=b_ln_conv[l], g_attn_grp=g_attn_grp[l],
                   g_conv_grp=g_conv_grp[l], w_out=w_out[l].astype(BF16), g_post_mix=g_post_mix[l],
                   g_pre_x=g_pre_x[l], w_xq=w_xq[l].astype(BF16), w_xo=w_xo[l].astype(BF16),
                   g_post_x=g_post_x[l], g_pre_ffn=g_pre_ffn[l], g_post_ffn=g_post_ffn[l])

        ks, vs, qu_s, w_in_b = _project_in(hs, wts, w_in[l], True)
        kp, vp, qu_p, _ = _project_in(hp, wts, w_in_b, False)
        attn_p, scores, sel = _moba_and_scores(qu_p, kp, vp, qu_s, cache_k, page_table, l)
        attn_s = _sample_attend(scores, sel, qu_s, ks, vs, cache_v, page_table, l)

        sample_state = (state_conv, cache_mem_k.reshape(depth, bs, mem_len, xw),
                        cache_mem_v.reshape(depth, bs, mem_len, xw))
        hs, cs, w_ffn_b = _after_attention(hs, attn_s, qu_s, sample_state, l, wts, (w_ffn_in[l], w_ffn_out[l]),
                                           True, conv_tile=ts, x_tile=ts)

        w_mem = jnp.concatenate([w_mem_k[l], w_mem_v[l]], axis=1).astype(BF16)
        mk_p, mv_p = _norm_matmul(mem_prompt.reshape(bp * mem_len, d), g_mem[l].reshape(1, d), w_mem,
                                  (xw, xw), tn=xw, tm=_row_tile(bp * mem_len))
        mk_p, mv_p = mk_p.reshape(bp, mem_len, xw), mv_p.reshape(bp, mem_len, xw)
        prompt_state = (jnp.zeros((1, bp, CONV_WIDTH - 1, ch), F32), mk_p[None], mv_p[None])
        hp, cp, _ = _after_attention(hp, attn_p, qu_p, prompt_state, 0, wts, w_ffn_b, False,
                                     conv_tile=_row_tile(tp), x_tile=_row_tile(tp))
        mem_shape = (bp, mem_len, X_HEADS, X_HEAD_DIM)
        heads = lambda a: a.reshape(a.shape[:2] + (N_HEADS, HEAD_DIM))
        for lst, a in zip(outs, (heads(kp), heads(vp), cp, mk_p.reshape(mem_shape), mv_p.reshape(mem_shape),
                                 heads(ks), heads(vs), cs)):
            lst.append(a)
    return (hp, hs) + tuple(jnp.stack(lst, 0) for lst in outs)
```

```python
import functools

import jax
import jax.numpy as jnp
from jax import lax
from jax.experimental import pallas as pl
from jax.experimental.pallas import tpu as pltpu

EPS = 1e-6
N_HEADS = 8
HEAD_DIM = 128
ATTN_WIDTH = N_HEADS * HEAD_DIM
CONV_WIDTH = 31
MOBA_BLOCK = 256
MOBA_TOPK = 3
X_HEADS = 4
X_HEAD_DIM = 128

CONV_HALO = 32
DMA_PRIORITIES = 2
ONES_ROWS = 16
VMEM_LIMIT = 56 * 1024 * 1024

F32 = jnp.float32
BF16 = jnp.bfloat16
NEG_INF = float("-inf")
LOG2_E = 1.4426950408889634
_NT = (((1,), (1,)), ((), ()))


def _params(*sem):
    return pltpu.CompilerParams(dimension_semantics=sem, vmem_limit_bytes=VMEM_LIMIT)


def _rms(x, g):
    return x * lax.rsqrt(jnp.mean(x * x, axis=-1, keepdims=True) + EPS) * g


def _top_blocks(gate, n_valid, axis):
    nb = gate.shape[axis]
    pos = lax.broadcasted_iota(jnp.int32, gate.shape, axis)
    g = jnp.where(pos < n_valid, gate, NEG_INF)
    idxs, oks = [], []
    for _ in range(MOBA_TOPK):
        m = jnp.max(g, axis=axis, keepdims=True)
        idx = jnp.min(jnp.where(g == m, pos, nb), axis=axis, keepdims=True)
        ok = m > NEG_INF
        idxs.append(idx)
        oks.append(ok)
        g = jnp.where((pos == idx) & ok, NEG_INF, g)
    return idxs, oks


def _norm_matmul_kernel(x_ref, g_ref, w_ref, *rest, tile_ranges):
    out_refs, xn_ref = rest[:-1], rest[-1]
    j = pl.program_id(1)

    @pl.when(j == 0)
    def _():
        xn_ref[...] = _rms(x_ref[...], g_ref[...]).astype(BF16)

    for o_ref, (lo, hi) in zip(out_refs, tile_ranges):
        @pl.when((j >= lo) & (j < hi))
        def _(o_ref=o_ref):
            o_ref[...] = jnp.dot(xn_ref[...], w_ref[...], preferred_element_type=F32)


def _norm_matmul(x, g, w, widths, tn, tm):
    m, d = x.shape
    n = w.shape[1]
    assert sum(widths) == n and all(wd % tn == 0 for wd in widths) and m % tm == 0
    tile_ranges, lo = [], 0
    for wd in widths:
        tile_ranges.append((lo, lo + wd // tn))
        lo += wd // tn

    def out_map(i, j, lo, cnt):
        return (i, jnp.clip(j - lo, 0, cnt - 1))

    out_specs = [pl.BlockSpec((tm, tn), functools.partial(out_map, lo=lo, cnt=hi - lo))
                 for lo, hi in tile_ranges]
    return pl.pallas_call(
        functools.partial(_norm_matmul_kernel, tile_ranges=tuple(tile_ranges)),
        grid=(m // tm, n // tn),
        in_specs=[pl.BlockSpec((tm, d), lambda i, j: (i, 0)),
                  pl.BlockSpec((1, d), lambda i, j: (0, 0)),
                  pl.BlockSpec((d, tn), lambda i, j: (0, j))],
        out_specs=out_specs,
        out_shape=[jax.ShapeDtypeStruct((m, wd), F32) for wd in widths],
        scratch_shapes=[pltpu.VMEM((tm, d), BF16)],
        compiler_params=_params("arbitrary", "arbitrary"),
        name="norm_matmul",
    )(x, g, w)


def _in_proj_src_tile(j):
    return jnp.where(j < 3, (j + 1) % 3, j)


def _in_proj_kernel(x_ref, g_ref, w_ref, *rest, emit):
    if emit:
        k_ref, v_ref, qu_ref, wt_ref, xn_ref = rest
    else:
        k_ref, v_ref, qu_ref, xn_ref = rest
    j = pl.program_id(1)

    @pl.when(j == 0)
    def _():
        xn_ref[...] = _rms(x_ref[...], g_ref[...]).astype(BF16)

    if emit:
        wt_ref[...] = w_ref[...].astype(BF16)
        w_ref = wt_ref

    for o_ref, cond in ((k_ref, j == 0), (v_ref, j == 1), (qu_ref, j >= 2)):
        @pl.when(cond)
        def _(o_ref=o_ref):
            o_ref[...] = jnp.dot(xn_ref[...], w_ref[...], preferred_element_type=F32)


def _in_proj(x, g, w, emit, tm):
    m, d = x.shape
    tn = ATTN_WIDTH
    nt = (w.shape[1] // tn) if emit else w.shape[0]
    assert m % tm == 0 and nt > 3 and (not emit or (m == tm and w.shape[1] % tn == 0))
    if emit:
        w_spec = pl.BlockSpec((d, tn), lambda i, j: (0, _in_proj_src_tile(j)))
    else:
        w_spec = pl.BlockSpec((None, d, tn), lambda i, j: (j, 0, 0))
    out_specs = [pl.BlockSpec((tm, tn), lambda i, j: (i, 0)),
                 pl.BlockSpec((tm, tn), lambda i, j: (i, 0)),
                 pl.BlockSpec((None, tm, tn), lambda i, j: (jnp.maximum(j - 2, 0), i, 0))]
    out_shape = [jax.ShapeDtypeStruct((m, tn), F32), jax.ShapeDtypeStruct((m, tn), F32),
                 jax.ShapeDtypeStruct((nt - 2, m, tn), F32)]
    if emit:
        out_specs.append(pl.BlockSpec((None, d, tn), lambda i, j: (j, 0, 0)))
        out_shape.append(jax.ShapeDtypeStruct((nt, d, tn), BF16))
    return pl.pallas_call(
        functools.partial(_in_proj_kernel, emit=emit),
        grid=(m // tm, nt),
        in_specs=[pl.BlockSpec((tm, d), lambda i, j: (i, 0)),
                  pl.BlockSpec((1, d), lambda i, j: (0, 0)),
                  w_spec],
        out_specs=out_specs,
        out_shape=out_shape,
        scratch_shapes=[pltpu.VMEM((tm, d), BF16)],
        compiler_params=_params("arbitrary", "arbitrary"),
        name="in_proj",
    )(x, g, w)


def _moba_prompt_body(h, g, q_ref, k_ref, v_ref, o_ref, kb_ref, vt_ref, km_ref, *, nb, tps):
    blk = MOBA_BLOCK

    @pl.when(g == 0)
    def _():
        for n in range(nb):
            kf = k_ref[0, n * blk:(n + 1) * blk, :]
            kb_ref[n * blk:(n + 1) * blk, :] = kf.astype(BF16)
            vt_ref[n, 0:HEAD_DIM, :] = v_ref[0, n * blk:(n + 1) * blk, :].T.astype(BF16)
            vt_ref[n, HEAD_DIM:, :] = jnp.ones((ONES_ROWS, blk), BF16)
            km_ref[n:n + 1, :] = jnp.mean(kf, axis=0, keepdims=True)

    scale = HEAD_DIM ** -0.5 * LOG2_E
    slope = jnp.exp2(jnp.broadcast_to(-8.0 * (h + 1).astype(F32) / N_HEADS, (1, blk))) * LOG2_E

    def attend(i, qi):
        q = q_ref[0, qi * blk:(qi + 1) * blk, :]
        qb = q.astype(BF16)
        kr = lax.broadcasted_iota(jnp.int32, (blk, blk), 0)
        qc = lax.broadcasted_iota(jnp.int32, (blk, blk), 1)
        base = slope * (qc - kr).astype(F32)
        s_own = lax.dot_general(kb_ref[i * blk:(i + 1) * blk, :], qb, _NT, preferred_element_type=F32)
        ts = [jnp.where(kr <= qc, s_own * scale - base, NEG_INF)]
        if i:
            gate = lax.dot_general(km_ref[0:i, :], q, _NT, precision=lax.Precision.HIGHEST,
                                   preferred_element_type=F32)
            idxs, oks = _top_blocks(gate, i, axis=0)
            row = lax.broadcasted_iota(jnp.int32, (i, blk), 0)
            sel = jnp.zeros((i, blk), F32)
            for idx, ok in zip(idxs, oks):
                sel = jnp.where((row == idx) & ok, 1.0, sel)
            for n in range(i):
                s = lax.dot_general(kb_ref[n * blk:(n + 1) * blk, :], qb, _NT, preferred_element_type=F32)
                neg = jnp.where(sel[n:n + 1, :] > 0.0, -slope * float((i - n) * blk), NEG_INF)
                ts.append(s * scale - base + neg)
        m = jnp.max(functools.reduce(jnp.maximum, ts), axis=0, keepdims=True)
        acc = None
        for n, t in enumerate(ts):
            p = jnp.exp2(t - m)
            vt = vt_ref[i] if n == 0 else vt_ref[n - 1]
            pv = jnp.dot(vt, p.astype(BF16), preferred_element_type=F32)
            acc = pv if acc is None else acc + pv
        l = acc[HEAD_DIM:HEAD_DIM + 1, :]
        o_ref[0, qi * blk:(qi + 1) * blk, :] = (acc[0:HEAD_DIM, :] / l).T

    for v in range(nb // tps):
        @pl.when(g == v)
        def _(v=v):
            for qi in range(tps):
                attend(v * tps + qi, qi)


def _head_page_copy(cache_ref, pt_ref, buf_ref, sem, layer, b, h, page_slot, dst_row, n_pages):
    page = pt_ref[b * n_pages + page_slot]
    ps = cache_ref.shape[2]
    return pltpu.make_async_copy(cache_ref.at[layer, page, :, h, :], buf_ref.at[pl.ds(dst_row, ps), :], sem)


def _start_k_pages(st, first, count, pt_ref, kc_ref, kbuf_ref, sem_ref, *, layer, n_pages):
    bb, hh = st // N_HEADS, st % N_HEADS
    sl = st % 2
    ps = kc_ref.shape[2]

    def issue(pair, carry):
        for k in range(DMA_PRIORITIES):
            p = first + pair * DMA_PRIORITIES + k
            _head_page_copy(kc_ref, pt_ref, kbuf_ref.at[sl], sem_ref.at[sl], layer, bb, hh,
                            p, pl.multiple_of(p * ps, ps), n_pages).start(priority=k)
        return carry

    lax.fori_loop(0, count // DMA_PRIORITIES, issue, 0, unroll=4)


def _sample_scores_body(step, n_steps, pt_ref, q_ref, kc_ref, s_ref, sel_ref, kbuf_ref, km_ref, gate_ref, sem_ref,
                        *, layer, n_pages, chunk):
    nh = N_HEADS
    b, h = step // nh, step % nh
    slot = step % 2
    ps = kc_ref.shape[2]
    past = n_pages * ps
    blk = MOBA_BLOCK
    t = q_ref.shape[1]
    start_pages = functools.partial(_start_k_pages, pt_ref=pt_ref, kc_ref=kc_ref, kbuf_ref=kbuf_ref,
                                    sem_ref=sem_ref, layer=layer, n_pages=n_pages)

    @pl.when(step == 0)
    def _():
        start_pages(step, 0, n_pages)

    @pl.when(step + 1 < n_steps)
    def _():
        start_pages(step + 1, 0, n_pages)

    pltpu.make_async_copy(kbuf_ref.at[1 - slot], kbuf_ref.at[slot], sem_ref.at[slot]).wait()

    q = q_ref[0]
    qb = q.astype(BF16)
    bpc = chunk // blk

    def body(ci, carry):
        start = pl.multiple_of(ci * chunk, chunk)
        kf = kbuf_ref[slot, pl.ds(start, chunk), :]
        sc = lax.dot_general(qb, kf.astype(BF16), _NT, preferred_element_type=F32)
        for jj in range(bpc):
            s_ref[0, 0, ci * bpc + jj] = sc[:, jj * blk:(jj + 1) * blk]
        km_ref[pl.ds(pl.multiple_of(ci * bpc, bpc), bpc), :] = jnp.mean(
            kf.reshape(bpc, blk, HEAD_DIM), axis=1)
        return carry

    lax.fori_loop(0, past // chunk, body, 0, unroll=True)

    gate_ref[h] = lax.dot_general(q, km_ref[...], _NT, precision=lax.Precision.HIGHEST,
                                  preferred_element_type=F32)

    @pl.when(h == nh - 1)
    def _():
        gate = gate_ref[...].reshape(nh * t, past // blk)
        idxs, _ = _top_blocks(gate, past // blk, axis=1)
        lane = lax.broadcasted_iota(jnp.int32, (nh * t, 128), 1)
        out = jnp.zeros((nh * t, 128), jnp.int32)
        for k, idx in enumerate(idxs):
            out = jnp.where(lane == k, idx, out)
        sel_ref[0] = out.reshape(nh, t, 128)


def _moba_scores_kernel(pt_ref, qs_ref, kc_ref, qp_ref, kp_ref, vp_ref, s_ref, sel_ref, o_ref,
                        kbuf_ref, km_ref, gate_ref, sem_ref, kb_ref, vt_ref, kmp_ref,
                        *, layer, n_pages, chunk, nb, tps, every):
    step = pl.program_id(0)
    n_steps = pl.num_programs(0)
    _sample_scores_body(step, n_steps, pt_ref, qs_ref, kc_ref, s_ref, sel_ref, kbuf_ref, km_ref,
                        gate_ref, sem_ref, layer=layer, n_pages=n_pages, chunk=chunk)

    @pl.when(step % every == 0)
    def _():
        item = step // every
        groups = nb // tps
        _moba_prompt_body((item // groups) % N_HEADS, item % groups, qp_ref, kp_ref, vp_ref, o_ref,
                          kb_ref, vt_ref, kmp_ref, nb=nb, tps=tps)


def _moba_and_scores(qu_p, k_p, v_p, qu_s, cache_k, page_table, layer, tps=4):
    _, bs, ts, _ = qu_s.shape
    bp, tp, _ = k_p.shape
    blk = MOBA_BLOCK
    n_pages = page_table.shape[1]
    ps = cache_k.shape[2]
    past = n_pages * ps
    chunk = 8 * blk
    assert past % chunk == 0 and past // blk >= MOBA_TOPK and n_pages % DMA_PRIORITIES == 0
    tps = min(tps, tp // blk)
    assert tp % (blk * tps) == 0 and n_pages % (tps * DMA_PRIORITIES) == 0
    nb = tp // blk
    groups = nb // tps
    n_steps, items = bs * N_HEADS, bp * N_HEADS * groups
    assert n_steps % items == 0
    every = n_steps // items

    def prompt_idx(step):
        item = step // every
        return item // (N_HEADS * groups), (item // groups) % N_HEADS, item % groups

    def qp_map(step, pt):
        bi, h, g = prompt_idx(step)
        return (0, bi, g, h)

    def kv_map(step, pt):
        bi, h, g = prompt_idx(step)
        return (bi, 0, h)

    def o_map(step, pt):
        bi, h, g = prompt_idx(step)
        return (bi, g, h)

    grid_spec = pltpu.PrefetchScalarGridSpec(
        num_scalar_prefetch=1,
        grid=(n_steps,),
        in_specs=[pl.BlockSpec((None, 1, ts, HEAD_DIM), lambda s, pt: (0, s // N_HEADS, 0, s % N_HEADS)),
                  pl.BlockSpec(memory_space=pl.ANY),
                  pl.BlockSpec((None, 1, blk * tps, HEAD_DIM), qp_map),
                  pl.BlockSpec((1, tp, HEAD_DIM), kv_map),
                  pl.BlockSpec((1, tp, HEAD_DIM), kv_map)],
        out_specs=[pl.BlockSpec((1, 1, past // blk, ts, blk), lambda s, pt: (s // N_HEADS, s % N_HEADS, 0, 0, 0)),
                   pl.BlockSpec((1, N_HEADS, ts, 128), lambda s, pt: (s // N_HEADS, 0, 0, 0)),
                   pl.BlockSpec((1, blk * tps, HEAD_DIM), o_map)],
        scratch_shapes=[pltpu.VMEM((2, past, HEAD_DIM), F32),
                        pltpu.VMEM((past // blk, HEAD_DIM), F32),
                        pltpu.VMEM((N_HEADS, ts, past // blk), F32),
                        pltpu.SemaphoreType.DMA((2,)),
                        pltpu.VMEM((tp, HEAD_DIM), BF16), pltpu.VMEM((nb, HEAD_DIM + ONES_ROWS, blk), BF16),
                        pltpu.VMEM((nb, HEAD_DIM), F32)],
    )
    scores, sel, attn = pl.pallas_call(
        functools.partial(_moba_scores_kernel, layer=layer, n_pages=n_pages, chunk=chunk, nb=nb, tps=tps,
                          every=every),
        grid_spec=grid_spec,
        out_shape=[jax.ShapeDtypeStruct((bs, N_HEADS, past // blk, ts, blk), F32),
                   jax.ShapeDtypeStruct((bs, N_HEADS, ts, 128), jnp.int32),
                   jax.ShapeDtypeStruct((bp, tp, ATTN_WIDTH), F32)],
        compiler_params=_params("arbitrary"),
        name="moba_and_scores",
    )(page_table.reshape(-1), qu_s, cache_k, qu_p, k_p, v_p)
    return attn, scores, sel


def _sample_attend_kernel(pt_ref, sel_ref, s_ref, q_ref, kn_ref, vn_ref, vc_ref, o_ref,
                          vbuf_ref, ssel_ref, sem_ref, *, layer, n_pages, q_start):
    b = pl.program_id(0)
    h = pl.program_id(1)
    nbt, nh = pl.num_programs(0), pl.num_programs(1)
    step = b * nh + h
    slot = step % 2
    t = q_ref.shape[1]
    ps = vc_ref.shape[2]
    blk = MOBA_BLOCK
    ppb = blk // ps
    nsel = t * MOBA_TOPK

    def sel_block(st, e):
        return sel_ref[st * nsel + e]

    def copies(st, sl):
        bb, hh = st // nh, st % nh

        def mk(e, pg):
            return _head_page_copy(vc_ref, pt_ref, vbuf_ref.at[sl], sem_ref.at[sl], layer, bb, hh,
                                   sel_block(st, e) * ppb + pg, e * blk + pg * ps, n_pages)
        return mk

    def start_all(st, sl):
        mk = copies(st, sl)
        for e in range(nsel):
            for pg in range(ppb):
                mk(e, pg).start(priority=(e * ppb + pg) % DMA_PRIORITIES)

    @pl.when(step == 0)
    def _():
        start_all(step, slot)

    @pl.when(step + 1 < nbt * nh)
    def _():
        start_all(step + 1, 1 - slot)

    scale = HEAD_DIM ** -0.5
    slope = jnp.exp2(jnp.broadcast_to(-8.0 * (h + 1).astype(F32) / N_HEADS, (1, 1)))
    lane = lax.broadcasted_iota(jnp.int32, (1, blk), 1)

    ssel_ref[...] = jnp.full(ssel_ref.shape, NEG_INF, F32)
    for e in range(nsel):
        ti = e // MOBA_TOPK
        n = sel_block(step, e)
        raw = s_ref[0, 0, n, ti:ti + 1, :]
        dist = (q_start + ti - n * blk - lane).astype(F32)
        ssel_ref[ti:ti + 1, e * blk:(e + 1) * blk] = raw * scale - slope * dist

    q = q_ref[0].astype(BF16)
    r = lax.broadcasted_iota(jnp.int32, (t, t), 0)
    c = lax.broadcasted_iota(jnp.int32, (t, t), 1)
    s_own = lax.dot_general(q, kn_ref[0].astype(BF16), _NT, preferred_element_type=F32) * scale
    s_own = jnp.where(c <= r, s_own - slope * (r - c).astype(F32), NEG_INF)

    s_sel = ssel_ref[...]
    m = jnp.maximum(jnp.max(s_sel, axis=1, keepdims=True), jnp.max(s_own, axis=1, keepdims=True))
    p_sel = jnp.exp(s_sel - m)
    p_own = jnp.exp(s_own - m)
    l = jnp.sum(p_sel, axis=1, keepdims=True) + jnp.sum(p_own, axis=1, keepdims=True)

    mk = copies(step, slot)
    for e in range(nsel):
        for pg in range(ppb):
            mk(e, pg).wait()

    acc = jnp.dot(p_sel.astype(BF16), vbuf_ref[slot].astype(BF16), preferred_element_type=F32)
    acc = acc + jnp.dot(p_own.astype(BF16), vn_ref[0].astype(BF16), preferred_element_type=F32)
    o_ref[0] = acc / l


def _sample_attend(scores, sel, qu, k_new, v_new, cache_v, page_table, layer):
    b, t, _ = k_new.shape
    n_pages = page_table.shape[1]
    ps = cache_v.shape[2]
    past = n_pages * ps
    assert MOBA_BLOCK % ps == 0 and past % MOBA_BLOCK == 0 and t <= MOBA_BLOCK
    nsel = t * MOBA_TOPK
    grid_spec = pltpu.PrefetchScalarGridSpec(
        num_scalar_prefetch=2,
        grid=(b, N_HEADS),
        in_specs=[pl.BlockSpec((1, 1, past // MOBA_BLOCK, t, MOBA_BLOCK), lambda bi, h, pt, sl: (bi, h, 0, 0, 0)),
                  pl.BlockSpec((None, 1, t, HEAD_DIM), lambda bi, h, pt, sl: (0, bi, 0, h)),
                  pl.BlockSpec((1, t, HEAD_DIM), lambda bi, h, pt, sl: (bi, 0, h)),
                  pl.BlockSpec((1, t, HEAD_DIM), lambda bi, h, pt, sl: (bi, 0, h)),
                  pl.BlockSpec(memory_space=pl.ANY)],
        out_specs=pl.BlockSpec((1, t, HEAD_DIM), lambda bi, h, pt, sl: (bi, 0, h)),
        scratch_shapes=[pltpu.VMEM((2, nsel * MOBA_BLOCK, HEAD_DIM), F32),
                        pltpu.VMEM((t, nsel * MOBA_BLOCK), F32),
                        pltpu.SemaphoreType.DMA((2,))],
    )
    return pl.pallas_call(
        functools.partial(_sample_attend_kernel, layer=layer, n_pages=n_pages, q_start=past),
        grid_spec=grid_spec,
        out_shape=jax.ShapeDtypeStruct((b, t, ATTN_WIDTH), F32),
        compiler_params=_params("arbitrary", "arbitrary"),
        name="sample_attend",
    )(page_table.reshape(-1), sel[..., :MOBA_TOPK].reshape(-1), scores, qu, k_new, v_new, cache_v)


def _conv_taps():
    pad = CONV_HALO - (CONV_WIDTH - 1)
    return [(ph, [(j, (pad + j) // 8) for j in range(CONV_WIDTH) if (pad + j) % 8 == ph]) for ph in range(8)]


def _conv_kernel(ua_ref, ug_ref, prev_ref, w_ref, b_ref, g_ref, bl_ref, y_ref, st_ref, buf_ref, par_ref, z_ref,
                 *, tt, rows, nrows, lanes):
    ti = pl.program_id(1)
    ch = y_ref.shape[2]
    hist = CONV_WIDTH - 1
    pad = CONV_HALO - hist
    i_bias, i_gain, i_beta = CONV_WIDTH, CONV_WIDTH + 1, CONV_WIDTH + 2

    @pl.when(ti == 0)
    def _():
        buf_ref[0:CONV_HALO, :] = jnp.zeros((CONV_HALO, ch), F32)
        buf_ref[pad:CONV_HALO, :] = prev_ref[0]
        for j in range(CONV_WIDTH):
            par_ref[j] = jnp.broadcast_to(w_ref[j:j + 1, :], (8, ch))
        par_ref[i_bias] = jnp.broadcast_to(b_ref[...], (8, ch))
        par_ref[i_gain] = jnp.broadcast_to(g_ref[...], (8, ch))
        par_ref[i_beta] = jnp.broadcast_to(bl_ref[...], (8, ch))

    @pl.when(ti > 0)
    def _():
        buf_ref[0:CONV_HALO, :] = buf_ref[tt:tt + CONV_HALO, :]

    def glu(ci, carry):
        r0 = pl.multiple_of(ci * rows, rows)
        buf_ref[pl.ds(CONV_HALO + r0, rows), :] = (ua_ref[0, pl.ds(r0, rows), :]
                                                   * jax.nn.sigmoid(ug_ref[0, pl.ds(r0, rows), :]))
        return carry

    lax.fori_loop(0, tt // rows, glu, 0)

    def conv(ci, carry):
        r0 = pl.multiple_of(ci * rows, rows)
        for c0 in range(0, ch, lanes):
            cs = slice(c0, c0 + lanes)
            win = buf_ref.at[pl.ds(r0, rows + CONV_HALO), pl.ds(c0, lanes)]
            acc = jnp.broadcast_to(par_ref[i_bias, :, cs][None], (rows // 8, 8, lanes))
            for ph, taps in _conv_taps():
                a_lo, a_hi = taps[0][1], taps[-1][1]
                x = win[8 * a_lo + ph:8 * a_hi + ph + rows, :]
                for j, a in taps:
                    xs = x[8 * (a - a_lo):8 * (a - a_lo) + rows, :].reshape(rows // 8, 8, lanes)
                    acc = acc + par_ref[j, :, cs][None] * xs
            z_ref[pl.ds(r0, rows), cs] = acc.reshape(rows, lanes)
        return carry

    lax.fori_loop(0, tt // rows, conv, 0)

    def norm(ci, carry):
        r0 = pl.multiple_of(ci * nrows, nrows)
        z = z_ref[pl.ds(r0, nrows), :]
        xc = z - jnp.mean(z, axis=-1, keepdims=True)
        xn = (xc * lax.rsqrt(jnp.mean(xc * xc, axis=-1, keepdims=True) + EPS)).reshape(nrows // 8, 8, ch)
        y = (xn * par_ref[i_gain][None] + par_ref[i_beta][None]).reshape(nrows, ch)
        y_ref[0, pl.ds(r0, nrows), :] = y * jax.nn.sigmoid(y)
        return carry

    lax.fori_loop(0, tt // nrows, norm, 0, unroll=min(8, tt // nrows))

    @pl.when(ti == pl.num_programs(1) - 1)
    def _():
        st_ref[0] = buf_ref[tt + pad:tt + CONV_HALO, :]


def _conformer_conv(qu, prev, layer, w_dw, b_dw, g_ln, b_ln, tt):
    _, b, t, ch = qu.shape
    assert ch == w_dw.shape[1]
    hist = CONV_WIDTH - 1
    assert t % tt == 0 and tt % 8 == 0
    rows = 64 if tt % 64 == 0 else 8
    nrows = 16 if tt % 16 == 0 else 8
    lanes = 256 if ch % 256 == 0 else ch
    row = lambda a: a.reshape(1, ch)
    return pl.pallas_call(
        functools.partial(_conv_kernel, tt=tt, rows=rows, nrows=nrows, lanes=lanes),
        grid=(b, t // tt),
        in_specs=[pl.BlockSpec((None, 1, tt, ch), lambda bi, ti: (1, bi, ti, 0)),
                  pl.BlockSpec((None, 1, tt, ch), lambda bi, ti: (2, bi, ti, 0)),
                  pl.BlockSpec((None, 1, hist, ch), lambda bi, ti: (layer, bi, 0, 0)),
                  pl.BlockSpec((CONV_WIDTH, ch), lambda bi, ti: (0, 0)),
                  pl.BlockSpec((1, ch), lambda bi, ti: (0, 0)),
                  pl.BlockSpec((1, ch), lambda bi, ti: (0, 0)),
                  pl.BlockSpec((1, ch), lambda bi, ti: (0, 0))],
        out_specs=[pl.BlockSpec((1, tt, ch), lambda bi, ti: (bi, ti, 0)),
                   pl.BlockSpec((1, hist, ch), lambda bi, ti: (bi, 0, 0))],
        out_shape=[jax.ShapeDtypeStruct((b, t, ch), F32), jax.ShapeDtypeStruct((b, hist, ch), F32)],
        scratch_shapes=[pltpu.VMEM((tt + CONV_HALO, ch), F32), pltpu.VMEM((CONV_WIDTH + 3, 8, ch), F32),
                        pltpu.VMEM((tt, ch), F32)],
        compiler_params=_params("arbitrary", "arbitrary"),
        name="conformer_conv",
    )(qu, qu, prev, w_dw, row(b_dw), row(g_ln), row(b_ln))


def _mix_out_kernel(attn_ref, conv_ref, h_ref, ga_ref, gc_ref, w_ref, gp_ref, o_ref):
    wa = attn_ref.shape[1]
    a = _rms(attn_ref[...], ga_ref[...]).astype(BF16)
    cv = _rms(conv_ref[...], gc_ref[...]).astype(BF16)
    mixed = (jnp.dot(a, w_ref[0:wa, :], preferred_element_type=F32)
             + jnp.dot(cv, w_ref[wa:, :], preferred_element_type=F32))
    o_ref[...] = h_ref[...] + _rms(mixed, gp_ref[...])


def _mix_out(attn, conv, h, g_attn, g_conv, w_out, g_post, tm):
    m, wa = attn.shape
    wc = conv.shape[1]
    d = h.shape[1]
    assert m % tm == 0
    row = lambda a: a.reshape(1, -1)
    return pl.pallas_call(
        _mix_out_kernel,
        grid=(m // tm,),
        in_specs=[pl.BlockSpec((tm, wa), lambda i: (i, 0)),
                  pl.BlockSpec((tm, wc), lambda i: (i, 0)),
                  pl.BlockSpec((tm, d), lambda i: (i, 0)),
                  pl.BlockSpec((1, wa), lambda i: (0, 0)),
                  pl.BlockSpec((1, wc), lambda i: (0, 0)),
                  pl.BlockSpec((wa + wc, d), lambda i: (0, 0)),
                  pl.BlockSpec((1, d), lambda i: (0, 0))],
        out_specs=pl.BlockSpec((tm, d), lambda i: (i, 0)),
        out_shape=jax.ShapeDtypeStruct((m, d), F32),
        compiler_params=_params("arbitrary"),
        name="mix_out",
    )(attn, conv, h, row(g_attn), row(g_conv), w_out, row(g_post))


def _xattn_kernel(h_ref, mk_ref, mv_ref, gpre_ref, wq_ref, wo_ref, gpost_ref, o_ref):
    h = h_ref[...]
    xn = _rms(h, gpre_ref[...]).astype(BF16)
    xq = jnp.dot(xn, wq_ref[...], preferred_element_type=F32)
    scale = X_HEAD_DIM ** -0.5
    outs = []
    for hh in range(X_HEADS):
        cols = slice(hh * X_HEAD_DIM, (hh + 1) * X_HEAD_DIM)
        s = lax.dot_general(xq[:, cols].astype(BF16), mk_ref[0, :, cols].astype(BF16), _NT,
                            preferred_element_type=F32) * scale
        p = jnp.exp(s - jnp.max(s, axis=1, keepdims=True))
        l = jnp.sum(p, axis=1, keepdims=True)
        outs.append(jnp.dot(p.astype(BF16), mv_ref[0, :, cols].astype(BF16), preferred_element_type=F32) / l)
    o = jnp.concatenate(outs, axis=1).astype(BF16)
    y = jnp.dot(o, wo_ref[...], preferred_element_type=F32)
    o_ref[...] = h + _rms(y, gpost_ref[...])


def _xattn(h, mem_k, mem_v, layer, g_pre, w_xq, w_xo, g_post, rows_per_batch, tm):
    m, d = h.shape
    assert rows_per_batch % tm == 0
    tiles = rows_per_batch // tm
    mem_len, xw = mem_k.shape[2:]
    row = lambda a: a.reshape(1, -1)
    return pl.pallas_call(
        _xattn_kernel,
        grid=(m // tm,),
        in_specs=[pl.BlockSpec((tm, d), lambda i: (i, 0)),
                  pl.BlockSpec((None, 1, mem_len, xw), lambda i: (layer, i // tiles, 0, 0)),
                  pl.BlockSpec((None, 1, mem_len, xw), lambda i: (layer, i // tiles, 0, 0)),
                  pl.BlockSpec((1, d), lambda i: (0, 0)),
                  pl.BlockSpec((d, xw), lambda i: (0, 0)),
                  pl.BlockSpec((xw, d), lambda i: (0, 0)),
                  pl.BlockSpec((1, d), lambda i: (0, 0))],
        out_specs=pl.BlockSpec((tm, d), lambda i: (i, 0)),
        out_shape=jax.ShapeDtypeStruct((m, d), F32),
        compiler_params=_params("arbitrary"),
        name="xattn",
    )(h, mem_k, mem_v, row(g_pre), w_xq, w_xo, row(g_post))


def _ffn_kernel(h_ref, gpre_ref, wg_ref, wu_ref, wo_ref, gpost_ref, o_ref, *rest, emit):
    if emit:
        wg_out, wu_out, wo_out, xn_ref, acc_ref = rest
        wg_out[...] = wg_ref[...].astype(BF16)
        wu_out[...] = wu_ref[...].astype(BF16)
        wo_out[...] = wo_ref[...].astype(BF16)
        wg_ref, wu_ref, wo_ref = wg_out, wu_out, wo_out
    else:
        xn_ref, acc_ref = rest
    f = pl.program_id(1)

    @pl.when(f == 0)
    def _():
        xn_ref[...] = _rms(h_ref[...], gpre_ref[...]).astype(BF16)
        acc_ref[...] = jnp.zeros(acc_ref.shape, F32)

    xn = xn_ref[...]
    g = jnp.dot(xn, wg_ref[...], preferred_element_type=F32)
    u = jnp.dot(xn, wu_ref[...], preferred_element_type=F32)
    a = (g * jax.nn.sigmoid(g) * u).astype(BF16)
    acc_ref[...] += jnp.dot(a, wo_ref[...], preferred_element_type=F32)

    @pl.when(f == pl.num_programs(1) - 1)
    def _():
        o_ref[...] = h_ref[...] + _rms(acc_ref[...], gpost_ref[...])


def _ffn(h, g_pre, w, g_post, tm, tf, emit):
    m, d = h.shape
    row = lambda a: a.reshape(1, -1)
    if emit:
        w_in, w_out = w
        hidden = w_out.shape[0]
        nf = hidden // tf
        assert m == tm and hidden % tf == 0 and w_in.shape[1] == 2 * hidden
        w_args = (w_in, w_in, w_out)
        w_specs = [pl.BlockSpec((d, tf), lambda i, f: (0, f)),
                   pl.BlockSpec((d, tf), lambda i, f: (0, f + nf)),
                   pl.BlockSpec((tf, d), lambda i, f: (f, 0))]
    else:
        w_args = w
        nf, hidden = w[0].shape[0], w[2].shape[0]
        assert m % tm == 0 and w[0].shape == w[1].shape == (nf, d, tf) and hidden == nf * tf
        w_specs = [pl.BlockSpec((None, d, tf), lambda i, f: (f, 0, 0)),
                   pl.BlockSpec((None, d, tf), lambda i, f: (f, 0, 0)),
                   pl.BlockSpec((tf, d), lambda i, f: (f, 0))]
    out_specs = [pl.BlockSpec((tm, d), lambda i, f: (i, 0))]
    out_shape = [jax.ShapeDtypeStruct((m, d), F32)]
    if emit:
        out_specs += [pl.BlockSpec((None, d, tf), lambda i, f: (f, 0, 0)),
                      pl.BlockSpec((None, d, tf), lambda i, f: (f, 0, 0)),
                      pl.BlockSpec((tf, d), lambda i, f: (f, 0))]
        out_shape += [jax.ShapeDtypeStruct((nf, d, tf), BF16), jax.ShapeDtypeStruct((nf, d, tf), BF16),
                      jax.ShapeDtypeStruct((hidden, d), BF16)]
    outs = pl.pallas_call(
        functools.partial(_ffn_kernel, emit=emit),
        grid=(m // tm, nf),
        in_specs=[pl.BlockSpec((tm, d), lambda i, f: (i, 0)),
                  pl.BlockSpec((1, d), lambda i, f: (0, 0)),
                  *w_specs,
                  pl.BlockSpec((1, d), lambda i, f: (0, 0))],
        out_specs=out_specs,
        out_shape=out_shape,
        scratch_shapes=[pltpu.VMEM((tm, d), BF16), pltpu.VMEM((tm, d), F32)],
        compiler_params=_params("arbitrary", "arbitrary"),
        name="ffn",
    )(h, row(g_pre), *w_args, row(g_post))
    return outs[0], tuple(outs[1:])


def _row_tile(m):
    return 512 if m % 512 == 0 else m


def _project_in(x, wts, w_in, emit):
    b, t, d = x.shape
    m = b * t
    assert wts["w_dw"].shape[1] == ATTN_WIDTH
    k_new, v_new, qu, *w_in_b = _in_proj(x.reshape(m, d), wts["g_pre_mix"].reshape(1, d), w_in, emit,
                                         tm=1024 if m % 1024 == 0 else _row_tile(m))
    k_new, v_new = k_new.reshape(b, t, ATTN_WIDTH), v_new.reshape(b, t, ATTN_WIDTH)
    return k_new, v_new, qu.reshape(qu.shape[0], b, t, ATTN_WIDTH), (w_in_b[0] if emit else None)


def _after_attention(x, attn, qu, state, state_layer, wts, w_ffn, emit, conv_tile, x_tile):
    conv_prev, mem_k, mem_v = state
    b, t, d = x.shape
    m = b * t
    tm = _row_tile(m)
    ch = wts["w_dw"].shape[1]
    conv, conv_state = _conformer_conv(qu, conv_prev, state_layer, wts["w_dw"], wts["b_dw"],
                                       wts["g_ln_conv"], wts["b_ln_conv"], conv_tile)
    h = _mix_out(attn.reshape(m, ATTN_WIDTH), conv.reshape(m, ch), x.reshape(m, d), wts["g_attn_grp"],
                 wts["g_conv_grp"], wts["w_out"], wts["g_post_mix"], tm=tm)
    h = _xattn(h, mem_k, mem_v, state_layer, wts["g_pre_x"], wts["w_xq"], wts["w_xo"], wts["g_post_x"],
               rows_per_batch=t, tm=x_tile)
    h, w_ffn_b = _ffn(h, wts["g_pre_ffn"], w_ffn, wts["g_post_ffn"], tm=tm, tf=512, emit=emit)
    return h.reshape(b, t, d), conv_state, (w_ffn_b if emit else None)


def kernel(x_prompt, x_sample, cache_k, cache_v, state_conv, cache_mem_k, cache_mem_v, page_table,
           mem_prompt, g_mem, w_mem_k, w_mem_v, g_pre_mix, w_in, w_dw, b_dw, g_ln_conv, b_ln_conv,
           g_attn_grp, g_conv_grp, w_out, g_post_mix, g_pre_x, w_xq, w_xo, g_post_x,
           g_pre_ffn, w_ffn_in, w_ffn_out, g_post_ffn):
    depth = w_in.shape[0]
    bp, tp, d = x_prompt.shape
    bs, ts, _ = x_sample.shape
    mem_len = mem_prompt.shape[1]
    xw = X_HEADS * X_HEAD_DIM
    ch = w_dw.shape[2]
    hp, hs = x_prompt, x_sample
    outs = [[] for _ in range(8)]
    for l in range(depth):
        wts = dict(g_pre_mix=g_pre_mix[l], w_dw=w_dw[l], b_dw=b_dw[l],
                   g_ln_conv=g_ln_conv[l], b_ln_conv=b_ln_conv[l], g_attn_grp=g_attn_grp[l],
                   g_conv_grp=g_conv_grp[l], w_out=w_out[l].astype(BF16), g_post_mix=g_post_mix[l],
                   g_pre_x=g_pre_x[l], w_xq=w_xq[l].astype(BF16), w_xo=w_xo[l].astype(BF16),
                   g_post_x=g_post_x[l], g_pre_ffn=g_pre_ffn[l], g_post_ffn=g_post_ffn[l])

        ks, vs, qu_s, w_in_b = _project_in(hs, wts, w_in[l], True)
        kp, vp, qu_p, _ = _project_in(hp, wts, w_in_b, False)
        attn_p, scores, sel = _moba_and_scores(qu_p, kp, vp, qu_s, cache_k, page_table, l)
        attn_s = _sample_attend(scores, sel, qu_s, ks, vs, cache_v, page_table, l)

        sample_state = (state_conv, cache_mem_k.reshape(depth, bs, mem_len, xw),
                        cache_mem_v.reshape(depth, bs, mem_len, xw))
        hs, cs, w_ffn_b = _after_attention(hs, attn_s, qu_s, sample_state, l, wts, (w_ffn_in[l], w_ffn_out[l]),
                                           True, conv_tile=ts, x_tile=ts)

        w_mem = jnp.concatenate([w_mem_k[l], w_mem_v[l]], axis=1).astype(BF16)
        mk_p, mv_p = _norm_matmul(mem_prompt.reshape(bp * mem_len, d), g_mem[l].reshape(1, d), w_mem,
                                  (xw, xw), tn=xw, tm=_row_tile(bp * mem_len))
        mk_p, mv_p = mk_p.reshape(bp, mem_len, xw), mv_p.reshape(bp, mem_len, xw)
        prompt_state = (jnp.zeros((1, bp, CONV_WIDTH - 1, ch), F32), mk_p[None], mv_p[None])
        hp, cp, _ = _after_attention(hp, attn_p, qu_p, prompt_state, 0, wts, w_ffn_b, False,
                                     conv_tile=_row_tile(tp), x_tile=_row_tile(tp))
        mem_shape = (bp, mem_len, X_HEADS, X_HEAD_DIM)
        heads = lambda a: a.reshape(a.shape[:2] + (N_HEADS, HEAD_DIM))
        for lst, a in zip(outs, (heads(kp), heads(vp), cp, mk_p.reshape(mem_shape), mv_p.reshape(mem_shape),
                                 heads(ks), heads(vs), cs)):
            lst.append(a)
    return (hp, hs) + tuple(jnp.stack(lst, 0) for lst in outs)
```

```python
import functools

import jax
import jax.numpy as jnp
from jax import lax
from jax.experimental import pallas as pl
from jax.experimental.pallas import tpu as pltpu

EPS = 1e-6
N_HEADS = 8
HEAD_DIM = 128
ATTN_WIDTH = N_HEADS * HEAD_DIM
CONV_WIDTH = 31
MOBA_BLOCK = 256
MOBA_TOPK = 3
X_HEADS = 4
X_HEAD_DIM = 128

CONV_HALO = 32
DMA_PRIORITIES = 2
ONES_ROWS = 16
VMEM_LIMIT = 56 * 1024 * 1024

F32 = jnp.float32
BF16 = jnp.bfloat16
NEG_INF = float("-inf")
LOG2_E = 1.4426950408889634
_NT = (((1,), (1,)), ((), ()))


def _params(*sem):
    return pltpu.CompilerParams(dimension_semantics=sem, vmem_limit_bytes=VMEM_LIMIT)


def _rms(x, g):
    return x * lax.rsqrt(jnp.mean(x * x, axis=-1, keepdims=True) + EPS) * g


def _top_blocks(gate, n_valid, axis):
    nb = gate.shape[axis]
    pos = lax.broadcasted_iota(jnp.int32, gate.shape, axis)
    g = jnp.where(pos < n_valid, gate, NEG_INF)
    idxs, oks = [], []
    for _ in range(MOBA_TOPK):
        m = jnp.max(g, axis=axis, keepdims=True)
        idx = jnp.min(jnp.where(g == m, pos, nb), axis=axis, keepdims=True)
        ok = m > NEG_INF
        idxs.append(idx)
        oks.append(ok)
        g = jnp.where((pos == idx) & ok, NEG_INF, g)
    return idxs, oks


def _norm_matmul_kernel(x_ref, g_ref, w_ref, *rest, tile_ranges):
    out_refs, xn_ref = rest[:-1], rest[-1]
    j = pl.program_id(1)

    @pl.when(j == 0)
    def _():
        xn_ref[...] = _rms(x_ref[...], g_ref[...]).astype(BF16)

    for o_ref, (lo, hi) in zip(out_refs, tile_ranges):
        @pl.when((j >= lo) & (j < hi))
        def _(o_ref=o_ref):
            o_ref[...] = jnp.dot(xn_ref[...], w_ref[...], preferred_element_type=F32)


def _norm_matmul(x, g, w, widths, tn, tm):
    m, d = x.shape
    n = w.shape[1]
    assert sum(widths) == n and all(wd % tn == 0 for wd in widths) and m % tm == 0
    tile_ranges, lo = [], 0
    for wd in widths:
        tile_ranges.append((lo, lo + wd // tn))
        lo += wd // tn

    def out_map(i, j, lo, cnt):
        return (i, jnp.clip(j - lo, 0, cnt - 1))

    out_specs = [pl.BlockSpec((tm, tn), functools.partial(out_map, lo=lo, cnt=hi - lo))
                 for lo, hi in tile_ranges]
    return pl.pallas_call(
        functools.partial(_norm_matmul_kernel, tile_ranges=tuple(tile_ranges)),
        grid=(m // tm, n // tn),
        in_specs=[pl.BlockSpec((tm, d), lambda i, j: (i, 0)),
                  pl.BlockSpec((1, d), lambda i, j: (0, 0)),
                  pl.BlockSpec((d, tn), lambda i, j: (0, j))],
        out_specs=out_specs,
        out_shape=[jax.ShapeDtypeStruct((m, wd), F32) for wd in widths],
        scratch_shapes=[pltpu.VMEM((tm, d), BF16)],
        compiler_params=_params("arbitrary", "arbitrary"),
        name="norm_matmul",
    )(x, g, w)


def _in_proj_src_tile(j):
    return jnp.where(j < 3, (j + 1) % 3, j)


def _in_proj_kernel(x_ref, g_ref, w_ref, *rest, emit):
    if emit:
        k_ref, v_ref, qu_ref, wt_ref, xn_ref = rest
    else:
        k_ref, v_ref, qu_ref, xn_ref = rest
    j = pl.program_id(1)

    @pl.when(j == 0)
    def _():
        xn_ref[...] = _rms(x_ref[...], g_ref[...]).astype(BF16)

    if emit:
        wt_ref[...] = w_ref[...].astype(BF16)
        w_ref = wt_ref

    for o_ref, cond in ((k_ref, j == 0), (v_ref, j == 1), (qu_ref, j >= 2)):
        @pl.when(cond)
        def _(o_ref=o_ref):
            o_ref[...] = jnp.dot(xn_ref[...], w_ref[...], preferred_element_type=F32)


def _in_proj(x, g, w, emit, tm):
    m, d = x.shape
    tn = ATTN_WIDTH
    nt = (w.shape[1] // tn) if emit else w.shape[0]
    assert m % tm == 0 and nt > 3 and (not emit or (m == tm and w.shape[1] % tn == 0))
    if emit:
        w_spec = pl.BlockSpec((d, tn), lambda i, j: (0, _in_proj_src_tile(j)))
    else:
        w_spec = pl.BlockSpec((None, d, tn), lambda i, j: (j, 0, 0))
    out_specs = [pl.BlockSpec((tm, tn), lambda i, j: (i, 0)),
                 pl.BlockSpec((tm, tn), lambda i, j: (i, 0)),
                 pl.BlockSpec((None, tm, tn), lambda i, j: (jnp.maximum(j - 2, 0), i, 0))]
    out_shape = [jax.ShapeDtypeStruct((m, tn), F32), jax.ShapeDtypeStruct((m, tn), F32),
                 jax.ShapeDtypeStruct((nt - 2, m, tn), F32)]
    if emit:
        out_specs.append(pl.BlockSpec((None, d, tn), lambda i, j: (j, 0, 0)))
        out_shape.append(jax.ShapeDtypeStruct((nt, d, tn), BF16))
    return pl.pallas_call(
        functools.partial(_in_proj_kernel, emit=emit),
        grid=(m // tm, nt),
        in_specs=[pl.BlockSpec((tm, d), lambda i, j: (i, 0)),
                  pl.BlockSpec((1, d), lambda i, j: (0, 0)),
                  w_spec],
        out_specs=out_specs,
        out_shape=out_shape,
        scratch_shapes=[pltpu.VMEM((tm, d), BF16)],
        compiler_params=_params("arbitrary", "arbitrary"),
        name="in_proj",
    )(x, g, w)


def _moba_prompt_body(h, g, q_ref, k_ref, v_ref, o_ref, kb_ref, vt_ref, km_ref, *, nb, tps):
    blk = MOBA_BLOCK

    @pl.when(g == 0)
    def _():
        for n in range(nb):
            kf = k_ref[0, n * blk:(n + 1) * blk, :]
            kb_ref[n * blk:(n + 1) * blk, :] = kf.astype(BF16)
            vt_ref[n, 0:HEAD_DIM, :] = v_ref[0, n * blk:(n + 1) * blk, :].T.astype(BF16)
            vt_ref[n, HEAD_DIM:, :] = jnp.ones((ONES_ROWS, blk), BF16)
            km_ref[n:n + 1, :] = jnp.mean(kf, axis=0, keepdims=True)

    scale = HEAD_DIM ** -0.5 * LOG2_E
    slope = jnp.exp2(jnp.broadcast_to(-8.0 * (h + 1).astype(F32) / N_HEADS, (1, blk))) * LOG2_E

    def attend(i, qi):
        q = q_ref[0, qi * blk:(qi + 1) * blk, :]
        qb = q.astype(BF16)
        kr = lax.broadcasted_iota(jnp.int32, (blk, blk), 0)
        qc = lax.broadcasted_iota(jnp.int32, (blk, blk), 1)
        base = slope * (qc - kr).astype(F32)
        s_own = lax.dot_general(kb_ref[i * blk:(i + 1) * blk, :], qb, _NT, preferred_element_type=F32)
        ts = [jnp.where(kr <= qc, s_own * scale - base, NEG_INF)]
        if i:
            gate = lax.dot_general(km_ref[0:i, :], q, _NT, precision=lax.Precision.HIGHEST,
                                   preferred_element_type=F32)
            idxs, oks = _top_blocks(gate, i, axis=0)
            row = lax.broadcasted_iota(jnp.int32, (i, blk), 0)
            sel = jnp.zeros((i, blk), F32)
            for idx, ok in zip(idxs, oks):
                sel = jnp.where((row == idx) & ok, 1.0, sel)
            for n in range(i):
                s = lax.dot_general(kb_ref[n * blk:(n + 1) * blk, :], qb, _NT, preferred_element_type=F32)
                neg = jnp.where(sel[n:n + 1, :] > 0.0, -slope * float((i - n) * blk), NEG_INF)
                ts.append(s * scale - base + neg)
        m = jnp.max(functools.reduce(jnp.maximum, ts), axis=0, keepdims=True)
        acc = None
        for n, t in enumerate(ts):
            p = jnp.exp2(t - m)
            vt = vt_ref[i] if n == 0 else vt_ref[n - 1]
            pv = jnp.dot(vt, p.astype(BF16), preferred_element_type=F32)
            acc = pv if acc is None else acc + pv
        l = acc[HEAD_DIM:HEAD_DIM + 1, :]
        o_ref[0, qi * blk:(qi + 1) * blk, :] = (acc[0:HEAD_DIM, :] / l).T

    for v in range(nb // tps):
        @pl.when(g == v)
        def _(v=v):
            for qi in range(tps):
                attend(v * tps + qi, qi)


def _head_page_copy(cache_ref, pt_ref, buf_ref, sem, layer, b, h, page_slot, dst_row, n_pages):
    page = pt_ref[b * n_pages + page_slot]
    ps = cache_ref.shape[2]
    return pltpu.make_async_copy(cache_ref.at[layer, page, :, h, :], buf_ref.at[pl.ds(dst_row, ps), :], sem)


def _start_k_pages(st, first, count, pt_ref, kc_ref, kbuf_ref, sem_ref, *, layer, n_pages):
    bb, hh = st // N_HEADS, st % N_HEADS
    sl = st % 2
    ps = kc_ref.shape[2]

    def issue(pair, carry):
        for k in range(DMA_PRIORITIES):
            p = first + pair * DMA_PRIORITIES + k
            _head_page_copy(kc_ref, pt_ref, kbuf_ref.at[sl], sem_ref.at[sl], layer, bb, hh,
                            p, pl.multiple_of(p * ps, ps), n_pages).start(priority=k)
        return carry

    lax.fori_loop(0, count // DMA_PRIORITIES, issue, 0, unroll=4)


def _sample_scores_body(step, n_steps, pt_ref, q_ref, kc_ref, s_ref, sel_ref, kbuf_ref, km_ref, gate_ref, sem_ref,
                        *, layer, n_pages, chunk):
    nh = N_HEADS
    b, h = step // nh, step % nh
    slot = step % 2
    ps = kc_ref.shape[2]
    past = n_pages * ps
    blk = MOBA_BLOCK
    t = q_ref.shape[1]
    start_pages = functools.partial(_start_k_pages, pt_ref=pt_ref, kc_ref=kc_ref, kbuf_ref=kbuf_ref,
                                    sem_ref=sem_ref, layer=layer, n_pages=n_pages)

    @pl.when(step == 0)
    def _():
        start_pages(step, 0, n_pages)

    @pl.when(step + 1 < n_steps)
    def _():
        start_pages(step + 1, 0, n_pages)

    pltpu.make_async_copy(kbuf_ref.at[1 - slot], kbuf_ref.at[slot], sem_ref.at[slot]).wait()

    q = q_ref[0]
    qb = q.astype(BF16)
    bpc = chunk // blk

    def body(ci, carry):
        start = pl.multiple_of(ci * chunk, chunk)
        kf = kbuf_ref[slot, pl.ds(start, chunk), :]
        sc = lax.dot_general(qb, kf.astype(BF16), _NT, preferred_element_type=F32)
        for jj in range(bpc):
            s_ref[0, 0, ci * bpc + jj] = sc[:, jj * blk:(jj + 1) * blk]
        km_ref[pl.ds(pl.multiple_of(ci * bpc, bpc), bpc), :] = jnp.mean(
            kf.reshape(bpc, blk, HEAD_DIM), axis=1)
        return carry

    lax.fori_loop(0, past // chunk, body, 0, unroll=True)

    gate_ref[h] = lax.dot_general(q, km_ref[...], _NT, precision=lax.Precision.HIGHEST,
                                  preferred_element_type=F32)

    @pl.when(h == nh - 1)
    def _():
        gate = gate_ref[...].reshape(nh * t, past // blk)
        idxs, _ = _top_blocks(gate, past // blk, axis=1)
        lane = lax.broadcasted_iota(jnp.int32, (nh * t, 128), 1)
        out = jnp.zeros((nh * t, 128), jnp.int32)
        for k, idx in enumerate(idxs):
            out = jnp.where(lane == k, idx, out)
        sel_ref[0] = out.reshape(nh, t, 128)


def _moba_scores_kernel(pt_ref, qs_ref, kc_ref, qp_ref, kp_ref, vp_ref, s_ref, sel_ref, o_ref,
                        kbuf_ref, km_ref, gate_ref, sem_ref, kb_ref, vt_ref, kmp_ref,
                        *, layer, n_pages, chunk, nb, tps, every):
    step = pl.program_id(0)
    n_steps = pl.num_programs(0)
    _sample_scores_body(step, n_steps, pt_ref, qs_ref, kc_ref, s_ref, sel_ref, kbuf_ref, km_ref,
                        gate_ref, sem_ref, layer=layer, n_pages=n_pages, chunk=chunk)

    @pl.when(step % every == 0)
    def _():
        item = step // every
        groups = nb // tps
        _moba_prompt_body((item // groups) % N_HEADS, item % groups, qp_ref, kp_ref, vp_ref, o_ref,
                          kb_ref, vt_ref, kmp_ref, nb=nb, tps=tps)


def _moba_and_scores(qu_p, k_p, v_p, qu_s, cache_k, page_table, layer, tps=4):
    _, bs, ts, _ = qu_s.shape
    bp, tp, _ = k_p.shape
    blk = MOBA_BLOCK
    n_pages = page_table.shape[1]
    ps = cache_k.shape[2]
    past = n_pages * ps
    chunk = 8 * blk
    assert past % chunk == 0 and past // blk >= MOBA_TOPK and n_pages % DMA_PRIORITIES == 0
    tps = min(tps, tp // blk)
    assert tp % (blk * tps) == 0 and n_pages % (tps * DMA_PRIORITIES) == 0
    nb = tp // blk
    groups = nb // tps
    n_steps, items = bs * N_HEADS, bp * N_HEADS * groups
    assert n_steps % items == 0
    every = n_steps // items

    def prompt_idx(step):
        item = step // every
        return item // (N_HEADS * groups), (item // groups) % N_HEADS, item % groups

    def qp_map(step, pt):
        bi, h, g = prompt_idx(step)
        return (0, bi, g, h)

    def kv_map(step, pt):
        bi, h, g = prompt_idx(step)
        return (bi, 0, h)

    def o_map(step, pt):
        bi, h, g = prompt_idx(step)
        return (bi, g, h)

    grid_spec = pltpu.PrefetchScalarGridSpec(
        num_scalar_prefetch=1,
        grid=(n_steps,),
        in_specs=[pl.BlockSpec((None, 1, ts, HEAD_DIM), lambda s, pt: (0, s // N_HEADS, 0, s % N_HEADS)),
                  pl.BlockSpec(memory_space=pl.ANY),
                  pl.BlockSpec((None, 1, blk * tps, HEAD_DIM), qp_map),
                  pl.BlockSpec((1, tp, HEAD_DIM), kv_map),
                  pl.BlockSpec((1, tp, HEAD_DIM), kv_map)],
        out_specs=[pl.BlockSpec((1, 1, past // blk, ts, blk), lambda s, pt: (s // N_HEADS, s % N_HEADS, 0, 0, 0)),
                   pl.BlockSpec((1, N_HEADS, ts, 128), lambda s, pt: (s // N_HEADS, 0, 0, 0)),
                   pl.BlockSpec((1, blk * tps, HEAD_DIM), o_map)],
        scratch_shapes=[pltpu.VMEM((2, past, HEAD_DIM), F32),
                        pltpu.VMEM((past // blk, HEAD_DIM), F32),
                        pltpu.VMEM((N_HEADS, ts, past // blk), F32),
                        pltpu.SemaphoreType.DMA((2,)),
                        pltpu.VMEM((tp, HEAD_DIM), BF16), pltpu.VMEM((nb, HEAD_DIM + ONES_ROWS, blk), BF16),
                        pltpu.VMEM((nb, HEAD_DIM), F32)],
    )
    scores, sel, attn = pl.pallas_call(
        functools.partial(_moba_scores_kernel, layer=layer, n_pages=n_pages, chunk=chunk, nb=nb, tps=tps,
                          every=every),
        grid_spec=grid_spec,
        out_shape=[jax.ShapeDtypeStruct((bs, N_HEADS, past // blk, ts, blk), F32),
                   jax.ShapeDtypeStruct((bs, N_HEADS, ts, 128), jnp.int32),
                   jax.ShapeDtypeStruct((bp, tp, ATTN_WIDTH), F32)],
        compiler_params=_params("arbitrary"),
        name="moba_and_scores",
    )(page_table.reshape(-1), qu_s, cache_k, qu_p, k_p, v_p)
    return attn, scores, sel


def _sample_attend_kernel(pt_ref, sel_ref, s_ref, q_ref, kn_ref, vn_ref, vc_ref, o_ref,
                          vbuf_ref, ssel_ref, sem_ref, *, layer, n_pages, q_start):
    b = pl.program_id(0)
    h = pl.program_id(1)
    nbt, nh = pl.num_programs(0), pl.num_programs(1)
    step = b * nh + h
    slot = step % 2
    t = q_ref.shape[1]
    ps = vc_ref.shape[2]
    blk = MOBA_BLOCK
    ppb = blk // ps
    nsel = t * MOBA_TOPK

    def sel_block(st, e):
        return sel_ref[st * nsel + e]

    def copies(st, sl):
        bb, hh = st // nh, st % nh

        def mk(e, pg):
            return _head_page_copy(vc_ref, pt_ref, vbuf_ref.at[sl], sem_ref.at[sl], layer, bb, hh,
                                   sel_block(st, e) * ppb + pg, e * blk + pg * ps, n_pages)
        return mk

    def start_all(st, sl):
        mk = copies(st, sl)
        for e in range(nsel):
            for pg in range(ppb):
                mk(e, pg).start(priority=(e * ppb + pg) % DMA_PRIORITIES)

    @pl.when(step == 0)
    def _():
        start_all(step, slot)

    @pl.when(step + 1 < nbt * nh)
    def _():
        start_all(step + 1, 1 - slot)

    scale = HEAD_DIM ** -0.5
    slope = jnp.exp2(jnp.broadcast_to(-8.0 * (h + 1).astype(F32) / N_HEADS, (1, 1)))
    lane = lax.broadcasted_iota(jnp.int32, (1, blk), 1)

    ssel_ref[...] = jnp.full(ssel_ref.shape, NEG_INF, F32)
    for e in range(nsel):
        ti = e // MOBA_TOPK
        n = sel_block(step, e)
        raw = s_ref[0, 0, n, ti:ti + 1, :]
        dist = (q_start + ti - n * blk - lane).astype(F32)
        ssel_ref[ti:ti + 1, e * blk:(e + 1) * blk] = raw * scale - slope * dist

    q = q_ref[0].astype(BF16)
    r = lax.broadcasted_iota(jnp.int32, (t, t), 0)
    c = lax.broadcasted_iota(jnp.int32, (t, t), 1)
    s_own = lax.dot_general(q, kn_ref[0].astype(BF16), _NT, preferred_element_type=F32) * scale
    s_own = jnp.where(c <= r, s_own - slope * (r - c).astype(F32), NEG_INF)

    s_sel = ssel_ref[...]
    m = jnp.maximum(jnp.max(s_sel, axis=1, keepdims=True), jnp.max(s_own, axis=1, keepdims=True))
    p_sel = jnp.exp(s_sel - m)
    p_own = jnp.exp(s_own - m)
    l = jnp.sum(p_sel, axis=1, keepdims=True) + jnp.sum(p_own, axis=1, keepdims=True)

    mk = copies(step, slot)
    for e in range(nsel):
        for pg in range(ppb):
            mk(e, pg).wait()

    acc = jnp.dot(p_sel.astype(BF16), vbuf_ref[slot].astype(BF16), preferred_element_type=F32)
    acc = acc + jnp.dot(p_own.astype(BF16), vn_ref[0].astype(BF16), preferred_element_type=F32)
    o_ref[0] = acc / l


def _sample_attend(scores, sel, qu, k_new, v_new, cache_v, page_table, layer):
    b, t, _ = k_new.shape
    n_pages = page_table.shape[1]
    ps = cache_v.shape[2]
    past = n_pages * ps
    assert MOBA_BLOCK % ps == 0 and past % MOBA_BLOCK == 0 and t <= MOBA_BLOCK
    nsel = t * MOBA_TOPK
    grid_spec = pltpu.PrefetchScalarGridSpec(
        num_scalar_prefetch=2,
        grid=(b, N_HEADS),
        in_specs=[pl.BlockSpec((1, 1, past // MOBA_BLOCK, t, MOBA_BLOCK), lambda bi, h, pt, sl: (bi, h, 0, 0, 0)),
                  pl.BlockSpec((None, 1, t, HEAD_DIM), lambda bi, h, pt, sl: (0, bi, 0, h)),
                  pl.BlockSpec((1, t, HEAD_DIM), lambda bi, h, pt, sl: (bi, 0, h)),
                  pl.BlockSpec((1, t, HEAD_DIM), lambda bi, h, pt, sl: (bi, 0, h)),
                  pl.BlockSpec(memory_space=pl.ANY)],
        out_specs=pl.BlockSpec((1, t, HEAD_DIM), lambda bi, h, pt, sl: (bi, 0, h)),
        scratch_shapes=[pltpu.VMEM((2, nsel * MOBA_BLOCK, HEAD_DIM), F32),
                        pltpu.VMEM((t, nsel * MOBA_BLOCK), F32),
                        pltpu.SemaphoreType.DMA((2,))],
    )
    return pl.pallas_call(
        functools.partial(_sample_attend_kernel, layer=layer, n_pages=n_pages, q_start=past),
        grid_spec=grid_spec,
        out_shape=jax.ShapeDtypeStruct((b, t, ATTN_WIDTH), F32),
        compiler_params=_params("arbitrary", "arbitrary"),
        name="sample_attend",
    )(page_table.reshape(-1), sel[..., :MOBA_TOPK].reshape(-1), scores, qu, k_new, v_new, cache_v)


def _conv_taps():
    pad = CONV_HALO - (CONV_WIDTH - 1)
    return [(ph, [(j, (pad + j) // 8) for j in range(CONV_WIDTH) if (pad + j) % 8 == ph]) for ph in range(8)]


def _conv_kernel(ua_ref, ug_ref, prev_ref, w_ref, b_ref, g_ref, bl_ref, y_ref, st_ref, buf_ref, par_ref, z_ref,
                 *, tt, rows, nrows, lanes):
    ti = pl.program_id(1)
    ch = y_ref.shape[2]
    hist = CONV_WIDTH - 1
    pad = CONV_HALO - hist
    i_bias, i_gain, i_beta = CONV_WIDTH, CONV_WIDTH + 1, CONV_WIDTH + 2

    @pl.when(ti == 0)
    def _():
        buf_ref[0:CONV_HALO, :] = jnp.zeros((CONV_HALO, ch), F32)
        buf_ref[pad:CONV_HALO, :] = prev_ref[0]
        for j in range(CONV_WIDTH):
            par_ref[j] = jnp.broadcast_to(w_ref[j:j + 1, :], (8, ch))
        par_ref[i_bias] = jnp.broadcast_to(b_ref[...], (8, ch))
        par_ref[i_gain] = jnp.broadcast_to(g_ref[...], (8, ch))
        par_ref[i_beta] = jnp.broadcast_to(bl_ref[...], (8, ch))

    @pl.when(ti > 0)
    def _():
        buf_ref[0:CONV_HALO, :] = buf_ref[tt:tt + CONV_HALO, :]

    def glu(ci, carry):
        r0 = pl.multiple_of(ci * rows, rows)
        buf_ref[pl.ds(CONV_HALO + r0, rows), :] = (ua_ref[0, pl.ds(r0, rows), :]
                                                   * jax.nn.sigmoid(ug_ref[0, pl.ds(r0, rows), :]))
        return carry

    lax.fori_loop(0, tt // rows, glu, 0)

    def conv(ci, carry):
        r0 = pl.multiple_of(ci * rows, rows)
        for c0 in range(0, ch, lanes):
            cs = slice(c0, c0 + lanes)
            win = buf_ref.at[pl.ds(r0, rows + CONV_HALO), pl.ds(c0, lanes)]
            acc = jnp.broadcast_to(par_ref[i_bias, :, cs][None], (rows // 8, 8, lanes))
            for ph, taps in _conv_taps():
                a_lo, a_hi = taps[0][1], taps[-1][1]
                x = win[8 * a_lo + ph:8 * a_hi + ph + rows, :]
                for j, a in taps:
                    xs = x[8 * (a - a_lo):8 * (a - a_lo) + rows, :].reshape(rows // 8, 8, lanes)
                    acc = acc + par_ref[j, :, cs][None] * xs
            z_ref[pl.ds(r0, rows), cs] = acc.reshape(rows, lanes)
        return carry

    lax.fori_loop(0, tt // rows, conv, 0)

    def norm(ci, carry):
        r0 = pl.multiple_of(ci * nrows, nrows)
        z = z_ref[pl.ds(r0, nrows), :]
        xc = z - jnp.mean(z, axis=-1, keepdims=True)
        xn = (xc * lax.rsqrt(jnp.mean(xc * xc, axis=-1, keepdims=True) + EPS)).reshape(nrows // 8, 8, ch)
        y = (xn * par_ref[i_gain][None] + par_ref[i_beta][None]).reshape(nrows, ch)
        y_ref[0, pl.ds(r0, nrows), :] = y * jax.nn.sigmoid(y)
        return carry

    lax.fori_loop(0, tt // nrows, norm, 0, unroll=min(8, tt // nrows))

    @pl.when(ti == pl.num_programs(1) - 1)
    def _():
        st_ref[0] = buf_ref[tt + pad:tt + CONV_HALO, :]


def _conformer_conv(qu, prev, layer, w_dw, b_dw, g_ln, b_ln, tt):
    _, b, t, ch = qu.shape
    assert ch == w_dw.shape[1]
    hist = CONV_WIDTH - 1
    assert t % tt == 0 and tt % 8 == 0
    rows = 64 if tt % 64 == 0 else 8
    nrows = 16 if tt % 16 == 0 else 8
    lanes = 256 if ch % 256 == 0 else ch
    row = lambda a: a.reshape(1, ch)
    return pl.pallas_call(
        functools.partial(_conv_kernel, tt=tt, rows=rows, nrows=nrows, lanes=lanes),
        grid=(b, t // tt),
        in_specs=[pl.BlockSpec((None, 1, tt, ch), lambda bi, ti: (1, bi, ti, 0)),
                  pl.BlockSpec((None, 1, tt, ch), lambda bi, ti: (2, bi, ti, 0)),
                  pl.BlockSpec((None, 1, hist, ch), lambda bi, ti: (layer, bi, 0, 0)),
                  pl.BlockSpec((CONV_WIDTH, ch), lambda bi, ti: (0, 0)),
                  pl.BlockSpec((1, ch), lambda bi, ti: (0, 0)),
                  pl.BlockSpec((1, ch), lambda bi, ti: (0, 0)),
                  pl.BlockSpec((1, ch), lambda bi, ti: (0, 0))],
        out_specs=[pl.BlockSpec((1, tt, ch), lambda bi, ti: (bi, ti, 0)),
                   pl.BlockSpec((1, hist, ch), lambda bi, ti: (bi, 0, 0))],
        out_shape=[jax.ShapeDtypeStruct((b, t, ch), F32), jax.ShapeDtypeStruct((b, hist, ch), F32)],
        scratch_shapes=[pltpu.VMEM((tt + CONV_HALO, ch), F32), pltpu.VMEM((CONV_WIDTH + 3, 8, ch), F32),
                        pltpu.VMEM((tt, ch), F32)],
        compiler_params=_params("arbitrary", "arbitrary"),
        name="conformer_conv",
    )(qu, qu, prev, w_dw, row(b_dw), row(g_ln), row(b_ln))


def _mix_out_kernel(attn_ref, conv_ref, h_ref, ga_ref, gc_ref, w_ref, gp_ref, o_ref):
    wa = attn_ref.shape[1]
    a = _rms(attn_ref[...], ga_ref[...]).astype(BF16)
    cv = _rms(conv_ref[...], gc_ref[...]).astype(BF16)
    mixed = (jnp.dot(a, w_ref[0:wa, :], preferred_element_type=F32)
             + jnp.dot(cv, w_ref[wa:, :], preferred_element_type=F32))
    o_ref[...] = h_ref[...] + _rms(mixed, gp_ref[...])


def _mix_out(attn, conv, h, g_attn, g_conv, w_out, g_post, tm):
    m, wa = attn.shape
    wc = conv.shape[1]
    d = h.shape[1]
    assert m % tm == 0
    row = lambda a: a.reshape(1, -1)
    return pl.pallas_call(
        _mix_out_kernel,
        grid=(m // tm,),
        in_specs=[pl.BlockSpec((tm, wa), lambda i: (i, 0)),
                  pl.BlockSpec((tm, wc), lambda i: (i, 0)),
                  pl.BlockSpec((tm, d), lambda i: (i, 0)),
                  pl.BlockSpec((1, wa), lambda i: (0, 0)),
                  pl.BlockSpec((1, wc), lambda i: (0, 0)),
                  pl.BlockSpec((wa + wc, d), lambda i: (0, 0)),
                  pl.BlockSpec((1, d), lambda i: (0, 0))],
        out_specs=pl.BlockSpec((tm, d), lambda i: (i, 0)),
        out_shape=jax.ShapeDtypeStruct((m, d), F32),
        compiler_params=_params("arbitrary"),
        name="mix_out",
    )(attn, conv, h, row(g_attn), row(g_conv), w_out, row(g_post))


def _xattn_kernel(h_ref, mk_ref, mv_ref, gpre_ref, wq_ref, wo_ref, gpost_ref, o_ref):
    h = h_ref[...]
    xn = _rms(h, gpre_ref[...]).astype(BF16)
    xq = jnp.dot(xn, wq_ref[...], preferred_element_type=F32)
    scale = X_HEAD_DIM ** -0.5
    seqs = mk_ref.shape[0]
    rows = h.shape[0] // seqs
    per_seq = []
    for bi in range(seqs):
        outs = []
        for hh in range(X_HEADS):
            cols = slice(hh * X_HEAD_DIM, (hh + 1) * X_HEAD_DIM)
            qh = xq[bi * rows:(bi + 1) * rows, cols].astype(BF16)
            s = lax.dot_general(qh, mk_ref[bi, :, cols].astype(BF16), _NT, preferred_element_type=F32) * scale
            p = jnp.exp(s - jnp.max(s, axis=1, keepdims=True))
            l = jnp.sum(p, axis=1, keepdims=True)
            outs.append(jnp.dot(p.astype(BF16), mv_ref[bi, :, cols].astype(BF16),
                                preferred_element_type=F32) / l)
        per_seq.append(jnp.concatenate(outs, axis=1))
    o = jnp.concatenate(per_seq, axis=0).astype(BF16)
    y = jnp.dot(o, wo_ref[...], preferred_element_type=F32)
    o_ref[...] = h + _rms(y, gpost_ref[...])


def _xattn(h, mem_k, mem_v, layer, g_pre, w_xq, w_xo, g_post, rows_per_batch, tm):
    m, d = h.shape
    seqs = max(1, tm // rows_per_batch)
    tiles = max(1, rows_per_batch // tm)
    assert m % tm == 0 and tm * tiles == rows_per_batch * seqs
    mem_len, xw = mem_k.shape[2:]
    row = lambda a: a.reshape(1, -1)
    return pl.pallas_call(
        _xattn_kernel,
        grid=(m // tm,),
        in_specs=[pl.BlockSpec((tm, d), lambda i: (i, 0)),
                  pl.BlockSpec((None, seqs, mem_len, xw), lambda i: (layer, i // tiles, 0, 0)),
                  pl.BlockSpec((None, seqs, mem_len, xw), lambda i: (layer, i // tiles, 0, 0)),
                  pl.BlockSpec((1, d), lambda i: (0, 0)),
                  pl.BlockSpec((d, xw), lambda i: (0, 0)),
                  pl.BlockSpec((xw, d), lambda i: (0, 0)),
                  pl.BlockSpec((1, d), lambda i: (0, 0))],
        out_specs=pl.BlockSpec((tm, d), lambda i: (i, 0)),
        out_shape=jax.ShapeDtypeStruct((m, d), F32),
        compiler_params=_params("arbitrary"),
        name="xattn",
    )(h, mem_k, mem_v, row(g_pre), w_xq, w_xo, row(g_post))


def _ffn_kernel(h_ref, gpre_ref, wg_ref, wu_ref, wo_ref, gpost_ref, o_ref, *rest, emit):
    if emit:
        wg_out, wu_out, wo_out, xn_ref, acc_ref = rest
        wg_out[...] = wg_ref[...].astype(BF16)
        wu_out[...] = wu_ref[...].astype(BF16)
        wo_out[...] = wo_ref[...].astype(BF16)
        wg_ref, wu_ref, wo_ref = wg_out, wu_out, wo_out
    else:
        xn_ref, acc_ref = rest
    f = pl.program_id(1)

    @pl.when(f == 0)
    def _():
        xn_ref[...] = _rms(h_ref[...], gpre_ref[...]).astype(BF16)
        acc_ref[...] = jnp.zeros(acc_ref.shape, F32)

    xn = xn_ref[...]
    g = jnp.dot(xn, wg_ref[...], preferred_element_type=F32)
    u = jnp.dot(xn, wu_ref[...], preferred_element_type=F32)
    a = (g * jax.nn.sigmoid(g) * u).astype(BF16)
    acc_ref[...] += jnp.dot(a, wo_ref[...], preferred_element_type=F32)

    @pl.when(f == pl.num_programs(1) - 1)
    def _():
        o_ref[...] = h_ref[...] + _rms(acc_ref[...], gpost_ref[...])


def _ffn(h, g_pre, w, g_post, tm, tf, emit):
    m, d = h.shape
    row = lambda a: a.reshape(1, -1)
    if emit:
        w_in, w_out = w
        hidden = w_out.shape[0]
        nf = hidden // tf
        assert m == tm and hidden % tf == 0 and w_in.shape[1] == 2 * hidden
        w_args = (w_in, w_in, w_out)
        w_specs = [pl.BlockSpec((d, tf), lambda i, f: (0, f)),
                   pl.BlockSpec((d, tf), lambda i, f: (0, f + nf)),
                   pl.BlockSpec((tf, d), lambda i, f: (f, 0))]
    else:
        w_args = w
        nf, hidden = w[0].shape[0], w[2].shape[0]
        assert m % tm == 0 and w[0].shape == w[1].shape == (nf, d, tf) and hidden == nf * tf
        w_specs = [pl.BlockSpec((None, d, tf), lambda i, f: (f, 0, 0)),
                   pl.BlockSpec((None, d, tf), lambda i, f: (f, 0, 0)),
                   pl.BlockSpec((tf, d), lambda i, f: (f, 0))]
    out_specs = [pl.BlockSpec((tm, d), lambda i, f: (i, 0))]
    out_shape = [jax.ShapeDtypeStruct((m, d), F32)]
    if emit:
        out_specs += [pl.BlockSpec((None, d, tf), lambda i, f: (f, 0, 0)),
                      pl.BlockSpec((None, d, tf), lambda i, f: (f, 0, 0)),
                      pl.BlockSpec((tf, d), lambda i, f: (f, 0))]
        out_shape += [jax.ShapeDtypeStruct((nf, d, tf), BF16), jax.ShapeDtypeStruct((nf, d, tf), BF16),
                      jax.ShapeDtypeStruct((hidden, d), BF16)]
    outs = pl.pallas_call(
        functools.partial(_ffn_kernel, emit=emit),
        grid=(m // tm, nf),
        in_specs=[pl.BlockSpec((tm, d), lambda i, f: (i, 0)),
                  pl.BlockSpec((1, d), lambda i, f: (0, 0)),
                  *w_specs,
                  pl.BlockSpec((1, d), lambda i, f: (0, 0))],
        out_specs=out_specs,
        out_shape=out_shape,
        scratch_shapes=[pltpu.VMEM((tm, d), BF16), pltpu.VMEM((tm, d), F32)],
        compiler_params=_params("arbitrary", "arbitrary"),
        name="ffn",
    )(h, row(g_pre), *w_args, row(g_post))
    return outs[0], tuple(outs[1:])


def _row_tile(m):
    return 512 if m % 512 == 0 else m


def _project_in(x, wts, w_in, emit):
    b, t, d = x.shape
    m = b * t
    assert wts["w_dw"].shape[1] == ATTN_WIDTH
    k_new, v_new, qu, *w_in_b = _in_proj(x.reshape(m, d), wts["g_pre_mix"].reshape(1, d), w_in, emit,
                                         tm=1024 if m % 1024 == 0 else _row_tile(m))
    k_new, v_new = k_new.reshape(b, t, ATTN_WIDTH), v_new.reshape(b, t, ATTN_WIDTH)
    return k_new, v_new, qu.reshape(qu.shape[0], b, t, ATTN_WIDTH), (w_in_b[0] if emit else None)


def _after_attention(x, attn, qu, state, state_layer, wts, w_ffn, emit, conv_tile, x_tile):
    conv_prev, mem_k, mem_v = state
    b, t, d = x.shape
    m = b * t
    tm = _row_tile(m)
    ch = wts["w_dw"].shape[1]
    conv, conv_state = _conformer_conv(qu, conv_prev, state_layer, wts["w_dw"], wts["b_dw"],
                                       wts["g_ln_conv"], wts["b_ln_conv"], conv_tile)
    h = _mix_out(attn.reshape(m, ATTN_WIDTH), conv.reshape(m, ch), x.reshape(m, d), wts["g_attn_grp"],
                 wts["g_conv_grp"], wts["w_out"], wts["g_post_mix"], tm=tm)
    h = _xattn(h, mem_k, mem_v, state_layer, wts["g_pre_x"], wts["w_xq"], wts["w_xo"], wts["g_post_x"],
               rows_per_batch=t, tm=x_tile)
    h, w_ffn_b = _ffn(h, wts["g_pre_ffn"], w_ffn, wts["g_post_ffn"], tm=tm, tf=512, emit=emit)
    return h.reshape(b, t, d), conv_state, (w_ffn_b if emit else None)


def kernel(x_prompt, x_sample, cache_k, cache_v, state_conv, cache_mem_k, cache_mem_v, page_table,
           mem_prompt, g_mem, w_mem_k, w_mem_v, g_pre_mix, w_in, w_dw, b_dw, g_ln_conv, b_ln_conv,
           g_attn_grp, g_conv_grp, w_out, g_post_mix, g_pre_x, w_xq, w_xo, g_post_x,
           g_pre_ffn, w_ffn_in, w_ffn_out, g_post_ffn):
    depth = w_in.shape[0]
    bp, tp, d = x_prompt.shape
    bs, ts, _ = x_sample.shape
    mem_len = mem_prompt.shape[1]
    xw = X_HEADS * X_HEAD_DIM
    ch = w_dw.shape[2]
    hp, hs = x_prompt, x_sample
    outs = [[] for _ in range(8)]
    for l in range(depth):
        wts = dict(g_pre_mix=g_pre_mix[l], w_dw=w_dw[l], b_dw=b_dw[l],
                   g_ln_conv=g_ln_conv[l], b_ln_conv=b_ln_conv[l], g_attn_grp=g_attn_grp[l],
                   g_conv_grp=g_conv_grp[l], w_out=w_out[l].astype(BF16), g_post_mix=g_post_mix[l],
                   g_pre_x=g_pre_x[l], w_xq=w_xq[l].astype(BF16), w_xo=w_xo[l].astype(BF16),
                   g_post_x=g_post_x[l], g_pre_ffn=g_pre_ffn[l], g_post_ffn=g_post_ffn[l])

        ks, vs, qu_s, w_in_b = _project_in(hs, wts, w_in[l], True)
        kp, vp, qu_p, _ = _project_in(hp, wts, w_in_b, False)
        attn_p, scores, sel = _moba_and_scores(qu_p, kp, vp, qu_s, cache_k, page_table, l)
        attn_s = _sample_attend(scores, sel, qu_s, ks, vs, cache_v, page_table, l)

        sample_state = (state_conv, cache_mem_k.reshape(depth, bs, mem_len, xw),
                        cache_mem_v.reshape(depth, bs, mem_len, xw))
        hs, cs, w_ffn_b = _after_attention(hs, attn_s, qu_s, sample_state, l, wts, (w_ffn_in[l], w_ffn_out[l]),
                                           True, conv_tile=ts, x_tile=bs * ts)

        w_mem = jnp.concatenate([w_mem_k[l], w_mem_v[l]], axis=1).astype(BF16)
        mk_p, mv_p = _norm_matmul(mem_prompt.reshape(bp * mem_len, d), g_mem[l].reshape(1, d), w_mem,
                                  (xw, xw), tn=xw, tm=_row_tile(bp * mem_len))
        mk_p, mv_p = mk_p.reshape(bp, mem_len, xw), mv_p.reshape(bp, mem_len, xw)
        prompt_state = (jnp.zeros((1, bp, CONV_WIDTH - 1, ch), F32), mk_p[None], mv_p[None])
        hp, cp, _ = _after_attention(hp, attn_p, qu_p, prompt_state, 0, wts, w_ffn_b, False,
                                     conv_tile=_row_tile(tp), x_tile=_row_tile(tp))
        mem_shape = (bp, mem_len, X_HEADS, X_HEAD_DIM)
        heads = lambda a: a.reshape(a.shape[:2] + (N_HEADS, HEAD_DIM))
        for lst, a in zip(outs, (heads(kp), heads(vp), cp, mk_p.reshape(mem_shape), mv_p.reshape(mem_shape),
                                 heads(ks), heads(vs), cs)):
            lst.append(a)
    return (hp, hs) + tuple(jnp.stack(lst, 0) for lst in outs)
```

```python
import functools

import jax
import jax.numpy as jnp
from jax import lax
from jax.experimental import pallas as pl
from jax.experimental.pallas import tpu as pltpu

EPS = 1e-6
N_HEADS = 8
HEAD_DIM = 128
ATTN_WIDTH = N_HEADS * HEAD_DIM
CONV_WIDTH = 31
MOBA_BLOCK = 256
MOBA_TOPK = 3
X_HEADS = 4
X_HEAD_DIM = 128

CONV_HALO = 32
DMA_PRIORITIES = 2
ONES_ROWS = 16
VMEM_LIMIT = 56 * 1024 * 1024
VMEM_LIMIT_MAX = 60 * 1024 * 1024

F32 = jnp.float32
BF16 = jnp.bfloat16
NEG_INF = float("-inf")
LOG2_E = 1.4426950408889634
_NT = (((1,), (1,)), ((), ()))


def _params(*sem, vmem_limit=VMEM_LIMIT):
    return pltpu.CompilerParams(dimension_semantics=sem, vmem_limit_bytes=vmem_limit)


def _rms(x, g):
    return x * lax.rsqrt(jnp.mean(x * x, axis=-1, keepdims=True) + EPS) * g


def _top_blocks(gate, n_valid, axis):
    nb = gate.shape[axis]
    pos = lax.broadcasted_iota(jnp.int32, gate.shape, axis)
    g = jnp.where(pos < n_valid, gate, NEG_INF)
    idxs, oks = [], []
    for _ in range(MOBA_TOPK):
        m = jnp.max(g, axis=axis, keepdims=True)
        idx = jnp.min(jnp.where(g == m, pos, nb), axis=axis, keepdims=True)
        ok = m > NEG_INF
        idxs.append(idx)
        oks.append(ok)
        g = jnp.where((pos == idx) & ok, NEG_INF, g)
    return idxs, oks


def _norm_matmul_kernel(x_ref, g_ref, w_ref, *rest, tile_ranges):
    out_refs, xn_ref = rest[:-1], rest[-1]
    j = pl.program_id(1)

    @pl.when(j == 0)
    def _():
        xn_ref[...] = _rms(x_ref[...], g_ref[...]).astype(BF16)

    for o_ref, (lo, hi) in zip(out_refs, tile_ranges):
        @pl.when((j >= lo) & (j < hi))
        def _(o_ref=o_ref):
            o_ref[...] = jnp.dot(xn_ref[...], w_ref[...], preferred_element_type=F32)


def _norm_matmul(x, g, w, widths, tn, tm):
    m, d = x.shape
    n = w.shape[1]
    assert sum(widths) == n and all(wd % tn == 0 for wd in widths) and m % tm == 0
    tile_ranges, lo = [], 0
    for wd in widths:
        tile_ranges.append((lo, lo + wd // tn))
        lo += wd // tn

    def out_map(i, j, lo, cnt):
        return (i, jnp.clip(j - lo, 0, cnt - 1))

    out_specs = [pl.BlockSpec((tm, tn), functools.partial(out_map, lo=lo, cnt=hi - lo))
                 for lo, hi in tile_ranges]
    return pl.pallas_call(
        functools.partial(_norm_matmul_kernel, tile_ranges=tuple(tile_ranges)),
        grid=(m // tm, n // tn),
        in_specs=[pl.BlockSpec((tm, d), lambda i, j: (i, 0)),
                  pl.BlockSpec((1, d), lambda i, j: (0, 0)),
                  pl.BlockSpec((d, tn), lambda i, j: (0, j))],
        out_specs=out_specs,
        out_shape=[jax.ShapeDtypeStruct((m, wd), F32) for wd in widths],
        scratch_shapes=[pltpu.VMEM((tm, d), BF16)],
        compiler_params=_params("arbitrary", "arbitrary"),
        name="norm_matmul",
    )(x, g, w)


def _in_proj_src_tile(j):
    return jnp.where(j < 3, (j + 1) % 3, j)


def _in_proj_kernel(x_ref, g_ref, w_ref, *rest, emit):
    if emit:
        k_ref, v_ref, qu_ref, wt_ref, xn_ref = rest
    else:
        k_ref, v_ref, qu_ref, xn_ref = rest
    j = pl.program_id(1)

    @pl.when(j == 0)
    def _():
        xn_ref[...] = _rms(x_ref[...], g_ref[...]).astype(BF16)

    if emit:
        wt_ref[...] = w_ref[...].astype(BF16)
        w_ref = wt_ref

    for o_ref, cond in ((k_ref, j == 0), (v_ref, j == 1), (qu_ref, j >= 2)):
        @pl.when(cond)
        def _(o_ref=o_ref):
            o_ref[...] = jnp.dot(xn_ref[...], w_ref[...], preferred_element_type=F32)


def _in_proj(x, g, w, emit, tm):
    m, d = x.shape
    tn = ATTN_WIDTH
    nt = (w.shape[1] // tn) if emit else w.shape[0]
    assert m % tm == 0 and nt > 3 and (not emit or (m == tm and w.shape[1] % tn == 0))
    if emit:
        w_spec = pl.BlockSpec((d, tn), lambda i, j: (0, _in_proj_src_tile(j)))
    else:
        w_spec = pl.BlockSpec((None, d, tn), lambda i, j: (j, 0, 0))
    out_specs = [pl.BlockSpec((tm, tn), lambda i, j: (i, 0)),
                 pl.BlockSpec((tm, tn), lambda i, j: (i, 0)),
                 pl.BlockSpec((None, tm, tn), lambda i, j: (jnp.maximum(j - 2, 0), i, 0))]
    out_shape = [jax.ShapeDtypeStruct((m, tn), F32), jax.ShapeDtypeStruct((m, tn), F32),
                 jax.ShapeDtypeStruct((nt - 2, m, tn), F32)]
    if emit:
        out_specs.append(pl.BlockSpec((None, d, tn), lambda i, j: (j, 0, 0)))
        out_shape.append(jax.ShapeDtypeStruct((nt, d, tn), BF16))
    return pl.pallas_call(
        functools.partial(_in_proj_kernel, emit=emit),
        grid=(m // tm, nt),
        in_specs=[pl.BlockSpec((tm, d), lambda i, j: (i, 0)),
                  pl.BlockSpec((1, d), lambda i, j: (0, 0)),
                  w_spec],
        out_specs=out_specs,
        out_shape=out_shape,
        scratch_shapes=[pltpu.VMEM((tm, d), BF16)],
        compiler_params=_params("arbitrary", "arbitrary"),
        name="in_proj",
    )(x, g, w)


def _moba_prompt_body(h, g, q_ref, k_ref, v_ref, o_ref, kb_ref, vt_ref, km_ref, *, nb, tps):
    blk = MOBA_BLOCK

    @pl.when(g == 0)
    def _():
        for n in range(nb):
            kf = k_ref[0, n * blk:(n + 1) * blk, :]
            kb_ref[n * blk:(n + 1) * blk, :] = kf.astype(BF16)
            vt_ref[n, 0:HEAD_DIM, :] = v_ref[0, n * blk:(n + 1) * blk, :].T.astype(BF16)
            vt_ref[n, HEAD_DIM:, :] = jnp.ones((ONES_ROWS, blk), BF16)
            km_ref[n:n + 1, :] = jnp.mean(kf, axis=0, keepdims=True)

    scale = HEAD_DIM ** -0.5 * LOG2_E
    slope = jnp.exp2(jnp.broadcast_to(-8.0 * (h + 1).astype(F32) / N_HEADS, (1, blk))) * LOG2_E

    def attend(i, qi):
        q = q_ref[0, qi * blk:(qi + 1) * blk, :]
        qb = q.astype(BF16)
        kr = lax.broadcasted_iota(jnp.int32, (blk, blk), 0)
        qc = lax.broadcasted_iota(jnp.int32, (blk, blk), 1)
        base = slope * (qc - kr).astype(F32)
        s_own = lax.dot_general(kb_ref[i * blk:(i + 1) * blk, :], qb, _NT, preferred_element_type=F32)
        ts = [jnp.where(kr <= qc, s_own * scale - base, NEG_INF)]
        if i:
            gate = lax.dot_general(km_ref[0:i, :], q, _NT, precision=lax.Precision.HIGHEST,
                                   preferred_element_type=F32)
            idxs, oks = _top_blocks(gate, i, axis=0)
            row = lax.broadcasted_iota(jnp.int32, (i, blk), 0)
            sel = jnp.zeros((i, blk), F32)
            for idx, ok in zip(idxs, oks):
                sel = jnp.where((row == idx) & ok, 1.0, sel)
            for n in range(i):
                s = lax.dot_general(kb_ref[n * blk:(n + 1) * blk, :], qb, _NT, preferred_element_type=F32)
                neg = jnp.where(sel[n:n + 1, :] > 0.0, -slope * float((i - n) * blk), NEG_INF)
                ts.append(s * scale - base + neg)
        m = jnp.max(functools.reduce(jnp.maximum, ts), axis=0, keepdims=True)
        acc = None
        for n, t in enumerate(ts):
            p = jnp.exp2(t - m)
            vt = vt_ref[i] if n == 0 else vt_ref[n - 1]
            pv = jnp.dot(vt, p.astype(BF16), preferred_element_type=F32)
            acc = pv if acc is None else acc + pv
        l = acc[HEAD_DIM:HEAD_DIM + 1, :]
        o_ref[0, qi * blk:(qi + 1) * blk, :] = (acc[0:HEAD_DIM, :] / l).T

    for v in range(nb // tps):
        @pl.when(g == v)
        def _(v=v):
            for qi in range(tps):
                attend(v * tps + qi, qi)


def _head_page_copy(cache_ref, pt_ref, buf_ref, sem, layer, b, h, page_slot, dst_row, n_pages):
    page = pt_ref[b * n_pages + page_slot]
    ps = cache_ref.shape[2]
    return pltpu.make_async_copy(cache_ref.at[layer, page, :, h, :], buf_ref.at[pl.ds(dst_row, ps), :], sem)


def _start_k_pages(st, first, count, pt_ref, kc_ref, kbuf_ref, sem_ref, *, layer, n_pages):
    bb, hh = st // N_HEADS, st % N_HEADS
    sl = st % 2
    ps = kc_ref.shape[2]

    def issue(pair, carry):
        for k in range(DMA_PRIORITIES):
            p = first + pair * DMA_PRIORITIES + k
            _head_page_copy(kc_ref, pt_ref, kbuf_ref.at[sl], sem_ref.at[sl], layer, bb, hh,
                            p, pl.multiple_of(p * ps, ps), n_pages).start(priority=k)
        return carry

    lax.fori_loop(0, count // DMA_PRIORITIES, issue, 0, unroll=4)


def _sample_scores_body(step, n_steps, pt_ref, q_ref, kc_ref, s_ref, sel_ref, kbuf_ref, km_ref, gate_ref, sem_ref,
                        *, layer, n_pages, chunk):
    nh = N_HEADS
    b, h = step // nh, step % nh
    slot = step % 2
    ps = kc_ref.shape[2]
    past = n_pages * ps
    blk = MOBA_BLOCK
    t = q_ref.shape[1]
    start_pages = functools.partial(_start_k_pages, pt_ref=pt_ref, kc_ref=kc_ref, kbuf_ref=kbuf_ref,
                                    sem_ref=sem_ref, layer=layer, n_pages=n_pages)

    @pl.when(step == 0)
    def _():
        start_pages(step, 0, n_pages)

    @pl.when(step + 1 < n_steps)
    def _():
        start_pages(step + 1, 0, n_pages)

    pltpu.make_async_copy(kbuf_ref.at[1 - slot], kbuf_ref.at[slot], sem_ref.at[slot]).wait()

    q = q_ref[0]
    qb = q.astype(BF16)
    bpc = chunk // blk

    def body(ci, carry):
        start = pl.multiple_of(ci * chunk, chunk)
        kf = kbuf_ref[slot, pl.ds(start, chunk), :]
        sc = lax.dot_general(qb, kf.astype(BF16), _NT, preferred_element_type=F32)
        for jj in range(bpc):
            s_ref[0, 0, ci * bpc + jj] = sc[:, jj * blk:(jj + 1) * blk]
        km_ref[pl.ds(pl.multiple_of(ci * bpc, bpc), bpc), :] = jnp.mean(
            kf.reshape(bpc, blk, HEAD_DIM), axis=1)
        return carry

    lax.fori_loop(0, past // chunk, body, 0, unroll=True)

    gate_ref[h] = lax.dot_general(q, km_ref[...], _NT, precision=lax.Precision.HIGHEST,
                                  preferred_element_type=F32)

    @pl.when(h == nh - 1)
    def _():
        gate = gate_ref[...].reshape(nh * t, past // blk)
        idxs, _ = _top_blocks(gate, past // blk, axis=1)
        lane = lax.broadcasted_iota(jnp.int32, (nh * t, 128), 1)
        out = jnp.zeros((nh * t, 128), jnp.int32)
        for k, idx in enumerate(idxs):
            out = jnp.where(lane == k, idx, out)
        sel_ref[0] = out.reshape(nh, t, 128)


def _moba_scores_kernel(pt_ref, qs_ref, kc_ref, qp_ref, kp_ref, vp_ref, s_ref, sel_ref, o_ref,
                        kbuf_ref, km_ref, gate_ref, sem_ref, kb_ref, vt_ref, kmp_ref,
                        *, layer, n_pages, chunk, nb, tps, every):
    step = pl.program_id(0)
    n_steps = pl.num_programs(0)
    _sample_scores_body(step, n_steps, pt_ref, qs_ref, kc_ref, s_ref, sel_ref, kbuf_ref, km_ref,
                        gate_ref, sem_ref, layer=layer, n_pages=n_pages, chunk=chunk)

    @pl.when(step % every == 0)
    def _():
        item = step // every
        groups = nb // tps
        _moba_prompt_body((item // groups) % N_HEADS, item % groups, qp_ref, kp_ref, vp_ref, o_ref,
                          kb_ref, vt_ref, kmp_ref, nb=nb, tps=tps)


def _moba_and_scores(qu_p, k_p, v_p, qu_s, cache_k, page_table, layer, tps=4):
    _, bs, ts, _ = qu_s.shape
    bp, tp, _ = k_p.shape
    blk = MOBA_BLOCK
    n_pages = page_table.shape[1]
    ps = cache_k.shape[2]
    past = n_pages * ps
    chunk = 8 * blk
    assert past % chunk == 0 and past // blk >= MOBA_TOPK and n_pages % DMA_PRIORITIES == 0
    tps = min(tps, tp // blk)
    assert tp % (blk * tps) == 0 and n_pages % (tps * DMA_PRIORITIES) == 0
    nb = tp // blk
    groups = nb // tps
    n_steps, items = bs * N_HEADS, bp * N_HEADS * groups
    assert n_steps % items == 0
    every = n_steps // items

    def prompt_idx(step):
        item = step // every
        return item // (N_HEADS * groups), (item // groups) % N_HEADS, item % groups

    def qp_map(step, pt):
        bi, h, g = prompt_idx(step)
        return (0, bi, g, h)

    def kv_map(step, pt):
        bi, h, g = prompt_idx(step)
        return (bi, 0, h)

    def o_map(step, pt):
        bi, h, g = prompt_idx(step)
        return (bi, g, h)

    grid_spec = pltpu.PrefetchScalarGridSpec(
        num_scalar_prefetch=1,
        grid=(n_steps,),
        in_specs=[pl.BlockSpec((None, 1, ts, HEAD_DIM), lambda s, pt: (0, s // N_HEADS, 0, s % N_HEADS)),
                  pl.BlockSpec(memory_space=pl.ANY),
                  pl.BlockSpec((None, 1, blk * tps, HEAD_DIM), qp_map),
                  pl.BlockSpec((1, tp, HEAD_DIM), kv_map),
                  pl.BlockSpec((1, tp, HEAD_DIM), kv_map)],
        out_specs=[pl.BlockSpec((1, 1, past // blk, ts, blk), lambda s, pt: (s // N_HEADS, s % N_HEADS, 0, 0, 0)),
                   pl.BlockSpec((1, N_HEADS, ts, 128), lambda s, pt: (s // N_HEADS, 0, 0, 0)),
                   pl.BlockSpec((1, blk * tps, HEAD_DIM), o_map)],
        scratch_shapes=[pltpu.VMEM((2, past, HEAD_DIM), F32),
                        pltpu.VMEM((past // blk, HEAD_DIM), F32),
                        pltpu.VMEM((N_HEADS, ts, past // blk), F32),
                        pltpu.SemaphoreType.DMA((2,)),
                        pltpu.VMEM((tp, HEAD_DIM), BF16), pltpu.VMEM((nb, HEAD_DIM + ONES_ROWS, blk), BF16),
                        pltpu.VMEM((nb, HEAD_DIM), F32)],
    )
    scores, sel, attn = pl.pallas_call(
        functools.partial(_moba_scores_kernel, layer=layer, n_pages=n_pages, chunk=chunk, nb=nb, tps=tps,
                          every=every),
        grid_spec=grid_spec,
        out_shape=[jax.ShapeDtypeStruct((bs, N_HEADS, past // blk, ts, blk), F32),
                   jax.ShapeDtypeStruct((bs, N_HEADS, ts, 128), jnp.int32),
                   jax.ShapeDtypeStruct((bp, tp, ATTN_WIDTH), F32)],
        compiler_params=_params("arbitrary"),
        name="moba_and_scores",
    )(page_table.reshape(-1), qu_s, cache_k, qu_p, k_p, v_p)
    return attn, scores, sel


def _sample_attend_kernel(pt_ref, sel_ref, s_ref, q_ref, kn_ref, vn_ref, vc_ref, o_ref,
                          vbuf_ref, ssel_ref, sem_ref, *, layer, n_pages, q_start):
    b = pl.program_id(0)
    h = pl.program_id(1)
    nbt, nh = pl.num_programs(0), pl.num_programs(1)
    step = b * nh + h
    slot = step % 2
    t = q_ref.shape[1]
    ps = vc_ref.shape[2]
    blk = MOBA_BLOCK
    ppb = blk // ps
    nsel = t * MOBA_TOPK

    def sel_block(st, e):
        return sel_ref[st * nsel + e]

    def copies(st, sl):
        bb, hh = st // nh, st % nh

        def mk(e, pg):
            return _head_page_copy(vc_ref, pt_ref, vbuf_ref.at[sl], sem_ref.at[sl], layer, bb, hh,
                                   sel_block(st, e) * ppb + pg, e * blk + pg * ps, n_pages)
        return mk

    def start_all(st, sl):
        mk = copies(st, sl)
        for e in range(nsel):
            for pg in range(ppb):
                mk(e, pg).start(priority=(e * ppb + pg) % DMA_PRIORITIES)

    @pl.when(step == 0)
    def _():
        start_all(step, slot)

    @pl.when(step + 1 < nbt * nh)
    def _():
        start_all(step + 1, 1 - slot)

    scale = HEAD_DIM ** -0.5
    slope = jnp.exp2(jnp.broadcast_to(-8.0 * (h + 1).astype(F32) / N_HEADS, (1, 1)))
    lane = lax.broadcasted_iota(jnp.int32, (1, blk), 1)

    ssel_ref[...] = jnp.full(ssel_ref.shape, NEG_INF, F32)
    for e in range(nsel):
        ti = e // MOBA_TOPK
        n = sel_block(step, e)
        raw = s_ref[0, 0, n, ti:ti + 1, :]
        dist = (q_start + ti - n * blk - lane).astype(F32)
        ssel_ref[ti:ti + 1, e * blk:(e + 1) * blk] = raw * scale - slope * dist

    q = q_ref[0].astype(BF16)
    r = lax.broadcasted_iota(jnp.int32, (t, t), 0)
    c = lax.broadcasted_iota(jnp.int32, (t, t), 1)
    s_own = lax.dot_general(q, kn_ref[0].astype(BF16), _NT, preferred_element_type=F32) * scale
    s_own = jnp.where(c <= r, s_own - slope * (r - c).astype(F32), NEG_INF)

    s_sel = ssel_ref[...]
    m = jnp.maximum(jnp.max(s_sel, axis=1, keepdims=True), jnp.max(s_own, axis=1, keepdims=True))
    p_sel = jnp.exp(s_sel - m)
    p_own = jnp.exp(s_own - m)
    l = jnp.sum(p_sel, axis=1, keepdims=True) + jnp.sum(p_own, axis=1, keepdims=True)

    mk = copies(step, slot)
    for e in range(nsel):
        for pg in range(ppb):
            mk(e, pg).wait()

    acc = jnp.dot(p_sel.astype(BF16), vbuf_ref[slot].astype(BF16), preferred_element_type=F32)
    acc = acc + jnp.dot(p_own.astype(BF16), vn_ref[0].astype(BF16), preferred_element_type=F32)
    o_ref[0] = acc / l


def _sample_attend(scores, sel, qu, k_new, v_new, cache_v, page_table, layer):
    b, t, _ = k_new.shape
    n_pages = page_table.shape[1]
    ps = cache_v.shape[2]
    past = n_pages * ps
    assert MOBA_BLOCK % ps == 0 and past % MOBA_BLOCK == 0 and t <= MOBA_BLOCK
    nsel = t * MOBA_TOPK
    grid_spec = pltpu.PrefetchScalarGridSpec(
        num_scalar_prefetch=2,
        grid=(b, N_HEADS),
        in_specs=[pl.BlockSpec((1, 1, past // MOBA_BLOCK, t, MOBA_BLOCK), lambda bi, h, pt, sl: (bi, h, 0, 0, 0)),
                  pl.BlockSpec((None, 1, t, HEAD_DIM), lambda bi, h, pt, sl: (0, bi, 0, h)),
                  pl.BlockSpec((1, t, HEAD_DIM), lambda bi, h, pt, sl: (bi, 0, h)),
                  pl.BlockSpec((1, t, HEAD_DIM), lambda bi, h, pt, sl: (bi, 0, h)),
                  pl.BlockSpec(memory_space=pl.ANY)],
        out_specs=pl.BlockSpec((1, t, HEAD_DIM), lambda bi, h, pt, sl: (bi, 0, h)),
        scratch_shapes=[pltpu.VMEM((2, nsel * MOBA_BLOCK, HEAD_DIM), F32),
                        pltpu.VMEM((t, nsel * MOBA_BLOCK), F32),
                        pltpu.SemaphoreType.DMA((2,))],
    )
    return pl.pallas_call(
        functools.partial(_sample_attend_kernel, layer=layer, n_pages=n_pages, q_start=past),
        grid_spec=grid_spec,
        out_shape=jax.ShapeDtypeStruct((b, t, ATTN_WIDTH), F32),
        compiler_params=_params("arbitrary", "arbitrary"),
        name="sample_attend",
    )(page_table.reshape(-1), sel[..., :MOBA_TOPK].reshape(-1), scores, qu, k_new, v_new, cache_v)


def _conv_taps():
    pad = CONV_HALO - (CONV_WIDTH - 1)
    return [(ph, [(j, (pad + j) // 8) for j in range(CONV_WIDTH) if (pad + j) % 8 == ph]) for ph in range(8)]


def _conv_kernel(ua_ref, ug_ref, prev_ref, w_ref, b_ref, g_ref, bl_ref, y_ref, st_ref, buf_ref, par_ref, z_ref,
                 *, tt, rows, nrows, lanes):
    ti = pl.program_id(1)
    ch = y_ref.shape[2]
    hist = CONV_WIDTH - 1
    pad = CONV_HALO - hist
    i_bias, i_gain, i_beta = CONV_WIDTH, CONV_WIDTH + 1, CONV_WIDTH + 2

    @pl.when(ti == 0)
    def _():
        buf_ref[0:CONV_HALO, :] = jnp.zeros((CONV_HALO, ch), F32)
        buf_ref[pad:CONV_HALO, :] = prev_ref[0]
        for j in range(CONV_WIDTH):
            par_ref[j] = jnp.broadcast_to(w_ref[j:j + 1, :], (8, ch))
        par_ref[i_bias] = jnp.broadcast_to(b_ref[...], (8, ch))
        par_ref[i_gain] = jnp.broadcast_to(g_ref[...], (8, ch))
        par_ref[i_beta] = jnp.broadcast_to(bl_ref[...], (8, ch))

    @pl.when(ti > 0)
    def _():
        buf_ref[0:CONV_HALO, :] = buf_ref[tt:tt + CONV_HALO, :]

    def glu(ci, carry):
        r0 = pl.multiple_of(ci * rows, rows)
        buf_ref[pl.ds(CONV_HALO + r0, rows), :] = (ua_ref[0, pl.ds(r0, rows), :]
                                                   * jax.nn.sigmoid(ug_ref[0, pl.ds(r0, rows), :]))
        return carry

    lax.fori_loop(0, tt // rows, glu, 0)

    def conv(ci, carry):
        r0 = pl.multiple_of(ci * rows, rows)
        for c0 in range(0, ch, lanes):
            cs = slice(c0, c0 + lanes)
            win = buf_ref.at[pl.ds(r0, rows + CONV_HALO), pl.ds(c0, lanes)]
            acc = jnp.broadcast_to(par_ref[i_bias, :, cs][None], (rows // 8, 8, lanes))
            for ph, taps in _conv_taps():
                a_lo, a_hi = taps[0][1], taps[-1][1]
                x = win[8 * a_lo + ph:8 * a_hi + ph + rows, :]
                for j, a in taps:
                    xs = x[8 * (a - a_lo):8 * (a - a_lo) + rows, :].reshape(rows // 8, 8, lanes)
                    acc = acc + par_ref[j, :, cs][None] * xs
            z_ref[pl.ds(r0, rows), cs] = acc.reshape(rows, lanes)
        return carry

    lax.fori_loop(0, tt // rows, conv, 0)

    def norm(ci, carry):
        r0 = pl.multiple_of(ci * nrows, nrows)
        z = z_ref[pl.ds(r0, nrows), :]
        xc = z - jnp.mean(z, axis=-1, keepdims=True)
        xn = (xc * lax.rsqrt(jnp.mean(xc * xc, axis=-1, keepdims=True) + EPS)).reshape(nrows // 8, 8, ch)
        y = (xn * par_ref[i_gain][None] + par_ref[i_beta][None]).reshape(nrows, ch)
        y_ref[0, pl.ds(r0, nrows), :] = y * jax.nn.sigmoid(y)
        return carry

    lax.fori_loop(0, tt // nrows, norm, 0, unroll=min(8, tt // nrows))

    @pl.when(ti == pl.num_programs(1) - 1)
    def _():
        st_ref[0] = buf_ref[tt + pad:tt + CONV_HALO, :]


def _conv_specs(qu, prev, layer, w_dw, b_dw, g_ln, b_ln, tt):
    _, b, t, ch = qu.shape
    assert ch == w_dw.shape[1]
    hist = CONV_WIDTH - 1
    assert t % tt == 0 and tt % 8 == 0
    rows = 64 if tt % 64 == 0 else 8
    nrows = 16 if tt % 16 == 0 else 8
    lanes = 256 if ch % 256 == 0 else ch
    row = lambda a: a.reshape(1, ch)
    in_specs = [pl.BlockSpec((None, 1, tt, ch), lambda bi, ti: (1, bi, ti, 0)),
                pl.BlockSpec((None, 1, tt, ch), lambda bi, ti: (2, bi, ti, 0)),
                pl.BlockSpec((None, 1, hist, ch), lambda bi, ti: (layer, bi, 0, 0)),
                pl.BlockSpec((CONV_WIDTH, ch), lambda bi, ti: (0, 0)),
                pl.BlockSpec((1, ch), lambda bi, ti: (0, 0)),
                pl.BlockSpec((1, ch), lambda bi, ti: (0, 0)),
                pl.BlockSpec((1, ch), lambda bi, ti: (0, 0))]
    out_specs = [pl.BlockSpec((1, tt, ch), lambda bi, ti: (bi, ti, 0)),
                 pl.BlockSpec((1, hist, ch), lambda bi, ti: (bi, 0, 0))]
    out_shape = [jax.ShapeDtypeStruct((b, t, ch), F32), jax.ShapeDtypeStruct((b, hist, ch), F32)]
    scratch = [pltpu.VMEM((tt + CONV_HALO, ch), F32), pltpu.VMEM((CONV_WIDTH + 3, 8, ch), F32),
               pltpu.VMEM((tt, ch), F32)]
    args = (qu, qu, prev, w_dw, row(b_dw), row(g_ln), row(b_ln))
    return (b, t // tt), in_specs, out_specs, out_shape, scratch, args, dict(tt=tt, rows=rows, nrows=nrows,
                                                                            lanes=lanes)


def _conformer_conv(qu, prev, layer, w_dw, b_dw, g_ln, b_ln, tt):
    grid, in_specs, out_specs, out_shape, scratch, args, kw = _conv_specs(qu, prev, layer, w_dw, b_dw, g_ln,
                                                                         b_ln, tt)
    return pl.pallas_call(
        functools.partial(_conv_kernel, **kw),
        grid=grid,
        in_specs=in_specs,
        out_specs=out_specs,
        out_shape=out_shape,
        scratch_shapes=scratch,
        compiler_params=_params("arbitrary", "arbitrary"),
        name="conformer_conv",
    )(*args)


def _mix_out_kernel(attn_ref, conv_ref, h_ref, ga_ref, gc_ref, w_ref, gp_ref, o_ref):
    wa = attn_ref.shape[1]
    a = _rms(attn_ref[...], ga_ref[...]).astype(BF16)
    cv = _rms(conv_ref[...], gc_ref[...]).astype(BF16)
    mixed = (jnp.dot(a, w_ref[0:wa, :], preferred_element_type=F32)
             + jnp.dot(cv, w_ref[wa:, :], preferred_element_type=F32))
    o_ref[...] = h_ref[...] + _rms(mixed, gp_ref[...])


def _mix_out(attn, conv, h, g_attn, g_conv, w_out, g_post, tm):
    m, wa = attn.shape
    wc = conv.shape[1]
    d = h.shape[1]
    assert m % tm == 0
    row = lambda a: a.reshape(1, -1)
    return pl.pallas_call(
        _mix_out_kernel,
        grid=(m // tm,),
        in_specs=[pl.BlockSpec((tm, wa), lambda i: (i, 0)),
                  pl.BlockSpec((tm, wc), lambda i: (i, 0)),
                  pl.BlockSpec((tm, d), lambda i: (i, 0)),
                  pl.BlockSpec((1, wa), lambda i: (0, 0)),
                  pl.BlockSpec((1, wc), lambda i: (0, 0)),
                  pl.BlockSpec((wa + wc, d), lambda i: (0, 0)),
                  pl.BlockSpec((1, d), lambda i: (0, 0))],
        out_specs=pl.BlockSpec((tm, d), lambda i: (i, 0)),
        out_shape=jax.ShapeDtypeStruct((m, d), F32),
        compiler_params=_params("arbitrary"),
        name="mix_out",
    )(attn, conv, h, row(g_attn), row(g_conv), w_out, row(g_post))


def _xattn_kernel(h_ref, mk_ref, mv_ref, gpre_ref, wq_ref, wo_ref, gpost_ref, o_ref):
    h = h_ref[...]
    xn = _rms(h, gpre_ref[...]).astype(BF16)
    xq = jnp.dot(xn, wq_ref[...], preferred_element_type=F32)
    scale = X_HEAD_DIM ** -0.5
    seqs = mk_ref.shape[0]
    rows = h.shape[0] // seqs
    per_seq = []
    for bi in range(seqs):
        outs = []
        for hh in range(X_HEADS):
            cols = slice(hh * X_HEAD_DIM, (hh + 1) * X_HEAD_DIM)
            qh = xq[bi * rows:(bi + 1) * rows, cols].astype(BF16)
            s = lax.dot_general(qh, mk_ref[bi, :, cols].astype(BF16), _NT, preferred_element_type=F32) * scale
            p = jnp.exp(s - jnp.max(s, axis=1, keepdims=True))
            l = jnp.sum(p, axis=1, keepdims=True)
            outs.append(jnp.dot(p.astype(BF16), mv_ref[bi, :, cols].astype(BF16),
                                preferred_element_type=F32) / l)
        per_seq.append(jnp.concatenate(outs, axis=1))
    o = jnp.concatenate(per_seq, axis=0).astype(BF16)
    y = jnp.dot(o, wo_ref[...], preferred_element_type=F32)
    o_ref[...] = h + _rms(y, gpost_ref[...])


def _xattn(h, mem_k, mem_v, layer, g_pre, w_xq, w_xo, g_post, rows_per_batch, tm):
    m, d = h.shape
    seqs = max(1, tm // rows_per_batch)
    tiles = max(1, rows_per_batch // tm)
    assert m % tm == 0 and tm * tiles == rows_per_batch * seqs
    mem_len, xw = mem_k.shape[2:]
    row = lambda a: a.reshape(1, -1)
    return pl.pallas_call(
        _xattn_kernel,
        grid=(m // tm,),
        in_specs=[pl.BlockSpec((tm, d), lambda i: (i, 0)),
                  pl.BlockSpec((None, seqs, mem_len, xw), lambda i: (layer, i // tiles, 0, 0)),
                  pl.BlockSpec((None, seqs, mem_len, xw), lambda i: (layer, i // tiles, 0, 0)),
                  pl.BlockSpec((1, d), lambda i: (0, 0)),
                  pl.BlockSpec((d, xw), lambda i: (0, 0)),
                  pl.BlockSpec((xw, d), lambda i: (0, 0)),
                  pl.BlockSpec((1, d), lambda i: (0, 0))],
        out_specs=pl.BlockSpec((tm, d), lambda i: (i, 0)),
        out_shape=jax.ShapeDtypeStruct((m, d), F32),
        compiler_params=_params("arbitrary"),
        name="xattn",
    )(h, mem_k, mem_v, row(g_pre), w_xq, w_xo, row(g_post))


def _ffn_body(f, nf, h_ref, gpre_ref, wg_ref, wu_ref, wo_ref, gpost_ref, o_ref, rest, emit):
    if emit:
        wg_out, wu_out, wo_out, xn_ref, acc_ref = rest
        wg_out[...] = wg_ref[...].astype(BF16)
        wu_out[...] = wu_ref[...].astype(BF16)
        wo_out[...] = wo_ref[...].astype(BF16)
        wg_ref, wu_ref, wo_ref = wg_out, wu_out, wo_out
    else:
        xn_ref, acc_ref = rest

    @pl.when(f == 0)
    def _():
        xn_ref[...] = _rms(h_ref[...], gpre_ref[...]).astype(BF16)
        acc_ref[...] = jnp.zeros(acc_ref.shape, F32)

    xn = xn_ref[...]
    g = jnp.dot(xn, wg_ref[...], preferred_element_type=F32)
    u = jnp.dot(xn, wu_ref[...], preferred_element_type=F32)
    a = (g * jax.nn.sigmoid(g) * u).astype(BF16)
    acc_ref[...] += jnp.dot(a, wo_ref[...], preferred_element_type=F32)

    @pl.when(f == nf - 1)
    def _():
        o_ref[...] = h_ref[...] + _rms(acc_ref[...], gpost_ref[...])


def _ffn_kernel(h_ref, gpre_ref, wg_ref, wu_ref, wo_ref, gpost_ref, o_ref, *rest, emit):
    _ffn_body(pl.program_id(1), pl.num_programs(1), h_ref, gpre_ref, wg_ref, wu_ref, wo_ref, gpost_ref, o_ref,
              rest, emit)


def _ffn_specs(h, w, tm, tf, emit, tile):
    m, d = h.shape
    if emit:
        w_in, w_out = w
        hidden = w_out.shape[0]
        nf = hidden // tf
        assert m == tm and hidden % tf == 0 and w_in.shape[1] == 2 * hidden
        w_args = (w_in, w_in, w_out)
        w_specs = [pl.BlockSpec((d, tf), lambda *ids: (0, tile(*ids)[1])),
                   pl.BlockSpec((d, tf), lambda *ids: (0, tile(*ids)[1] + nf)),
                   pl.BlockSpec((tf, d), lambda *ids: (tile(*ids)[1], 0))]
    else:
        w_args = w
        nf, hidden = w[0].shape[0], w[2].shape[0]
        assert m % tm == 0 and w[0].shape == w[1].shape == (nf, d, tf) and hidden == nf * tf
        w_specs = [pl.BlockSpec((None, d, tf), lambda *ids: (tile(*ids)[1], 0, 0)),
                   pl.BlockSpec((None, d, tf), lambda *ids: (tile(*ids)[1], 0, 0)),
                   pl.BlockSpec((tf, d), lambda *ids: (tile(*ids)[1], 0))]
    in_specs = [pl.BlockSpec((tm, d), lambda *ids: (tile(*ids)[0], 0)),
                pl.BlockSpec((1, d), lambda *ids: (0, 0)),
                *w_specs,
                pl.BlockSpec((1, d), lambda *ids: (0, 0))]
    out_specs = [pl.BlockSpec((tm, d), lambda *ids: (tile(*ids)[0], 0))]
    out_shape = [jax.ShapeDtypeStruct((m, d), F32)]
    if emit:
        out_specs += [pl.BlockSpec((None, d, tf), lambda *ids: (tile(*ids)[1], 0, 0)),
                      pl.BlockSpec((None, d, tf), lambda *ids: (tile(*ids)[1], 0, 0)),
                      pl.BlockSpec((tf, d), lambda *ids: (tile(*ids)[1], 0))]
        out_shape += [jax.ShapeDtypeStruct((nf, d, tf), BF16), jax.ShapeDtypeStruct((nf, d, tf), BF16),
                      jax.ShapeDtypeStruct((hidden, d), BF16)]
    scratch = [pltpu.VMEM((tm, d), BF16), pltpu.VMEM((tm, d), F32)]
    return nf, in_specs, out_specs, out_shape, scratch, w_args


def _ffn(h, g_pre, w, g_post, tm, tf, emit):
    m = h.shape[0]
    row = lambda a: a.reshape(1, -1)
    nf, in_specs, out_specs, out_shape, scratch, w_args = _ffn_specs(h, w, tm, tf, emit, lambda i, f: (i, f))
    outs = pl.pallas_call(
        functools.partial(_ffn_kernel, emit=emit),
        grid=(m // tm, nf),
        in_specs=in_specs,
        out_specs=out_specs,
        out_shape=out_shape,
        scratch_shapes=scratch,
        compiler_params=_params("arbitrary", "arbitrary"),
        name="ffn",
    )(h, row(g_pre), *w_args, row(g_post))
    return outs[0], tuple(outs[1:])


def _conv_ffn_kernel(*refs, conv_kw, nf):
    ci, fi, co, fo, cs, fs = refs[0:7], refs[7:13], refs[13:15], refs[15:19], refs[19:22], refs[22:24]
    step = pl.program_id(0) * pl.num_programs(1) + pl.program_id(1)

    @pl.when(step < nf)
    def _():
        _ffn_body(step, nf, *fi, fo[0], tuple(fo[1:]) + tuple(fs), True)

    _conv_kernel(*ci, *co, *cs, **conv_kw)


def _conv_and_emit_ffn(conv_args, h, g_pre, w, g_post, tf):
    grid, c_in, c_out, c_shape, c_scratch, c_args, conv_kw = _conv_specs(*conv_args)
    nt = grid[1]
    row = lambda a: a.reshape(1, -1)
    tile = lambda bi, ti: (0, jnp.minimum(bi * nt + ti, nf - 1))
    nf = w[1].shape[0] // tf
    if grid[0] * nt < nf:
        return None
    nf, f_in, f_out, f_shape, f_scratch, w_args = _ffn_specs(h, w, h.shape[0], tf, True, tile)
    outs = pl.pallas_call(
        functools.partial(_conv_ffn_kernel, conv_kw=conv_kw, nf=nf),
        grid=grid,
        in_specs=c_in + f_in,
        out_specs=c_out + f_out,
        out_shape=c_shape + f_shape,
        scratch_shapes=c_scratch + f_scratch,
        compiler_params=_params("arbitrary", "arbitrary", vmem_limit=VMEM_LIMIT_MAX),
        name="conv_and_ffn",
    )(*c_args, h, row(g_pre), *w_args, row(g_post))
    return outs[0], outs[1], outs[2], tuple(outs[3:])


def _row_tile(m):
    return 512 if m % 512 == 0 else m


def _project_in(x, wts, w_in, emit):
    b, t, d = x.shape
    m = b * t
    assert wts["w_dw"].shape[1] == ATTN_WIDTH
    k_new, v_new, qu, *w_in_b = _in_proj(x.reshape(m, d), wts["g_pre_mix"].reshape(1, d), w_in, emit,
                                         tm=1024 if m % 1024 == 0 else _row_tile(m))
    k_new, v_new = k_new.reshape(b, t, ATTN_WIDTH), v_new.reshape(b, t, ATTN_WIDTH)
    return k_new, v_new, qu.reshape(qu.shape[0], b, t, ATTN_WIDTH), (w_in_b[0] if emit else None)


def _mix_and_xattn(x, attn, conv, mem, mem_layer, wts, x_tile):
    b, t, d = x.shape
    m = b * t
    h = _mix_out(attn.reshape(m, ATTN_WIDTH), conv.reshape(m, -1), x.reshape(m, d), wts["g_attn_grp"],
                 wts["g_conv_grp"], wts["w_out"], wts["g_post_mix"], tm=_row_tile(m))
    return _xattn(h, mem[0], mem[1], mem_layer, wts["g_pre_x"], wts["w_xq"], wts["w_xo"], wts["g_post_x"],
                  rows_per_batch=t, tm=x_tile)


def kernel(x_prompt, x_sample, cache_k, cache_v, state_conv, cache_mem_k, cache_mem_v, page_table,
           mem_prompt, g_mem, w_mem_k, w_mem_v, g_pre_mix, w_in, w_dw, b_dw, g_ln_conv, b_ln_conv,
           g_attn_grp, g_conv_grp, w_out, g_post_mix, g_pre_x, w_xq, w_xo, g_post_x,
           g_pre_ffn, w_ffn_in, w_ffn_out, g_post_ffn):
    depth = w_in.shape[0]
    bp, tp, d = x_prompt.shape
    bs, ts, _ = x_sample.shape
    mem_len = mem_prompt.shape[1]
    xw = X_HEADS * X_HEAD_DIM
    ch = w_dw.shape[2]
    hp, hs = x_prompt, x_sample
    outs = [[] for _ in range(8)]
    for l in range(depth):
        wts = dict(g_pre_mix=g_pre_mix[l], w_dw=w_dw[l], b_dw=b_dw[l],
                   g_ln_conv=g_ln_conv[l], b_ln_conv=b_ln_conv[l], g_attn_grp=g_attn_grp[l],
                   g_conv_grp=g_conv_grp[l], w_out=w_out[l].astype(BF16), g_post_mix=g_post_mix[l],
                   g_pre_x=g_pre_x[l], w_xq=w_xq[l].astype(BF16), w_xo=w_xo[l].astype(BF16),
                   g_post_x=g_post_x[l], g_pre_ffn=g_pre_ffn[l], g_post_ffn=g_post_ffn[l])

        ks, vs, qu_s, w_in_b = _project_in(hs, wts, w_in[l], True)
        kp, vp, qu_p, _ = _project_in(hp, wts, w_in_b, False)
        attn_p, scores, sel = _moba_and_scores(qu_p, kp, vp, qu_s, cache_k, page_table, l)
        attn_s = _sample_attend(scores, sel, qu_s, ks, vs, cache_v, page_table, l)

        conv_w = (wts["w_dw"], wts["b_dw"], wts["g_ln_conv"], wts["b_ln_conv"])
        conv_s, cs = _conformer_conv(qu_s, state_conv, l, *conv_w, ts)
        sample_mem = (cache_mem_k.reshape(depth, bs, mem_len, xw), cache_mem_v.reshape(depth, bs, mem_len, xw))
        hs_mid = _mix_and_xattn(hs, attn_s, conv_s, sample_mem, l, wts, x_tile=bs * ts)

        conv_args = (qu_p, jnp.zeros((1, bp, CONV_WIDTH - 1, ch), F32), 0, *conv_w, _row_tile(tp))
        ffn_w = (w_ffn_in[l], w_ffn_out[l])
        fused = _conv_and_emit_ffn(conv_args, hs_mid, wts["g_pre_ffn"], ffn_w, wts["g_post_ffn"], tf=512)
        if fused is None:
            conv_p, cp = _conformer_conv(*conv_args)
            hs_out, w_ffn_b = _ffn(hs_mid, wts["g_pre_ffn"], ffn_w, wts["g_post_ffn"], tm=bs * ts, tf=512,
                                   emit=True)
        else:
            conv_p, cp, hs_out, w_ffn_b = fused
        hs = hs_out.reshape(bs, ts, d)

        w_mem = jnp.concatenate([w_mem_k[l], w_mem_v[l]], axis=1).astype(BF16)
        mk_p, mv_p = _norm_matmul(mem_prompt.reshape(bp * mem_len, d), g_mem[l].reshape(1, d), w_mem,
                                  (xw, xw), tn=xw, tm=_row_tile(bp * mem_len))
        mk_p, mv_p = mk_p.reshape(bp, mem_len, xw), mv_p.reshape(bp, mem_len, xw)
        hp_mid = _mix_and_xattn(hp, attn_p, conv_p, (mk_p[None], mv_p[None]), 0, wts, x_tile=_row_tile(tp))
        hp_out, _ = _ffn(hp_mid, wts["g_pre_ffn"], w_ffn_b, wts["g_post_ffn"], tm=_row_tile(bp * tp), tf=512,
                         emit=False)
        hp = hp_out.reshape(bp, tp, d)
        mem_shape = (bp, mem_len, X_HEADS, X_HEAD_DIM)
        heads = lambda a: a.reshape(a.shape[:2] + (N_HEADS, HEAD_DIM))
        for lst, a in zip(outs, (heads(kp), heads(vp), cp, mk_p.reshape(mem_shape), mv_p.reshape(mem_shape),
                                 heads(ks), heads(vs), cs)):
            lst.append(a)
    return (hp, hs) + tuple(jnp.stack(lst, 0) for lst in outs)
```

```python
import functools

import jax
import jax.numpy as jnp
from jax import lax
from jax.experimental import pallas as pl
from jax.experimental.pallas import tpu as pltpu

EPS = 1e-6
N_HEADS = 8
HEAD_DIM = 128
ATTN_WIDTH = N_HEADS * HEAD_DIM
CONV_WIDTH = 31
MOBA_BLOCK = 256
MOBA_TOPK = 3
X_HEADS = 4
X_HEAD_DIM = 128

CONV_HALO = 32
DMA_PRIORITIES = 2
ONES_ROWS = 16
VMEM_LIMIT = 56 * 1024 * 1024
VMEM_LIMIT_MAX = 60 * 1024 * 1024

F32 = jnp.float32
BF16 = jnp.bfloat16
NEG_INF = float("-inf")
LOG2_E = 1.4426950408889634
_NT = (((1,), (1,)), ((), ()))


def _params(*sem, vmem_limit=VMEM_LIMIT):
    return pltpu.CompilerParams(dimension_semantics=sem, vmem_limit_bytes=vmem_limit)


def _rms(x, g):
    return x * lax.rsqrt(jnp.mean(x * x, axis=-1, keepdims=True) + EPS) * g


def _sigmoid(x):
    return 0.5 * jnp.tanh(0.5 * x) + 0.5


def _top_blocks(gate, n_valid, axis):
    nb = gate.shape[axis]
    pos = lax.broadcasted_iota(jnp.int32, gate.shape, axis)
    g = jnp.where(pos < n_valid, gate, NEG_INF)
    idxs, oks = [], []
    for _ in range(MOBA_TOPK):
        m = jnp.max(g, axis=axis, keepdims=True)
        idx = jnp.min(jnp.where(g == m, pos, nb), axis=axis, keepdims=True)
        ok = m > NEG_INF
        idxs.append(idx)
        oks.append(ok)
        g = jnp.where((pos == idx) & ok, NEG_INF, g)
    return idxs, oks


def _norm_matmul_kernel(x_ref, g_ref, w_ref, *rest, tile_ranges):
    out_refs, xn_ref = rest[:-1], rest[-1]
    j = pl.program_id(1)

    @pl.when(j == 0)
    def _():
        xn_ref[...] = _rms(x_ref[...], g_ref[...]).astype(BF16)

    for o_ref, (lo, hi) in zip(out_refs, tile_ranges):
        @pl.when((j >= lo) & (j < hi))
        def _(o_ref=o_ref):
            o_ref[...] = jnp.dot(xn_ref[...], w_ref[...], preferred_element_type=F32)


def _norm_matmul(x, g, w, widths, tn, tm):
    m, d = x.shape
    n = w.shape[1]
    assert sum(widths) == n and all(wd % tn == 0 for wd in widths) and m % tm == 0
    tile_ranges, lo = [], 0
    for wd in widths:
        tile_ranges.append((lo, lo + wd // tn))
        lo += wd // tn

    def out_map(i, j, lo, cnt):
        return (i, jnp.clip(j - lo, 0, cnt - 1))

    out_specs = [pl.BlockSpec((tm, tn), functools.partial(out_map, lo=lo, cnt=hi - lo))
                 for lo, hi in tile_ranges]
    return pl.pallas_call(
        functools.partial(_norm_matmul_kernel, tile_ranges=tuple(tile_ranges)),
        grid=(m // tm, n // tn),
        in_specs=[pl.BlockSpec((tm, d), lambda i, j: (i, 0)),
                  pl.BlockSpec((1, d), lambda i, j: (0, 0)),
                  pl.BlockSpec((d, tn), lambda i, j: (0, j))],
        out_specs=out_specs,
        out_shape=[jax.ShapeDtypeStruct((m, wd), F32) for wd in widths],
        scratch_shapes=[pltpu.VMEM((tm, d), BF16)],
        compiler_params=_params("arbitrary", "arbitrary"),
        name="norm_matmul",
    )(x, g, w)


def _in_proj_src_tile(j):
    return jnp.where(j < 3, (j + 1) % 3, j)


def _in_proj_kernel(x_ref, g_ref, w_ref, *rest, emit):
    if emit:
        k_ref, v_ref, qu_ref, wt_ref, xn_ref = rest
    else:
        k_ref, v_ref, qu_ref, xn_ref = rest
    j = pl.program_id(1)

    @pl.when(j == 0)
    def _():
        xn_ref[...] = _rms(x_ref[...], g_ref[...]).astype(BF16)

    if emit:
        wt_ref[...] = w_ref[...].astype(BF16)
        w_ref = wt_ref

    for o_ref, cond in ((k_ref, j == 0), (v_ref, j == 1), (qu_ref, j >= 2)):
        @pl.when(cond)
        def _(o_ref=o_ref):
            o_ref[...] = jnp.dot(xn_ref[...], w_ref[...], preferred_element_type=F32)


def _in_proj(x, g, w, emit, tm):
    m, d = x.shape
    tn = ATTN_WIDTH
    nt = (w.shape[1] // tn) if emit else w.shape[0]
    assert m % tm == 0 and nt > 3 and (not emit or (m == tm and w.shape[1] % tn == 0))
    if emit:
        w_spec = pl.BlockSpec((d, tn), lambda i, j: (0, _in_proj_src_tile(j)))
    else:
        w_spec = pl.BlockSpec((None, d, tn), lambda i, j: (j, 0, 0))
    out_specs = [pl.BlockSpec((tm, tn), lambda i, j: (i, 0)),
                 pl.BlockSpec((tm, tn), lambda i, j: (i, 0)),
                 pl.BlockSpec((None, tm, tn), lambda i, j: (jnp.maximum(j - 2, 0), i, 0))]
    out_shape = [jax.ShapeDtypeStruct((m, tn), F32), jax.ShapeDtypeStruct((m, tn), F32),
                 jax.ShapeDtypeStruct((nt - 2, m, tn), F32)]
    if emit:
        out_specs.append(pl.BlockSpec((None, d, tn), lambda i, j: (j, 0, 0)))
        out_shape.append(jax.ShapeDtypeStruct((nt, d, tn), BF16))
    return pl.pallas_call(
        functools.partial(_in_proj_kernel, emit=emit),
        grid=(m // tm, nt),
        in_specs=[pl.BlockSpec((tm, d), lambda i, j: (i, 0)),
                  pl.BlockSpec((1, d), lambda i, j: (0, 0)),
                  w_spec],
        out_specs=out_specs,
        out_shape=out_shape,
        scratch_shapes=[pltpu.VMEM((tm, d), BF16)],
        compiler_params=_params("arbitrary", "arbitrary"),
        name="in_proj",
    )(x, g, w)


def _moba_prompt_body(h, g, q_ref, k_ref, v_ref, o_ref, kb_ref, vt_ref, km_ref, *, nb, tps):
    blk = MOBA_BLOCK

    @pl.when(g == 0)
    def _():
        for n in range(nb):
            kf = k_ref[0, n * blk:(n + 1) * blk, :]
            kb_ref[n * blk:(n + 1) * blk, :] = kf.astype(BF16)
            vt_ref[n, 0:HEAD_DIM, :] = v_ref[0, n * blk:(n + 1) * blk, :].T.astype(BF16)
            vt_ref[n, HEAD_DIM:, :] = jnp.ones((ONES_ROWS, blk), BF16)
            km_ref[n:n + 1, :] = jnp.mean(kf, axis=0, keepdims=True)

    scale = HEAD_DIM ** -0.5 * LOG2_E
    slope = jnp.exp2(jnp.broadcast_to(-8.0 * (h + 1).astype(F32) / N_HEADS, (1, blk))) * LOG2_E

    def attend(i, qi):
        q = q_ref[0, qi * blk:(qi + 1) * blk, :]
        qb = q.astype(BF16)
        kr = lax.broadcasted_iota(jnp.int32, (blk, blk), 0)
        qc = lax.broadcasted_iota(jnp.int32, (blk, blk), 1)
        base = slope * (qc - kr).astype(F32)
        s_own = lax.dot_general(kb_ref[i * blk:(i + 1) * blk, :], qb, _NT, preferred_element_type=F32)
        ts = [jnp.where(kr <= qc, s_own * scale - base, NEG_INF)]
        if i:
            gate = lax.dot_general(km_ref[0:i, :], q, _NT, precision=lax.Precision.HIGHEST,
                                   preferred_element_type=F32)
            idxs, oks = _top_blocks(gate, i, axis=0)
            row = lax.broadcasted_iota(jnp.int32, (i, blk), 0)
            sel = jnp.zeros((i, blk), F32)
            for idx, ok in zip(idxs, oks):
                sel = jnp.where((row == idx) & ok, 1.0, sel)
            for n in range(i):
                s = lax.dot_general(kb_ref[n * blk:(n + 1) * blk, :], qb, _NT, preferred_element_type=F32)
                neg = jnp.where(sel[n:n + 1, :] > 0.0, -slope * float((i - n) * blk), NEG_INF)
                ts.append(s * scale - base + neg)
        m = jnp.max(functools.reduce(jnp.maximum, ts), axis=0, keepdims=True)
        acc = None
        for n, t in enumerate(ts):
            p = jnp.exp2(t - m)
            vt = vt_ref[i] if n == 0 else vt_ref[n - 1]
            pv = jnp.dot(vt, p.astype(BF16), preferred_element_type=F32)
            acc = pv if acc is None else acc + pv
        l = acc[HEAD_DIM:HEAD_DIM + 1, :]
        o_ref[0, qi * blk:(qi + 1) * blk, :] = (acc[0:HEAD_DIM, :] / l).T

    for v in range(nb // tps):
        @pl.when(g == v)
        def _(v=v):
            for qi in range(tps):
                attend(v * tps + qi, qi)


def _head_page_copy(cache_ref, pt_ref, buf_ref, sem, layer, b, h, page_slot, dst_row, n_pages):
    page = pt_ref[b * n_pages + page_slot]
    ps = cache_ref.shape[2]
    return pltpu.make_async_copy(cache_ref.at[layer, page, :, h, :], buf_ref.at[pl.ds(dst_row, ps), :], sem)


def _start_k_pages(st, first, count, pt_ref, kc_ref, kbuf_ref, sem_ref, *, layer, n_pages):
    bb, hh = st // N_HEADS, st % N_HEADS
    sl = st % 2
    ps = kc_ref.shape[2]

    def issue(pair, carry):
        for k in range(DMA_PRIORITIES):
            p = first + pair * DMA_PRIORITIES + k
            _head_page_copy(kc_ref, pt_ref, kbuf_ref.at[sl], sem_ref.at[sl], layer, bb, hh,
                            p, pl.multiple_of(p * ps, ps), n_pages).start(priority=k)
        return carry

    lax.fori_loop(0, count // DMA_PRIORITIES, issue, 0, unroll=4)


def _sample_scores_body(step, n_steps, pt_ref, q_ref, kc_ref, s_ref, sel_ref, kbuf_ref, km_ref, gate_ref, sem_ref,
                        *, layer, n_pages, chunk):
    nh = N_HEADS
    b, h = step // nh, step % nh
    slot = step % 2
    ps = kc_ref.shape[2]
    past = n_pages * ps
    blk = MOBA_BLOCK
    t = q_ref.shape[1]
    start_pages = functools.partial(_start_k_pages, pt_ref=pt_ref, kc_ref=kc_ref, kbuf_ref=kbuf_ref,
                                    sem_ref=sem_ref, layer=layer, n_pages=n_pages)

    @pl.when(step == 0)
    def _():
        start_pages(step, 0, n_pages)

    @pl.when(step + 1 < n_steps)
    def _():
        start_pages(step + 1, 0, n_pages)

    pltpu.make_async_copy(kbuf_ref.at[1 - slot], kbuf_ref.at[slot], sem_ref.at[slot]).wait()

    q = q_ref[0]
    qb = q.astype(BF16)
    bpc = chunk // blk

    def body(ci, carry):
        start = pl.multiple_of(ci * chunk, chunk)
        kf = kbuf_ref[slot, pl.ds(start, chunk), :]
        sc = lax.dot_general(qb, kf.astype(BF16), _NT, preferred_element_type=F32)
        for jj in range(bpc):
            s_ref[0, 0, ci * bpc + jj] = sc[:, jj * blk:(jj + 1) * blk]
        km_ref[pl.ds(pl.multiple_of(ci * bpc, bpc), bpc), :] = jnp.mean(
            kf.reshape(bpc, blk, HEAD_DIM), axis=1)
        return carry

    lax.fori_loop(0, past // chunk, body, 0, unroll=True)

    gate_ref[h] = lax.dot_general(q, km_ref[...], _NT, precision=lax.Precision.HIGHEST,
                                  preferred_element_type=F32)

    @pl.when(h == nh - 1)
    def _():
        gate = gate_ref[...].reshape(nh * t, past // blk)
        idxs, _ = _top_blocks(gate, past // blk, axis=1)
        lane = lax.broadcasted_iota(jnp.int32, (nh * t, 128), 1)
        out = jnp.zeros((nh * t, 128), jnp.int32)
        for k, idx in enumerate(idxs):
            out = jnp.where(lane == k, idx, out)
        sel_ref[0] = out.reshape(nh, t, 128)


def _moba_scores_kernel(pt_ref, qs_ref, kc_ref, qp_ref, kp_ref, vp_ref, s_ref, sel_ref, o_ref,
                        kbuf_ref, km_ref, gate_ref, sem_ref, kb_ref, vt_ref, kmp_ref,
                        *, layer, n_pages, chunk, nb, tps, every):
    step = pl.program_id(0)
    n_steps = pl.num_programs(0)
    _sample_scores_body(step, n_steps, pt_ref, qs_ref, kc_ref, s_ref, sel_ref, kbuf_ref, km_ref,
                        gate_ref, sem_ref, layer=layer, n_pages=n_pages, chunk=chunk)

    @pl.when(step % every == 0)
    def _():
        item = step // every
        groups = nb // tps
        _moba_prompt_body((item // groups) % N_HEADS, item % groups, qp_ref, kp_ref, vp_ref, o_ref,
                          kb_ref, vt_ref, kmp_ref, nb=nb, tps=tps)


def _moba_and_scores(qu_p, k_p, v_p, qu_s, cache_k, page_table, layer, tps=4):
    _, bs, ts, _ = qu_s.shape
    bp, tp, _ = k_p.shape
    blk = MOBA_BLOCK
    n_pages = page_table.shape[1]
    ps = cache_k.shape[2]
    past = n_pages * ps
    chunk = 8 * blk
    assert past % chunk == 0 and past // blk >= MOBA_TOPK and n_pages % DMA_PRIORITIES == 0
    tps = min(tps, tp // blk)
    assert tp % (blk * tps) == 0 and n_pages % (tps * DMA_PRIORITIES) == 0
    nb = tp // blk
    groups = nb // tps
    n_steps, items = bs * N_HEADS, bp * N_HEADS * groups
    assert n_steps % items == 0
    every = n_steps // items

    def prompt_idx(step):
        item = step // every
        return item // (N_HEADS * groups), (item // groups) % N_HEADS, item % groups

    def qp_map(step, pt):
        bi, h, g = prompt_idx(step)
        return (0, bi, g, h)

    def kv_map(step, pt):
        bi, h, g = prompt_idx(step)
        return (bi, 0, h)

    def o_map(step, pt):
        bi, h, g = prompt_idx(step)
        return (bi, g, h)

    grid_spec = pltpu.PrefetchScalarGridSpec(
        num_scalar_prefetch=1,
        grid=(n_steps,),
        in_specs=[pl.BlockSpec((None, 1, ts, HEAD_DIM), lambda s, pt: (0, s // N_HEADS, 0, s % N_HEADS)),
                  pl.BlockSpec(memory_space=pl.ANY),
                  pl.BlockSpec((None, 1, blk * tps, HEAD_DIM), qp_map),
                  pl.BlockSpec((1, tp, HEAD_DIM), kv_map),
                  pl.BlockSpec((1, tp, HEAD_DIM), kv_map)],
        out_specs=[pl.BlockSpec((1, 1, past // blk, ts, blk), lambda s, pt: (s // N_HEADS, s % N_HEADS, 0, 0, 0)),
                   pl.BlockSpec((1, N_HEADS, ts, 128), lambda s, pt: (s // N_HEADS, 0, 0, 0)),
                   pl.BlockSpec((1, blk * tps, HEAD_DIM), o_map)],
        scratch_shapes=[pltpu.VMEM((2, past, HEAD_DIM), F32),
                        pltpu.VMEM((past // blk, HEAD_DIM), F32),
                        pltpu.VMEM((N_HEADS, ts, past // blk), F32),
                        pltpu.SemaphoreType.DMA((2,)),
                        pltpu.VMEM((tp, HEAD_DIM), BF16), pltpu.VMEM((nb, HEAD_DIM + ONES_ROWS, blk), BF16),
                        pltpu.VMEM((nb, HEAD_DIM), F32)],
    )
    scores, sel, attn = pl.pallas_call(
        functools.partial(_moba_scores_kernel, layer=layer, n_pages=n_pages, chunk=chunk, nb=nb, tps=tps,
                          every=every),
        grid_spec=grid_spec,
        out_shape=[jax.ShapeDtypeStruct((bs, N_HEADS, past // blk, ts, blk), F32),
                   jax.ShapeDtypeStruct((bs, N_HEADS, ts, 128), jnp.int32),
                   jax.ShapeDtypeStruct((bp, tp, ATTN_WIDTH), F32)],
        compiler_params=_params("arbitrary"),
        name="moba_and_scores",
    )(page_table.reshape(-1), qu_s, cache_k, qu_p, k_p, v_p)
    return attn, scores, sel


def _sample_attend_kernel(pt_ref, sel_ref, s_ref, q_ref, kn_ref, vn_ref, vc_ref, o_ref,
                          vbuf_ref, ssel_ref, sem_ref, *, layer, n_pages, q_start):
    b = pl.program_id(0)
    h = pl.program_id(1)
    nbt, nh = pl.num_programs(0), pl.num_programs(1)
    step = b * nh + h
    slot = step % 2
    t = q_ref.shape[1]
    ps = vc_ref.shape[2]
    blk = MOBA_BLOCK
    ppb = blk // ps
    nsel = t * MOBA_TOPK

    def sel_block(st, e):
        return sel_ref[st * nsel + e]

    def copies(st, sl):
        bb, hh = st // nh, st % nh

        def mk(e, pg):
            return _head_page_copy(vc_ref, pt_ref, vbuf_ref.at[sl], sem_ref.at[sl], layer, bb, hh,
                                   sel_block(st, e) * ppb + pg, e * blk + pg * ps, n_pages)
        return mk

    def start_all(st, sl):
        mk = copies(st, sl)
        for e in range(nsel):
            for pg in range(ppb):
                mk(e, pg).start(priority=(e * ppb + pg) % DMA_PRIORITIES)

    @pl.when(step == 0)
    def _():
        start_all(step, slot)

    @pl.when(step + 1 < nbt * nh)
    def _():
        start_all(step + 1, 1 - slot)

    scale = HEAD_DIM ** -0.5
    slope = jnp.exp2(jnp.broadcast_to(-8.0 * (h + 1).astype(F32) / N_HEADS, (1, 1)))
    lane = lax.broadcasted_iota(jnp.int32, (1, blk), 1)

    ssel_ref[...] = jnp.full(ssel_ref.shape, NEG_INF, F32)
    for e in range(nsel):
        ti = e // MOBA_TOPK
        n = sel_block(step, e)
        raw = s_ref[0, 0, n, ti:ti + 1, :]
        dist = (q_start + ti - n * blk - lane).astype(F32)
        ssel_ref[ti:ti + 1, e * blk:(e + 1) * blk] = raw * scale - slope * dist

    q = q_ref[0].astype(BF16)
    r = lax.broadcasted_iota(jnp.int32, (t, t), 0)
    c = lax.broadcasted_iota(jnp.int32, (t, t), 1)
    s_own = lax.dot_general(q, kn_ref[0].astype(BF16), _NT, preferred_element_type=F32) * scale
    s_own = jnp.where(c <= r, s_own - slope * (r - c).astype(F32), NEG_INF)

    s_sel = ssel_ref[...]
    m = jnp.maximum(jnp.max(s_sel, axis=1, keepdims=True), jnp.max(s_own, axis=1, keepdims=True))
    p_sel = jnp.exp(s_sel - m)
    p_own = jnp.exp(s_own - m)
    l = jnp.sum(p_sel, axis=1, keepdims=True) + jnp.sum(p_own, axis=1, keepdims=True)

    mk = copies(step, slot)
    for e in range(nsel):
        for pg in range(ppb):
            mk(e, pg).wait()

    acc = jnp.dot(p_sel.astype(BF16), vbuf_ref[slot].astype(BF16), preferred_element_type=F32)
    acc = acc + jnp.dot(p_own.astype(BF16), vn_ref[0].astype(BF16), preferred_element_type=F32)
    o_ref[0] = acc / l


def _sample_attend(scores, sel, qu, k_new, v_new, cache_v, page_table, layer):
    b, t, _ = k_new.shape
    n_pages = page_table.shape[1]
    ps = cache_v.shape[2]
    past = n_pages * ps
    assert MOBA_BLOCK % ps == 0 and past % MOBA_BLOCK == 0 and t <= MOBA_BLOCK
    nsel = t * MOBA_TOPK
    grid_spec = pltpu.PrefetchScalarGridSpec(
        num_scalar_prefetch=2,
        grid=(b, N_HEADS),
        in_specs=[pl.BlockSpec((1, 1, past // MOBA_BLOCK, t, MOBA_BLOCK), lambda bi, h, pt, sl: (bi, h, 0, 0, 0)),
                  pl.BlockSpec((None, 1, t, HEAD_DIM), lambda bi, h, pt, sl: (0, bi, 0, h)),
                  pl.BlockSpec((1, t, HEAD_DIM), lambda bi, h, pt, sl: (bi, 0, h)),
                  pl.BlockSpec((1, t, HEAD_DIM), lambda bi, h, pt, sl: (bi, 0, h)),
                  pl.BlockSpec(memory_space=pl.ANY)],
        out_specs=pl.BlockSpec((1, t, HEAD_DIM), lambda bi, h, pt, sl: (bi, 0, h)),
        scratch_shapes=[pltpu.VMEM((2, nsel * MOBA_BLOCK, HEAD_DIM), F32),
                        pltpu.VMEM((t, nsel * MOBA_BLOCK), F32),
                        pltpu.SemaphoreType.DMA((2,))],
    )
    return pl.pallas_call(
        functools.partial(_sample_attend_kernel, layer=layer, n_pages=n_pages, q_start=past),
        grid_spec=grid_spec,
        out_shape=jax.ShapeDtypeStruct((b, t, ATTN_WIDTH), F32),
        compiler_params=_params("arbitrary", "arbitrary"),
        name="sample_attend",
    )(page_table.reshape(-1), sel[..., :MOBA_TOPK].reshape(-1), scores, qu, k_new, v_new, cache_v)


def _conv_taps():
    pad = CONV_HALO - (CONV_WIDTH - 1)
    return [(ph, [(j, (pad + j) // 8) for j in range(CONV_WIDTH) if (pad + j) % 8 == ph]) for ph in range(8)]


def _conv_kernel(ua_ref, ug_ref, prev_ref, w_ref, b_ref, g_ref, bl_ref, y_ref, st_ref, buf_ref, par_ref, z_ref,
                 *, tt, rows, nrows, lanes):
    ti = pl.program_id(1)
    ch = y_ref.shape[2]
    hist = CONV_WIDTH - 1
    pad = CONV_HALO - hist
    i_bias, i_gain, i_beta = CONV_WIDTH, CONV_WIDTH + 1, CONV_WIDTH + 2

    @pl.when(ti == 0)
    def _():
        buf_ref[0:CONV_HALO, :] = jnp.zeros((CONV_HALO, ch), F32)
        buf_ref[pad:CONV_HALO, :] = prev_ref[0]
        for j in range(CONV_WIDTH):
            par_ref[j] = jnp.broadcast_to(w_ref[j:j + 1, :], (8, ch))
        par_ref[i_bias] = jnp.broadcast_to(b_ref[...], (8, ch))
        par_ref[i_gain] = jnp.broadcast_to(g_ref[...], (8, ch))
        par_ref[i_beta] = jnp.broadcast_to(bl_ref[...], (8, ch))

    @pl.when(ti > 0)
    def _():
        buf_ref[0:CONV_HALO, :] = buf_ref[tt:tt + CONV_HALO, :]

    def glu(ci, carry):
        r0 = pl.multiple_of(ci * rows, rows)
        buf_ref[pl.ds(CONV_HALO + r0, rows), :] = (ua_ref[0, pl.ds(r0, rows), :]
                                                   * _sigmoid(ug_ref[0, pl.ds(r0, rows), :]))
        return carry

    lax.fori_loop(0, tt // rows, glu, 0)

    def conv(ci, carry):
        r0 = pl.multiple_of(ci * rows, rows)
        for c0 in range(0, ch, lanes):
            cs = slice(c0, c0 + lanes)
            win = buf_ref.at[pl.ds(r0, rows + CONV_HALO), pl.ds(c0, lanes)]
            acc = jnp.broadcast_to(par_ref[i_bias, :, cs][None], (rows // 8, 8, lanes))
            for ph, taps in _conv_taps():
                a_lo, a_hi = taps[0][1], taps[-1][1]
                x = win[8 * a_lo + ph:8 * a_hi + ph + rows, :]
                for j, a in taps:
                    xs = x[8 * (a - a_lo):8 * (a - a_lo) + rows, :].reshape(rows // 8, 8, lanes)
                    acc = acc + par_ref[j, :, cs][None] * xs
            z_ref[pl.ds(r0, rows), cs] = acc.reshape(rows, lanes)
        return carry

    lax.fori_loop(0, tt // rows, conv, 0)

    def norm(ci, carry):
        r0 = pl.multiple_of(ci * nrows, nrows)
        z = z_ref[pl.ds(r0, nrows), :]
        xc = z - jnp.mean(z, axis=-1, keepdims=True)
        xn = (xc * lax.rsqrt(jnp.mean(xc * xc, axis=-1, keepdims=True) + EPS)).reshape(nrows // 8, 8, ch)
        y = (xn * par_ref[i_gain][None] + par_ref[i_beta][None]).reshape(nrows, ch)
        y_ref[0, pl.ds(r0, nrows), :] = y * _sigmoid(y)
        return carry

    lax.fori_loop(0, tt // nrows, norm, 0, unroll=min(8, tt // nrows))

    @pl.when(ti == pl.num_programs(1) - 1)
    def _():
        st_ref[0] = buf_ref[tt + pad:tt + CONV_HALO, :]


def _conv_specs(qu, prev, layer, w_dw, b_dw, g_ln, b_ln, tt):
    _, b, t, ch = qu.shape
    assert ch == w_dw.shape[1]
    hist = CONV_WIDTH - 1
    assert t % tt == 0 and tt % 8 == 0
    rows = 64 if tt % 64 == 0 else 8
    nrows = 16 if tt % 16 == 0 else 8
    lanes = 256 if ch % 256 == 0 else ch
    row = lambda a: a.reshape(1, ch)
    in_specs = [pl.BlockSpec((None, 1, tt, ch), lambda bi, ti: (1, bi, ti, 0)),
                pl.BlockSpec((None, 1, tt, ch), lambda bi, ti: (2, bi, ti, 0)),
                pl.BlockSpec((None, 1, hist, ch), lambda bi, ti: (layer, bi, 0, 0)),
                pl.BlockSpec((CONV_WIDTH, ch), lambda bi, ti: (0, 0)),
                pl.BlockSpec((1, ch), lambda bi, ti: (0, 0)),
                pl.BlockSpec((1, ch), lambda bi, ti: (0, 0)),
                pl.BlockSpec((1, ch), lambda bi, ti: (0, 0))]
    out_specs = [pl.BlockSpec((1, tt, ch), lambda bi, ti: (bi, ti, 0)),
                 pl.BlockSpec((1, hist, ch), lambda bi, ti: (bi, 0, 0))]
    out_shape = [jax.ShapeDtypeStruct((b, t, ch), F32), jax.ShapeDtypeStruct((b, hist, ch), F32)]
    scratch = [pltpu.VMEM((tt + CONV_HALO, ch), F32), pltpu.VMEM((CONV_WIDTH + 3, 8, ch), F32),
               pltpu.VMEM((tt, ch), F32)]
    args = (qu, qu, prev, w_dw, row(b_dw), row(g_ln), row(b_ln))
    return (b, t // tt), in_specs, out_specs, out_shape, scratch, args, dict(tt=tt, rows=rows, nrows=nrows,
                                                                            lanes=lanes)


def _conformer_conv(qu, prev, layer, w_dw, b_dw, g_ln, b_ln, tt):
    grid, in_specs, out_specs, out_shape, scratch, args, kw = _conv_specs(qu, prev, layer, w_dw, b_dw, g_ln,
                                                                         b_ln, tt)
    return pl.pallas_call(
        functools.partial(_conv_kernel, **kw),
        grid=grid,
        in_specs=in_specs,
        out_specs=out_specs,
        out_shape=out_shape,
        scratch_shapes=scratch,
        compiler_params=_params("arbitrary", "arbitrary"),
        name="conformer_conv",
    )(*args)


def _mix_out_kernel(attn_ref, conv_ref, h_ref, ga_ref, gc_ref, w_ref, gp_ref, o_ref, *rest, emit):
    if emit:
        (wt_ref,) = rest
        wt_ref[...] = w_ref[...].astype(BF16)
        w_ref = wt_ref
    wa = attn_ref.shape[1]
    a = _rms(attn_ref[...], ga_ref[...]).astype(BF16)
    cv = _rms(conv_ref[...], gc_ref[...]).astype(BF16)
    mixed = (jnp.dot(a, w_ref[0:wa, :], preferred_element_type=F32)
             + jnp.dot(cv, w_ref[wa:, :], preferred_element_type=F32))
    o_ref[...] = h_ref[...] + _rms(mixed, gp_ref[...])


def _mix_out(attn, conv, h, g_attn, g_conv, w_out, g_post, tm, emit):
    m, wa = attn.shape
    wc = conv.shape[1]
    d = h.shape[1]
    assert m % tm == 0 and (not emit or m == tm)
    row = lambda a: a.reshape(1, -1)
    out_specs = [pl.BlockSpec((tm, d), lambda i: (i, 0))]
    out_shape = [jax.ShapeDtypeStruct((m, d), F32)]
    if emit:
        out_specs.append(pl.BlockSpec((wa + wc, d), lambda i: (0, 0)))
        out_shape.append(jax.ShapeDtypeStruct((wa + wc, d), BF16))
    outs = pl.pallas_call(
        functools.partial(_mix_out_kernel, emit=emit),
        grid=(m // tm,),
        in_specs=[pl.BlockSpec((tm, wa), lambda i: (i, 0)),
                  pl.BlockSpec((tm, wc), lambda i: (i, 0)),
                  pl.BlockSpec((tm, d), lambda i: (i, 0)),
                  pl.BlockSpec((1, wa), lambda i: (0, 0)),
                  pl.BlockSpec((1, wc), lambda i: (0, 0)),
                  pl.BlockSpec((wa + wc, d), lambda i: (0, 0)),
                  pl.BlockSpec((1, d), lambda i: (0, 0))],
        out_specs=out_specs,
        out_shape=out_shape,
        compiler_params=_params("arbitrary", vmem_limit=VMEM_LIMIT_MAX if emit else VMEM_LIMIT),
        name="mix_out",
    )(attn, conv, h, row(g_attn), row(g_conv), w_out, row(g_post))
    return outs[0], (outs[1] if emit else None)


def _xattn_kernel(h_ref, mk_ref, mv_ref, gpre_ref, wq_ref, wo_ref, gpost_ref, o_ref, *rest, emit):
    if emit:
        wq_out, wo_out = rest
        wq_out[...] = wq_ref[...].astype(BF16)
        wo_out[...] = wo_ref[...].astype(BF16)
        wq_ref, wo_ref = wq_out, wo_out
    h = h_ref[...]
    xn = _rms(h, gpre_ref[...]).astype(BF16)
    xq = jnp.dot(xn, wq_ref[...], preferred_element_type=F32)
    scale = X_HEAD_DIM ** -0.5
    seqs = mk_ref.shape[0]
    rows = h.shape[0] // seqs
    per_seq = []
    for bi in range(seqs):
        outs = []
        for hh in range(X_HEADS):
            cols = slice(hh * X_HEAD_DIM, (hh + 1) * X_HEAD_DIM)
            qh = xq[bi * rows:(bi + 1) * rows, cols].astype(BF16)
            s = lax.dot_general(qh, mk_ref[bi, :, cols].astype(BF16), _NT, preferred_element_type=F32) * scale
            p = jnp.exp(s - jnp.max(s, axis=1, keepdims=True))
            l = jnp.sum(p, axis=1, keepdims=True)
            outs.append(jnp.dot(p.astype(BF16), mv_ref[bi, :, cols].astype(BF16),
                                preferred_element_type=F32) / l)
        per_seq.append(jnp.concatenate(outs, axis=1))
    o = jnp.concatenate(per_seq, axis=0).astype(BF16)
    y = jnp.dot(o, wo_ref[...], preferred_element_type=F32)
    o_ref[...] = h + _rms(y, gpost_ref[...])


def _xattn(h, mem_k, mem_v, layer, g_pre, w_xq, w_xo, g_post, rows_per_batch, tm, emit):
    m, d = h.shape
    seqs = max(1, tm // rows_per_batch)
    tiles = max(1, rows_per_batch // tm)
    assert m % tm == 0 and tm * tiles == rows_per_batch * seqs and (not emit or m == tm)
    mem_len, xw = mem_k.shape[2:]
    row = lambda a: a.reshape(1, -1)
    out_specs = [pl.BlockSpec((tm, d), lambda i: (i, 0))]
    out_shape = [jax.ShapeDtypeStruct((m, d), F32)]
    if emit:
        out_specs += [pl.BlockSpec((d, xw), lambda i: (0, 0)), pl.BlockSpec((xw, d), lambda i: (0, 0))]
        out_shape += [jax.ShapeDtypeStruct((d, xw), BF16), jax.ShapeDtypeStruct((xw, d), BF16)]
    outs = pl.pallas_call(
        functools.partial(_xattn_kernel, emit=emit),
        grid=(m // tm,),
        in_specs=[pl.BlockSpec((tm, d), lambda i: (i, 0)),
                  pl.BlockSpec((None, seqs, mem_len, xw), lambda i: (layer, i // tiles, 0, 0)),
                  pl.BlockSpec((None, seqs, mem_len, xw), lambda i: (layer, i // tiles, 0, 0)),
                  pl.BlockSpec((1, d), lambda i: (0, 0)),
                  pl.BlockSpec((d, xw), lambda i: (0, 0)),
                  pl.BlockSpec((xw, d), lambda i: (0, 0)),
                  pl.BlockSpec((1, d), lambda i: (0, 0))],
        out_specs=out_specs,
        out_shape=out_shape,
        compiler_params=_params("arbitrary"),
        name="xattn",
    )(h, mem_k, mem_v, row(g_pre), w_xq, w_xo, row(g_post))
    return outs[0], (tuple(outs[1:]) if emit else None)


def _ffn_body(f, nf, h_ref, gpre_ref, wg_ref, wu_ref, wo_ref, gpost_ref, o_ref, rest, emit):
    if emit:
        wg_out, wu_out, wo_out, xn_ref, acc_ref = rest
        wg_out[...] = wg_ref[...].astype(BF16)
        wu_out[...] = wu_ref[...].astype(BF16)
        wo_out[...] = wo_ref[...].astype(BF16)
        wg_ref, wu_ref, wo_ref = wg_out, wu_out, wo_out
    else:
        xn_ref, acc_ref = rest

    @pl.when(f == 0)
    def _():
        xn_ref[...] = _rms(h_ref[...], gpre_ref[...]).astype(BF16)
        acc_ref[...] = jnp.zeros(acc_ref.shape, F32)

    xn = xn_ref[...]
    g = jnp.dot(xn, wg_ref[...], preferred_element_type=F32)
    u = jnp.dot(xn, wu_ref[...], preferred_element_type=F32)
    a = (g * jax.nn.sigmoid(g) * u).astype(BF16)
    acc_ref[...] += jnp.dot(a, wo_ref[...], preferred_element_type=F32)

    @pl.when(f == nf - 1)
    def _():
        o_ref[...] = h_ref[...] + _rms(acc_ref[...], gpost_ref[...])


def _ffn_kernel(h_ref, gpre_ref, wg_ref, wu_ref, wo_ref, gpost_ref, o_ref, *rest, emit):
    _ffn_body(pl.program_id(1), pl.num_programs(1), h_ref, gpre_ref, wg_ref, wu_ref, wo_ref, gpost_ref, o_ref,
              rest, emit)


def _ffn_specs(h, w, tm, tf, emit, tile):
    m, d = h.shape
    if emit:
        w_in, w_out = w
        hidden = w_out.shape[0]
        nf = hidden // tf
        assert m == tm and hidden % tf == 0 and w_in.shape[1] == 2 * hidden
        w_args = (w_in, w_in, w_out)
        w_specs = [pl.BlockSpec((d, tf), lambda *ids: (0, tile(*ids)[1])),
                   pl.BlockSpec((d, tf), lambda *ids: (0, tile(*ids)[1] + nf)),
                   pl.BlockSpec((tf, d), lambda *ids: (tile(*ids)[1], 0))]
    else:
        w_args = w
        nf, hidden = w[0].shape[0], w[2].shape[0]
        assert m % tm == 0 and w[0].shape == w[1].shape == (nf, d, tf) and hidden == nf * tf
        w_specs = [pl.BlockSpec((None, d, tf), lambda *ids: (tile(*ids)[1], 0, 0)),
                   pl.BlockSpec((None, d, tf), lambda *ids: (tile(*ids)[1], 0, 0)),
                   pl.BlockSpec((tf, d), lambda *ids: (tile(*ids)[1], 0))]
    in_specs = [pl.BlockSpec((tm, d), lambda *ids: (tile(*ids)[0], 0)),
                pl.BlockSpec((1, d), lambda *ids: (0, 0)),
                *w_specs,
                pl.BlockSpec((1, d), lambda *ids: (0, 0))]
    out_specs = [pl.BlockSpec((tm, d), lambda *ids: (tile(*ids)[0], 0))]
    out_shape = [jax.ShapeDtypeStruct((m, d), F32)]
    if emit:
        out_specs += [pl.BlockSpec((None, d, tf), lambda *ids: (tile(*ids)[1], 0, 0)),
                      pl.BlockSpec((None, d, tf), lambda *ids: (tile(*ids)[1], 0, 0)),
                      pl.BlockSpec((tf, d), lambda *ids: (tile(*ids)[1], 0))]
        out_shape += [jax.ShapeDtypeStruct((nf, d, tf), BF16), jax.ShapeDtypeStruct((nf, d, tf), BF16),
                      jax.ShapeDtypeStruct((hidden, d), BF16)]
    scratch = [pltpu.VMEM((tm, d), BF16), pltpu.VMEM((tm, d), F32)]
    return nf, in_specs, out_specs, out_shape, scratch, w_args


def _ffn(h, g_pre, w, g_post, tm, tf, emit):
    m = h.shape[0]
    row = lambda a: a.reshape(1, -1)
    nf, in_specs, out_specs, out_shape, scratch, w_args = _ffn_specs(h, w, tm, tf, emit, lambda i, f: (i, f))
    outs = pl.pallas_call(
        functools.partial(_ffn_kernel, emit=emit),
        grid=(m // tm, nf),
        in_specs=in_specs,
        out_specs=out_specs,
        out_shape=out_shape,
        scratch_shapes=scratch,
        compiler_params=_params("arbitrary", "arbitrary"),
        name="ffn",
    )(h, row(g_pre), *w_args, row(g_post))
    return outs[0], tuple(outs[1:])


def _conv_ffn_kernel(*refs, conv_kw, nf):
    ci, fi, co, fo, cs, fs = refs[0:7], refs[7:13], refs[13:15], refs[15:19], refs[19:22], refs[22:24]
    step = pl.program_id(0) * pl.num_programs(1) + pl.program_id(1)

    @pl.when(step < nf)
    def _():
        _ffn_body(step, nf, *fi, fo[0], tuple(fo[1:]) + tuple(fs), True)

    _conv_kernel(*ci, *co, *cs, **conv_kw)


def _conv_and_emit_ffn(conv_args, h, g_pre, w, g_post, tf):
    grid, c_in, c_out, c_shape, c_scratch, c_args, conv_kw = _conv_specs(*conv_args)
    nt = grid[1]
    row = lambda a: a.reshape(1, -1)
    tile = lambda bi, ti: (0, jnp.minimum(bi * nt + ti, nf - 1))
    nf = w[1].shape[0] // tf
    if grid[0] * nt < nf:
        return None
    nf, f_in, f_out, f_shape, f_scratch, w_args = _ffn_specs(h, w, h.shape[0], tf, True, tile)
    outs = pl.pallas_call(
        functools.partial(_conv_ffn_kernel, conv_kw=conv_kw, nf=nf),
        grid=grid,
        in_specs=c_in + f_in,
        out_specs=c_out + f_out,
        out_shape=c_shape + f_shape,
        scratch_shapes=c_scratch + f_scratch,
        compiler_params=_params("arbitrary", "arbitrary", vmem_limit=VMEM_LIMIT_MAX),
        name="conv_and_ffn",
    )(*c_args, h, row(g_pre), *w_args, row(g_post))
    return outs[0], outs[1], outs[2], tuple(outs[3:])


def _row_tile(m):
    return 512 if m % 512 == 0 else m


def _project_in(x, wts, w_in, emit):
    b, t, d = x.shape
    m = b * t
    assert wts["w_dw"].shape[1] == ATTN_WIDTH
    k_new, v_new, qu, *w_in_b = _in_proj(x.reshape(m, d), wts["g_pre_mix"].reshape(1, d), w_in, emit,
                                         tm=1024 if m % 1024 == 0 else _row_tile(m))
    k_new, v_new = k_new.reshape(b, t, ATTN_WIDTH), v_new.reshape(b, t, ATTN_WIDTH)
    return k_new, v_new, qu.reshape(qu.shape[0], b, t, ATTN_WIDTH), (w_in_b[0] if emit else None)


def _mix_and_xattn(x, attn, conv, mem, mem_layer, wts, w, emit, x_tile):
    b, t, d = x.shape
    m = b * t
    h, w_out_b = _mix_out(attn.reshape(m, ATTN_WIDTH), conv.reshape(m, -1), x.reshape(m, d), wts["g_attn_grp"],
                          wts["g_conv_grp"], w[0], wts["g_post_mix"], tm=_row_tile(m), emit=emit)
    h, w_x_b = _xattn(h, mem[0], mem[1], mem_layer, wts["g_pre_x"], w[1], w[2], wts["g_post_x"],
                      rows_per_batch=t, tm=x_tile, emit=emit)
    return h, ((w_out_b,) + w_x_b if emit else None)


def kernel(x_prompt, x_sample, cache_k, cache_v, state_conv, cache_mem_k, cache_mem_v, page_table,
           mem_prompt, g_mem, w_mem_k, w_mem_v, g_pre_mix, w_in, w_dw, b_dw, g_ln_conv, b_ln_conv,
           g_attn_grp, g_conv_grp, w_out, g_post_mix, g_pre_x, w_xq, w_xo, g_post_x,
           g_pre_ffn, w_ffn_in, w_ffn_out, g_post_ffn):
    depth = w_in.shape[0]
    bp, tp, d = x_prompt.shape
    bs, ts, _ = x_sample.shape
    mem_len = mem_prompt.shape[1]
    xw = X_HEADS * X_HEAD_DIM
    ch = w_dw.shape[2]
    hp, hs = x_prompt, x_sample
    outs = [[] for _ in range(8)]
    for l in range(depth):
        wts = dict(g_pre_mix=g_pre_mix[l], w_dw=w_dw[l], b_dw=b_dw[l],
                   g_ln_conv=g_ln_conv[l], b_ln_conv=b_ln_conv[l], g_attn_grp=g_attn_grp[l],
                   g_conv_grp=g_conv_grp[l], g_post_mix=g_post_mix[l], g_pre_x=g_pre_x[l],
                   g_post_x=g_post_x[l], g_pre_ffn=g_pre_ffn[l], g_post_ffn=g_post_ffn[l])

        ks, vs, qu_s, w_in_b = _project_in(hs, wts, w_in[l], True)
        kp, vp, qu_p, _ = _project_in(hp, wts, w_in_b, False)
        attn_p, scores, sel = _moba_and_scores(qu_p, kp, vp, qu_s, cache_k, page_table, l)
        attn_s = _sample_attend(scores, sel, qu_s, ks, vs, cache_v, page_table, l)

        conv_w = (wts["w_dw"], wts["b_dw"], wts["g_ln_conv"], wts["b_ln_conv"])
        conv_s, cs = _conformer_conv(qu_s, state_conv, l, *conv_w, ts)
        sample_mem = (cache_mem_k.reshape(depth, bs, mem_len, xw), cache_mem_v.reshape(depth, bs, mem_len, xw))
        hs_mid, w_mid_b = _mix_and_xattn(hs, attn_s, conv_s, sample_mem, l, wts, (w_out[l], w_xq[l], w_xo[l]),
                                         True, x_tile=bs * ts)

        conv_args = (qu_p, jnp.zeros((1, bp, CONV_WIDTH - 1, ch), F32), 0, *conv_w, _row_tile(tp))
        ffn_w = (w_ffn_in[l], w_ffn_out[l])
        fused = _conv_and_emit_ffn(conv_args, hs_mid, wts["g_pre_ffn"], ffn_w, wts["g_post_ffn"], tf=512)
        if fused is None:
            conv_p, cp = _conformer_conv(*conv_args)
            hs_out, w_ffn_b = _ffn(hs_mid, wts["g_pre_ffn"], ffn_w, wts["g_post_ffn"], tm=bs * ts, tf=512,
                                   emit=True)
        else:
            conv_p, cp, hs_out, w_ffn_b = fused
        hs = hs_out.reshape(bs, ts, d)

        w_mem = jnp.concatenate([w_mem_k[l], w_mem_v[l]], axis=1).astype(BF16)
        mk_p, mv_p = _norm_matmul(mem_prompt.reshape(bp * mem_len, d), g_mem[l].reshape(1, d), w_mem,
                                  (xw, xw), tn=xw, tm=_row_tile(bp * mem_len))
        mk_p, mv_p = mk_p.reshape(bp, mem_len, xw), mv_p.reshape(bp, mem_len, xw)
        hp_mid, _ = _mix_and_xattn(hp, attn_p, conv_p, (mk_p[None], mv_p[None]), 0, wts, w_mid_b, False,
                                   x_tile=_row_tile(tp))
        hp_out, _ = _ffn(hp_mid, wts["g_pre_ffn"], w_ffn_b, wts["g_post_ffn"], tm=_row_tile(bp * tp), tf=512,
                         emit=False)
        hp = hp_out.reshape(bp, tp, d)
        mem_shape = (bp, mem_len, X_HEADS, X_HEAD_DIM)
        heads = lambda a: a.reshape(a.shape[:2] + (N_HEADS, HEAD_DIM))
        for lst, a in zip(outs, (heads(kp), heads(vp), cp, mk_p.reshape(mem_shape), mv_p.reshape(mem_shape),
                                 heads(ks), heads(vs), cs)):
            lst.append(a)
    return (hp, hs) + tuple(jnp.stack(lst, 0) for lst in outs)
```

```python
import functools

import jax
import jax.numpy as jnp
from jax import lax
from jax.experimental import pallas as pl
from jax.experimental.pallas import tpu as pltpu

EPS = 1e-6
N_HEADS = 8
HEAD_DIM = 128
ATTN_WIDTH = N_HEADS * HEAD_DIM
CONV_WIDTH = 31
MOBA_BLOCK = 256
MOBA_TOPK = 3
X_HEADS = 4
X_HEAD_DIM = 128

CONV_HALO = 32
DMA_PRIORITIES = 2
ONES_ROWS = 16
VMEM_LIMIT = 56 * 1024 * 1024
VMEM_LIMIT_MAX = 60 * 1024 * 1024

F32 = jnp.float32
BF16 = jnp.bfloat16
NEG_INF = float("-inf")
LOG2_E = 1.4426950408889634
_NT = (((1,), (1,)), ((), ()))


def _params(*sem, vmem_limit=VMEM_LIMIT):
    return pltpu.CompilerParams(dimension_semantics=sem, vmem_limit_bytes=vmem_limit)


def _rms(x, g):
    return x * lax.rsqrt(jnp.mean(x * x, axis=-1, keepdims=True) + EPS) * g


def _sigmoid(x):
    return 0.5 * jnp.tanh(0.5 * x) + 0.5


def _top_blocks(gate, n_valid, axis):
    nb = gate.shape[axis]
    pos = lax.broadcasted_iota(jnp.int32, gate.shape, axis)
    g = jnp.where(pos < n_valid, gate, NEG_INF)
    idxs, oks = [], []
    for _ in range(MOBA_TOPK):
        m = jnp.max(g, axis=axis, keepdims=True)
        idx = jnp.min(jnp.where(g == m, pos, nb), axis=axis, keepdims=True)
        ok = m > NEG_INF
        idxs.append(idx)
        oks.append(ok)
        g = jnp.where((pos == idx) & ok, NEG_INF, g)
    return idxs, oks


def _norm_matmul_kernel(x_ref, g_ref, w_ref, *rest, tile_ranges):
    out_refs, xn_ref = rest[:-1], rest[-1]
    j = pl.program_id(1)

    @pl.when(j == 0)
    def _():
        xn_ref[...] = _rms(x_ref[...], g_ref[...]).astype(BF16)

    for o_ref, (lo, hi) in zip(out_refs, tile_ranges):
        @pl.when((j >= lo) & (j < hi))
        def _(o_ref=o_ref):
            o_ref[...] = jnp.dot(xn_ref[...], w_ref[...], preferred_element_type=F32)


def _norm_matmul(x, g, w, widths, tn, tm):
    m, d = x.shape
    n = w.shape[1]
    assert sum(widths) == n and all(wd % tn == 0 for wd in widths) and m % tm == 0
    tile_ranges, lo = [], 0
    for wd in widths:
        tile_ranges.append((lo, lo + wd // tn))
        lo += wd // tn

    def out_map(i, j, lo, cnt):
        return (i, jnp.clip(j - lo, 0, cnt - 1))

    out_specs = [pl.BlockSpec((tm, tn), functools.partial(out_map, lo=lo, cnt=hi - lo))
                 for lo, hi in tile_ranges]
    return pl.pallas_call(
        functools.partial(_norm_matmul_kernel, tile_ranges=tuple(tile_ranges)),
        grid=(m // tm, n // tn),
        in_specs=[pl.BlockSpec((tm, d), lambda i, j: (i, 0)),
                  pl.BlockSpec((1, d), lambda i, j: (0, 0)),
                  pl.BlockSpec((d, tn), lambda i, j: (0, j))],
        out_specs=out_specs,
        out_shape=[jax.ShapeDtypeStruct((m, wd), F32) for wd in widths],
        scratch_shapes=[pltpu.VMEM((tm, d), BF16)],
        compiler_params=_params("arbitrary", "arbitrary"),
        name="norm_matmul",
    )(x, g, w)


def _in_proj_src_tile(j):
    return jnp.where(j < 3, (j + 1) % 3, j)


def _in_proj_kernel(x_ref, g_ref, w_ref, *rest, emit):
    if emit:
        k_ref, v_ref, qu_ref, wt_ref, xn_ref = rest
    else:
        k_ref, v_ref, qu_ref, xn_ref = rest
    j = pl.program_id(1)

    if emit:
        wt_ref[...] = w_ref[...].astype(BF16)
        w_ref = wt_ref

    for o_ref, cond, first in ((k_ref, j == 0, True), (v_ref, j == 1, False), (qu_ref, j >= 2, False)):
        @pl.when(cond)
        def _(o_ref=o_ref, first=first):
            if first:
                xn = _rms(x_ref[...], g_ref[...]).astype(BF16)
                xn_ref[...] = xn
            else:
                xn = xn_ref[...]
            o_ref[...] = jnp.dot(xn, w_ref[...], preferred_element_type=F32)


def _in_proj(x, g, w, emit, tm):
    m, d = x.shape
    tn = ATTN_WIDTH
    nt = (w.shape[1] // tn) if emit else w.shape[0]
    assert m % tm == 0 and nt > 3 and (not emit or (m == tm and w.shape[1] % tn == 0))
    if emit:
        w_spec = pl.BlockSpec((d, tn), lambda i, j: (0, _in_proj_src_tile(j)))
    else:
        w_spec = pl.BlockSpec((None, d, tn), lambda i, j: (j, 0, 0))
    out_specs = [pl.BlockSpec((tm, tn), lambda i, j: (i, 0)),
                 pl.BlockSpec((tm, tn), lambda i, j: (i, 0)),
                 pl.BlockSpec((None, tm, tn), lambda i, j: (jnp.maximum(j - 2, 0), i, 0))]
    out_shape = [jax.ShapeDtypeStruct((m, tn), F32), jax.ShapeDtypeStruct((m, tn), F32),
                 jax.ShapeDtypeStruct((nt - 2, m, tn), F32)]
    if emit:
        out_specs.append(pl.BlockSpec((None, d, tn), lambda i, j: (j, 0, 0)))
        out_shape.append(jax.ShapeDtypeStruct((nt, d, tn), BF16))
    return pl.pallas_call(
        functools.partial(_in_proj_kernel, emit=emit),
        grid=(m // tm, nt),
        in_specs=[pl.BlockSpec((tm, d), lambda i, j: (i, 0)),
                  pl.BlockSpec((1, d), lambda i, j: (0, 0)),
                  w_spec],
        out_specs=out_specs,
        out_shape=out_shape,
        scratch_shapes=[pltpu.VMEM((tm, d), BF16)],
        compiler_params=_params("arbitrary", "arbitrary"),
        name="in_proj",
    )(x, g, w)


def _moba_prompt_body(h, g, q_ref, k_ref, v_ref, o_ref, kb_ref, vt_ref, km_ref, *, nb, tps):
    blk = MOBA_BLOCK

    @pl.when(g == 0)
    def _():
        for n in range(nb):
            kf = k_ref[0, n * blk:(n + 1) * blk, :]
            kb_ref[n * blk:(n + 1) * blk, :] = kf.astype(BF16)
            vt_ref[n, 0:HEAD_DIM, :] = v_ref[0, n * blk:(n + 1) * blk, :].T.astype(BF16)
            vt_ref[n, HEAD_DIM:, :] = jnp.ones((ONES_ROWS, blk), BF16)
            km_ref[n:n + 1, :] = jnp.mean(kf, axis=0, keepdims=True)

    scale = HEAD_DIM ** -0.5 * LOG2_E
    slope = jnp.exp2(jnp.broadcast_to(-8.0 * (h + 1).astype(F32) / N_HEADS, (1, blk))) * LOG2_E

    def attend(i, qi):
        q = q_ref[0, qi * blk:(qi + 1) * blk, :]
        qb = q.astype(BF16)
        kr = lax.broadcasted_iota(jnp.int32, (blk, blk), 0)
        qc = lax.broadcasted_iota(jnp.int32, (blk, blk), 1)
        base = slope * (qc - kr).astype(F32)
        s_own = lax.dot_general(kb_ref[i * blk:(i + 1) * blk, :], qb, _NT, preferred_element_type=F32)
        ts = [jnp.where(kr <= qc, s_own * scale - base, NEG_INF)]
        if i:
            gate = lax.dot_general(km_ref[0:i, :], q, _NT, precision=lax.Precision.HIGHEST,
                                   preferred_element_type=F32)
            idxs, oks = _top_blocks(gate, i, axis=0)
            row = lax.broadcasted_iota(jnp.int32, (i, blk), 0)
            sel = jnp.zeros((i, blk), F32)
            for idx, ok in zip(idxs, oks):
                sel = jnp.where((row == idx) & ok, 1.0, sel)
            for n in range(i):
                s = lax.dot_general(kb_ref[n * blk:(n + 1) * blk, :], qb, _NT, preferred_element_type=F32)
                neg = jnp.where(sel[n:n + 1, :] > 0.0, -slope * float((i - n) * blk), NEG_INF)
                ts.append(s * scale - base + neg)
        m = jnp.max(functools.reduce(jnp.maximum, ts), axis=0, keepdims=True)
        acc = None
        for n, t in enumerate(ts):
            p = jnp.exp2(t - m)
            vt = vt_ref[i] if n == 0 else vt_ref[n - 1]
            pv = jnp.dot(vt, p.astype(BF16), preferred_element_type=F32)
            acc = pv if acc is None else acc + pv
        l = acc[HEAD_DIM:HEAD_DIM + 1, :]
        o_ref[0, qi * blk:(qi + 1) * blk, :] = (acc[0:HEAD_DIM, :] / l).T

    for v in range(nb // tps):
        @pl.when(g == v)
        def _(v=v):
            for qi in range(tps):
                attend(v * tps + qi, qi)


def _head_page_copy(cache_ref, pt_ref, buf_ref, sem, layer, b, h, page_slot, dst_row, n_pages):
    page = pt_ref[b * n_pages + page_slot]
    ps = cache_ref.shape[2]
    return pltpu.make_async_copy(cache_ref.at[layer, page, :, h, :], buf_ref.at[pl.ds(dst_row, ps), :], sem)


def _start_k_pages(st, first, count, pt_ref, kc_ref, kbuf_ref, sem_ref, *, layer, n_pages):
    bb, hh = st // N_HEADS, st % N_HEADS
    sl = st % 2
    ps = kc_ref.shape[2]

    def issue(pair, carry):
        for k in range(DMA_PRIORITIES):
            p = first + pair * DMA_PRIORITIES + k
            _head_page_copy(kc_ref, pt_ref, kbuf_ref.at[sl], sem_ref.at[sl], layer, bb, hh,
                            p, pl.multiple_of(p * ps, ps), n_pages).start(priority=k)
        return carry

    lax.fori_loop(0, count // DMA_PRIORITIES, issue, 0, unroll=4)


def _sample_scores_body(step, n_steps, pt_ref, q_ref, kc_ref, s_ref, sel_ref, kbuf_ref, km_ref, gate_ref, sem_ref,
                        *, layer, n_pages, chunk):
    nh = N_HEADS
    b, h = step // nh, step % nh
    slot = step % 2
    ps = kc_ref.shape[2]
    past = n_pages * ps
    blk = MOBA_BLOCK
    t = q_ref.shape[1]
    start_pages = functools.partial(_start_k_pages, pt_ref=pt_ref, kc_ref=kc_ref, kbuf_ref=kbuf_ref,
                                    sem_ref=sem_ref, layer=layer, n_pages=n_pages)

    @pl.when(step == 0)
    def _():
        start_pages(step, 0, n_pages)

    @pl.when(step + 1 < n_steps)
    def _():
        start_pages(step + 1, 0, n_pages)

    pltpu.make_async_copy(kbuf_ref.at[1 - slot], kbuf_ref.at[slot], sem_ref.at[slot]).wait()

    q = q_ref[0]
    qb = q.astype(BF16)
    bpc = chunk // blk

    def body(ci, carry):
        start = pl.multiple_of(ci * chunk, chunk)
        kf = kbuf_ref[slot, pl.ds(start, chunk), :]
        sc = lax.dot_general(qb, kf.astype(BF16), _NT, preferred_element_type=F32)
        for jj in range(bpc):
            s_ref[0, 0, ci * bpc + jj] = sc[:, jj * blk:(jj + 1) * blk]
        km_ref[pl.ds(pl.multiple_of(ci * bpc, bpc), bpc), :] = jnp.mean(
            kf.reshape(bpc, blk, HEAD_DIM), axis=1)
        return carry

    lax.fori_loop(0, past // chunk, body, 0, unroll=True)

    gate_ref[h] = lax.dot_general(q, km_ref[...], _NT, precision=lax.Precision.HIGHEST,
                                  preferred_element_type=F32)

    @pl.when(h == nh - 1)
    def _():
        gate = gate_ref[...].reshape(nh * t, past // blk)
        idxs, _ = _top_blocks(gate, past // blk, axis=1)
        lane = lax.broadcasted_iota(jnp.int32, (nh * t, 128), 1)
        out = jnp.zeros((nh * t, 128), jnp.int32)
        for k, idx in enumerate(idxs):
            out = jnp.where(lane == k, idx, out)
        sel_ref[0] = out.reshape(nh, t, 128)


def _moba_scores_kernel(pt_ref, qs_ref, kc_ref, qp_ref, kp_ref, vp_ref, s_ref, sel_ref, o_ref,
                        kbuf_ref, km_ref, gate_ref, sem_ref, kb_ref, vt_ref, kmp_ref,
                        *, layer, n_pages, chunk, nb, tps, every):
    step = pl.program_id(0)
    n_steps = pl.num_programs(0)
    _sample_scores_body(step, n_steps, pt_ref, qs_ref, kc_ref, s_ref, sel_ref, kbuf_ref, km_ref,
                        gate_ref, sem_ref, layer=layer, n_pages=n_pages, chunk=chunk)

    @pl.when(step % every == 0)
    def _():
        item = step // every
        groups = nb // tps
        _moba_prompt_body((item // groups) % N_HEADS, item % groups, qp_ref, kp_ref, vp_ref, o_ref,
                          kb_ref, vt_ref, kmp_ref, nb=nb, tps=tps)


def _moba_and_scores(qu_p, k_p, v_p, qu_s, cache_k, page_table, layer, tps=4):
    _, bs, ts, _ = qu_s.shape
    bp, tp, _ = k_p.shape
    blk = MOBA_BLOCK
    n_pages = page_table.shape[1]
    ps = cache_k.shape[2]
    past = n_pages * ps
    chunk = 8 * blk
    assert past % chunk == 0 and past // blk >= MOBA_TOPK and n_pages % DMA_PRIORITIES == 0
    tps = min(tps, tp // blk)
    assert tp % (blk * tps) == 0 and n_pages % (tps * DMA_PRIORITIES) == 0
    nb = tp // blk
    groups = nb // tps
    n_steps, items = bs * N_HEADS, bp * N_HEADS * groups
    assert n_steps % items == 0
    every = n_steps // items

    def prompt_idx(step):
        item = step // every
        return item // (N_HEADS * groups), (item // groups) % N_HEADS, item % groups

    def qp_map(step, pt):
        bi, h, g = prompt_idx(step)
        return (0, bi, g, h)

    def kv_map(step, pt):
        bi, h, g = prompt_idx(step)
        return (bi, 0, h)

    def o_map(step, pt):
        bi, h, g = prompt_idx(step)
        return (bi, g, h)

    grid_spec = pltpu.PrefetchScalarGridSpec(
        num_scalar_prefetch=1,
        grid=(n_steps,),
        in_specs=[pl.BlockSpec((None, 1, ts, HEAD_DIM), lambda s, pt: (0, s // N_HEADS, 0, s % N_HEADS)),
                  pl.BlockSpec(memory_space=pl.ANY),
                  pl.BlockSpec((None, 1, blk * tps, HEAD_DIM), qp_map),
                  pl.BlockSpec((1, tp, HEAD_DIM), kv_map),
                  pl.BlockSpec((1, tp, HEAD_DIM), kv_map)],
        out_specs=[pl.BlockSpec((1, 1, past // blk, ts, blk), lambda s, pt: (s // N_HEADS, s % N_HEADS, 0, 0, 0)),
                   pl.BlockSpec((1, N_HEADS, ts, 128), lambda s, pt: (s // N_HEADS, 0, 0, 0)),
                   pl.BlockSpec((1, blk * tps, HEAD_DIM), o_map)],
        scratch_shapes=[pltpu.VMEM((2, past, HEAD_DIM), F32),
                        pltpu.VMEM((past // blk, HEAD_DIM), F32),
                        pltpu.VMEM((N_HEADS, ts, past // blk), F32),
                        pltpu.SemaphoreType.DMA((2,)),
                        pltpu.VMEM((tp, HEAD_DIM), BF16), pltpu.VMEM((nb, HEAD_DIM + ONES_ROWS, blk), BF16),
                        pltpu.VMEM((nb, HEAD_DIM), F32)],
    )
    scores, sel, attn = pl.pallas_call(
        functools.partial(_moba_scores_kernel, layer=layer, n_pages=n_pages, chunk=chunk, nb=nb, tps=tps,
                          every=every),
        grid_spec=grid_spec,
        out_shape=[jax.ShapeDtypeStruct((bs, N_HEADS, past // blk, ts, blk), F32),
                   jax.ShapeDtypeStruct((bs, N_HEADS, ts, 128), jnp.int32),
                   jax.ShapeDtypeStruct((bp, tp, ATTN_WIDTH), F32)],
        compiler_params=_params("arbitrary"),
        name="moba_and_scores",
    )(page_table.reshape(-1), qu_s, cache_k, qu_p, k_p, v_p)
    return attn, scores, sel


def _sample_attend_kernel(pt_ref, sel_ref, s_ref, q_ref, kn_ref, vn_ref, vc_ref, o_ref,
                          vbuf_ref, ssel_ref, sem_ref, *, layer, n_pages, q_start):
    b = pl.program_id(0)
    h = pl.program_id(1)
    nbt, nh = pl.num_programs(0), pl.num_programs(1)
    step = b * nh + h
    slot = step % 2
    t = q_ref.shape[1]
    ps = vc_ref.shape[2]
    blk = MOBA_BLOCK
    ppb = blk // ps
    nsel = t * MOBA_TOPK

    def sel_block(st, e):
        return sel_ref[st * nsel + e]

    def copies(st, sl):
        bb, hh = st // nh, st % nh

        def mk(e, pg):
            return _head_page_copy(vc_ref, pt_ref, vbuf_ref.at[sl], sem_ref.at[sl], layer, bb, hh,
                                   sel_block(st, e) * ppb + pg, e * blk + pg * ps, n_pages)
        return mk

    def start_all(st, sl):
        mk = copies(st, sl)
        for e in range(nsel):
            for pg in range(ppb):
                mk(e, pg).start(priority=(e * ppb + pg) % DMA_PRIORITIES)

    @pl.when(step == 0)
    def _():
        start_all(step, slot)

    @pl.when(step + 1 < nbt * nh)
    def _():
        start_all(step + 1, 1 - slot)

    scale = HEAD_DIM ** -0.5
    slope = jnp.exp2(jnp.broadcast_to(-8.0 * (h + 1).astype(F32) / N_HEADS, (1, 1)))
    lane = lax.broadcasted_iota(jnp.int32, (1, blk), 1)

    ssel_ref[...] = jnp.full(ssel_ref.shape, NEG_INF, F32)
    for e in range(nsel):
        ti = e // MOBA_TOPK
        n = sel_block(step, e)
        raw = s_ref[0, 0, n, ti:ti + 1, :]
        dist = (q_start + ti - n * blk - lane).astype(F32)
        ssel_ref[ti:ti + 1, e * blk:(e + 1) * blk] = raw * scale - slope * dist

    q = q_ref[0].astype(BF16)
    r = lax.broadcasted_iota(jnp.int32, (t, t), 0)
    c = lax.broadcasted_iota(jnp.int32, (t, t), 1)
    s_own = lax.dot_general(q, kn_ref[0].astype(BF16), _NT, preferred_element_type=F32) * scale
    s_own = jnp.where(c <= r, s_own - slope * (r - c).astype(F32), NEG_INF)

    s_sel = ssel_ref[...]
    m = jnp.maximum(jnp.max(s_sel, axis=1, keepdims=True), jnp.max(s_own, axis=1, keepdims=True))
    p_sel = jnp.exp(s_sel - m)
    p_own = jnp.exp(s_own - m)
    l = jnp.sum(p_sel, axis=1, keepdims=True) + jnp.sum(p_own, axis=1, keepdims=True)

    mk = copies(step, slot)
    for e in range(nsel):
        for pg in range(ppb):
            mk(e, pg).wait()

    acc = jnp.dot(p_sel.astype(BF16), vbuf_ref[slot].astype(BF16), preferred_element_type=F32)
    acc = acc + jnp.dot(p_own.astype(BF16), vn_ref[0].astype(BF16), preferred_element_type=F32)
    o_ref[0] = acc / l


def _sample_attend(scores, sel, qu, k_new, v_new, cache_v, page_table, layer):
    b, t, _ = k_new.shape
    n_pages = page_table.shape[1]
    ps = cache_v.shape[2]
    past = n_pages * ps
    assert MOBA_BLOCK % ps == 0 and past % MOBA_BLOCK == 0 and t <= MOBA_BLOCK
    nsel = t * MOBA_TOPK
    grid_spec = pltpu.PrefetchScalarGridSpec(
        num_scalar_prefetch=2,
        grid=(b, N_HEADS),
        in_specs=[pl.BlockSpec((1, 1, past // MOBA_BLOCK, t, MOBA_BLOCK), lambda bi, h, pt, sl: (bi, h, 0, 0, 0)),
                  pl.BlockSpec((None, 1, t, HEAD_DIM), lambda bi, h, pt, sl: (0, bi, 0, h)),
                  pl.BlockSpec((1, t, HEAD_DIM), lambda bi, h, pt, sl: (bi, 0, h)),
                  pl.BlockSpec((1, t, HEAD_DIM), lambda bi, h, pt, sl: (bi, 0, h)),
                  pl.BlockSpec(memory_space=pl.ANY)],
        out_specs=pl.BlockSpec((1, t, HEAD_DIM), lambda bi, h, pt, sl: (bi, 0, h)),
        scratch_shapes=[pltpu.VMEM((2, nsel * MOBA_BLOCK, HEAD_DIM), F32),
                        pltpu.VMEM((t, nsel * MOBA_BLOCK), F32),
                        pltpu.SemaphoreType.DMA((2,))],
    )
    return pl.pallas_call(
        functools.partial(_sample_attend_kernel, layer=layer, n_pages=n_pages, q_start=past),
        grid_spec=grid_spec,
        out_shape=jax.ShapeDtypeStruct((b, t, ATTN_WIDTH), F32),
        compiler_params=_params("arbitrary", "arbitrary"),
        name="sample_attend",
    )(page_table.reshape(-1), sel[..., :MOBA_TOPK].reshape(-1), scores, qu, k_new, v_new, cache_v)


def _conv_taps():
    pad = CONV_HALO - (CONV_WIDTH - 1)
    return [(ph, [(j, (pad + j) // 8) for j in range(CONV_WIDTH) if (pad + j) % 8 == ph]) for ph in range(8)]


def _conv_kernel(ua_ref, ug_ref, prev_ref, w_ref, b_ref, g_ref, bl_ref, y_ref, st_ref, buf_ref, par_ref, z_ref,
                 *, tt, rows, nrows, lanes):
    ti = pl.program_id(1)
    ch = y_ref.shape[2]
    hist = CONV_WIDTH - 1
    pad = CONV_HALO - hist
    i_bias, i_gain, i_beta = CONV_WIDTH, CONV_WIDTH + 1, CONV_WIDTH + 2

    @pl.when(ti == 0)
    def _():
        buf_ref[0:CONV_HALO, :] = jnp.zeros((CONV_HALO, ch), F32)
        buf_ref[pad:CONV_HALO, :] = prev_ref[0]
        for j in range(CONV_WIDTH):
            par_ref[j] = jnp.broadcast_to(w_ref[j:j + 1, :], (8, ch))
        par_ref[i_bias] = jnp.broadcast_to(b_ref[...], (8, ch))
        par_ref[i_gain] = jnp.broadcast_to(g_ref[...], (8, ch))
        par_ref[i_beta] = jnp.broadcast_to(bl_ref[...], (8, ch))

    @pl.when(ti > 0)
    def _():
        buf_ref[0:CONV_HALO, :] = buf_ref[tt:tt + CONV_HALO, :]

    def glu(ci, carry):
        r0 = pl.multiple_of(ci * rows, rows)
        buf_ref[pl.ds(CONV_HALO + r0, rows), :] = (ua_ref[0, pl.ds(r0, rows), :]
                                                   * _sigmoid(ug_ref[0, pl.ds(r0, rows), :]))
        return carry

    lax.fori_loop(0, tt // rows, glu, 0)

    def conv(ci, carry):
        r0 = pl.multiple_of(ci * rows, rows)
        for c0 in range(0, ch, lanes):
            cs = slice(c0, c0 + lanes)
            win = buf_ref.at[pl.ds(r0, rows + CONV_HALO), pl.ds(c0, lanes)]
            acc = jnp.broadcast_to(par_ref[i_bias, :, cs][None], (rows // 8, 8, lanes))
            for ph, taps in _conv_taps():
                a_lo, a_hi = taps[0][1], taps[-1][1]
                x = win[8 * a_lo + ph:8 * a_hi + ph + rows, :]
                for j, a in taps:
                    xs = x[8 * (a - a_lo):8 * (a - a_lo) + rows, :].reshape(rows // 8, 8, lanes)
                    acc = acc + par_ref[j, :, cs][None] * xs
            z_ref[pl.ds(r0, rows), cs] = acc.reshape(rows, lanes)
        return carry

    lax.fori_loop(0, tt // rows, conv, 0)

    def norm(ci, carry):
        r0 = pl.multiple_of(ci * nrows, nrows)
        z = z_ref[pl.ds(r0, nrows), :]
        xc = z - jnp.mean(z, axis=-1, keepdims=True)
        xn = (xc * lax.rsqrt(jnp.mean(xc * xc, axis=-1, keepdims=True) + EPS)).reshape(nrows // 8, 8, ch)
        y = (xn * par_ref[i_gain][None] + par_ref[i_beta][None]).reshape(nrows, ch)
        y_ref[0, pl.ds(r0, nrows), :] = y * _sigmoid(y)
        return carry

    lax.fori_loop(0, tt // nrows, norm, 0, unroll=min(8, tt // nrows))

    @pl.when(ti == pl.num_programs(1) - 1)
    def _():
        st_ref[0] = buf_ref[tt + pad:tt + CONV_HALO, :]


def _conv_specs(qu, prev, layer, w_dw, b_dw, g_ln, b_ln, tt):
    _, b, t, ch = qu.shape
    assert ch == w_dw.shape[1]
    hist = CONV_WIDTH - 1
    assert t % tt == 0 and tt % 8 == 0
    rows = 64 if tt % 64 == 0 else 8
    nrows = 16 if tt % 16 == 0 else 8
    lanes = 256 if ch % 256 == 0 else ch
    row = lambda a: a.reshape(1, ch)
    in_specs = [pl.BlockSpec((None, 1, tt, ch), lambda bi, ti: (1, bi, ti, 0)),
                pl.BlockSpec((None, 1, tt, ch), lambda bi, ti: (2, bi, ti, 0)),
                pl.BlockSpec((None, 1, hist, ch), lambda bi, ti: (layer, bi, 0, 0)),
                pl.BlockSpec((CONV_WIDTH, ch), lambda bi, ti: (0, 0)),
                pl.BlockSpec((1, ch), lambda bi, ti: (0, 0)),
                pl.BlockSpec((1, ch), lambda bi, ti: (0, 0)),
                pl.BlockSpec((1, ch), lambda bi, ti: (0, 0))]
    out_specs = [pl.BlockSpec((1, tt, ch), lambda bi, ti: (bi, ti, 0)),
                 pl.BlockSpec((1, hist, ch), lambda bi, ti: (bi, 0, 0))]
    out_shape = [jax.ShapeDtypeStruct((b, t, ch), F32), jax.ShapeDtypeStruct((b, hist, ch), F32)]
    scratch = [pltpu.VMEM((tt + CONV_HALO, ch), F32), pltpu.VMEM((CONV_WIDTH + 3, 8, ch), F32),
               pltpu.VMEM((tt, ch), F32)]
    args = (qu, qu, prev, w_dw, row(b_dw), row(g_ln), row(b_ln))
    return (b, t // tt), in_specs, out_specs, out_shape, scratch, args, dict(tt=tt, rows=rows, nrows=nrows,
                                                                            lanes=lanes)


def _conformer_conv(qu, prev, layer, w_dw, b_dw, g_ln, b_ln, tt):
    grid, in_specs, out_specs, out_shape, scratch, args, kw = _conv_specs(qu, prev, layer, w_dw, b_dw, g_ln,
                                                                         b_ln, tt)
    return pl.pallas_call(
        functools.partial(_conv_kernel, **kw),
        grid=grid,
        in_specs=in_specs,
        out_specs=out_specs,
        out_shape=out_shape,
        scratch_shapes=scratch,
        compiler_params=_params("arbitrary", "arbitrary"),
        name="conformer_conv",
    )(*args)


def _mix_out_kernel(attn_ref, conv_ref, h_ref, ga_ref, gc_ref, w_ref, gp_ref, o_ref, *rest, emit):
    if emit:
        (wt_ref,) = rest
        wt_ref[...] = w_ref[...].astype(BF16)
        w_ref = wt_ref
    wa = attn_ref.shape[1]
    a = _rms(attn_ref[...], ga_ref[...]).astype(BF16)
    cv = _rms(conv_ref[...], gc_ref[...]).astype(BF16)
    mixed = (jnp.dot(a, w_ref[0:wa, :], preferred_element_type=F32)
             + jnp.dot(cv, w_ref[wa:, :], preferred_element_type=F32))
    o_ref[...] = h_ref[...] + _rms(mixed, gp_ref[...])


def _mix_out(attn, conv, h, g_attn, g_conv, w_out, g_post, tm, emit):
    m, wa = attn.shape
    wc = conv.shape[1]
    d = h.shape[1]
    assert m % tm == 0 and (not emit or m == tm)
    row = lambda a: a.reshape(1, -1)
    out_specs = [pl.BlockSpec((tm, d), lambda i: (i, 0))]
    out_shape = [jax.ShapeDtypeStruct((m, d), F32)]
    if emit:
        out_specs.append(pl.BlockSpec((wa + wc, d), lambda i: (0, 0)))
        out_shape.append(jax.ShapeDtypeStruct((wa + wc, d), BF16))
    outs = pl.pallas_call(
        functools.partial(_mix_out_kernel, emit=emit),
        grid=(m // tm,),
        in_specs=[pl.BlockSpec((tm, wa), lambda i: (i, 0)),
                  pl.BlockSpec((tm, wc), lambda i: (i, 0)),
                  pl.BlockSpec((tm, d), lambda i: (i, 0)),
                  pl.BlockSpec((1, wa), lambda i: (0, 0)),
                  pl.BlockSpec((1, wc), lambda i: (0, 0)),
                  pl.BlockSpec((wa + wc, d), lambda i: (0, 0)),
                  pl.BlockSpec((1, d), lambda i: (0, 0))],
        out_specs=out_specs,
        out_shape=out_shape,
        compiler_params=_params("arbitrary", vmem_limit=VMEM_LIMIT_MAX if emit else VMEM_LIMIT),
        name="mix_out",
    )(attn, conv, h, row(g_attn), row(g_conv), w_out, row(g_post))
    return outs[0], (outs[1] if emit else None)


def _xattn_kernel(h_ref, mk_ref, mv_ref, gpre_ref, wq_ref, wo_ref, gpost_ref, o_ref, *rest, emit):
    if emit:
        wq_out, wo_out = rest
        wq_out[...] = wq_ref[...].astype(BF16)
        wo_out[...] = wo_ref[...].astype(BF16)
        wq_ref, wo_ref = wq_out, wo_out
    h = h_ref[...]
    xn = _rms(h, gpre_ref[...]).astype(BF16)
    xq = jnp.dot(xn, wq_ref[...], preferred_element_type=F32)
    scale = X_HEAD_DIM ** -0.5
    seqs = mk_ref.shape[0]
    rows = h.shape[0] // seqs
    per_seq = []
    for bi in range(seqs):
        outs = []
        for hh in range(X_HEADS):
            cols = slice(hh * X_HEAD_DIM, (hh + 1) * X_HEAD_DIM)
            qh = xq[bi * rows:(bi + 1) * rows, cols].astype(BF16)
            s = lax.dot_general(qh, mk_ref[bi, :, cols].astype(BF16), _NT, preferred_element_type=F32) * scale
            p = jnp.exp(s - jnp.max(s, axis=1, keepdims=True))
            l = jnp.sum(p, axis=1, keepdims=True)
            outs.append(jnp.dot(p.astype(BF16), mv_ref[bi, :, cols].astype(BF16),
                                preferred_element_type=F32) / l)
        per_seq.append(jnp.concatenate(outs, axis=1))
    o = jnp.concatenate(per_seq, axis=0).astype(BF16)
    y = jnp.dot(o, wo_ref[...], preferred_element_type=F32)
    o_ref[...] = h + _rms(y, gpost_ref[...])


def _xattn(h, mem_k, mem_v, layer, g_pre, w_xq, w_xo, g_post, rows_per_batch, tm, emit):
    m, d = h.shape
    seqs = max(1, tm // rows_per_batch)
    tiles = max(1, rows_per_batch // tm)
    assert m % tm == 0 and tm * tiles == rows_per_batch * seqs and (not emit or m == tm)
    mem_len, xw = mem_k.shape[2:]
    row = lambda a: a.reshape(1, -1)
    out_specs = [pl.BlockSpec((tm, d), lambda i: (i, 0))]
    out_shape = [jax.ShapeDtypeStruct((m, d), F32)]
    if emit:
        out_specs += [pl.BlockSpec((d, xw), lambda i: (0, 0)), pl.BlockSpec((xw, d), lambda i: (0, 0))]
        out_shape += [jax.ShapeDtypeStruct((d, xw), BF16), jax.ShapeDtypeStruct((xw, d), BF16)]
    outs = pl.pallas_call(
        functools.partial(_xattn_kernel, emit=emit),
        grid=(m // tm,),
        in_specs=[pl.BlockSpec((tm, d), lambda i: (i, 0)),
                  pl.BlockSpec((None, seqs, mem_len, xw), lambda i: (layer, i // tiles, 0, 0)),
                  pl.BlockSpec((None, seqs, mem_len, xw), lambda i: (layer, i // tiles, 0, 0)),
                  pl.BlockSpec((1, d), lambda i: (0, 0)),
                  pl.BlockSpec((d, xw), lambda i: (0, 0)),
                  pl.BlockSpec((xw, d), lambda i: (0, 0)),
                  pl.BlockSpec((1, d), lambda i: (0, 0))],
        out_specs=out_specs,
        out_shape=out_shape,
        compiler_params=_params("arbitrary"),
        name="xattn",
    )(h, mem_k, mem_v, row(g_pre), w_xq, w_xo, row(g_post))
    return outs[0], (tuple(outs[1:]) if emit else None)


def _ffn_body(f, nf, h_ref, gpre_ref, wg_ref, wu_ref, wo_ref, gpost_ref, o_ref, rest, emit):
    if emit:
        wg_out, wu_out, wo_out, xn_ref, acc_ref = rest
        wg_out[...] = wg_ref[...].astype(BF16)
        wu_out[...] = wu_ref[...].astype(BF16)
        wo_out[...] = wo_ref[...].astype(BF16)
        wg_ref, wu_ref, wo_ref = wg_out, wu_out, wo_out
    else:
        xn_ref, acc_ref = rest

    def tile(first, last):
        if first:
            xn = _rms(h_ref[...], gpre_ref[...]).astype(BF16)
            xn_ref[...] = xn
        else:
            xn = xn_ref[...]
        g = jnp.dot(xn, wg_ref[...], preferred_element_type=F32)
        u = jnp.dot(xn, wu_ref[...], preferred_element_type=F32)
        a = (g * jax.nn.sigmoid(g) * u).astype(BF16)
        y = jnp.dot(a, wo_ref[...], preferred_element_type=F32)
        acc = y if first else acc_ref[...] + y
        if last:
            o_ref[...] = h_ref[...] + _rms(acc, gpost_ref[...])
        else:
            acc_ref[...] = acc

    assert nf >= 2
    pl.when(f == 0)(functools.partial(tile, True, False))
    pl.when((f > 0) & (f < nf - 1))(functools.partial(tile, False, False))
    pl.when(f == nf - 1)(functools.partial(tile, False, True))


def _ffn_kernel(h_ref, gpre_ref, wg_ref, wu_ref, wo_ref, gpost_ref, o_ref, *rest, emit, nf):
    _ffn_body(pl.program_id(1), nf, h_ref, gpre_ref, wg_ref, wu_ref, wo_ref, gpost_ref, o_ref, rest, emit)


def _ffn_specs(h, w, tm, tf, emit, tile):
    m, d = h.shape
    if emit:
        w_in, w_out = w
        hidden = w_out.shape[0]
        nf = hidden // tf
        assert m == tm and hidden % tf == 0 and w_in.shape[1] == 2 * hidden
        w_args = (w_in, w_in, w_out)
        w_specs = [pl.BlockSpec((d, tf), lambda *ids: (0, tile(*ids)[1])),
                   pl.BlockSpec((d, tf), lambda *ids: (0, tile(*ids)[1] + nf)),
                   pl.BlockSpec((tf, d), lambda *ids: (tile(*ids)[1], 0))]
    else:
        w_args = w
        nf, hidden = w[0].shape[0], w[2].shape[0]
        assert m % tm == 0 and w[0].shape == w[1].shape == (nf, d, tf) and hidden == nf * tf
        w_specs = [pl.BlockSpec((None, d, tf), lambda *ids: (tile(*ids)[1], 0, 0)),
                   pl.BlockSpec((None, d, tf), lambda *ids: (tile(*ids)[1], 0, 0)),
                   pl.BlockSpec((tf, d), lambda *ids: (tile(*ids)[1], 0))]
    in_specs = [pl.BlockSpec((tm, d), lambda *ids: (tile(*ids)[0], 0)),
                pl.BlockSpec((1, d), lambda *ids: (0, 0)),
                *w_specs,
                pl.BlockSpec((1, d), lambda *ids: (0, 0))]
    out_specs = [pl.BlockSpec((tm, d), lambda *ids: (tile(*ids)[0], 0))]
    out_shape = [jax.ShapeDtypeStruct((m, d), F32)]
    if emit:
        out_specs += [pl.BlockSpec((None, d, tf), lambda *ids: (tile(*ids)[1], 0, 0)),
                      pl.BlockSpec((None, d, tf), lambda *ids: (tile(*ids)[1], 0, 0)),
                      pl.BlockSpec((tf, d), lambda *ids: (tile(*ids)[1], 0))]
        out_shape += [jax.ShapeDtypeStruct((nf, d, tf), BF16), jax.ShapeDtypeStruct((nf, d, tf), BF16),
                      jax.ShapeDtypeStruct((hidden, d), BF16)]
    scratch = [pltpu.VMEM((tm, d), BF16), pltpu.VMEM((tm, d), F32)]
    return nf, in_specs, out_specs, out_shape, scratch, w_args


def _ffn(h, g_pre, w, g_post, tm, tf, emit):
    m = h.shape[0]
    row = lambda a: a.reshape(1, -1)
    nf, in_specs, out_specs, out_shape, scratch, w_args = _ffn_specs(h, w, tm, tf, emit, lambda i, f: (i, f))
    outs = pl.pallas_call(
        functools.partial(_ffn_kernel, emit=emit, nf=nf),
        grid=(m // tm, nf),
        in_specs=in_specs,
        out_specs=out_specs,
        out_shape=out_shape,
        scratch_shapes=scratch,
        compiler_params=_params("arbitrary", "arbitrary"),
        name="ffn",
    )(h, row(g_pre), *w_args, row(g_post))
    return outs[0], tuple(outs[1:])


def _conv_ffn_kernel(*refs, conv_kw, nf):
    ci, fi, co, fo, cs, fs = refs[0:7], refs[7:13], refs[13:15], refs[15:19], refs[19:22], refs[22:24]
    step = pl.program_id(0) * pl.num_programs(1) + pl.program_id(1)

    @pl.when(step < nf)
    def _():
        _ffn_body(step, nf, *fi, fo[0], tuple(fo[1:]) + tuple(fs), True)

    _conv_kernel(*ci, *co, *cs, **conv_kw)


def _conv_and_emit_ffn(conv_args, h, g_pre, w, g_post, tf):
    grid, c_in, c_out, c_shape, c_scratch, c_args, conv_kw = _conv_specs(*conv_args)
    nt = grid[1]
    row = lambda a: a.reshape(1, -1)
    tile = lambda bi, ti: (0, jnp.minimum(bi * nt + ti, nf - 1))
    nf = w[1].shape[0] // tf
    if grid[0] * nt < nf:
        return None
    nf, f_in, f_out, f_shape, f_scratch, w_args = _ffn_specs(h, w, h.shape[0], tf, True, tile)
    outs = pl.pallas_call(
        functools.partial(_conv_ffn_kernel, conv_kw=conv_kw, nf=nf),
        grid=grid,
        in_specs=c_in + f_in,
        out_specs=c_out + f_out,
        out_shape=c_shape + f_shape,
        scratch_shapes=c_scratch + f_scratch,
        compiler_params=_params("arbitrary", "arbitrary", vmem_limit=VMEM_LIMIT_MAX),
        name="conv_and_ffn",
    )(*c_args, h, row(g_pre), *w_args, row(g_post))
    return outs[0], outs[1], outs[2], tuple(outs[3:])


def _row_tile(m):
    return 512 if m % 512 == 0 else m


def _project_in(x, wts, w_in, emit):
    b, t, d = x.shape
    m = b * t
    assert wts["w_dw"].shape[1] == ATTN_WIDTH
    k_new, v_new, qu, *w_in_b = _in_proj(x.reshape(m, d), wts["g_pre_mix"].reshape(1, d), w_in, emit,
                                         tm=1024 if m % 1024 == 0 else _row_tile(m))
    k_new, v_new = k_new.reshape(b, t, ATTN_WIDTH), v_new.reshape(b, t, ATTN_WIDTH)
    return k_new, v_new, qu.reshape(qu.shape[0], b, t, ATTN_WIDTH), (w_in_b[0] if emit else None)


def _mix_and_xattn(x, attn, conv, mem, mem_layer, wts, w, emit, x_tile):
    b, t, d = x.shape
    m = b * t
    h, w_out_b = _mix_out(attn.reshape(m, ATTN_WIDTH), conv.reshape(m, -1), x.reshape(m, d), wts["g_attn_grp"],
                          wts["g_conv_grp"], w[0], wts["g_post_mix"], tm=_row_tile(m), emit=emit)
    h, w_x_b = _xattn(h, mem[0], mem[1], mem_layer, wts["g_pre_x"], w[1], w[2], wts["g_post_x"],
                      rows_per_batch=t, tm=x_tile, emit=emit)
    return h, ((w_out_b,) + w_x_b if emit else None)


def kernel(x_prompt, x_sample, cache_k, cache_v, state_conv, cache_mem_k, cache_mem_v, page_table,
           mem_prompt, g_mem, w_mem_k, w_mem_v, g_pre_mix, w_in, w_dw, b_dw, g_ln_conv, b_ln_conv,
           g_attn_grp, g_conv_grp, w_out, g_post_mix, g_pre_x, w_xq, w_xo, g_post_x,
           g_pre_ffn, w_ffn_in, w_ffn_out, g_post_ffn):
    depth = w_in.shape[0]
    bp, tp, d = x_prompt.shape
    bs, ts, _ = x_sample.shape
    mem_len = mem_prompt.shape[1]
    xw = X_HEADS * X_HEAD_DIM
    ch = w_dw.shape[2]
    hp, hs = x_prompt, x_sample
    outs = [[] for _ in range(8)]
    for l in range(depth):
        wts = dict(g_pre_mix=g_pre_mix[l], w_dw=w_dw[l], b_dw=b_dw[l],
                   g_ln_conv=g_ln_conv[l], b_ln_conv=b_ln_conv[l], g_attn_grp=g_attn_grp[l],
                   g_conv_grp=g_conv_grp[l], g_post_mix=g_post_mix[l], g_pre_x=g_pre_x[l],
                   g_post_x=g_post_x[l], g_pre_ffn=g_pre_ffn[l], g_post_ffn=g_post_ffn[l])

        ks, vs, qu_s, w_in_b = _project_in(hs, wts, w_in[l], True)
        kp, vp, qu_p, _ = _project_in(hp, wts, w_in_b, False)
        attn_p, scores, sel = _moba_and_scores(qu_p, kp, vp, qu_s, cache_k, page_table, l)
        attn_s = _sample_attend(scores, sel, qu_s, ks, vs, cache_v, page_table, l)

        conv_w = (wts["w_dw"], wts["b_dw"], wts["g_ln_conv"], wts["b_ln_conv"])
        conv_s, cs = _conformer_conv(qu_s, state_conv, l, *conv_w, ts)
        sample_mem = (cache_mem_k.reshape(depth, bs, mem_len, xw), cache_mem_v.reshape(depth, bs, mem_len, xw))
        hs_mid, w_mid_b = _mix_and_xattn(hs, attn_s, conv_s, sample_mem, l, wts, (w_out[l], w_xq[l], w_xo[l]),
                                         True, x_tile=bs * ts)

        conv_args = (qu_p, jnp.zeros((1, bp, CONV_WIDTH - 1, ch), F32), 0, *conv_w, _row_tile(tp))
        ffn_w = (w_ffn_in[l], w_ffn_out[l])
        fused = _conv_and_emit_ffn(conv_args, hs_mid, wts["g_pre_ffn"], ffn_w, wts["g_post_ffn"], tf=512)
        if fused is None:
            conv_p, cp = _conformer_conv(*conv_args)
            hs_out, w_ffn_b = _ffn(hs_mid, wts["g_pre_ffn"], ffn_w, wts["g_post_ffn"], tm=bs * ts, tf=512,
                                   emit=True)
        else:
            conv_p, cp, hs_out, w_ffn_b = fused
        hs = hs_out.reshape(bs, ts, d)

        w_mem = jnp.concatenate([w_mem_k[l], w_mem_v[l]], axis=1).astype(BF16)
        mk_p, mv_p = _norm_matmul(mem_prompt.reshape(bp * mem_len, d), g_mem[l].reshape(1, d), w_mem,
                                  (xw, xw), tn=xw, tm=_row_tile(bp * mem_len))
        mk_p, mv_p = mk_p.reshape(bp, mem_len, xw), mv_p.reshape(bp, mem_len, xw)
        hp_mid, _ = _mix_and_xattn(hp, attn_p, conv_p, (mk_p[None], mv_p[None]), 0, wts, w_mid_b, False,
                                   x_tile=_row_tile(tp))
        hp_out, _ = _ffn(hp_mid, wts["g_pre_ffn"], w_ffn_b, wts["g_post_ffn"], tm=_row_tile(bp * tp), tf=512,
                         emit=False)
        hp = hp_out.reshape(bp, tp, d)
        mem_shape = (bp, mem_len, X_HEADS, X_HEAD_DIM)
        heads = lambda a: a.reshape(a.shape[:2] + (N_HEADS, HEAD_DIM))
        for lst, a in zip(outs, (heads(kp), heads(vp), cp, mk_p.reshape(mem_shape), mv_p.reshape(mem_shape),
                                 heads(ks), heads(vs), cs)):
            lst.append(a)
    return (hp, hs) + tuple(jnp.stack(lst, 0) for lst in outs)
```

```python
import functools

import jax
import jax.numpy as jnp
from jax import lax
from jax.experimental import pallas as pl
from jax.experimental.pallas import tpu as pltpu

EPS = 1e-6
N_HEADS = 8
HEAD_DIM = 128
ATTN_WIDTH = N_HEADS * HEAD_DIM
CONV_WIDTH = 31
MOBA_BLOCK = 256
MOBA_TOPK = 3
X_HEADS = 4
X_HEAD_DIM = 128

CONV_HALO = 32
DMA_PRIORITIES = 2
ONES_ROWS = 16
VMEM_LIMIT = 56 * 1024 * 1024
VMEM_LIMIT_MAX = 60 * 1024 * 1024

F32 = jnp.float32
BF16 = jnp.bfloat16
NEG_INF = float("-inf")
LOG2_E = 1.4426950408889634
_NT = (((1,), (1,)), ((), ()))


def _params(*sem, vmem_limit=VMEM_LIMIT):
    return pltpu.CompilerParams(dimension_semantics=sem, vmem_limit_bytes=vmem_limit)


def _rms(x, g):
    return x * lax.rsqrt(jnp.mean(x * x, axis=-1, keepdims=True) + EPS) * g


def _sigmoid(x):
    return 0.5 * jnp.tanh(0.5 * x) + 0.5


def _top_blocks(gate, n_valid, axis):
    nb = gate.shape[axis]
    pos = lax.broadcasted_iota(jnp.int32, gate.shape, axis)
    g = jnp.where(pos < n_valid, gate, NEG_INF)
    idxs, oks = [], []
    for _ in range(MOBA_TOPK):
        m = jnp.max(g, axis=axis, keepdims=True)
        idx = jnp.min(jnp.where(g == m, pos, nb), axis=axis, keepdims=True)
        ok = m > NEG_INF
        idxs.append(idx)
        oks.append(ok)
        g = jnp.where((pos == idx) & ok, NEG_INF, g)
    return idxs, oks


def _norm_matmul_kernel(x_ref, g_ref, w_ref, *rest, tile_ranges):
    out_refs, xn_ref = rest[:-1], rest[-1]
    j = pl.program_id(1)

    @pl.when(j == 0)
    def _():
        xn_ref[...] = _rms(x_ref[...], g_ref[...]).astype(BF16)

    for o_ref, (lo, hi) in zip(out_refs, tile_ranges):
        @pl.when((j >= lo) & (j < hi))
        def _(o_ref=o_ref):
            o_ref[...] = jnp.dot(xn_ref[...], w_ref[...], preferred_element_type=F32)


def _norm_matmul(x, g, w, widths, tn, tm):
    m, d = x.shape
    n = w.shape[1]
    assert sum(widths) == n and all(wd % tn == 0 for wd in widths) and m % tm == 0
    tile_ranges, lo = [], 0
    for wd in widths:
        tile_ranges.append((lo, lo + wd // tn))
        lo += wd // tn

    def out_map(i, j, lo, cnt):
        return (i, jnp.clip(j - lo, 0, cnt - 1))

    out_specs = [pl.BlockSpec((tm, tn), functools.partial(out_map, lo=lo, cnt=hi - lo))
                 for lo, hi in tile_ranges]
    return pl.pallas_call(
        functools.partial(_norm_matmul_kernel, tile_ranges=tuple(tile_ranges)),
        grid=(m // tm, n // tn),
        in_specs=[pl.BlockSpec((tm, d), lambda i, j: (i, 0)),
                  pl.BlockSpec((1, d), lambda i, j: (0, 0)),
                  pl.BlockSpec((d, tn), lambda i, j: (0, j))],
        out_specs=out_specs,
        out_shape=[jax.ShapeDtypeStruct((m, wd), F32) for wd in widths],
        scratch_shapes=[pltpu.VMEM((tm, d), BF16)],
        compiler_params=_params("arbitrary", "arbitrary"),
        name="norm_matmul",
    )(x, g, w)


def _in_proj_src_tile(j):
    return jnp.where(j < 3, (j + 1) % 3, j)


def _in_proj_kernel(x_ref, g_ref, w_ref, *rest, emit):
    if emit:
        k_ref, v_ref, qu_ref, wt_ref, xn_ref = rest
    else:
        k_ref, v_ref, qu_ref, xn_ref = rest
    j = pl.program_id(1)

    if emit:
        wt_ref[...] = w_ref[...].astype(BF16)
        w_ref = wt_ref

    for o_ref, cond, first in ((k_ref, j == 0, True), (v_ref, j == 1, False), (qu_ref, j >= 2, False)):
        @pl.when(cond)
        def _(o_ref=o_ref, first=first):
            if first:
                xn = _rms(x_ref[...], g_ref[...]).astype(BF16)
                xn_ref[...] = xn
            else:
                xn = xn_ref[...]
            o_ref[...] = jnp.dot(xn, w_ref[...], preferred_element_type=F32)


def _in_proj(x, g, w, emit, tm):
    m, d = x.shape
    tn = ATTN_WIDTH
    nt = (w.shape[1] // tn) if emit else w.shape[0]
    assert m % tm == 0 and nt > 3 and (not emit or (m == tm and w.shape[1] % tn == 0))
    if emit:
        w_spec = pl.BlockSpec((d, tn), lambda i, j: (0, _in_proj_src_tile(j)))
    else:
        w_spec = pl.BlockSpec((None, d, tn), lambda i, j: (j, 0, 0))
    out_specs = [pl.BlockSpec((tm, tn), lambda i, j: (i, 0)),
                 pl.BlockSpec((tm, tn), lambda i, j: (i, 0)),
                 pl.BlockSpec((None, tm, tn), lambda i, j: (jnp.maximum(j - 2, 0), i, 0))]
    out_shape = [jax.ShapeDtypeStruct((m, tn), F32), jax.ShapeDtypeStruct((m, tn), F32),
                 jax.ShapeDtypeStruct((nt - 2, m, tn), F32)]
    if emit:
        out_specs.append(pl.BlockSpec((None, d, tn), lambda i, j: (j, 0, 0)))
        out_shape.append(jax.ShapeDtypeStruct((nt, d, tn), BF16))
    return pl.pallas_call(
        functools.partial(_in_proj_kernel, emit=emit),
        grid=(m // tm, nt),
        in_specs=[pl.BlockSpec((tm, d), lambda i, j: (i, 0)),
                  pl.BlockSpec((1, d), lambda i, j: (0, 0)),
                  w_spec],
        out_specs=out_specs,
        out_shape=out_shape,
        scratch_shapes=[pltpu.VMEM((tm, d), BF16)],
        compiler_params=_params("arbitrary", "arbitrary"),
        name="in_proj",
    )(x, g, w)


def _moba_prompt_body(h, g, q_ref, k_ref, v_ref, o_ref, kb_ref, vt_ref, km_ref, *, nb, tps):
    blk = MOBA_BLOCK

    def stage_keys():
        for n in range(nb):
            kf = k_ref[0, n * blk:(n + 1) * blk, :]
            kb_ref[n * blk:(n + 1) * blk, :] = kf.astype(BF16)
            vt_ref[n, 0:HEAD_DIM, :] = v_ref[0, n * blk:(n + 1) * blk, :].T.astype(BF16)
            vt_ref[n, HEAD_DIM:, :] = jnp.ones((ONES_ROWS, blk), BF16)
            km_ref[n:n + 1, :] = jnp.mean(kf, axis=0, keepdims=True)

    scale = HEAD_DIM ** -0.5 * LOG2_E
    slope = jnp.exp2(jnp.broadcast_to(-8.0 * (h + 1).astype(F32) / N_HEADS, (1, blk))) * LOG2_E

    def attend(i, qi):
        q = q_ref[0, qi * blk:(qi + 1) * blk, :]
        qb = q.astype(BF16)
        kr = lax.broadcasted_iota(jnp.int32, (blk, blk), 0)
        qc = lax.broadcasted_iota(jnp.int32, (blk, blk), 1)
        base = slope * (qc - kr).astype(F32)
        s_own = lax.dot_general(kb_ref[i * blk:(i + 1) * blk, :], qb, _NT, preferred_element_type=F32)
        ts = [jnp.where(kr <= qc, s_own * scale - base, NEG_INF)]
        if i:
            gate = lax.dot_general(km_ref[0:i, :], q, _NT, precision=lax.Precision.HIGHEST,
                                   preferred_element_type=F32)
            idxs, oks = _top_blocks(gate, i, axis=0)
            row = lax.broadcasted_iota(jnp.int32, (i, blk), 0)
            sel = jnp.zeros((i, blk), F32)
            for idx, ok in zip(idxs, oks):
                sel = jnp.where((row == idx) & ok, 1.0, sel)
            for n in range(i):
                s = lax.dot_general(kb_ref[n * blk:(n + 1) * blk, :], qb, _NT, preferred_element_type=F32)
                neg = jnp.where(sel[n:n + 1, :] > 0.0, -slope * float((i - n) * blk), NEG_INF)
                ts.append(s * scale - base + neg)
        m = jnp.max(functools.reduce(jnp.maximum, ts), axis=0, keepdims=True)
        acc = None
        for n, t in enumerate(ts):
            p = jnp.exp2(t - m)
            vt = vt_ref[i] if n == 0 else vt_ref[n - 1]
            pv = jnp.dot(vt, p.astype(BF16), preferred_element_type=F32)
            acc = pv if acc is None else acc + pv
        l = acc[HEAD_DIM:HEAD_DIM + 1, :]
        o_ref[0, qi * blk:(qi + 1) * blk, :] = (acc[0:HEAD_DIM, :] / l).T

    for v in range(nb // tps):
        @pl.when(g == v)
        def _(v=v):
            if v == 0:
                stage_keys()
            for qi in range(tps):
                attend(v * tps + qi, qi)


def _head_page_copy(cache_ref, pt_ref, buf_ref, sem, layer, b, h, page_slot, dst_row, n_pages):
    page = pt_ref[b * n_pages + page_slot]
    ps = cache_ref.shape[2]
    return pltpu.make_async_copy(cache_ref.at[layer, page, :, h, :], buf_ref.at[pl.ds(dst_row, ps), :], sem)


def _start_k_pages(st, first, count, pt_ref, kc_ref, kbuf_ref, sem_ref, *, layer, n_pages):
    bb, hh = st // N_HEADS, st % N_HEADS
    sl = st % 2
    ps = kc_ref.shape[2]

    def issue(pair, carry):
        for k in range(DMA_PRIORITIES):
            p = first + pair * DMA_PRIORITIES + k
            _head_page_copy(kc_ref, pt_ref, kbuf_ref.at[sl], sem_ref.at[sl], layer, bb, hh,
                            p, pl.multiple_of(p * ps, ps), n_pages).start(priority=k)
        return carry

    lax.fori_loop(0, count // DMA_PRIORITIES, issue, 0, unroll=4)


def _sample_scores_body(step, n_steps, pt_ref, q_ref, kc_ref, s_ref, sel_ref, kbuf_ref, km_ref, gate_ref, sem_ref,
                        *, layer, n_pages, chunk):
    nh = N_HEADS
    b, h = step // nh, step % nh
    slot = step % 2
    ps = kc_ref.shape[2]
    past = n_pages * ps
    blk = MOBA_BLOCK
    t = q_ref.shape[1]
    start_pages = functools.partial(_start_k_pages, pt_ref=pt_ref, kc_ref=kc_ref, kbuf_ref=kbuf_ref,
                                    sem_ref=sem_ref, layer=layer, n_pages=n_pages)

    @pl.when(step == 0)
    def _():
        start_pages(step, 0, n_pages)

    @pl.when(step + 1 < n_steps)
    def _():
        start_pages(step + 1, 0, n_pages)

    pltpu.make_async_copy(kbuf_ref.at[1 - slot], kbuf_ref.at[slot], sem_ref.at[slot]).wait()

    q = q_ref[0]
    qb = q.astype(BF16)
    bpc = chunk // blk

    def body(ci, carry):
        start = pl.multiple_of(ci * chunk, chunk)
        kf = kbuf_ref[slot, pl.ds(start, chunk), :]
        sc = lax.dot_general(qb, kf.astype(BF16), _NT, preferred_element_type=F32)
        for jj in range(bpc):
            s_ref[0, 0, ci * bpc + jj] = sc[:, jj * blk:(jj + 1) * blk]
        km_ref[pl.ds(pl.multiple_of(ci * bpc, bpc), bpc), :] = jnp.mean(
            kf.reshape(bpc, blk, HEAD_DIM), axis=1)
        return carry

    lax.fori_loop(0, past // chunk, body, 0, unroll=True)

    gate_ref[h] = lax.dot_general(q, km_ref[...], _NT, precision=lax.Precision.HIGHEST,
                                  preferred_element_type=F32)

    @pl.when(h == nh - 1)
    def _():
        gate = gate_ref[...].reshape(nh * t, past // blk)
        idxs, _ = _top_blocks(gate, past // blk, axis=1)
        lane = lax.broadcasted_iota(jnp.int32, (nh * t, 128), 1)
        out = jnp.zeros((nh * t, 128), jnp.int32)
        for k, idx in enumerate(idxs):
            out = jnp.where(lane == k, idx, out)
        sel_ref[0] = out.reshape(nh, t, 128)


def _moba_scores_kernel(pt_ref, qs_ref, kc_ref, qp_ref, kp_ref, vp_ref, s_ref, sel_ref, o_ref,
                        kbuf_ref, km_ref, gate_ref, sem_ref, kb_ref, vt_ref, kmp_ref,
                        *, layer, n_pages, chunk, nb, tps, every):
    step = pl.program_id(0)
    n_steps = pl.num_programs(0)
    _sample_scores_body(step, n_steps, pt_ref, qs_ref, kc_ref, s_ref, sel_ref, kbuf_ref, km_ref,
                        gate_ref, sem_ref, layer=layer, n_pages=n_pages, chunk=chunk)

    @pl.when(step % every == 0)
    def _():
        item = step // every
        groups = nb // tps
        _moba_prompt_body((item // groups) % N_HEADS, item % groups, qp_ref, kp_ref, vp_ref, o_ref,
                          kb_ref, vt_ref, kmp_ref, nb=nb, tps=tps)


def _moba_and_scores(qu_p, k_p, v_p, qu_s, cache_k, page_table, layer, tps=4):
    _, bs, ts, _ = qu_s.shape
    bp, tp, _ = k_p.shape
    blk = MOBA_BLOCK
    n_pages = page_table.shape[1]
    ps = cache_k.shape[2]
    past = n_pages * ps
    chunk = 8 * blk
    assert past % chunk == 0 and past // blk >= MOBA_TOPK and n_pages % DMA_PRIORITIES == 0
    tps = min(tps, tp // blk)
    assert tp % (blk * tps) == 0 and n_pages % (tps * DMA_PRIORITIES) == 0
    nb = tp // blk
    groups = nb // tps
    n_steps, items = bs * N_HEADS, bp * N_HEADS * groups
    assert n_steps % items == 0
    every = n_steps // items

    def prompt_idx(step):
        item = step // every
        return item // (N_HEADS * groups), (item // groups) % N_HEADS, item % groups

    def qp_map(step, pt):
        bi, h, g = prompt_idx(step)
        return (0, bi, g, h)

    def kv_map(step, pt):
        bi, h, g = prompt_idx(step)
        return (bi, 0, h)

    def o_map(step, pt):
        bi, h, g = prompt_idx(step)
        return (bi, g, h)

    grid_spec = pltpu.PrefetchScalarGridSpec(
        num_scalar_prefetch=1,
        grid=(n_steps,),
        in_specs=[pl.BlockSpec((None, 1, ts, HEAD_DIM), lambda s, pt: (0, s // N_HEADS, 0, s % N_HEADS)),
                  pl.BlockSpec(memory_space=pl.ANY),
                  pl.BlockSpec((None, 1, blk * tps, HEAD_DIM), qp_map),
                  pl.BlockSpec((1, tp, HEAD_DIM), kv_map),
                  pl.BlockSpec((1, tp, HEAD_DIM), kv_map)],
        out_specs=[pl.BlockSpec((1, 1, past // blk, ts, blk), lambda s, pt: (s // N_HEADS, s % N_HEADS, 0, 0, 0)),
                   pl.BlockSpec((1, N_HEADS, ts, 128), lambda s, pt: (s // N_HEADS, 0, 0, 0)),
                   pl.BlockSpec((1, blk * tps, HEAD_DIM), o_map)],
        scratch_shapes=[pltpu.VMEM((2, past, HEAD_DIM), F32),
                        pltpu.VMEM((past // blk, HEAD_DIM), F32),
                        pltpu.VMEM((N_HEADS, ts, past // blk), F32),
                        pltpu.SemaphoreType.DMA((2,)),
                        pltpu.VMEM((tp, HEAD_DIM), BF16), pltpu.VMEM((nb, HEAD_DIM + ONES_ROWS, blk), BF16),
                        pltpu.VMEM((nb, HEAD_DIM), F32)],
    )
    scores, sel, attn = pl.pallas_call(
        functools.partial(_moba_scores_kernel, layer=layer, n_pages=n_pages, chunk=chunk, nb=nb, tps=tps,
                          every=every),
        grid_spec=grid_spec,
        out_shape=[jax.ShapeDtypeStruct((bs, N_HEADS, past // blk, ts, blk), F32),
                   jax.ShapeDtypeStruct((bs, N_HEADS, ts, 128), jnp.int32),
                   jax.ShapeDtypeStruct((bp, tp, ATTN_WIDTH), F32)],
        compiler_params=_params("arbitrary"),
        name="moba_and_scores",
    )(page_table.reshape(-1), qu_s, cache_k, qu_p, k_p, v_p)
    return attn, scores, sel


def _sample_attend_kernel(pt_ref, sel_ref, s_ref, q_ref, kn_ref, vn_ref, vc_ref, o_ref,
                          vbuf_ref, ssel_ref, sem_ref, *, layer, n_pages, q_start):
    b = pl.program_id(0)
    h = pl.program_id(1)
    nbt, nh = pl.num_programs(0), pl.num_programs(1)
    step = b * nh + h
    slot = step % 2
    t = q_ref.shape[1]
    ps = vc_ref.shape[2]
    blk = MOBA_BLOCK
    ppb = blk // ps
    nsel = t * MOBA_TOPK

    def sel_block(st, e):
        return sel_ref[st * nsel + e]

    def copies(st, sl):
        bb, hh = st // nh, st % nh

        def mk(e, pg):
            return _head_page_copy(vc_ref, pt_ref, vbuf_ref.at[sl], sem_ref.at[sl], layer, bb, hh,
                                   sel_block(st, e) * ppb + pg, e * blk + pg * ps, n_pages)
        return mk

    def start_all(st, sl):
        mk = copies(st, sl)
        for e in range(nsel):
            for pg in range(ppb):
                mk(e, pg).start(priority=(e * ppb + pg) % DMA_PRIORITIES)

    @pl.when(step == 0)
    def _():
        start_all(step, slot)

    @pl.when(step + 1 < nbt * nh)
    def _():
        start_all(step + 1, 1 - slot)

    scale = HEAD_DIM ** -0.5
    slope = jnp.exp2(jnp.broadcast_to(-8.0 * (h + 1).astype(F32) / N_HEADS, (1, 1)))
    lane = lax.broadcasted_iota(jnp.int32, (1, blk), 1)

    ssel_ref[...] = jnp.full(ssel_ref.shape, NEG_INF, F32)
    for e in range(nsel):
        ti = e // MOBA_TOPK
        n = sel_block(step, e)
        raw = s_ref[0, 0, n, ti:ti + 1, :]
        dist = (q_start + ti - n * blk - lane).astype(F32)
        ssel_ref[ti:ti + 1, e * blk:(e + 1) * blk] = raw * scale - slope * dist

    q = q_ref[0].astype(BF16)
    r = lax.broadcasted_iota(jnp.int32, (t, t), 0)
    c = lax.broadcasted_iota(jnp.int32, (t, t), 1)
    s_own = lax.dot_general(q, kn_ref[0].astype(BF16), _NT, preferred_element_type=F32) * scale
    s_own = jnp.where(c <= r, s_own - slope * (r - c).astype(F32), NEG_INF)

    s_sel = ssel_ref[...]
    m = jnp.maximum(jnp.max(s_sel, axis=1, keepdims=True), jnp.max(s_own, axis=1, keepdims=True))
    p_sel = jnp.exp(s_sel - m)
    p_own = jnp.exp(s_own - m)
    l = jnp.sum(p_sel, axis=1, keepdims=True) + jnp.sum(p_own, axis=1, keepdims=True)

    mk = copies(step, slot)
    for e in range(nsel):
        for pg in range(ppb):
            mk(e, pg).wait()

    acc = jnp.dot(p_sel.astype(BF16), vbuf_ref[slot].astype(BF16), preferred_element_type=F32)
    acc = acc + jnp.dot(p_own.astype(BF16), vn_ref[0].astype(BF16), preferred_element_type=F32)
    o_ref[0] = acc / l


def _sample_attend(scores, sel, qu, k_new, v_new, cache_v, page_table, layer):
    b, t, _ = k_new.shape
    n_pages = page_table.shape[1]
    ps = cache_v.shape[2]
    past = n_pages * ps
    assert MOBA_BLOCK % ps == 0 and past % MOBA_BLOCK == 0 and t <= MOBA_BLOCK
    nsel = t * MOBA_TOPK
    grid_spec = pltpu.PrefetchScalarGridSpec(
        num_scalar_prefetch=2,
        grid=(b, N_HEADS),
        in_specs=[pl.BlockSpec((1, 1, past // MOBA_BLOCK, t, MOBA_BLOCK), lambda bi, h, pt, sl: (bi, h, 0, 0, 0)),
                  pl.BlockSpec((None, 1, t, HEAD_DIM), lambda bi, h, pt, sl: (0, bi, 0, h)),
                  pl.BlockSpec((1, t, HEAD_DIM), lambda bi, h, pt, sl: (bi, 0, h)),
                  pl.BlockSpec((1, t, HEAD_DIM), lambda bi, h, pt, sl: (bi, 0, h)),
                  pl.BlockSpec(memory_space=pl.ANY)],
        out_specs=pl.BlockSpec((1, t, HEAD_DIM), lambda bi, h, pt, sl: (bi, 0, h)),
        scratch_shapes=[pltpu.VMEM((2, nsel * MOBA_BLOCK, HEAD_DIM), F32),
                        pltpu.VMEM((t, nsel * MOBA_BLOCK), F32),
                        pltpu.SemaphoreType.DMA((2,))],
    )
    return pl.pallas_call(
        functools.partial(_sample_attend_kernel, layer=layer, n_pages=n_pages, q_start=past),
        grid_spec=grid_spec,
        out_shape=jax.ShapeDtypeStruct((b, t, ATTN_WIDTH), F32),
        compiler_params=_params("arbitrary", "arbitrary"),
        name="sample_attend",
    )(page_table.reshape(-1), sel[..., :MOBA_TOPK].reshape(-1), scores, qu, k_new, v_new, cache_v)


def _conv_taps():
    pad = CONV_HALO - (CONV_WIDTH - 1)
    return [(ph, [(j, (pad + j) // 8) for j in range(CONV_WIDTH) if (pad + j) % 8 == ph]) for ph in range(8)]


def _conv_kernel(ua_ref, ug_ref, prev_ref, w_ref, b_ref, g_ref, bl_ref, y_ref, st_ref, buf_ref, par_ref, z_ref,
                 *, tt, rows, nrows, lanes):
    ti = pl.program_id(1)
    ch = y_ref.shape[2]
    hist = CONV_WIDTH - 1
    pad = CONV_HALO - hist
    i_bias, i_gain, i_beta = CONV_WIDTH, CONV_WIDTH + 1, CONV_WIDTH + 2

    @pl.when(ti == 0)
    def _():
        buf_ref[0:CONV_HALO, :] = jnp.zeros((CONV_HALO, ch), F32)
        buf_ref[pad:CONV_HALO, :] = prev_ref[0]
        for j in range(CONV_WIDTH):
            par_ref[j] = jnp.broadcast_to(w_ref[j:j + 1, :], (8, ch))
        par_ref[i_bias] = jnp.broadcast_to(b_ref[...], (8, ch))
        par_ref[i_gain] = jnp.broadcast_to(g_ref[...], (8, ch))
        par_ref[i_beta] = jnp.broadcast_to(bl_ref[...], (8, ch))

    @pl.when(ti > 0)
    def _():
        buf_ref[0:CONV_HALO, :] = buf_ref[tt:tt + CONV_HALO, :]

    def glu(ci, carry):
        r0 = pl.multiple_of(ci * rows, rows)
        buf_ref[pl.ds(CONV_HALO + r0, rows), :] = (ua_ref[0, pl.ds(r0, rows), :]
                                                   * _sigmoid(ug_ref[0, pl.ds(r0, rows), :]))
        return carry

    lax.fori_loop(0, tt // rows, glu, 0)

    def conv(ci, carry):
        r0 = pl.multiple_of(ci * rows, rows)
        for c0 in range(0, ch, lanes):
            cs = slice(c0, c0 + lanes)
            win = buf_ref.at[pl.ds(r0, rows + CONV_HALO), pl.ds(c0, lanes)]
            acc = jnp.broadcast_to(par_ref[i_bias, :, cs][None], (rows // 8, 8, lanes))
            for ph, taps in _conv_taps():
                a_lo, a_hi = taps[0][1], taps[-1][1]
                x = win[8 * a_lo + ph:8 * a_hi + ph + rows, :]
                for j, a in taps:
                    xs = x[8 * (a - a_lo):8 * (a - a_lo) + rows, :].reshape(rows // 8, 8, lanes)
                    acc = acc + par_ref[j, :, cs][None] * xs
            z_ref[pl.ds(r0, rows), cs] = acc.reshape(rows, lanes)
        return carry

    lax.fori_loop(0, tt // rows, conv, 0)

    def norm(ci, carry):
        r0 = pl.multiple_of(ci * nrows, nrows)
        z = z_ref[pl.ds(r0, nrows), :]
        xc = z - jnp.mean(z, axis=-1, keepdims=True)
        xn = (xc * lax.rsqrt(jnp.mean(xc * xc, axis=-1, keepdims=True) + EPS)).reshape(nrows // 8, 8, ch)
        y = (xn * par_ref[i_gain][None] + par_ref[i_beta][None]).reshape(nrows, ch)
        y_ref[0, pl.ds(r0, nrows), :] = y * _sigmoid(y)
        return carry

    lax.fori_loop(0, tt // nrows, norm, 0, unroll=min(8, tt // nrows))

    @pl.when(ti == pl.num_programs(1) - 1)
    def _():
        st_ref[0] = buf_ref[tt + pad:tt + CONV_HALO, :]


def _conv_specs(qu, prev, layer, w_dw, b_dw, g_ln, b_ln, tt):
    _, b, t, ch = qu.shape
    assert ch == w_dw.shape[1]
    hist = CONV_WIDTH - 1
    assert t % tt == 0 and tt % 8 == 0
    rows = 64 if tt % 64 == 0 else 8
    nrows = 16 if tt % 16 == 0 else 8
    lanes = 256 if ch % 256 == 0 else ch
    row = lambda a: a.reshape(1, ch)
    in_specs = [pl.BlockSpec((None, 1, tt, ch), lambda bi, ti: (1, bi, ti, 0)),
                pl.BlockSpec((None, 1, tt, ch), lambda bi, ti: (2, bi, ti, 0)),
                pl.BlockSpec((None, 1, hist, ch), lambda bi, ti: (layer, bi, 0, 0)),
                pl.BlockSpec((CONV_WIDTH, ch), lambda bi, ti: (0, 0)),
                pl.BlockSpec((1, ch), lambda bi, ti: (0, 0)),
                pl.BlockSpec((1, ch), lambda bi, ti: (0, 0)),
                pl.BlockSpec((1, ch), lambda bi, ti: (0, 0))]
    out_specs = [pl.BlockSpec((1, tt, ch), lambda bi, ti: (bi, ti, 0)),
                 pl.BlockSpec((1, hist, ch), lambda bi, ti: (bi, 0, 0))]
    out_shape = [jax.ShapeDtypeStruct((b, t, ch), F32), jax.ShapeDtypeStruct((b, hist, ch), F32)]
    scratch = [pltpu.VMEM((tt + CONV_HALO, ch), F32), pltpu.VMEM((CONV_WIDTH + 3, 8, ch), F32),
               pltpu.VMEM((tt, ch), F32)]
    args = (qu, qu, prev, w_dw, row(b_dw), row(g_ln), row(b_ln))
    return (b, t // tt), in_specs, out_specs, out_shape, scratch, args, dict(tt=tt, rows=rows, nrows=nrows,
                                                                            lanes=lanes)


def _conformer_conv(qu, prev, layer, w_dw, b_dw, g_ln, b_ln, tt):
    grid, in_specs, out_specs, out_shape, scratch, args, kw = _conv_specs(qu, prev, layer, w_dw, b_dw, g_ln,
                                                                         b_ln, tt)
    return pl.pallas_call(
        functools.partial(_conv_kernel, **kw),
        grid=grid,
        in_specs=in_specs,
        out_specs=out_specs,
        out_shape=out_shape,
        scratch_shapes=scratch,
        compiler_params=_params("arbitrary", "arbitrary"),
        name="conformer_conv",
    )(*args)


def _mix_out_kernel(attn_ref, conv_ref, h_ref, ga_ref, gc_ref, w_ref, gp_ref, o_ref, *rest, emit):
    if emit:
        (wt_ref,) = rest
        wt_ref[...] = w_ref[...].astype(BF16)
        w_ref = wt_ref
    wa = attn_ref.shape[1]
    a = _rms(attn_ref[...], ga_ref[...]).astype(BF16)
    cv = _rms(conv_ref[...], gc_ref[...]).astype(BF16)
    mixed = (jnp.dot(a, w_ref[0:wa, :], preferred_element_type=F32)
             + jnp.dot(cv, w_ref[wa:, :], preferred_element_type=F32))
    o_ref[...] = h_ref[...] + _rms(mixed, gp_ref[...])


def _mix_out(attn, conv, h, g_attn, g_conv, w_out, g_post, tm, emit):
    m, wa = attn.shape
    wc = conv.shape[1]
    d = h.shape[1]
    assert m % tm == 0 and (not emit or m == tm)
    row = lambda a: a.reshape(1, -1)
    out_specs = [pl.BlockSpec((tm, d), lambda i: (i, 0))]
    out_shape = [jax.ShapeDtypeStruct((m, d), F32)]
    if emit:
        out_specs.append(pl.BlockSpec((wa + wc, d), lambda i: (0, 0)))
        out_shape.append(jax.ShapeDtypeStruct((wa + wc, d), BF16))
    outs = pl.pallas_call(
        functools.partial(_mix_out_kernel, emit=emit),
        grid=(m // tm,),
        in_specs=[pl.BlockSpec((tm, wa), lambda i: (i, 0)),
                  pl.BlockSpec((tm, wc), lambda i: (i, 0)),
                  pl.BlockSpec((tm, d), lambda i: (i, 0)),
                  pl.BlockSpec((1, wa), lambda i: (0, 0)),
                  pl.BlockSpec((1, wc), lambda i: (0, 0)),
                  pl.BlockSpec((wa + wc, d), lambda i: (0, 0)),
                  pl.BlockSpec((1, d), lambda i: (0, 0))],
        out_specs=out_specs,
        out_shape=out_shape,
        compiler_params=_params("arbitrary", vmem_limit=VMEM_LIMIT_MAX if emit else VMEM_LIMIT),
        name="mix_out",
    )(attn, conv, h, row(g_attn), row(g_conv), w_out, row(g_post))
    return outs[0], (outs[1] if emit else None)


def _xattn_kernel(h_ref, mk_ref, mv_ref, gpre_ref, wq_ref, wo_ref, gpost_ref, o_ref, *rest, emit):
    if emit:
        wq_out, wo_out = rest
        wq_out[...] = wq_ref[...].astype(BF16)
        wo_out[...] = wo_ref[...].astype(BF16)
        wq_ref, wo_ref = wq_out, wo_out
    h = h_ref[...]
    xn = _rms(h, gpre_ref[...]).astype(BF16)
    xq = jnp.dot(xn, wq_ref[...], preferred_element_type=F32)
    scale = X_HEAD_DIM ** -0.5
    seqs = mk_ref.shape[0]
    rows = h.shape[0] // seqs
    per_seq = []
    for bi in range(seqs):
        outs = []
        for hh in range(X_HEADS):
            cols = slice(hh * X_HEAD_DIM, (hh + 1) * X_HEAD_DIM)
            qh = xq[bi * rows:(bi + 1) * rows, cols].astype(BF16)
            s = lax.dot_general(qh, mk_ref[bi, :, cols].astype(BF16), _NT, preferred_element_type=F32) * scale
            p = jnp.exp(s - jnp.max(s, axis=1, keepdims=True))
            l = jnp.sum(p, axis=1, keepdims=True)
            outs.append(jnp.dot(p.astype(BF16), mv_ref[bi, :, cols].astype(BF16),
                                preferred_element_type=F32) / l)
        per_seq.append(jnp.concatenate(outs, axis=1))
    o = jnp.concatenate(per_seq, axis=0).astype(BF16)
    y = jnp.dot(o, wo_ref[...], preferred_element_type=F32)
    o_ref[...] = h + _rms(y, gpost_ref[...])


def _xattn(h, mem_k, mem_v, layer, g_pre, w_xq, w_xo, g_post, rows_per_batch, tm, emit):
    m, d = h.shape
    seqs = max(1, tm // rows_per_batch)
    tiles = max(1, rows_per_batch // tm)
    assert m % tm == 0 and tm * tiles == rows_per_batch * seqs and (not emit or m == tm)
    mem_len, xw = mem_k.shape[2:]
    row = lambda a: a.reshape(1, -1)
    out_specs = [pl.BlockSpec((tm, d), lambda i: (i, 0))]
    out_shape = [jax.ShapeDtypeStruct((m, d), F32)]
    if emit:
        out_specs += [pl.BlockSpec((d, xw), lambda i: (0, 0)), pl.BlockSpec((xw, d), lambda i: (0, 0))]
        out_shape += [jax.ShapeDtypeStruct((d, xw), BF16), jax.ShapeDtypeStruct((xw, d), BF16)]
    outs = pl.pallas_call(
        functools.partial(_xattn_kernel, emit=emit),
        grid=(m // tm,),
        in_specs=[pl.BlockSpec((tm, d), lambda i: (i, 0)),
                  pl.BlockSpec((None, seqs, mem_len, xw), lambda i: (layer, i // tiles, 0, 0)),
                  pl.BlockSpec((None, seqs, mem_len, xw), lambda i: (layer, i // tiles, 0, 0)),
                  pl.BlockSpec((1, d), lambda i: (0, 0)),
                  pl.BlockSpec((d, xw), lambda i: (0, 0)),
                  pl.BlockSpec((xw, d), lambda i: (0, 0)),
                  pl.BlockSpec((1, d), lambda i: (0, 0))],
        out_specs=out_specs,
        out_shape=out_shape,
        compiler_params=_params("arbitrary"),
        name="xattn",
    )(h, mem_k, mem_v, row(g_pre), w_xq, w_xo, row(g_post))
    return outs[0], (tuple(outs[1:]) if emit else None)


def _ffn_body(f, nf, h_ref, gpre_ref, wg_ref, wu_ref, wo_ref, gpost_ref, o_ref, rest, emit):
    if emit:
        wg_out, wu_out, wo_out, xn_ref, acc_ref = rest
        wg_out[...] = wg_ref[...].astype(BF16)
        wu_out[...] = wu_ref[...].astype(BF16)
        wo_out[...] = wo_ref[...].astype(BF16)
        wg_ref, wu_ref, wo_ref = wg_out, wu_out, wo_out
    else:
        xn_ref, acc_ref = rest

    def tile(first, last):
        if first:
            xn = _rms(h_ref[...], gpre_ref[...]).astype(BF16)
            xn_ref[...] = xn
        else:
            xn = xn_ref[...]
        g = jnp.dot(xn, wg_ref[...], preferred_element_type=F32)
        u = jnp.dot(xn, wu_ref[...], preferred_element_type=F32)
        a = (g * jax.nn.sigmoid(g) * u).astype(BF16)
        y = jnp.dot(a, wo_ref[...], preferred_element_type=F32)
        acc = y if first else acc_ref[...] + y
        if last:
            o_ref[...] = h_ref[...] + _rms(acc, gpost_ref[...])
        else:
            acc_ref[...] = acc

    assert nf >= 2
    pl.when(f == 0)(functools.partial(tile, True, False))
    pl.when((f > 0) & (f < nf - 1))(functools.partial(tile, False, False))
    pl.when(f == nf - 1)(functools.partial(tile, False, True))


def _ffn_kernel(h_ref, gpre_ref, wg_ref, wu_ref, wo_ref, gpost_ref, o_ref, *rest, emit, nf):
    _ffn_body(pl.program_id(1), nf, h_ref, gpre_ref, wg_ref, wu_ref, wo_ref, gpost_ref, o_ref, rest, emit)


def _ffn_specs(h, w, tm, tf, emit, tile):
    m, d = h.shape
    if emit:
        w_in, w_out = w
        hidden = w_out.shape[0]
        nf = hidden // tf
        assert m == tm and hidden % tf == 0 and w_in.shape[1] == 2 * hidden
        w_args = (w_in, w_in, w_out)
        w_specs = [pl.BlockSpec((d, tf), lambda *ids: (0, tile(*ids)[1])),
                   pl.BlockSpec((d, tf), lambda *ids: (0, tile(*ids)[1] + nf)),
                   pl.BlockSpec((tf, d), lambda *ids: (tile(*ids)[1], 0))]
    else:
        w_args = w
        nf, hidden = w[0].shape[0], w[2].shape[0]
        assert m % tm == 0 and w[0].shape == w[1].shape == (nf, d, tf) and hidden == nf * tf
        w_specs = [pl.BlockSpec((None, d, tf), lambda *ids: (tile(*ids)[1], 0, 0)),
                   pl.BlockSpec((None, d, tf), lambda *ids: (tile(*ids)[1], 0, 0)),
                   pl.BlockSpec((tf, d), lambda *ids: (tile(*ids)[1], 0))]
    in_specs = [pl.BlockSpec((tm, d), lambda *ids: (tile(*ids)[0], 0)),
                pl.BlockSpec((1, d), lambda *ids: (0, 0)),
                *w_specs,
                pl.BlockSpec((1, d), lambda *ids: (0, 0))]
    out_specs = [pl.BlockSpec((tm, d), lambda *ids: (tile(*ids)[0], 0))]
    out_shape = [jax.ShapeDtypeStruct((m, d), F32)]
    if emit:
        out_specs += [pl.BlockSpec((None, d, tf), lambda *ids: (tile(*ids)[1], 0, 0)),
                      pl.BlockSpec((None, d, tf), lambda *ids: (tile(*ids)[1], 0, 0)),
                      pl.BlockSpec((tf, d), lambda *ids: (tile(*ids)[1], 0))]
        out_shape += [jax.ShapeDtypeStruct((nf, d, tf), BF16), jax.ShapeDtypeStruct((nf, d, tf), BF16),
                      jax.ShapeDtypeStruct((hidden, d), BF16)]
    scratch = [pltpu.VMEM((tm, d), BF16), pltpu.VMEM((tm, d), F32)]
    return nf, in_specs, out_specs, out_shape, scratch, w_args


def _ffn(h, g_pre, w, g_post, tm, tf, emit):
    m = h.shape[0]
    row = lambda a: a.reshape(1, -1)
    nf, in_specs, out_specs, out_shape, scratch, w_args = _ffn_specs(h, w, tm, tf, emit, lambda i, f: (i, f))
    outs = pl.pallas_call(
        functools.partial(_ffn_kernel, emit=emit, nf=nf),
        grid=(m // tm, nf),
        in_specs=in_specs,
        out_specs=out_specs,
        out_shape=out_shape,
        scratch_shapes=scratch,
        compiler_params=_params("arbitrary", "arbitrary"),
        name="ffn",
    )(h, row(g_pre), *w_args, row(g_post))
    return outs[0], tuple(outs[1:])


def _conv_ffn_kernel(*refs, conv_kw, nf):
    ci, fi, co, fo, cs, fs = refs[0:7], refs[7:13], refs[13:15], refs[15:19], refs[19:22], refs[22:24]
    step = pl.program_id(0) * pl.num_programs(1) + pl.program_id(1)

    @pl.when(step < nf)
    def _():
        _ffn_body(step, nf, *fi, fo[0], tuple(fo[1:]) + tuple(fs), True)

    _conv_kernel(*ci, *co, *cs, **conv_kw)


def _conv_and_emit_ffn(conv_args, h, g_pre, w, g_post, tf):
    grid, c_in, c_out, c_shape, c_scratch, c_args, conv_kw = _conv_specs(*conv_args)
    nt = grid[1]
    row = lambda a: a.reshape(1, -1)
    tile = lambda bi, ti: (0, jnp.minimum(bi * nt + ti, nf - 1))
    nf = w[1].shape[0] // tf
    if grid[0] * nt < nf:
        return None
    nf, f_in, f_out, f_shape, f_scratch, w_args = _ffn_specs(h, w, h.shape[0], tf, True, tile)
    outs = pl.pallas_call(
        functools.partial(_conv_ffn_kernel, conv_kw=conv_kw, nf=nf),
        grid=grid,
        in_specs=c_in + f_in,
        out_specs=c_out + f_out,
        out_shape=c_shape + f_shape,
        scratch_shapes=c_scratch + f_scratch,
        compiler_params=_params("arbitrary", "arbitrary", vmem_limit=VMEM_LIMIT_MAX),
        name="conv_and_ffn",
    )(*c_args, h, row(g_pre), *w_args, row(g_post))
    return outs[0], outs[1], outs[2], tuple(outs[3:])


def _row_tile(m):
    return 512 if m % 512 == 0 else m


def _project_in(x, wts, w_in, emit):
    b, t, d = x.shape
    m = b * t
    assert wts["w_dw"].shape[1] == ATTN_WIDTH
    k_new, v_new, qu, *w_in_b = _in_proj(x.reshape(m, d), wts["g_pre_mix"].reshape(1, d), w_in, emit,
                                         tm=1024 if m % 1024 == 0 else _row_tile(m))
    k_new, v_new = k_new.reshape(b, t, ATTN_WIDTH), v_new.reshape(b, t, ATTN_WIDTH)
    return k_new, v_new, qu.reshape(qu.shape[0], b, t, ATTN_WIDTH), (w_in_b[0] if emit else None)


def _mix_and_xattn(x, attn, conv, mem, mem_layer, wts, w, emit, x_tile):
    b, t, d = x.shape
    m = b * t
    h, w_out_b = _mix_out(attn.reshape(m, ATTN_WIDTH), conv.reshape(m, -1), x.reshape(m, d), wts["g_attn_grp"],
                          wts["g_conv_grp"], w[0], wts["g_post_mix"], tm=_row_tile(m), emit=emit)
    h, w_x_b = _xattn(h, mem[0], mem[1], mem_layer, wts["g_pre_x"], w[1], w[2], wts["g_post_x"],
                      rows_per_batch=t, tm=x_tile, emit=emit)
    return h, ((w_out_b,) + w_x_b if emit else None)


def kernel(x_prompt, x_sample, cache_k, cache_v, state_conv, cache_mem_k, cache_mem_v, page_table,
           mem_prompt, g_mem, w_mem_k, w_mem_v, g_pre_mix, w_in, w_dw, b_dw, g_ln_conv, b_ln_conv,
           g_attn_grp, g_conv_grp, w_out, g_post_mix, g_pre_x, w_xq, w_xo, g_post_x,
           g_pre_ffn, w_ffn_in, w_ffn_out, g_post_ffn):
    depth = w_in.shape[0]
    bp, tp, d = x_prompt.shape
    bs, ts, _ = x_sample.shape
    mem_len = mem_prompt.shape[1]
    xw = X_HEADS * X_HEAD_DIM
    ch = w_dw.shape[2]
    hp, hs = x_prompt, x_sample
    outs = [[] for _ in range(8)]
    for l in range(depth):
        wts = dict(g_pre_mix=g_pre_mix[l], w_dw=w_dw[l], b_dw=b_dw[l],
                   g_ln_conv=g_ln_conv[l], b_ln_conv=b_ln_conv[l], g_attn_grp=g_attn_grp[l],
                   g_conv_grp=g_conv_grp[l], g_post_mix=g_post_mix[l], g_pre_x=g_pre_x[l],
                   g_post_x=g_post_x[l], g_pre_ffn=g_pre_ffn[l], g_post_ffn=g_post_ffn[l])

        ks, vs, qu_s, w_in_b = _project_in(hs, wts, w_in[l], True)
        kp, vp, qu_p, _ = _project_in(hp, wts, w_in_b, False)
        attn_p, scores, sel = _moba_and_scores(qu_p, kp, vp, qu_s, cache_k, page_table, l)
        attn_s = _sample_attend(scores, sel, qu_s, ks, vs, cache_v, page_table, l)

        conv_w = (wts["w_dw"], wts["b_dw"], wts["g_ln_conv"], wts["b_ln_conv"])
        conv_s, cs = _conformer_conv(qu_s, state_conv, l, *conv_w, ts)
        sample_mem = (cache_mem_k.reshape(depth, bs, mem_len, xw), cache_mem_v.reshape(depth, bs, mem_len, xw))
        hs_mid, w_mid_b = _mix_and_xattn(hs, attn_s, conv_s, sample_mem, l, wts, (w_out[l], w_xq[l], w_xo[l]),
                                         True, x_tile=bs * ts)

        conv_args = (qu_p, jnp.zeros((1, bp, CONV_WIDTH - 1, ch), F32), 0, *conv_w, _row_tile(tp))
        ffn_w = (w_ffn_in[l], w_ffn_out[l])
        fused = _conv_and_emit_ffn(conv_args, hs_mid, wts["g_pre_ffn"], ffn_w, wts["g_post_ffn"], tf=512)
        if fused is None:
            conv_p, cp = _conformer_conv(*conv_args)
            hs_out, w_ffn_b = _ffn(hs_mid, wts["g_pre_ffn"], ffn_w, wts["g_post_ffn"], tm=bs * ts, tf=512,
                                   emit=True)
        else:
            conv_p, cp, hs_out, w_ffn_b = fused
        hs = hs_out.reshape(bs, ts, d)

        w_mem = jnp.concatenate([w_mem_k[l], w_mem_v[l]], axis=1).astype(BF16)
        mk_p, mv_p = _norm_matmul(mem_prompt.reshape(bp * mem_len, d), g_mem[l].reshape(1, d), w_mem,
                                  (xw, xw), tn=xw, tm=_row_tile(bp * mem_len))
        mk_p, mv_p = mk_p.reshape(bp, mem_len, xw), mv_p.reshape(bp, mem_len, xw)
        hp_mid, _ = _mix_and_xattn(hp, attn_p, conv_p, (mk_p[None], mv_p[None]), 0, wts, w_mid_b, False,
                                   x_tile=_row_tile(tp))
        hp_out, _ = _ffn(hp_mid, wts["g_pre_ffn"], w_ffn_b, wts["g_post_ffn"], tm=_row_tile(bp * tp), tf=512,
                         emit=False)
        hp = hp_out.reshape(bp, tp, d)
        mem_shape = (bp, mem_len, X_HEADS, X_HEAD_DIM)
        heads = lambda a: a.reshape(a.shape[:2] + (N_HEADS, HEAD_DIM))
        for lst, a in zip(outs, (heads(kp), heads(vp), cp, mk_p.reshape(mem_shape), mv_p.reshape(mem_shape),
                                 heads(ks), heads(vs), cs)):
            lst.append(a)
    return (hp, hs) + tuple(jnp.stack(lst, 0) for lst in outs)
```

```python
import functools

import jax
import jax.numpy as jnp
from jax import lax
from jax.experimental import pallas as pl
from jax.experimental.pallas import tpu as pltpu

EPS = 1e-6
N_HEADS = 8
HEAD_DIM = 128
ATTN_WIDTH = N_HEADS * HEAD_DIM
CONV_WIDTH = 31
MOBA_BLOCK = 256
MOBA_TOPK = 3
X_HEADS = 4
X_HEAD_DIM = 128

CONV_HALO = 32
DMA_PRIORITIES = 2
ONES_ROWS = 16
VMEM_LIMIT = 56 * 1024 * 1024
VMEM_LIMIT_MAX = 60 * 1024 * 1024

F32 = jnp.float32
BF16 = jnp.bfloat16
NEG_INF = float("-inf")
LOG2_E = 1.4426950408889634
_NT = (((1,), (1,)), ((), ()))


def _params(*sem, vmem_limit=VMEM_LIMIT):
    return pltpu.CompilerParams(dimension_semantics=sem, vmem_limit_bytes=vmem_limit)


def _rms(x, g):
    return x * lax.rsqrt(jnp.mean(x * x, axis=-1, keepdims=True) + EPS) * g


def _sigmoid(x):
    return 0.5 * jnp.tanh(0.5 * x) + 0.5


def _top_blocks(gate, n_valid, axis):
    nb = gate.shape[axis]
    pos = lax.broadcasted_iota(jnp.int32, gate.shape, axis)
    g = jnp.where(pos < n_valid, gate, NEG_INF)
    idxs, oks = [], []
    for _ in range(MOBA_TOPK):
        m = jnp.max(g, axis=axis, keepdims=True)
        idx = jnp.min(jnp.where(g == m, pos, nb), axis=axis, keepdims=True)
        ok = m > NEG_INF
        idxs.append(idx)
        oks.append(ok)
        g = jnp.where((pos == idx) & ok, NEG_INF, g)
    return idxs, oks


def _norm_matmul_kernel(x_ref, g_ref, w_ref, *rest, tile_ranges):
    out_refs, xn_ref = rest[:-1], rest[-1]
    j = pl.program_id(1)

    @pl.when(j == 0)
    def _():
        xn_ref[...] = _rms(x_ref[...], g_ref[...]).astype(BF16)

    for o_ref, (lo, hi) in zip(out_refs, tile_ranges):
        @pl.when((j >= lo) & (j < hi))
        def _(o_ref=o_ref):
            o_ref[...] = jnp.dot(xn_ref[...], w_ref[...], preferred_element_type=F32)


def _norm_matmul(x, g, w, widths, tn, tm):
    m, d = x.shape
    n = w.shape[1]
    assert sum(widths) == n and all(wd % tn == 0 for wd in widths) and m % tm == 0
    tile_ranges, lo = [], 0
    for wd in widths:
        tile_ranges.append((lo, lo + wd // tn))
        lo += wd // tn

    def out_map(i, j, lo, cnt):
        return (i, jnp.clip(j - lo, 0, cnt - 1))

    out_specs = [pl.BlockSpec((tm, tn), functools.partial(out_map, lo=lo, cnt=hi - lo))
                 for lo, hi in tile_ranges]
    return pl.pallas_call(
        functools.partial(_norm_matmul_kernel, tile_ranges=tuple(tile_ranges)),
        grid=(m // tm, n // tn),
        in_specs=[pl.BlockSpec((tm, d), lambda i, j: (i, 0)),
                  pl.BlockSpec((1, d), lambda i, j: (0, 0)),
                  pl.BlockSpec((d, tn), lambda i, j: (0, j))],
        out_specs=out_specs,
        out_shape=[jax.ShapeDtypeStruct((m, wd), F32) for wd in widths],
        scratch_shapes=[pltpu.VMEM((tm, d), BF16)],
        compiler_params=_params("arbitrary", "arbitrary"),
        name="norm_matmul",
    )(x, g, w)


def _in_proj_src_tile(j):
    return jnp.where(j < 3, (j + 1) % 3, j)


def _in_proj_kernel(x_ref, g_ref, w_ref, *rest, emit):
    if emit:
        k_ref, v_ref, qu_ref, wt_ref, xn_ref = rest
    else:
        k_ref, v_ref, qu_ref, xn_ref = rest
    j = pl.program_id(1)

    if emit:
        wt_ref[...] = w_ref[...].astype(BF16)
        w_ref = wt_ref

    for o_ref, cond, first in ((k_ref, j == 0, True), (v_ref, j == 1, False), (qu_ref, j >= 2, False)):
        @pl.when(cond)
        def _(o_ref=o_ref, first=first):
            if first:
                xn = _rms(x_ref[...], g_ref[...]).astype(BF16)
                xn_ref[...] = xn
            else:
                xn = xn_ref[...]
            o_ref[...] = jnp.dot(xn, w_ref[...], preferred_element_type=F32)


def _in_proj(x, g, w, emit, tm):
    m, d = x.shape
    tn = ATTN_WIDTH
    nt = (w.shape[1] // tn) if emit else w.shape[0]
    assert m % tm == 0 and nt > 3 and (not emit or (m == tm and w.shape[1] % tn == 0))
    if emit:
        w_spec = pl.BlockSpec((d, tn), lambda i, j: (0, _in_proj_src_tile(j)))
    else:
        w_spec = pl.BlockSpec((None, d, tn), lambda i, j: (j, 0, 0))
    out_specs = [pl.BlockSpec((tm, tn), lambda i, j: (i, 0)),
                 pl.BlockSpec((tm, tn), lambda i, j: (i, 0)),
                 pl.BlockSpec((None, tm, tn), lambda i, j: (jnp.maximum(j - 2, 0), i, 0))]
    out_shape = [jax.ShapeDtypeStruct((m, tn), F32), jax.ShapeDtypeStruct((m, tn), F32),
                 jax.ShapeDtypeStruct((nt - 2, m, tn), F32)]
    if emit:
        out_specs.append(pl.BlockSpec((None, d, tn), lambda i, j: (j, 0, 0)))
        out_shape.append(jax.ShapeDtypeStruct((nt, d, tn), BF16))
    return pl.pallas_call(
        functools.partial(_in_proj_kernel, emit=emit),
        grid=(m // tm, nt),
        in_specs=[pl.BlockSpec((tm, d), lambda i, j: (i, 0)),
                  pl.BlockSpec((1, d), lambda i, j: (0, 0)),
                  w_spec],
        out_specs=out_specs,
        out_shape=out_shape,
        scratch_shapes=[pltpu.VMEM((tm, d), BF16)],
        compiler_params=_params("arbitrary", "arbitrary"),
        name="in_proj",
    )(x, g, w)


def _moba_prompt_body(h, g, q_ref, k_ref, v_ref, o_ref, kb_ref, vt_ref, km_ref, *, nb, tps):
    blk = MOBA_BLOCK

    def stage_keys():
        for n in range(nb):
            kf = k_ref[0, n * blk:(n + 1) * blk, :]
            kb_ref[n * blk:(n + 1) * blk, :] = kf.astype(BF16)
            vt_ref[n, 0:HEAD_DIM, :] = v_ref[0, n * blk:(n + 1) * blk, :].T.astype(BF16)
            vt_ref[n, HEAD_DIM:, :] = jnp.ones((ONES_ROWS, blk), BF16)
            km_ref[n:n + 1, :] = jnp.mean(kf, axis=0, keepdims=True)

    scale = HEAD_DIM ** -0.5 * LOG2_E
    slope = jnp.exp2(jnp.broadcast_to(-8.0 * (h + 1).astype(F32) / N_HEADS, (1, blk))) * LOG2_E

    def attend(i, qi):
        q = q_ref[0, qi * blk:(qi + 1) * blk, :]
        qb = q.astype(BF16)
        kr = lax.broadcasted_iota(jnp.int32, (blk, blk), 0)
        qc = lax.broadcasted_iota(jnp.int32, (blk, blk), 1)
        base = slope * (qc - kr).astype(F32)
        s_own = lax.dot_general(kb_ref[i * blk:(i + 1) * blk, :], qb, _NT, preferred_element_type=F32)
        ts = [jnp.where(kr <= qc, s_own * scale - base, NEG_INF)]
        if i:
            gate = lax.dot_general(km_ref[0:i, :], q, _NT, precision=lax.Precision.HIGHEST,
                                   preferred_element_type=F32)
            idxs, oks = _top_blocks(gate, i, axis=0)
            row = lax.broadcasted_iota(jnp.int32, (i, blk), 0)
            sel = jnp.zeros((i, blk), F32)
            for idx, ok in zip(idxs, oks):
                sel = jnp.where((row == idx) & ok, 1.0, sel)
            for n in range(i):
                s = lax.dot_general(kb_ref[n * blk:(n + 1) * blk, :], qb, _NT, preferred_element_type=F32)
                neg = jnp.where(sel[n:n + 1, :] > 0.0, -slope * float((i - n) * blk), NEG_INF)
                ts.append(s * scale - base + neg)
        m = jnp.max(functools.reduce(jnp.maximum, ts), axis=0, keepdims=True)
        acc = None
        for n, t in enumerate(ts):
            p = jnp.exp2(t - m)
            vt = vt_ref[i] if n == 0 else vt_ref[n - 1]
            pv = jnp.dot(vt, p.astype(BF16), preferred_element_type=F32)
            acc = pv if acc is None else acc + pv
        l = acc[HEAD_DIM:HEAD_DIM + 1, :]
        o_ref[0, qi * blk:(qi + 1) * blk, :] = (acc[0:HEAD_DIM, :] / l).T

    for v in range(nb // tps):
        @pl.when(g == v)
        def _(v=v):
            if v == 0:
                stage_keys()
            for qi in range(tps):
                attend(v * tps + qi, qi)


def _head_page_copy(cache_ref, pt_ref, buf_ref, sem, layer, b, h, page_slot, dst_row, n_pages):
    page = pt_ref[b * n_pages + page_slot]
    ps = cache_ref.shape[2]
    return pltpu.make_async_copy(cache_ref.at[layer, page, :, h, :], buf_ref.at[pl.ds(dst_row, ps), :], sem)


def _start_k_pages(st, pt_ref, kc_ref, kbuf_ref, sem_ref, *, layer, n_pages):
    hps = kbuf_ref.shape[1]
    groups = N_HEADS // hps
    bb, hg = st // groups, st % groups
    sl = st % 2
    ps = kc_ref.shape[2]

    def issue(pair, carry):
        for k in range(DMA_PRIORITIES):
            p = pair * DMA_PRIORITIES + k
            for hh in range(hps):
                _head_page_copy(kc_ref, pt_ref, kbuf_ref.at[sl, hh], sem_ref.at[sl], layer, bb, hg * hps + hh,
                                p, pl.multiple_of(p * ps, ps), n_pages).start(priority=k)
        return carry

    lax.fori_loop(0, n_pages // DMA_PRIORITIES, issue, 0, unroll=4)


def _sample_scores_body(step, n_steps, pt_ref, q_ref, kc_ref, s_ref, sel_ref, kbuf_ref, km_ref, gate_ref, sem_ref,
                        *, layer, n_pages, chunk):
    nh = N_HEADS
    hps = kbuf_ref.shape[1]
    groups = nh // hps
    hg = step % groups
    slot = step % 2
    ps = kc_ref.shape[2]
    past = n_pages * ps
    blk = MOBA_BLOCK
    t = q_ref.shape[1]
    start_pages = functools.partial(_start_k_pages, pt_ref=pt_ref, kc_ref=kc_ref, kbuf_ref=kbuf_ref,
                                    sem_ref=sem_ref, layer=layer, n_pages=n_pages)

    @pl.when(step == 0)
    def _():
        start_pages(step)

    @pl.when(step + 1 < n_steps)
    def _():
        start_pages(step + 1)

    pltpu.make_async_copy(kbuf_ref.at[1 - slot], kbuf_ref.at[slot], sem_ref.at[slot]).wait()

    bpc = chunk // blk
    for hh in range(hps):
        q = q_ref[0, :, hh * HEAD_DIM:(hh + 1) * HEAD_DIM]
        qb = q.astype(BF16)

        def body(ci, carry, hh=hh, qb=qb):
            start = pl.multiple_of(ci * chunk, chunk)
            kf = kbuf_ref[slot, hh, pl.ds(start, chunk), :]
            sc = lax.dot_general(qb, kf.astype(BF16), _NT, preferred_element_type=F32)
            for jj in range(bpc):
                s_ref[0, hh, ci * bpc + jj] = sc[:, jj * blk:(jj + 1) * blk]
            km_ref[pl.ds(pl.multiple_of(ci * bpc, bpc), bpc), :] = jnp.mean(
                kf.reshape(bpc, blk, HEAD_DIM), axis=1)
            return carry

        lax.fori_loop(0, past // chunk, body, 0, unroll=True)

        gate_ref[hg * hps + hh] = lax.dot_general(q, km_ref[...], _NT, precision=lax.Precision.HIGHEST,
                                                  preferred_element_type=F32)

    @pl.when(hg == groups - 1)
    def _():
        gate = gate_ref[...].reshape(nh * t, past // blk)
        idxs, _ = _top_blocks(gate, past // blk, axis=1)
        lane = lax.broadcasted_iota(jnp.int32, (nh * t, 128), 1)
        out = jnp.zeros((nh * t, 128), jnp.int32)
        for k, idx in enumerate(idxs):
            out = jnp.where(lane == k, idx, out)
        sel_ref[0] = out.reshape(nh, t, 128)


def _moba_scores_kernel(pt_ref, qs_ref, kc_ref, qp_ref, kp_ref, vp_ref, s_ref, sel_ref, o_ref,
                        kbuf_ref, km_ref, gate_ref, sem_ref, kb_ref, vt_ref, kmp_ref,
                        *, layer, n_pages, chunk, nb, tps, every):
    step = pl.program_id(0)
    n_steps = pl.num_programs(0)
    _sample_scores_body(step, n_steps, pt_ref, qs_ref, kc_ref, s_ref, sel_ref, kbuf_ref, km_ref,
                        gate_ref, sem_ref, layer=layer, n_pages=n_pages, chunk=chunk)

    @pl.when(step % every == 0)
    def _():
        item = step // every
        groups = nb // tps
        _moba_prompt_body((item // groups) % N_HEADS, item % groups, qp_ref, kp_ref, vp_ref, o_ref,
                          kb_ref, vt_ref, kmp_ref, nb=nb, tps=tps)


def _moba_and_scores(qu_p, k_p, v_p, qu_s, cache_k, page_table, layer, tps=8, hps=2):
    _, bs, ts, _ = qu_s.shape
    bp, tp, _ = k_p.shape
    blk = MOBA_BLOCK
    n_pages = page_table.shape[1]
    ps = cache_k.shape[2]
    past = n_pages * ps
    chunk = 8 * blk
    assert past % chunk == 0 and past // blk >= MOBA_TOPK and n_pages % DMA_PRIORITIES == 0
    tps = min(tps, tp // blk)
    assert tp % (blk * tps) == 0 and N_HEADS % hps == 0
    nb = tp // blk
    groups = nb // tps
    hgs = N_HEADS // hps
    n_steps, items = bs * hgs, bp * N_HEADS * groups
    assert n_steps % items == 0
    every = n_steps // items

    def prompt_idx(step):
        item = step // every
        return item // (N_HEADS * groups), (item // groups) % N_HEADS, item % groups

    def qp_map(step, pt):
        bi, h, g = prompt_idx(step)
        return (0, bi, g, h)

    def kv_map(step, pt):
        bi, h, g = prompt_idx(step)
        return (bi, 0, h)

    def o_map(step, pt):
        bi, h, g = prompt_idx(step)
        return (bi, g, h)

    grid_spec = pltpu.PrefetchScalarGridSpec(
        num_scalar_prefetch=1,
        grid=(n_steps,),
        in_specs=[pl.BlockSpec((None, 1, ts, hps * HEAD_DIM), lambda s, pt: (0, s // hgs, 0, s % hgs)),
                  pl.BlockSpec(memory_space=pl.ANY),
                  pl.BlockSpec((None, 1, blk * tps, HEAD_DIM), qp_map),
                  pl.BlockSpec((1, tp, HEAD_DIM), kv_map),
                  pl.BlockSpec((1, tp, HEAD_DIM), kv_map)],
        out_specs=[pl.BlockSpec((1, hps, past // blk, ts, blk), lambda s, pt: (s // hgs, s % hgs, 0, 0, 0)),
                   pl.BlockSpec((1, N_HEADS, ts, 128), lambda s, pt: (s // hgs, 0, 0, 0)),
                   pl.BlockSpec((1, blk * tps, HEAD_DIM), o_map)],
        scratch_shapes=[pltpu.VMEM((2, hps, past, HEAD_DIM), F32),
                        pltpu.VMEM((past // blk, HEAD_DIM), F32),
                        pltpu.VMEM((N_HEADS, ts, past // blk), F32),
                        pltpu.SemaphoreType.DMA((2,)),
                        pltpu.VMEM((tp, HEAD_DIM), BF16), pltpu.VMEM((nb, HEAD_DIM + ONES_ROWS, blk), BF16),
                        pltpu.VMEM((nb, HEAD_DIM), F32)],
    )
    scores, sel, attn = pl.pallas_call(
        functools.partial(_moba_scores_kernel, layer=layer, n_pages=n_pages, chunk=chunk, nb=nb, tps=tps,
                          every=every),
        grid_spec=grid_spec,
        out_shape=[jax.ShapeDtypeStruct((bs, N_HEADS, past // blk, ts, blk), F32),
                   jax.ShapeDtypeStruct((bs, N_HEADS, ts, 128), jnp.int32),
                   jax.ShapeDtypeStruct((bp, tp, ATTN_WIDTH), F32)],
        compiler_params=_params("arbitrary"),
        name="moba_and_scores",
    )(page_table.reshape(-1), qu_s, cache_k, qu_p, k_p, v_p)
    return attn, scores, sel


def _sample_attend_kernel(pt_ref, sel_ref, s_ref, q_ref, kn_ref, vn_ref, vc_ref, o_ref,
                          vbuf_ref, ssel_ref, sem_ref, *, layer, n_pages, q_start):
    b = pl.program_id(0)
    h = pl.program_id(1)
    nbt, nh = pl.num_programs(0), pl.num_programs(1)
    step = b * nh + h
    slot = step % 2
    t = q_ref.shape[1]
    ps = vc_ref.shape[2]
    blk = MOBA_BLOCK
    ppb = blk // ps
    nsel = t * MOBA_TOPK

    def sel_block(st, e):
        return sel_ref[st * nsel + e]

    def copies(st, sl):
        bb, hh = st // nh, st % nh

        def mk(e, pg):
            return _head_page_copy(vc_ref, pt_ref, vbuf_ref.at[sl], sem_ref.at[sl], layer, bb, hh,
                                   sel_block(st, e) * ppb + pg, e * blk + pg * ps, n_pages)
        return mk

    def start_all(st, sl):
        mk = copies(st, sl)
        for e in range(nsel):
            for pg in range(ppb):
                mk(e, pg).start(priority=(e * ppb + pg) % DMA_PRIORITIES)

    @pl.when(step == 0)
    def _():
        start_all(step, slot)

    @pl.when(step + 1 < nbt * nh)
    def _():
        start_all(step + 1, 1 - slot)

    scale = HEAD_DIM ** -0.5
    slope = jnp.exp2(jnp.broadcast_to(-8.0 * (h + 1).astype(F32) / N_HEADS, (1, 1)))
    lane = lax.broadcasted_iota(jnp.int32, (1, blk), 1)

    ssel_ref[...] = jnp.full(ssel_ref.shape, NEG_INF, F32)
    for e in range(nsel):
        ti = e // MOBA_TOPK
        n = sel_block(step, e)
        raw = s_ref[0, 0, n, ti:ti + 1, :]
        dist = (q_start + ti - n * blk - lane).astype(F32)
        ssel_ref[ti:ti + 1, e * blk:(e + 1) * blk] = raw * scale - slope * dist

    q = q_ref[0].astype(BF16)
    r = lax.broadcasted_iota(jnp.int32, (t, t), 0)
    c = lax.broadcasted_iota(jnp.int32, (t, t), 1)
    s_own = lax.dot_general(q, kn_ref[0].astype(BF16), _NT, preferred_element_type=F32) * scale
    s_own = jnp.where(c <= r, s_own - slope * (r - c).astype(F32), NEG_INF)

    s_sel = ssel_ref[...]
    m = jnp.maximum(jnp.max(s_sel, axis=1, keepdims=True), jnp.max(s_own, axis=1, keepdims=True))
    p_sel = jnp.exp(s_sel - m)
    p_own = jnp.exp(s_own - m)
    l = jnp.sum(p_sel, axis=1, keepdims=True) + jnp.sum(p_own, axis=1, keepdims=True)

    mk = copies(step, slot)
    for e in range(nsel):
        for pg in range(ppb):
            mk(e, pg).wait()

    acc = jnp.dot(p_sel.astype(BF16), vbuf_ref[slot].astype(BF16), preferred_element_type=F32)
    acc = acc + jnp.dot(p_own.astype(BF16), vn_ref[0].astype(BF16), preferred_element_type=F32)
    o_ref[0] = acc / l


def _sample_attend(scores, sel, qu, k_new, v_new, cache_v, page_table, layer):
    b, t, _ = k_new.shape
    n_pages = page_table.shape[1]
    ps = cache_v.shape[2]
    past = n_pages * ps
    assert MOBA_BLOCK % ps == 0 and past % MOBA_BLOCK == 0 and t <= MOBA_BLOCK
    nsel = t * MOBA_TOPK
    grid_spec = pltpu.PrefetchScalarGridSpec(
        num_scalar_prefetch=2,
        grid=(b, N_HEADS),
        in_specs=[pl.BlockSpec((1, 1, past // MOBA_BLOCK, t, MOBA_BLOCK), lambda bi, h, pt, sl: (bi, h, 0, 0, 0)),
                  pl.BlockSpec((None, 1, t, HEAD_DIM), lambda bi, h, pt, sl: (0, bi, 0, h)),
                  pl.BlockSpec((1, t, HEAD_DIM), lambda bi, h, pt, sl: (bi, 0, h)),
                  pl.BlockSpec((1, t, HEAD_DIM), lambda bi, h, pt, sl: (bi, 0, h)),
                  pl.BlockSpec(memory_space=pl.ANY)],
        out_specs=pl.BlockSpec((1, t, HEAD_DIM), lambda bi, h, pt, sl: (bi, 0, h)),
        scratch_shapes=[pltpu.VMEM((2, nsel * MOBA_BLOCK, HEAD_DIM), F32),
                        pltpu.VMEM((t, nsel * MOBA_BLOCK), F32),
                        pltpu.SemaphoreType.DMA((2,))],
    )
    return pl.pallas_call(
        functools.partial(_sample_attend_kernel, layer=layer, n_pages=n_pages, q_start=past),
        grid_spec=grid_spec,
        out_shape=jax.ShapeDtypeStruct((b, t, ATTN_WIDTH), F32),
        compiler_params=_params("arbitrary", "arbitrary"),
        name="sample_attend",
    )(page_table.reshape(-1), sel[..., :MOBA_TOPK].reshape(-1), scores, qu, k_new, v_new, cache_v)


def _conv_taps():
    pad = CONV_HALO - (CONV_WIDTH - 1)
    return [(ph, [(j, (pad + j) // 8) for j in range(CONV_WIDTH) if (pad + j) % 8 == ph]) for ph in range(8)]


def _conv_kernel(ua_ref, ug_ref, prev_ref, w_ref, b_ref, g_ref, bl_ref, y_ref, st_ref, buf_ref, par_ref, z_ref,
                 *, tt, rows, nrows, lanes):
    ti = pl.program_id(1)
    ch = y_ref.shape[2]
    hist = CONV_WIDTH - 1
    pad = CONV_HALO - hist
    i_bias, i_gain, i_beta = CONV_WIDTH, CONV_WIDTH + 1, CONV_WIDTH + 2

    @pl.when(ti == 0)
    def _():
        buf_ref[0:CONV_HALO, :] = jnp.zeros((CONV_HALO, ch), F32)
        buf_ref[pad:CONV_HALO, :] = prev_ref[0]
        for j in range(CONV_WIDTH):
            par_ref[j] = jnp.broadcast_to(w_ref[j:j + 1, :], (8, ch))
        par_ref[i_bias] = jnp.broadcast_to(b_ref[...], (8, ch))
        par_ref[i_gain] = jnp.broadcast_to(g_ref[...], (8, ch))
        par_ref[i_beta] = jnp.broadcast_to(bl_ref[...], (8, ch))

    @pl.when(ti > 0)
    def _():
        buf_ref[0:CONV_HALO, :] = buf_ref[tt:tt + CONV_HALO, :]

    def glu(ci, carry):
        r0 = pl.multiple_of(ci * rows, rows)
        buf_ref[pl.ds(CONV_HALO + r0, rows), :] = (ua_ref[0, pl.ds(r0, rows), :]
                                                   * _sigmoid(ug_ref[0, pl.ds(r0, rows), :]))
        return carry

    lax.fori_loop(0, tt // rows, glu, 0)

    def conv(ci, carry):
        r0 = pl.multiple_of(ci * rows, rows)
        for c0 in range(0, ch, lanes):
            cs = slice(c0, c0 + lanes)
            win = buf_ref.at[pl.ds(r0, rows + CONV_HALO), pl.ds(c0, lanes)]
            acc = jnp.broadcast_to(par_ref[i_bias, :, cs][None], (rows // 8, 8, lanes))
            for ph, taps in _conv_taps():
                a_lo, a_hi = taps[0][1], taps[-1][1]
                x = win[8 * a_lo + ph:8 * a_hi + ph + rows, :]
                for j, a in taps:
                    xs = x[8 * (a - a_lo):8 * (a - a_lo) + rows, :].reshape(rows // 8, 8, lanes)
                    acc = acc + par_ref[j, :, cs][None] * xs
            z_ref[pl.ds(r0, rows), cs] = acc.reshape(rows, lanes)
        return carry

    lax.fori_loop(0, tt // rows, conv, 0)

    def norm(ci, carry):
        r0 = pl.multiple_of(ci * nrows, nrows)
        z = z_ref[pl.ds(r0, nrows), :]
        xc = z - jnp.mean(z, axis=-1, keepdims=True)
        xn = (xc * lax.rsqrt(jnp.mean(xc * xc, axis=-1, keepdims=True) + EPS)).reshape(nrows // 8, 8, ch)
        y = (xn * par_ref[i_gain][None] + par_ref[i_beta][None]).reshape(nrows, ch)
        y_ref[0, pl.ds(r0, nrows), :] = y * _sigmoid(y)
        return carry

    lax.fori_loop(0, tt // nrows, norm, 0, unroll=min(8, tt // nrows))

    @pl.when(ti == pl.num_programs(1) - 1)
    def _():
        st_ref[0] = buf_ref[tt + pad:tt + CONV_HALO, :]


def _conv_specs(qu, prev, layer, w_dw, b_dw, g_ln, b_ln, tt):
    _, b, t, ch = qu.shape
    assert ch == w_dw.shape[1]
    hist = CONV_WIDTH - 1
    assert t % tt == 0 and tt % 8 == 0
    rows = 64 if tt % 64 == 0 else 8
    nrows = 16 if tt % 16 == 0 else 8
    lanes = 256 if ch % 256 == 0 else ch
    row = lambda a: a.reshape(1, ch)
    in_specs = [pl.BlockSpec((None, 1, tt, ch), lambda bi, ti: (1, bi, ti, 0)),
                pl.BlockSpec((None, 1, tt, ch), lambda bi, ti: (2, bi, ti, 0)),
                pl.BlockSpec((None, 1, hist, ch), lambda bi, ti: (layer, bi, 0, 0)),
                pl.BlockSpec((CONV_WIDTH, ch), lambda bi, ti: (0, 0)),
                pl.BlockSpec((1, ch), lambda bi, ti: (0, 0)),
                pl.BlockSpec((1, ch), lambda bi, ti: (0, 0)),
                pl.BlockSpec((1, ch), lambda bi, ti: (0, 0))]
    out_specs = [pl.BlockSpec((1, tt, ch), lambda bi, ti: (bi, ti, 0)),
                 pl.BlockSpec((1, hist, ch), lambda bi, ti: (bi, 0, 0))]
    out_shape = [jax.ShapeDtypeStruct((b, t, ch), F32), jax.ShapeDtypeStruct((b, hist, ch), F32)]
    scratch = [pltpu.VMEM((tt + CONV_HALO, ch), F32), pltpu.VMEM((CONV_WIDTH + 3, 8, ch), F32),
               pltpu.VMEM((tt, ch), F32)]
    args = (qu, qu, prev, w_dw, row(b_dw), row(g_ln), row(b_ln))
    return (b, t // tt), in_specs, out_specs, out_shape, scratch, args, dict(tt=tt, rows=rows, nrows=nrows,
                                                                            lanes=lanes)


def _conformer_conv(qu, prev, layer, w_dw, b_dw, g_ln, b_ln, tt):
    grid, in_specs, out_specs, out_shape, scratch, args, kw = _conv_specs(qu, prev, layer, w_dw, b_dw, g_ln,
                                                                         b_ln, tt)
    return pl.pallas_call(
        functools.partial(_conv_kernel, **kw),
        grid=grid,
        in_specs=in_specs,
        out_specs=out_specs,
        out_shape=out_shape,
        scratch_shapes=scratch,
        compiler_params=_params("arbitrary", "arbitrary"),
        name="conformer_conv",
    )(*args)


def _mix_out_kernel(attn_ref, conv_ref, h_ref, ga_ref, gc_ref, w_ref, gp_ref, o_ref, *rest, emit):
    if emit:
        (wt_ref,) = rest
        wt_ref[...] = w_ref[...].astype(BF16)
        w_ref = wt_ref
    wa = attn_ref.shape[1]
    a = _rms(attn_ref[...], ga_ref[...]).astype(BF16)
    cv = _rms(conv_ref[...], gc_ref[...]).astype(BF16)
    mixed = (jnp.dot(a, w_ref[0:wa, :], preferred_element_type=F32)
             + jnp.dot(cv, w_ref[wa:, :], preferred_element_type=F32))
    o_ref[...] = h_ref[...] + _rms(mixed, gp_ref[...])


def _mix_out(attn, conv, h, g_attn, g_conv, w_out, g_post, tm, emit):
    m, wa = attn.shape
    wc = conv.shape[1]
    d = h.shape[1]
    assert m % tm == 0 and (not emit or m == tm)
    row = lambda a: a.reshape(1, -1)
    out_specs = [pl.BlockSpec((tm, d), lambda i: (i, 0))]
    out_shape = [jax.ShapeDtypeStruct((m, d), F32)]
    if emit:
        out_specs.append(pl.BlockSpec((wa + wc, d), lambda i: (0, 0)))
        out_shape.append(jax.ShapeDtypeStruct((wa + wc, d), BF16))
    outs = pl.pallas_call(
        functools.partial(_mix_out_kernel, emit=emit),
        grid=(m // tm,),
        in_specs=[pl.BlockSpec((tm, wa), lambda i: (i, 0)),
                  pl.BlockSpec((tm, wc), lambda i: (i, 0)),
                  pl.BlockSpec((tm, d), lambda i: (i, 0)),
                  pl.BlockSpec((1, wa), lambda i: (0, 0)),
                  pl.BlockSpec((1, wc), lambda i: (0, 0)),
                  pl.BlockSpec((wa + wc, d), lambda i: (0, 0)),
                  pl.BlockSpec((1, d), lambda i: (0, 0))],
        out_specs=out_specs,
        out_shape=out_shape,
        compiler_params=_params("arbitrary", vmem_limit=VMEM_LIMIT_MAX if emit else VMEM_LIMIT),
        name="mix_out",
    )(attn, conv, h, row(g_attn), row(g_conv), w_out, row(g_post))
    return outs[0], (outs[1] if emit else None)


def _xattn_kernel(h_ref, mk_ref, mv_ref, gpre_ref, wq_ref, wo_ref, gpost_ref, o_ref, *rest, emit):
    if emit:
        wq_out, wo_out = rest
        wq_out[...] = wq_ref[...].astype(BF16)
        wo_out[...] = wo_ref[...].astype(BF16)
        wq_ref, wo_ref = wq_out, wo_out
    h = h_ref[...]
    xn = _rms(h, gpre_ref[...]).astype(BF16)
    xq = jnp.dot(xn, wq_ref[...], preferred_element_type=F32)
    scale = X_HEAD_DIM ** -0.5
    seqs = mk_ref.shape[0]
    rows = h.shape[0] // seqs
    per_seq = []
    for bi in range(seqs):
        outs = []
        for hh in range(X_HEADS):
            cols = slice(hh * X_HEAD_DIM, (hh + 1) * X_HEAD_DIM)
            qh = xq[bi * rows:(bi + 1) * rows, cols].astype(BF16)
            s = lax.dot_general(qh, mk_ref[bi, :, cols].astype(BF16), _NT, preferred_element_type=F32) * scale
            p = jnp.exp(s - jnp.max(s, axis=1, keepdims=True))
            l = jnp.sum(p, axis=1, keepdims=True)
            outs.append(jnp.dot(p.astype(BF16), mv_ref[bi, :, cols].astype(BF16),
                                preferred_element_type=F32) / l)
        per_seq.append(jnp.concatenate(outs, axis=1))
    o = jnp.concatenate(per_seq, axis=0).astype(BF16)
    y = jnp.dot(o, wo_ref[...], preferred_element_type=F32)
    o_ref[...] = h + _rms(y, gpost_ref[...])


def _xattn(h, mem_k, mem_v, layer, g_pre, w_xq, w_xo, g_post, rows_per_batch, tm, emit):
    m, d = h.shape
    seqs = max(1, tm // rows_per_batch)
    tiles = max(1, rows_per_batch // tm)
    assert m % tm == 0 and tm * tiles == rows_per_batch * seqs and (not emit or m == tm)
    mem_len, xw = mem_k.shape[2:]
    row = lambda a: a.reshape(1, -1)
    out_specs = [pl.BlockSpec((tm, d), lambda i: (i, 0))]
    out_shape = [jax.ShapeDtypeStruct((m, d), F32)]
    if emit:
        out_specs += [pl.BlockSpec((d, xw), lambda i: (0, 0)), pl.BlockSpec((xw, d), lambda i: (0, 0))]
        out_shape += [jax.ShapeDtypeStruct((d, xw), BF16), jax.ShapeDtypeStruct((xw, d), BF16)]
    outs = pl.pallas_call(
        functools.partial(_xattn_kernel, emit=emit),
        grid=(m // tm,),
        in_specs=[pl.BlockSpec((tm, d), lambda i: (i, 0)),
                  pl.BlockSpec((None, seqs, mem_len, xw), lambda i: (layer, i // tiles, 0, 0)),
                  pl.BlockSpec((None, seqs, mem_len, xw), lambda i: (layer, i // tiles, 0, 0)),
                  pl.BlockSpec((1, d), lambda i: (0, 0)),
                  pl.BlockSpec((d, xw), lambda i: (0, 0)),
                  pl.BlockSpec((xw, d), lambda i: (0, 0)),
                  pl.BlockSpec((1, d), lambda i: (0, 0))],
        out_specs=out_specs,
        out_shape=out_shape,
        compiler_params=_params("arbitrary"),
        name="xattn",
    )(h, mem_k, mem_v, row(g_pre), w_xq, w_xo, row(g_post))
    return outs[0], (tuple(outs[1:]) if emit else None)


def _ffn_body(f, nf, h_ref, gpre_ref, wg_ref, wu_ref, wo_ref, gpost_ref, o_ref, rest, emit):
    if emit:
        wg_out, wu_out, wo_out, xn_ref, acc_ref = rest
        wg_out[...] = wg_ref[...].astype(BF16)
        wu_out[...] = wu_ref[...].astype(BF16)
        wo_out[...] = wo_ref[...].astype(BF16)
        wg_ref, wu_ref, wo_ref = wg_out, wu_out, wo_out
    else:
        xn_ref, acc_ref = rest

    def tile(first, last):
        if first:
            xn = _rms(h_ref[...], gpre_ref[...]).astype(BF16)
            xn_ref[...] = xn
        else:
            xn = xn_ref[...]
        g = jnp.dot(xn, wg_ref[...], preferred_element_type=F32)
        u = jnp.dot(xn, wu_ref[...], preferred_element_type=F32)
        a = (g * jax.nn.sigmoid(g) * u).astype(BF16)
        y = jnp.dot(a, wo_ref[...], preferred_element_type=F32)
        acc = y if first else acc_ref[...] + y
        if last:
            o_ref[...] = h_ref[...] + _rms(acc, gpost_ref[...])
        else:
            acc_ref[...] = acc

    assert nf >= 2
    pl.when(f == 0)(functools.partial(tile, True, False))
    pl.when((f > 0) & (f < nf - 1))(functools.partial(tile, False, False))
    pl.when(f == nf - 1)(functools.partial(tile, False, True))


def _ffn_kernel(h_ref, gpre_ref, wg_ref, wu_ref, wo_ref, gpost_ref, o_ref, *rest, emit, nf):
    _ffn_body(pl.program_id(1), nf, h_ref, gpre_ref, wg_ref, wu_ref, wo_ref, gpost_ref, o_ref, rest, emit)


def _ffn_specs(h, w, tm, tf, emit, tile):
    m, d = h.shape
    if emit:
        w_in, w_out = w
        hidden = w_out.shape[0]
        nf = hidden // tf
        assert m == tm and hidden % tf == 0 and w_in.shape[1] == 2 * hidden
        w_args = (w_in, w_in, w_out)
        w_specs = [pl.BlockSpec((d, tf), lambda *ids: (0, tile(*ids)[1])),
                   pl.BlockSpec((d, tf), lambda *ids: (0, tile(*ids)[1] + nf)),
                   pl.BlockSpec((tf, d), lambda *ids: (tile(*ids)[1], 0))]
    else:
        w_args = w
        nf, hidden = w[0].shape[0], w[2].shape[0]
        assert m % tm == 0 and w[0].shape == w[1].shape == (nf, d, tf) and hidden == nf * tf
        w_specs = [pl.BlockSpec((None, d, tf), lambda *ids: (tile(*ids)[1], 0, 0)),
                   pl.BlockSpec((None, d, tf), lambda *ids: (tile(*ids)[1], 0, 0)),
                   pl.BlockSpec((tf, d), lambda *ids: (tile(*ids)[1], 0))]
    in_specs = [pl.BlockSpec((tm, d), lambda *ids: (tile(*ids)[0], 0)),
                pl.BlockSpec((1, d), lambda *ids: (0, 0)),
                *w_specs,
                pl.BlockSpec((1, d), lambda *ids: (0, 0))]
    out_specs = [pl.BlockSpec((tm, d), lambda *ids: (tile(*ids)[0], 0))]
    out_shape = [jax.ShapeDtypeStruct((m, d), F32)]
    if emit:
        out_specs += [pl.BlockSpec((None, d, tf), lambda *ids: (tile(*ids)[1], 0, 0)),
                      pl.BlockSpec((None, d, tf), lambda *ids: (tile(*ids)[1], 0, 0)),
                      pl.BlockSpec((tf, d), lambda *ids: (tile(*ids)[1], 0))]
        out_shape += [jax.ShapeDtypeStruct((nf, d, tf), BF16), jax.ShapeDtypeStruct((nf, d, tf), BF16),
                      jax.ShapeDtypeStruct((hidden, d), BF16)]
    scratch = [pltpu.VMEM((tm, d), BF16), pltpu.VMEM((tm, d), F32)]
    return nf, in_specs, out_specs, out_shape, scratch, w_args


def _ffn(h, g_pre, w, g_post, tm, tf, emit):
    m = h.shape[0]
    row = lambda a: a.reshape(1, -1)
    nf, in_specs, out_specs, out_shape, scratch, w_args = _ffn_specs(h, w, tm, tf, emit, lambda i, f: (i, f))
    outs = pl.pallas_call(
        functools.partial(_ffn_kernel, emit=emit, nf=nf),
        grid=(m // tm, nf),
        in_specs=in_specs,
        out_specs=out_specs,
        out_shape=out_shape,
        scratch_shapes=scratch,
        compiler_params=_params("arbitrary", "arbitrary"),
        name="ffn",
    )(h, row(g_pre), *w_args, row(g_post))
    return outs[0], tuple(outs[1:])


def _conv_ffn_kernel(*refs, conv_kw, nf):
    ci, fi, co, fo, cs, fs = refs[0:7], refs[7:13], refs[13:15], refs[15:19], refs[19:22], refs[22:24]
    step = pl.program_id(0) * pl.num_programs(1) + pl.program_id(1)

    @pl.when(step < nf)
    def _():
        _ffn_body(step, nf, *fi, fo[0], tuple(fo[1:]) + tuple(fs), True)

    _conv_kernel(*ci, *co, *cs, **conv_kw)


def _conv_and_emit_ffn(conv_args, h, g_pre, w, g_post, tf):
    grid, c_in, c_out, c_shape, c_scratch, c_args, conv_kw = _conv_specs(*conv_args)
    nt = grid[1]
    row = lambda a: a.reshape(1, -1)
    tile = lambda bi, ti: (0, jnp.minimum(bi * nt + ti, nf - 1))
    nf = w[1].shape[0] // tf
    if grid[0] * nt < nf:
        return None
    nf, f_in, f_out, f_shape, f_scratch, w_args = _ffn_specs(h, w, h.shape[0], tf, True, tile)
    outs = pl.pallas_call(
        functools.partial(_conv_ffn_kernel, conv_kw=conv_kw, nf=nf),
        grid=grid,
        in_specs=c_in + f_in,
        out_specs=c_out + f_out,
        out_shape=c_shape + f_shape,
        scratch_shapes=c_scratch + f_scratch,
        compiler_params=_params("arbitrary", "arbitrary", vmem_limit=VMEM_LIMIT_MAX),
        name="conv_and_ffn",
    )(*c_args, h, row(g_pre), *w_args, row(g_post))
    return outs[0], outs[1], outs[2], tuple(outs[3:])


def _row_tile(m):
    return 512 if m % 512 == 0 else m


def _project_in(x, wts, w_in, emit):
    b, t, d = x.shape
    m = b * t
    assert wts["w_dw"].shape[1] == ATTN_WIDTH
    k_new, v_new, qu, *w_in_b = _in_proj(x.reshape(m, d), wts["g_pre_mix"].reshape(1, d), w_in, emit,
                                         tm=1024 if m % 1024 == 0 else _row_tile(m))
    k_new, v_new = k_new.reshape(b, t, ATTN_WIDTH), v_new.reshape(b, t, ATTN_WIDTH)
    return k_new, v_new, qu.reshape(qu.shape[0], b, t, ATTN_WIDTH), (w_in_b[0] if emit else None)


def _mix_and_xattn(x, attn, conv, mem, mem_layer, wts, w, emit, x_tile):
    b, t, d = x.shape
    m = b * t
    h, w_out_b = _mix_out(attn.reshape(m, ATTN_WIDTH), conv.reshape(m, -1), x.reshape(m, d), wts["g_attn_grp"],
                          wts["g_conv_grp"], w[0], wts["g_post_mix"], tm=_row_tile(m), emit=emit)
    h, w_x_b = _xattn(h, mem[0], mem[1], mem_layer, wts["g_pre_x"], w[1], w[2], wts["g_post_x"],
                      rows_per_batch=t, tm=x_tile, emit=emit)
    return h, ((w_out_b,) + w_x_b if emit else None)


def kernel(x_prompt, x_sample, cache_k, cache_v, state_conv, cache_mem_k, cache_mem_v, page_table,
           mem_prompt, g_mem, w_mem_k, w_mem_v, g_pre_mix, w_in, w_dw, b_dw, g_ln_conv, b_ln_conv,
           g_attn_grp, g_conv_grp, w_out, g_post_mix, g_pre_x, w_xq, w_xo, g_post_x,
           g_pre_ffn, w_ffn_in, w_ffn_out, g_post_ffn):
    depth = w_in.shape[0]
    bp, tp, d = x_prompt.shape
    bs, ts, _ = x_sample.shape
    mem_len = mem_prompt.shape[1]
    xw = X_HEADS * X_HEAD_DIM
    ch = w_dw.shape[2]
    hp, hs = x_prompt, x_sample
    outs = [[] for _ in range(8)]
    for l in range(depth):
        wts = dict(g_pre_mix=g_pre_mix[l], w_dw=w_dw[l], b_dw=b_dw[l],
                   g_ln_conv=g_ln_conv[l], b_ln_conv=b_ln_conv[l], g_attn_grp=g_attn_grp[l],
                   g_conv_grp=g_conv_grp[l], g_post_mix=g_post_mix[l], g_pre_x=g_pre_x[l],
                   g_post_x=g_post_x[l], g_pre_ffn=g_pre_ffn[l], g_post_ffn=g_post_ffn[l])

        ks, vs, qu_s, w_in_b = _project_in(hs, wts, w_in[l], True)
        kp, vp, qu_p, _ = _project_in(hp, wts, w_in_b, False)
        attn_p, scores, sel = _moba_and_scores(qu_p, kp, vp, qu_s, cache_k, page_table, l)
        attn_s = _sample_attend(scores, sel, qu_s, ks, vs, cache_v, page_table, l)

        conv_w = (wts["w_dw"], wts["b_dw"], wts["g_ln_conv"], wts["b_ln_conv"])
        conv_s, cs = _conformer_conv(qu_s, state_conv, l, *conv_w, ts)
        sample_mem = (cache_mem_k.reshape(depth, bs, mem_len, xw), cache_mem_v.reshape(depth, bs, mem_len, xw))
        hs_mid, w_mid_b = _mix_and_xattn(hs, attn_s, conv_s, sample_mem, l, wts, (w_out[l], w_xq[l], w_xo[l]),
                                         True, x_tile=bs * ts)

        conv_args = (qu_p, jnp.zeros((1, bp, CONV_WIDTH - 1, ch), F32), 0, *conv_w, _row_tile(tp))
        ffn_w = (w_ffn_in[l], w_ffn_out[l])
        fused = _conv_and_emit_ffn(conv_args, hs_mid, wts["g_pre_ffn"], ffn_w, wts["g_post_ffn"], tf=512)
        if fused is None:
            conv_p, cp = _conformer_conv(*conv_args)
            hs_out, w_ffn_b = _ffn(hs_mid, wts["g_pre_ffn"], ffn_w, wts["g_post_ffn"], tm=bs * ts, tf=512,
                                   emit=True)
        else:
            conv_p, cp, hs_out, w_ffn_b = fused
        hs = hs_out.reshape(bs, ts, d)

        w_mem = jnp.concatenate([w_mem_k[l], w_mem_v[l]], axis=1).astype(BF16)
        mk_p, mv_p = _norm_matmul(mem_prompt.reshape(bp * mem_len, d), g_mem[l].reshape(1, d), w_mem,
                                  (xw, xw), tn=xw, tm=_row_tile(bp * mem_len))
        mk_p, mv_p = mk_p.reshape(bp, mem_len, xw), mv_p.reshape(bp, mem_len, xw)
        hp_mid, _ = _mix_and_xattn(hp, attn_p, conv_p, (mk_p[None], mv_p[None]), 0, wts, w_mid_b, False,
                                   x_tile=_row_tile(tp))
        hp_out, _ = _ffn(hp_mid, wts["g_pre_ffn"], w_ffn_b, wts["g_post_ffn"], tm=_row_tile(bp * tp), tf=512,
                         emit=False)
        hp = hp_out.reshape(bp, tp, d)
        mem_shape = (bp, mem_len, X_HEADS, X_HEAD_DIM)
        heads = lambda a: a.reshape(a.shape[:2] + (N_HEADS, HEAD_DIM))
        for lst, a in zip(outs, (heads(kp), heads(vp), cp, mk_p.reshape(mem_shape), mv_p.reshape(mem_shape),
                                 heads(ks), heads(vs), cs)):
            lst.append(a)
    return (hp, hs) + tuple(jnp.stack(lst, 0) for lst in outs)
```

```python
import functools

import jax
import jax.numpy as jnp
from jax import lax
from jax.experimental import pallas as pl
from jax.experimental.pallas import tpu as pltpu

EPS = 1e-6
N_HEADS = 8
HEAD_DIM = 128
ATTN_WIDTH = N_HEADS * HEAD_DIM
CONV_WIDTH = 31
MOBA_BLOCK = 256
MOBA_TOPK = 3
X_HEADS = 4
X_HEAD_DIM = 128

CONV_HALO = 32
DMA_PRIORITIES = 2
ONES_ROWS = 16
VMEM_LIMIT = 56 * 1024 * 1024
VMEM_LIMIT_MAX = 60 * 1024 * 1024

F32 = jnp.float32
BF16 = jnp.bfloat16
NEG_INF = float("-inf")
LOG2_E = 1.4426950408889634
_NT = (((1,), (1,)), ((), ()))


def _params(*sem, vmem_limit=VMEM_LIMIT):
    return pltpu.CompilerParams(dimension_semantics=sem, vmem_limit_bytes=vmem_limit)


def _rms(x, g):
    return x * lax.rsqrt(jnp.mean(x * x, axis=-1, keepdims=True) + EPS) * g


def _sigmoid(x):
    return 0.5 * jnp.tanh(0.5 * x) + 0.5


def _top_blocks(gate, n_valid, axis):
    nb = gate.shape[axis]
    pos = lax.broadcasted_iota(jnp.int32, gate.shape, axis)
    g = jnp.where(pos < n_valid, gate, NEG_INF)
    idxs, oks = [], []
    for _ in range(MOBA_TOPK):
        m = jnp.max(g, axis=axis, keepdims=True)
        idx = jnp.min(jnp.where(g == m, pos, nb), axis=axis, keepdims=True)
        ok = m > NEG_INF
        idxs.append(idx)
        oks.append(ok)
        g = jnp.where((pos == idx) & ok, NEG_INF, g)
    return idxs, oks


def _norm_matmul_kernel(x_ref, g_ref, w_ref, *rest, tile_ranges):
    out_refs, xn_ref = rest[:-1], rest[-1]
    j = pl.program_id(1)

    @pl.when(j == 0)
    def _():
        xn_ref[...] = _rms(x_ref[...], g_ref[...]).astype(BF16)

    for o_ref, (lo, hi) in zip(out_refs, tile_ranges):
        @pl.when((j >= lo) & (j < hi))
        def _(o_ref=o_ref):
            o_ref[...] = jnp.dot(xn_ref[...], w_ref[...], preferred_element_type=F32)


def _norm_matmul(x, g, w, widths, tn, tm):
    m, d = x.shape
    n = w.shape[1]
    assert sum(widths) == n and all(wd % tn == 0 for wd in widths) and m % tm == 0
    tile_ranges, lo = [], 0
    for wd in widths:
        tile_ranges.append((lo, lo + wd // tn))
        lo += wd // tn

    def out_map(i, j, lo, cnt):
        return (i, jnp.clip(j - lo, 0, cnt - 1))

    out_specs = [pl.BlockSpec((tm, tn), functools.partial(out_map, lo=lo, cnt=hi - lo))
                 for lo, hi in tile_ranges]
    return pl.pallas_call(
        functools.partial(_norm_matmul_kernel, tile_ranges=tuple(tile_ranges)),
        grid=(m // tm, n // tn),
        in_specs=[pl.BlockSpec((tm, d), lambda i, j: (i, 0)),
                  pl.BlockSpec((1, d), lambda i, j: (0, 0)),
                  pl.BlockSpec((d, tn), lambda i, j: (0, j))],
        out_specs=out_specs,
        out_shape=[jax.ShapeDtypeStruct((m, wd), F32) for wd in widths],
        scratch_shapes=[pltpu.VMEM((tm, d), BF16)],
        compiler_params=_params("arbitrary", "arbitrary"),
        name="norm_matmul",
    )(x, g, w)


def _in_proj_src_tile(j):
    return jnp.where(j < 3, (j + 1) % 3, j)


def _in_proj_kernel(x_ref, g_ref, w_ref, *rest, emit):
    if emit:
        k_ref, v_ref, qu_ref, wt_ref, xn_ref = rest
    else:
        k_ref, v_ref, qu_ref, xn_ref = rest
    j = pl.program_id(1)

    if emit:
        wt_ref[...] = w_ref[...].astype(BF16)
        w_ref = wt_ref

    for o_ref, cond, first in ((k_ref, j == 0, True), (v_ref, j == 1, False), (qu_ref, j >= 2, False)):
        @pl.when(cond)
        def _(o_ref=o_ref, first=first):
            if first:
                xn = _rms(x_ref[...], g_ref[...]).astype(BF16)
                xn_ref[...] = xn
            else:
                xn = xn_ref[...]
            o_ref[...] = jnp.dot(xn, w_ref[...], preferred_element_type=F32)


def _in_proj(x, g, w, emit, tm):
    m, d = x.shape
    tn = ATTN_WIDTH
    nt = (w.shape[1] // tn) if emit else w.shape[0]
    assert m % tm == 0 and nt > 3 and (not emit or (m == tm and w.shape[1] % tn == 0))
    if emit:
        w_spec = pl.BlockSpec((d, tn), lambda i, j: (0, _in_proj_src_tile(j)))
    else:
        w_spec = pl.BlockSpec((None, d, tn), lambda i, j: (j, 0, 0))
    out_specs = [pl.BlockSpec((tm, tn), lambda i, j: (i, 0)),
                 pl.BlockSpec((tm, tn), lambda i, j: (i, 0)),
                 pl.BlockSpec((None, tm, tn), lambda i, j: (jnp.maximum(j - 2, 0), i, 0))]
    out_shape = [jax.ShapeDtypeStruct((m, tn), F32), jax.ShapeDtypeStruct((m, tn), F32),
                 jax.ShapeDtypeStruct((nt - 2, m, tn), F32)]
    if emit:
        out_specs.append(pl.BlockSpec((None, d, tn), lambda i, j: (j, 0, 0)))
        out_shape.append(jax.ShapeDtypeStruct((nt, d, tn), BF16))
    return pl.pallas_call(
        functools.partial(_in_proj_kernel, emit=emit),
        grid=(m // tm, nt),
        in_specs=[pl.BlockSpec((tm, d), lambda i, j: (i, 0)),
                  pl.BlockSpec((1, d), lambda i, j: (0, 0)),
                  w_spec],
        out_specs=out_specs,
        out_shape=out_shape,
        scratch_shapes=[pltpu.VMEM((tm, d), BF16)],
        compiler_params=_params("arbitrary", "arbitrary"),
        name="in_proj",
    )(x, g, w)


def _moba_prompt_body(h, g, q_ref, k_ref, v_ref, o_ref, kb_ref, vt_ref, km_ref, *, nb, tps):
    blk = MOBA_BLOCK

    def stage_keys():
        for n in range(nb):
            kf = k_ref[0, n * blk:(n + 1) * blk, :]
            kb_ref[n * blk:(n + 1) * blk, :] = kf.astype(BF16)
            vt_ref[n, 0:HEAD_DIM, :] = v_ref[0, n * blk:(n + 1) * blk, :].T.astype(BF16)
            vt_ref[n, HEAD_DIM:, :] = jnp.ones((ONES_ROWS, blk), BF16)
            km_ref[n:n + 1, :] = jnp.mean(kf, axis=0, keepdims=True)

    scale = HEAD_DIM ** -0.5 * LOG2_E
    slope = jnp.exp2(jnp.broadcast_to(-8.0 * (h + 1).astype(F32) / N_HEADS, (1, blk))) * LOG2_E

    def attend(i, qi):
        q = q_ref[0, qi * blk:(qi + 1) * blk, :]
        qb = q.astype(BF16)
        kr = lax.broadcasted_iota(jnp.int32, (blk, blk), 0)
        qc = lax.broadcasted_iota(jnp.int32, (blk, blk), 1)
        base = slope * (qc - kr).astype(F32)
        s_own = lax.dot_general(kb_ref[i * blk:(i + 1) * blk, :], qb, _NT, preferred_element_type=F32)
        ts = [jnp.where(kr <= qc, s_own * scale - base, NEG_INF)]
        if i:
            gate = lax.dot_general(km_ref[0:i, :], q, _NT, precision=lax.Precision.HIGHEST,
                                   preferred_element_type=F32)
            idxs, oks = _top_blocks(gate, i, axis=0)
            row = lax.broadcasted_iota(jnp.int32, (i, blk), 0)
            sel = jnp.zeros((i, blk), F32)
            for idx, ok in zip(idxs, oks):
                sel = jnp.where((row == idx) & ok, 1.0, sel)
            for n in range(i):
                s = lax.dot_general(kb_ref[n * blk:(n + 1) * blk, :], qb, _NT, preferred_element_type=F32)
                neg = jnp.where(sel[n:n + 1, :] > 0.0, -slope * float((i - n) * blk), NEG_INF)
                ts.append(s * scale - base + neg)
        m = jnp.max(functools.reduce(jnp.maximum, ts), axis=0, keepdims=True)
        acc = None
        for n, t in enumerate(ts):
            p = jnp.exp2(t - m)
            vt = vt_ref[i] if n == 0 else vt_ref[n - 1]
            pv = jnp.dot(vt, p.astype(BF16), preferred_element_type=F32)
            acc = pv if acc is None else acc + pv
        l = acc[HEAD_DIM:HEAD_DIM + 1, :]
        o_ref[0, qi * blk:(qi + 1) * blk, :] = (acc[0:HEAD_DIM, :] / l).T

    for v in range(nb // tps):
        @pl.when(g == v)
        def _(v=v):
            if v == 0:
                stage_keys()
            for qi in range(tps):
                attend(v * tps + qi, qi)


def _head_page_copy(cache_ref, pt_ref, buf_ref, sem, layer, b, h, page_slot, dst_row, n_pages):
    page = pt_ref[b * n_pages + page_slot]
    ps = cache_ref.shape[2]
    return pltpu.make_async_copy(cache_ref.at[layer, page, :, h, :], buf_ref.at[pl.ds(dst_row, ps), :], sem)


def _start_k_pages(st, pt_ref, kc_ref, kbuf_ref, sem_ref, *, layer, n_pages):
    hps = kbuf_ref.shape[1]
    groups = N_HEADS // hps
    bb, hg = st // groups, st % groups
    sl = st % 2
    ps = kc_ref.shape[2]

    def issue(pair, carry):
        for k in range(DMA_PRIORITIES):
            p = pair * DMA_PRIORITIES + k
            for hh in range(hps):
                _head_page_copy(kc_ref, pt_ref, kbuf_ref.at[sl, hh], sem_ref.at[sl], layer, bb, hg * hps + hh,
                                p, pl.multiple_of(p * ps, ps), n_pages).start(priority=k)
        return carry

    lax.fori_loop(0, n_pages // DMA_PRIORITIES, issue, 0, unroll=4)


def _sample_scores_body(step, n_steps, pt_ref, q_ref, kc_ref, s_ref, sel_ref, kbuf_ref, km_ref, gate_ref, sem_ref,
                        *, layer, n_pages, chunk):
    nh = N_HEADS
    hps = kbuf_ref.shape[1]
    groups = nh // hps
    hg = step % groups
    slot = step % 2
    ps = kc_ref.shape[2]
    past = n_pages * ps
    blk = MOBA_BLOCK
    t = q_ref.shape[1]
    start_pages = functools.partial(_start_k_pages, pt_ref=pt_ref, kc_ref=kc_ref, kbuf_ref=kbuf_ref,
                                    sem_ref=sem_ref, layer=layer, n_pages=n_pages)

    @pl.when(step == 0)
    def _():
        start_pages(step)

    @pl.when(step + 1 < n_steps)
    def _():
        start_pages(step + 1)

    pltpu.make_async_copy(kbuf_ref.at[1 - slot], kbuf_ref.at[slot], sem_ref.at[slot]).wait()

    bpc = chunk // blk
    for hh in range(hps):
        q = q_ref[0, :, hh * HEAD_DIM:(hh + 1) * HEAD_DIM]
        qb = q.astype(BF16)

        def body(ci, carry, hh=hh, qb=qb):
            start = pl.multiple_of(ci * chunk, chunk)
            kf = kbuf_ref[slot, hh, pl.ds(start, chunk), :]
            sc = lax.dot_general(qb, kf.astype(BF16), _NT, preferred_element_type=F32)
            for jj in range(bpc):
                s_ref[0, hh, ci * bpc + jj] = sc[:, jj * blk:(jj + 1) * blk]
            km_ref[pl.ds(pl.multiple_of(ci * bpc, bpc), bpc), :] = jnp.mean(
                kf.reshape(bpc, blk, HEAD_DIM), axis=1)
            return carry

        lax.fori_loop(0, past // chunk, body, 0, unroll=True)

        gate_ref[hg * hps + hh] = lax.dot_general(q, km_ref[...], _NT, precision=lax.Precision.HIGHEST,
                                                  preferred_element_type=F32)

    @pl.when(hg == groups - 1)
    def _():
        gate = gate_ref[...].reshape(nh * t, past // blk)
        idxs, _ = _top_blocks(gate, past // blk, axis=1)
        lane = lax.broadcasted_iota(jnp.int32, (nh * t, 128), 1)
        out = jnp.zeros((nh * t, 128), jnp.int32)
        for k, idx in enumerate(idxs):
            out = jnp.where(lane == k, idx, out)
        sel_ref[0] = out.reshape(nh, t, 128)


def _moba_scores_kernel(pt_ref, qs_ref, kc_ref, qp_ref, kp_ref, vp_ref, s_ref, sel_ref, o_ref,
                        kbuf_ref, km_ref, gate_ref, sem_ref, kb_ref, vt_ref, kmp_ref,
                        *, layer, n_pages, chunk, nb, tps, every):
    step = pl.program_id(0)
    n_steps = pl.num_programs(0)
    _sample_scores_body(step, n_steps, pt_ref, qs_ref, kc_ref, s_ref, sel_ref, kbuf_ref, km_ref,
                        gate_ref, sem_ref, layer=layer, n_pages=n_pages, chunk=chunk)

    @pl.when(step % every == 0)
    def _():
        item = step // every
        groups = nb // tps
        _moba_prompt_body((item // groups) % N_HEADS, item % groups, qp_ref, kp_ref, vp_ref, o_ref,
                          kb_ref, vt_ref, kmp_ref, nb=nb, tps=tps)


def _moba_and_scores(qu_p, k_p, v_p, qu_s, cache_k, page_table, layer, tps=8, hps=2):
    _, bs, ts, _ = qu_s.shape
    bp, tp, _ = k_p.shape
    blk = MOBA_BLOCK
    n_pages = page_table.shape[1]
    ps = cache_k.shape[2]
    past = n_pages * ps
    chunk = 8 * blk
    assert past % chunk == 0 and past // blk >= MOBA_TOPK and n_pages % DMA_PRIORITIES == 0
    tps = min(tps, tp // blk)
    assert tp % (blk * tps) == 0 and N_HEADS % hps == 0
    nb = tp // blk
    groups = nb // tps
    hgs = N_HEADS // hps
    n_steps, items = bs * hgs, bp * N_HEADS * groups
    assert n_steps % items == 0
    every = n_steps // items

    def prompt_idx(step):
        item = step // every
        return item // (N_HEADS * groups), (item // groups) % N_HEADS, item % groups

    def qp_map(step, pt):
        bi, h, g = prompt_idx(step)
        return (0, bi, g, h)

    def kv_map(step, pt):
        bi, h, g = prompt_idx(step)
        return (bi, 0, h)

    def o_map(step, pt):
        bi, h, g = prompt_idx(step)
        return (bi, g, h)

    grid_spec = pltpu.PrefetchScalarGridSpec(
        num_scalar_prefetch=1,
        grid=(n_steps,),
        in_specs=[pl.BlockSpec((None, 1, ts, hps * HEAD_DIM), lambda s, pt: (0, s // hgs, 0, s % hgs)),
                  pl.BlockSpec(memory_space=pl.ANY),
                  pl.BlockSpec((None, 1, blk * tps, HEAD_DIM), qp_map),
                  pl.BlockSpec((1, tp, HEAD_DIM), kv_map),
                  pl.BlockSpec((1, tp, HEAD_DIM), kv_map)],
        out_specs=[pl.BlockSpec((1, hps, past // blk, ts, blk), lambda s, pt: (s // hgs, s % hgs, 0, 0, 0)),
                   pl.BlockSpec((1, N_HEADS, ts, 128), lambda s, pt: (s // hgs, 0, 0, 0)),
                   pl.BlockSpec((1, blk * tps, HEAD_DIM), o_map)],
        scratch_shapes=[pltpu.VMEM((2, hps, past, HEAD_DIM), F32),
                        pltpu.VMEM((past // blk, HEAD_DIM), F32),
                        pltpu.VMEM((N_HEADS, ts, past // blk), F32),
                        pltpu.SemaphoreType.DMA((2,)),
                        pltpu.VMEM((tp, HEAD_DIM), BF16), pltpu.VMEM((nb, HEAD_DIM + ONES_ROWS, blk), BF16),
                        pltpu.VMEM((nb, HEAD_DIM), F32)],
    )
    scores, sel, attn = pl.pallas_call(
        functools.partial(_moba_scores_kernel, layer=layer, n_pages=n_pages, chunk=chunk, nb=nb, tps=tps,
                          every=every),
        grid_spec=grid_spec,
        out_shape=[jax.ShapeDtypeStruct((bs, N_HEADS, past // blk, ts, blk), F32),
                   jax.ShapeDtypeStruct((bs, N_HEADS, ts, 128), jnp.int32),
                   jax.ShapeDtypeStruct((bp, tp, ATTN_WIDTH), F32)],
        compiler_params=_params("arbitrary"),
        name="moba_and_scores",
    )(page_table.reshape(-1), qu_s, cache_k, qu_p, k_p, v_p)
    return attn, scores, sel


def _sample_attend_kernel(pt_ref, sel_ref, s_ref, q_ref, kn_ref, vn_ref, vc_ref, o_ref,
                          vbuf_ref, ssel_ref, sem_ref, *, layer, n_pages, q_start):
    b = pl.program_id(0)
    h = pl.program_id(1)
    nbt, nh = pl.num_programs(0), pl.num_programs(1)
    step = b * nh + h
    slot = step % 2
    t = q_ref.shape[1]
    ps = vc_ref.shape[2]
    blk = MOBA_BLOCK
    ppb = blk // ps
    nsel = t * MOBA_TOPK

    def sel_block(st, e):
        return sel_ref[st * nsel + e]

    def copies(st, sl):
        bb, hh = st // nh, st % nh

        def mk(e, pg):
            return _head_page_copy(vc_ref, pt_ref, vbuf_ref.at[sl], sem_ref.at[sl], layer, bb, hh,
                                   sel_block(st, e) * ppb + pg, e * blk + pg * ps, n_pages)
        return mk

    def start_all(st, sl):
        mk = copies(st, sl)
        for e in range(nsel):
            for pg in range(ppb):
                mk(e, pg).start(priority=(e * ppb + pg) % DMA_PRIORITIES)

    @pl.when(step == 0)
    def _():
        start_all(step, slot)

    @pl.when(step + 1 < nbt * nh)
    def _():
        start_all(step + 1, 1 - slot)

    scale = HEAD_DIM ** -0.5
    slope = jnp.exp2(jnp.broadcast_to(-8.0 * (h + 1).astype(F32) / N_HEADS, (1, 1)))
    lane = lax.broadcasted_iota(jnp.int32, (1, blk), 1)

    ssel_ref[...] = jnp.full(ssel_ref.shape, NEG_INF, F32)
    for e in range(nsel):
        ti = e // MOBA_TOPK
        n = sel_block(step, e)
        raw = s_ref[0, 0, n, ti:ti + 1, :]
        dist = (q_start + ti - n * blk - lane).astype(F32)
        ssel_ref[ti:ti + 1, e * blk:(e + 1) * blk] = raw * scale - slope * dist

    q = q_ref[0].astype(BF16)
    r = lax.broadcasted_iota(jnp.int32, (t, t), 0)
    c = lax.broadcasted_iota(jnp.int32, (t, t), 1)
    s_own = lax.dot_general(q, kn_ref[0].astype(BF16), _NT, preferred_element_type=F32) * scale
    s_own = jnp.where(c <= r, s_own - slope * (r - c).astype(F32), NEG_INF)

    s_sel = ssel_ref[...]
    m = jnp.maximum(jnp.max(s_sel, axis=1, keepdims=True), jnp.max(s_own, axis=1, keepdims=True))
    p_sel = jnp.exp(s_sel - m)
    p_own = jnp.exp(s_own - m)
    l = jnp.sum(p_sel, axis=1, keepdims=True) + jnp.sum(p_own, axis=1, keepdims=True)

    mk = copies(step, slot)
    for e in range(nsel):
        for pg in range(ppb):
            mk(e, pg).wait()

    acc = jnp.dot(p_sel.astype(BF16), vbuf_ref[slot].astype(BF16), preferred_element_type=F32)
    acc = acc + jnp.dot(p_own.astype(BF16), vn_ref[0].astype(BF16), preferred_element_type=F32)
    o_ref[0] = acc / l


def _sample_attend(scores, sel, qu, k_new, v_new, cache_v, page_table, layer):
    b, t, _ = k_new.shape
    n_pages = page_table.shape[1]
    ps = cache_v.shape[2]
    past = n_pages * ps
    assert MOBA_BLOCK % ps == 0 and past % MOBA_BLOCK == 0 and t <= MOBA_BLOCK
    nsel = t * MOBA_TOPK
    grid_spec = pltpu.PrefetchScalarGridSpec(
        num_scalar_prefetch=2,
        grid=(b, N_HEADS),
        in_specs=[pl.BlockSpec((1, 1, past // MOBA_BLOCK, t, MOBA_BLOCK), lambda bi, h, pt, sl: (bi, h, 0, 0, 0)),
                  pl.BlockSpec((None, 1, t, HEAD_DIM), lambda bi, h, pt, sl: (0, bi, 0, h)),
                  pl.BlockSpec((1, t, HEAD_DIM), lambda bi, h, pt, sl: (bi, 0, h)),
                  pl.BlockSpec((1, t, HEAD_DIM), lambda bi, h, pt, sl: (bi, 0, h)),
                  pl.BlockSpec(memory_space=pl.ANY)],
        out_specs=pl.BlockSpec((1, t, HEAD_DIM), lambda bi, h, pt, sl: (bi, 0, h)),
        scratch_shapes=[pltpu.VMEM((2, nsel * MOBA_BLOCK, HEAD_DIM), F32),
                        pltpu.VMEM((t, nsel * MOBA_BLOCK), F32),
                        pltpu.SemaphoreType.DMA((2,))],
    )
    return pl.pallas_call(
        functools.partial(_sample_attend_kernel, layer=layer, n_pages=n_pages, q_start=past),
        grid_spec=grid_spec,
        out_shape=jax.ShapeDtypeStruct((b, t, ATTN_WIDTH), F32),
        compiler_params=_params("arbitrary", "arbitrary"),
        name="sample_attend",
    )(page_table.reshape(-1), sel[..., :MOBA_TOPK].reshape(-1), scores, qu, k_new, v_new, cache_v)


def _conv_taps():
    pad = CONV_HALO - (CONV_WIDTH - 1)
    return [(ph, [(j, (pad + j) // 8) for j in range(CONV_WIDTH) if (pad + j) % 8 == ph]) for ph in range(8)]


def _conv_kernel(ua_ref, ug_ref, prev_ref, w_ref, b_ref, g_ref, bl_ref, y_ref, st_ref, buf_ref, par_ref, z_ref,
                 *, tt, rows, nrows, lanes):
    ti = pl.program_id(1)
    ch = y_ref.shape[2]
    hist = CONV_WIDTH - 1
    pad = CONV_HALO - hist
    i_bias, i_gain, i_beta = CONV_WIDTH, CONV_WIDTH + 1, CONV_WIDTH + 2

    @pl.when(ti == 0)
    def _():
        buf_ref[0:CONV_HALO, :] = jnp.zeros((CONV_HALO, ch), F32)
        buf_ref[pad:CONV_HALO, :] = prev_ref[0]
        for j in range(CONV_WIDTH):
            par_ref[j] = jnp.broadcast_to(w_ref[j:j + 1, :], (8, ch))
        par_ref[i_bias] = jnp.broadcast_to(b_ref[...], (8, ch))
        par_ref[i_gain] = jnp.broadcast_to(g_ref[...], (8, ch))
        par_ref[i_beta] = jnp.broadcast_to(bl_ref[...], (8, ch))

    @pl.when(ti > 0)
    def _():
        buf_ref[0:CONV_HALO, :] = buf_ref[tt:tt + CONV_HALO, :]

    def glu(ci, carry):
        r0 = pl.multiple_of(ci * rows, rows)
        buf_ref[pl.ds(CONV_HALO + r0, rows), :] = (ua_ref[0, pl.ds(r0, rows), :]
                                                   * _sigmoid(ug_ref[0, pl.ds(r0, rows), :]))
        return carry

    lax.fori_loop(0, tt // rows, glu, 0)

    def conv(ci, carry):
        r0 = pl.multiple_of(ci * rows, rows)
        for c0 in range(0, ch, lanes):
            cs = slice(c0, c0 + lanes)
            win = buf_ref.at[pl.ds(r0, rows + CONV_HALO), pl.ds(c0, lanes)]
            acc = jnp.broadcast_to(par_ref[i_bias, :, cs][None], (rows // 8, 8, lanes))
            for ph, taps in _conv_taps():
                a_lo, a_hi = taps[0][1], taps[-1][1]
                x = win[8 * a_lo + ph:8 * a_hi + ph + rows, :]
                for j, a in taps:
                    xs = x[8 * (a - a_lo):8 * (a - a_lo) + rows, :].reshape(rows // 8, 8, lanes)
                    acc = acc + par_ref[j, :, cs][None] * xs
            z_ref[pl.ds(r0, rows), cs] = acc.reshape(rows, lanes)
        return carry

    lax.fori_loop(0, tt // rows, conv, 0)

    def norm(ci, carry):
        r0 = pl.multiple_of(ci * nrows, nrows)
        z = z_ref[pl.ds(r0, nrows), :]
        xc = z - jnp.mean(z, axis=-1, keepdims=True)
        xn = (xc * lax.rsqrt(jnp.mean(xc * xc, axis=-1, keepdims=True) + EPS)).reshape(nrows // 8, 8, ch)
        y = (xn * par_ref[i_gain][None] + par_ref[i_beta][None]).reshape(nrows, ch)
        y_ref[0, pl.ds(r0, nrows), :] = y * _sigmoid(y)
        return carry

    lax.fori_loop(0, tt // nrows, norm, 0, unroll=min(8, tt // nrows))

    @pl.when(ti == pl.num_programs(1) - 1)
    def _():
        st_ref[0] = buf_ref[tt + pad:tt + CONV_HALO, :]


def _conv_specs(qu, prev, layer, w_dw, b_dw, g_ln, b_ln, tt):
    _, b, t, ch = qu.shape
    assert ch == w_dw.shape[1]
    hist = CONV_WIDTH - 1
    assert t % tt == 0 and tt % 8 == 0
    rows = 64 if tt % 64 == 0 else 8
    nrows = 16 if tt % 16 == 0 else 8
    lanes = 256 if ch % 256 == 0 else ch
    row = lambda a: a.reshape(1, ch)
    in_specs = [pl.BlockSpec((None, 1, tt, ch), lambda bi, ti: (1, bi, ti, 0)),
                pl.BlockSpec((None, 1, tt, ch), lambda bi, ti: (2, bi, ti, 0)),
                pl.BlockSpec((None, 1, hist, ch), lambda bi, ti: (layer, bi, 0, 0)),
                pl.BlockSpec((CONV_WIDTH, ch), lambda bi, ti: (0, 0)),
                pl.BlockSpec((1, ch), lambda bi, ti: (0, 0)),
                pl.BlockSpec((1, ch), lambda bi, ti: (0, 0)),
                pl.BlockSpec((1, ch), lambda bi, ti: (0, 0))]
    out_specs = [pl.BlockSpec((1, tt, ch), lambda bi, ti: (bi, ti, 0)),
                 pl.BlockSpec((1, hist, ch), lambda bi, ti: (bi, 0, 0))]
    out_shape = [jax.ShapeDtypeStruct((b, t, ch), F32), jax.ShapeDtypeStruct((b, hist, ch), F32)]
    scratch = [pltpu.VMEM((tt + CONV_HALO, ch), F32), pltpu.VMEM((CONV_WIDTH + 3, 8, ch), F32),
               pltpu.VMEM((tt, ch), F32)]
    args = (qu, qu, prev, w_dw, row(b_dw), row(g_ln), row(b_ln))
    return (b, t // tt), in_specs, out_specs, out_shape, scratch, args, dict(tt=tt, rows=rows, nrows=nrows,
                                                                            lanes=lanes)


def _conformer_conv(qu, prev, layer, w_dw, b_dw, g_ln, b_ln, tt):
    grid, in_specs, out_specs, out_shape, scratch, args, kw = _conv_specs(qu, prev, layer, w_dw, b_dw, g_ln,
                                                                         b_ln, tt)
    return pl.pallas_call(
        functools.partial(_conv_kernel, **kw),
        grid=grid,
        in_specs=in_specs,
        out_specs=out_specs,
        out_shape=out_shape,
        scratch_shapes=scratch,
        compiler_params=_params("arbitrary", "arbitrary"),
        name="conformer_conv",
    )(*args)


def _mix_out_kernel(attn_ref, conv_ref, h_ref, ga_ref, gc_ref, w_ref, gp_ref, o_ref, *rest, emit):
    if emit:
        (wt_ref,) = rest
        wt_ref[...] = w_ref[...].astype(BF16)
        w_ref = wt_ref
    wa = attn_ref.shape[1]
    a = _rms(attn_ref[...], ga_ref[...]).astype(BF16)
    cv = _rms(conv_ref[...], gc_ref[...]).astype(BF16)
    mixed = (jnp.dot(a, w_ref[0:wa, :], preferred_element_type=F32)
             + jnp.dot(cv, w_ref[wa:, :], preferred_element_type=F32))
    o_ref[...] = h_ref[...] + _rms(mixed, gp_ref[...])


def _mix_out(attn, conv, h, g_attn, g_conv, w_out, g_post, tm, emit):
    m, wa = attn.shape
    wc = conv.shape[1]
    d = h.shape[1]
    assert m % tm == 0 and (not emit or m == tm)
    row = lambda a: a.reshape(1, -1)
    out_specs = [pl.BlockSpec((tm, d), lambda i: (i, 0))]
    out_shape = [jax.ShapeDtypeStruct((m, d), F32)]
    if emit:
        out_specs.append(pl.BlockSpec((wa + wc, d), lambda i: (0, 0)))
        out_shape.append(jax.ShapeDtypeStruct((wa + wc, d), BF16))
    outs = pl.pallas_call(
        functools.partial(_mix_out_kernel, emit=emit),
        grid=(m // tm,),
        in_specs=[pl.BlockSpec((tm, wa), lambda i: (i, 0)),
                  pl.BlockSpec((tm, wc), lambda i: (i, 0)),
                  pl.BlockSpec((tm, d), lambda i: (i, 0)),
                  pl.BlockSpec((1, wa), lambda i: (0, 0)),
                  pl.BlockSpec((1, wc), lambda i: (0, 0)),
                  pl.BlockSpec((wa + wc, d), lambda i: (0, 0)),
                  pl.BlockSpec((1, d), lambda i: (0, 0))],
        out_specs=out_specs,
        out_shape=out_shape,
        compiler_params=_params("arbitrary", vmem_limit=VMEM_LIMIT_MAX if emit else VMEM_LIMIT),
        name="mix_out",
    )(attn, conv, h, row(g_attn), row(g_conv), w_out, row(g_post))
    return outs[0], (outs[1] if emit else None)


def _xattn_kernel(h_ref, mk_ref, mv_ref, gpre_ref, wq_ref, wo_ref, gpost_ref, o_ref, *rest, emit):
    if emit:
        wq_out, wo_out = rest
        wq_out[...] = wq_ref[...].astype(BF16)
        wo_out[...] = wo_ref[...].astype(BF16)
        wq_ref, wo_ref = wq_out, wo_out
    h = h_ref[...]
    xn = _rms(h, gpre_ref[...]).astype(BF16)
    xq = jnp.dot(xn, wq_ref[...], preferred_element_type=F32)
    scale = X_HEAD_DIM ** -0.5
    seqs = mk_ref.shape[0]
    rows = h.shape[0] // seqs
    per_seq = []
    for bi in range(seqs):
        outs = []
        for hh in range(X_HEADS):
            cols = slice(hh * X_HEAD_DIM, (hh + 1) * X_HEAD_DIM)
            qh = xq[bi * rows:(bi + 1) * rows, cols].astype(BF16)
            s = lax.dot_general(qh, mk_ref[bi, :, cols].astype(BF16), _NT, preferred_element_type=F32) * scale
            p = jnp.exp(s - jnp.max(s, axis=1, keepdims=True))
            l = jnp.sum(p, axis=1, keepdims=True)
            outs.append(jnp.dot(p.astype(BF16), mv_ref[bi, :, cols].astype(BF16),
                                preferred_element_type=F32) / l)
        per_seq.append(jnp.concatenate(outs, axis=1))
    o = jnp.concatenate(per_seq, axis=0).astype(BF16)
    y = jnp.dot(o, wo_ref[...], preferred_element_type=F32)
    o_ref[...] = h + _rms(y, gpost_ref[...])


def _xattn(h, mem_k, mem_v, layer, g_pre, w_xq, w_xo, g_post, rows_per_batch, tm, emit):
    m, d = h.shape
    seqs = max(1, tm // rows_per_batch)
    tiles = max(1, rows_per_batch // tm)
    assert m % tm == 0 and tm * tiles == rows_per_batch * seqs and (not emit or m == tm)
    mem_len, xw = mem_k.shape[2:]
    row = lambda a: a.reshape(1, -1)
    out_specs = [pl.BlockSpec((tm, d), lambda i: (i, 0))]
    out_shape = [jax.ShapeDtypeStruct((m, d), F32)]
    if emit:
        out_specs += [pl.BlockSpec((d, xw), lambda i: (0, 0)), pl.BlockSpec((xw, d), lambda i: (0, 0))]
        out_shape += [jax.ShapeDtypeStruct((d, xw), BF16), jax.ShapeDtypeStruct((xw, d), BF16)]
    outs = pl.pallas_call(
        functools.partial(_xattn_kernel, emit=emit),
        grid=(m // tm,),
        in_specs=[pl.BlockSpec((tm, d), lambda i: (i, 0)),
                  pl.BlockSpec((None, seqs, mem_len, xw), lambda i: (layer, i // tiles, 0, 0)),
                  pl.BlockSpec((None, seqs, mem_len, xw), lambda i: (layer, i // tiles, 0, 0)),
                  pl.BlockSpec((1, d), lambda i: (0, 0)),
                  pl.BlockSpec((d, xw), lambda i: (0, 0)),
                  pl.BlockSpec((xw, d), lambda i: (0, 0)),
                  pl.BlockSpec((1, d), lambda i: (0, 0))],
        out_specs=out_specs,
        out_shape=out_shape,
        compiler_params=_params("arbitrary"),
        name="xattn",
    )(h, mem_k, mem_v, row(g_pre), w_xq, w_xo, row(g_post))
    return outs[0], (tuple(outs[1:]) if emit else None)


def _ffn_body(f, nf, h_ref, gpre_ref, wg_ref, wu_ref, wo_ref, gpost_ref, o_ref, rest, emit):
    if emit:
        wg_out, wu_out, wo_out, xn_ref, acc_ref = rest
        wg_out[...] = wg_ref[...].astype(BF16)
        wu_out[...] = wu_ref[...].astype(BF16)
        wo_out[...] = wo_ref[...].astype(BF16)
        wg_ref, wu_ref, wo_ref = wg_out, wu_out, wo_out
    else:
        xn_ref, acc_ref = rest

    def tile(first, last):
        if first:
            xn = _rms(h_ref[...], gpre_ref[...]).astype(BF16)
            xn_ref[...] = xn
        else:
            xn = xn_ref[...]
        g = jnp.dot(xn, wg_ref[...], preferred_element_type=F32)
        u = jnp.dot(xn, wu_ref[...], preferred_element_type=F32)
        a = (g * jax.nn.sigmoid(g) * u).astype(BF16)
        y = jnp.dot(a, wo_ref[...], preferred_element_type=F32)
        acc = y if first else acc_ref[...] + y
        if last:
            o_ref[...] = h_ref[...] + _rms(acc, gpost_ref[...])
        else:
            acc_ref[...] = acc

    assert nf >= 2
    pl.when(f == 0)(functools.partial(tile, True, False))
    pl.when((f > 0) & (f < nf - 1))(functools.partial(tile, False, False))
    pl.when(f == nf - 1)(functools.partial(tile, False, True))


def _ffn_kernel(h_ref, gpre_ref, wg_ref, wu_ref, wo_ref, gpost_ref, o_ref, *rest, emit, nf):
    _ffn_body(pl.program_id(1), nf, h_ref, gpre_ref, wg_ref, wu_ref, wo_ref, gpost_ref, o_ref, rest, emit)


def _ffn_specs(h, w, tm, tf, emit, tile):
    m, d = h.shape
    if emit:
        w_in, w_out = w
        hidden = w_out.shape[0]
        nf = hidden // tf
        assert m == tm and hidden % tf == 0 and w_in.shape[1] == 2 * hidden
        w_args = (w_in, w_in, w_out)
        w_specs = [pl.BlockSpec((d, tf), lambda *ids: (0, tile(*ids)[1])),
                   pl.BlockSpec((d, tf), lambda *ids: (0, tile(*ids)[1] + nf)),
                   pl.BlockSpec((tf, d), lambda *ids: (tile(*ids)[1], 0))]
    else:
        w_args = w
        nf, hidden = w[0].shape[0], w[2].shape[0]
        assert m % tm == 0 and w[0].shape == w[1].shape == (nf, d, tf) and hidden == nf * tf
        w_specs = [pl.BlockSpec((None, d, tf), lambda *ids: (tile(*ids)[1], 0, 0)),
                   pl.BlockSpec((None, d, tf), lambda *ids: (tile(*ids)[1], 0, 0)),
                   pl.BlockSpec((tf, d), lambda *ids: (tile(*ids)[1], 0))]
    in_specs = [pl.BlockSpec((tm, d), lambda *ids: (tile(*ids)[0], 0)),
                pl.BlockSpec((1, d), lambda *ids: (0, 0)),
                *w_specs,
                pl.BlockSpec((1, d), lambda *ids: (0, 0))]
    out_specs = [pl.BlockSpec((tm, d), lambda *ids: (tile(*ids)[0], 0))]
    out_shape = [jax.ShapeDtypeStruct((m, d), F32)]
    if emit:
        out_specs += [pl.BlockSpec((None, d, tf), lambda *ids: (tile(*ids)[1], 0, 0)),
                      pl.BlockSpec((None, d, tf), lambda *ids: (tile(*ids)[1], 0, 0)),
                      pl.BlockSpec((tf, d), lambda *ids: (tile(*ids)[1], 0))]
        out_shape += [jax.ShapeDtypeStruct((nf, d, tf), BF16), jax.ShapeDtypeStruct((nf, d, tf), BF16),
                      jax.ShapeDtypeStruct((hidden, d), BF16)]
    scratch = [pltpu.VMEM((tm, d), BF16), pltpu.VMEM((tm, d), F32)]
    return nf, in_specs, out_specs, out_shape, scratch, w_args


FFN_RING = 3


def _ffn_ring_kernel(h_ref, gpre_ref, wg_hbm, wu_hbm, wo_hbm, gpost_ref, o_ref,
                     wg_buf, wu_buf, wo_buf, sem_ref, xn_ref, acc_ref, *, nf, tf):
    i = pl.program_id(0)
    total = pl.num_programs(0) * nf
    base = i * nf

    def copies(c):
        f, s = c % nf, c % FFN_RING
        rows = pl.ds(pl.multiple_of(f * tf, tf), tf)
        return (pltpu.make_async_copy(wg_hbm.at[f], wg_buf.at[s], sem_ref.at[0, s]),
                pltpu.make_async_copy(wu_hbm.at[f], wu_buf.at[s], sem_ref.at[1, s]),
                pltpu.make_async_copy(wo_hbm.at[rows, :], wo_buf.at[s], sem_ref.at[2, s]))

    @pl.when(i == 0)
    def _():
        for c in range(FFN_RING - 1):
            for cp in copies(c):
                cp.start()

    def tile(f, first, last):
        c = base + f
        for cp in copies(c):
            cp.wait()

        @pl.when(c + FFN_RING - 1 < total)
        def _():
            for cp in copies(c + FFN_RING - 1):
                cp.start()

        s = c % FFN_RING
        if first:
            xn = _rms(h_ref[...], gpre_ref[...]).astype(BF16)
            xn_ref[...] = xn
        else:
            xn = xn_ref[...]
        g = jnp.dot(xn, wg_buf[s], preferred_element_type=F32)
        u = jnp.dot(xn, wu_buf[s], preferred_element_type=F32)
        a = (g * jax.nn.sigmoid(g) * u).astype(BF16)
        y = jnp.dot(a, wo_buf[s], preferred_element_type=F32)
        acc = y if first else acc_ref[...] + y
        if last:
            o_ref[...] = h_ref[...] + _rms(acc, gpost_ref[...])
        else:
            acc_ref[...] = acc

    tile(0, True, False)
    lax.fori_loop(1, nf - 1, lambda f, carry: (tile(f, False, False), carry)[1], 0)
    tile(nf - 1, False, True)


def _ffn_ring(h, g_pre, w, g_post, tm, tf):
    m, d = h.shape
    nf = w[0].shape[0]
    assert m % tm == 0 and nf >= FFN_RING and w[0].shape == w[1].shape == (nf, d, tf) and w[2].shape == (nf * tf, d)
    row = lambda a: a.reshape(1, -1)
    any_spec = pl.BlockSpec(memory_space=pl.ANY)
    return pl.pallas_call(
        functools.partial(_ffn_ring_kernel, nf=nf, tf=tf),
        grid=(m // tm,),
        in_specs=[pl.BlockSpec((tm, d), lambda i: (i, 0)), pl.BlockSpec((1, d), lambda i: (0, 0)),
                  any_spec, any_spec, any_spec, pl.BlockSpec((1, d), lambda i: (0, 0))],
        out_specs=pl.BlockSpec((tm, d), lambda i: (i, 0)),
        out_shape=jax.ShapeDtypeStruct((m, d), F32),
        scratch_shapes=[pltpu.VMEM((FFN_RING, d, tf), BF16), pltpu.VMEM((FFN_RING, d, tf), BF16),
                        pltpu.VMEM((FFN_RING, tf, d), BF16), pltpu.SemaphoreType.DMA((3, FFN_RING)),
                        pltpu.VMEM((tm, d), BF16), pltpu.VMEM((tm, d), F32)],
        compiler_params=_params("arbitrary"),
        name="ffn_ring",
    )(h, row(g_pre), *w, row(g_post))


def _ffn(h, g_pre, w, g_post, tm, tf, emit):
    m = h.shape[0]
    row = lambda a: a.reshape(1, -1)
    nf, in_specs, out_specs, out_shape, scratch, w_args = _ffn_specs(h, w, tm, tf, emit, lambda i, f: (i, f))
    outs = pl.pallas_call(
        functools.partial(_ffn_kernel, emit=emit, nf=nf),
        grid=(m // tm, nf),
        in_specs=in_specs,
        out_specs=out_specs,
        out_shape=out_shape,
        scratch_shapes=scratch,
        compiler_params=_params("arbitrary", "arbitrary"),
        name="ffn",
    )(h, row(g_pre), *w_args, row(g_post))
    return outs[0], tuple(outs[1:])


def _conv_ffn_kernel(*refs, conv_kw, nf):
    ci, fi, co, fo, cs, fs = refs[0:7], refs[7:13], refs[13:15], refs[15:19], refs[19:22], refs[22:24]
    step = pl.program_id(0) * pl.num_programs(1) + pl.program_id(1)

    @pl.when(step < nf)
    def _():
        _ffn_body(step, nf, *fi, fo[0], tuple(fo[1:]) + tuple(fs), True)

    _conv_kernel(*ci, *co, *cs, **conv_kw)


def _conv_and_emit_ffn(conv_args, h, g_pre, w, g_post, tf):
    grid, c_in, c_out, c_shape, c_scratch, c_args, conv_kw = _conv_specs(*conv_args)
    nt = grid[1]
    row = lambda a: a.reshape(1, -1)
    tile = lambda bi, ti: (0, jnp.minimum(bi * nt + ti, nf - 1))
    nf = w[1].shape[0] // tf
    if grid[0] * nt < nf:
        return None
    nf, f_in, f_out, f_shape, f_scratch, w_args = _ffn_specs(h, w, h.shape[0], tf, True, tile)
    outs = pl.pallas_call(
        functools.partial(_conv_ffn_kernel, conv_kw=conv_kw, nf=nf),
        grid=grid,
        in_specs=c_in + f_in,
        out_specs=c_out + f_out,
        out_shape=c_shape + f_shape,
        scratch_shapes=c_scratch + f_scratch,
        compiler_params=_params("arbitrary", "arbitrary", vmem_limit=VMEM_LIMIT_MAX),
        name="conv_and_ffn",
    )(*c_args, h, row(g_pre), *w_args, row(g_post))
    return outs[0], outs[1], outs[2], tuple(outs[3:])


def _row_tile(m):
    return 512 if m % 512 == 0 else m


def _project_in(x, wts, w_in, emit):
    b, t, d = x.shape
    m = b * t
    assert wts["w_dw"].shape[1] == ATTN_WIDTH
    k_new, v_new, qu, *w_in_b = _in_proj(x.reshape(m, d), wts["g_pre_mix"].reshape(1, d), w_in, emit,
                                         tm=1024 if m % 1024 == 0 else _row_tile(m))
    k_new, v_new = k_new.reshape(b, t, ATTN_WIDTH), v_new.reshape(b, t, ATTN_WIDTH)
    return k_new, v_new, qu.reshape(qu.shape[0], b, t, ATTN_WIDTH), (w_in_b[0] if emit else None)


def _mix_and_xattn(x, attn, conv, mem, mem_layer, wts, w, emit, x_tile):
    b, t, d = x.shape
    m = b * t
    h, w_out_b = _mix_out(attn.reshape(m, ATTN_WIDTH), conv.reshape(m, -1), x.reshape(m, d), wts["g_attn_grp"],
                          wts["g_conv_grp"], w[0], wts["g_post_mix"], tm=_row_tile(m), emit=emit)
    h, w_x_b = _xattn(h, mem[0], mem[1], mem_layer, wts["g_pre_x"], w[1], w[2], wts["g_post_x"],
                      rows_per_batch=t, tm=x_tile, emit=emit)
    return h, ((w_out_b,) + w_x_b if emit else None)


def kernel(x_prompt, x_sample, cache_k, cache_v, state_conv, cache_mem_k, cache_mem_v, page_table,
           mem_prompt, g_mem, w_mem_k, w_mem_v, g_pre_mix, w_in, w_dw, b_dw, g_ln_conv, b_ln_conv,
           g_attn_grp, g_conv_grp, w_out, g_post_mix, g_pre_x, w_xq, w_xo, g_post_x,
           g_pre_ffn, w_ffn_in, w_ffn_out, g_post_ffn):
    depth = w_in.shape[0]
    bp, tp, d = x_prompt.shape
    bs, ts, _ = x_sample.shape
    mem_len = mem_prompt.shape[1]
    xw = X_HEADS * X_HEAD_DIM
    ch = w_dw.shape[2]
    hp, hs = x_prompt, x_sample
    outs = [[] for _ in range(8)]
    for l in range(depth):
        wts = dict(g_pre_mix=g_pre_mix[l], w_dw=w_dw[l], b_dw=b_dw[l],
                   g_ln_conv=g_ln_conv[l], b_ln_conv=b_ln_conv[l], g_attn_grp=g_attn_grp[l],
                   g_conv_grp=g_conv_grp[l], g_post_mix=g_post_mix[l], g_pre_x=g_pre_x[l],
                   g_post_x=g_post_x[l], g_pre_ffn=g_pre_ffn[l], g_post_ffn=g_post_ffn[l])

        ks, vs, qu_s, w_in_b = _project_in(hs, wts, w_in[l], True)
        kp, vp, qu_p, _ = _project_in(hp, wts, w_in_b, False)
        attn_p, scores, sel = _moba_and_scores(qu_p, kp, vp, qu_s, cache_k, page_table, l)
        attn_s = _sample_attend(scores, sel, qu_s, ks, vs, cache_v, page_table, l)

        conv_w = (wts["w_dw"], wts["b_dw"], wts["g_ln_conv"], wts["b_ln_conv"])
        conv_s, cs = _conformer_conv(qu_s, state_conv, l, *conv_w, ts)
        sample_mem = (cache_mem_k.reshape(depth, bs, mem_len, xw), cache_mem_v.reshape(depth, bs, mem_len, xw))
        hs_mid, w_mid_b = _mix_and_xattn(hs, attn_s, conv_s, sample_mem, l, wts, (w_out[l], w_xq[l], w_xo[l]),
                                         True, x_tile=bs * ts)

        conv_args = (qu_p, jnp.zeros((1, bp, CONV_WIDTH - 1, ch), F32), 0, *conv_w, _row_tile(tp))
        ffn_w = (w_ffn_in[l], w_ffn_out[l])
        fused = _conv_and_emit_ffn(conv_args, hs_mid, wts["g_pre_ffn"], ffn_w, wts["g_post_ffn"], tf=512)
        if fused is None:
            conv_p, cp = _conformer_conv(*conv_args)
            hs_out, w_ffn_b = _ffn(hs_mid, wts["g_pre_ffn"], ffn_w, wts["g_post_ffn"], tm=bs * ts, tf=512,
                                   emit=True)
        else:
            conv_p, cp, hs_out, w_ffn_b = fused
        hs = hs_out.reshape(bs, ts, d)

        w_mem = jnp.concatenate([w_mem_k[l], w_mem_v[l]], axis=1).astype(BF16)
        mk_p, mv_p = _norm_matmul(mem_prompt.reshape(bp * mem_len, d), g_mem[l].reshape(1, d), w_mem,
                                  (xw, xw), tn=xw, tm=_row_tile(bp * mem_len))
        mk_p, mv_p = mk_p.reshape(bp, mem_len, xw), mv_p.reshape(bp, mem_len, xw)
        hp_mid, _ = _mix_and_xattn(hp, attn_p, conv_p, (mk_p[None], mv_p[None]), 0, wts, w_mid_b, False,
                                   x_tile=_row_tile(tp))
        hp_out = _ffn_ring(hp_mid, wts["g_pre_ffn"], w_ffn_b, wts["g_post_ffn"], tm=_row_tile(bp * tp), tf=512)
        hp = hp_out.reshape(bp, tp, d)
        mem_shape = (bp, mem_len, X_HEADS, X_HEAD_DIM)
        heads = lambda a: a.reshape(a.shape[:2] + (N_HEADS, HEAD_DIM))
        for lst, a in zip(outs, (heads(kp), heads(vp), cp, mk_p.reshape(mem_shape), mv_p.reshape(mem_shape),
                                 heads(ks), heads(vs), cs)):
            lst.append(a)
    return (hp, hs) + tuple(jnp.stack(lst, 0) for lst in outs)
```

```python
import functools

import jax
import jax.numpy as jnp
from jax import lax
from jax.experimental import pallas as pl
from jax.experimental.pallas import tpu as pltpu

EPS = 1e-6
N_HEADS = 8
HEAD_DIM = 128
ATTN_WIDTH = N_HEADS * HEAD_DIM
CONV_WIDTH = 31
MOBA_BLOCK = 256
MOBA_TOPK = 3
X_HEADS = 4
X_HEAD_DIM = 128

CONV_HALO = 32
DMA_PRIORITIES = 2
ONES_ROWS = 16
VMEM_LIMIT = 56 * 1024 * 1024
VMEM_LIMIT_MAX = 60 * 1024 * 1024

F32 = jnp.float32
BF16 = jnp.bfloat16
NEG_INF = float("-inf")
LOG2_E = 1.4426950408889634
_NT = (((1,), (1,)), ((), ()))


def _params(*sem, vmem_limit=VMEM_LIMIT):
    return pltpu.CompilerParams(dimension_semantics=sem, vmem_limit_bytes=vmem_limit)


def _rms(x, g):
    return x * lax.rsqrt(jnp.mean(x * x, axis=-1, keepdims=True) + EPS) * g


def _sigmoid(x):
    return 0.5 * jnp.tanh(0.5 * x) + 0.5


def _top_blocks(gate, n_valid, axis):
    nb = gate.shape[axis]
    pos = lax.broadcasted_iota(jnp.int32, gate.shape, axis)
    g = jnp.where(pos < n_valid, gate, NEG_INF)
    idxs, oks = [], []
    for _ in range(MOBA_TOPK):
        m = jnp.max(g, axis=axis, keepdims=True)
        idx = jnp.min(jnp.where(g == m, pos, nb), axis=axis, keepdims=True)
        ok = m > NEG_INF
        idxs.append(idx)
        oks.append(ok)
        g = jnp.where((pos == idx) & ok, NEG_INF, g)
    return idxs, oks


def _norm_matmul_kernel(x_ref, g_ref, w_ref, *rest, tile_ranges):
    out_refs, xn_ref = rest[:-1], rest[-1]
    j = pl.program_id(1)

    @pl.when(j == 0)
    def _():
        xn_ref[...] = _rms(x_ref[...], g_ref[...]).astype(BF16)

    for o_ref, (lo, hi) in zip(out_refs, tile_ranges):
        @pl.when((j >= lo) & (j < hi))
        def _(o_ref=o_ref):
            o_ref[...] = jnp.dot(xn_ref[...], w_ref[...], preferred_element_type=F32)


def _norm_matmul(x, g, w, widths, tn, tm):
    m, d = x.shape
    n = w.shape[1]
    assert sum(widths) == n and all(wd % tn == 0 for wd in widths) and m % tm == 0
    tile_ranges, lo = [], 0
    for wd in widths:
        tile_ranges.append((lo, lo + wd // tn))
        lo += wd // tn

    def out_map(i, j, lo, cnt):
        return (i, jnp.clip(j - lo, 0, cnt - 1))

    out_specs = [pl.BlockSpec((tm, tn), functools.partial(out_map, lo=lo, cnt=hi - lo))
                 for lo, hi in tile_ranges]
    return pl.pallas_call(
        functools.partial(_norm_matmul_kernel, tile_ranges=tuple(tile_ranges)),
        grid=(m // tm, n // tn),
        in_specs=[pl.BlockSpec((tm, d), lambda i, j: (i, 0)),
                  pl.BlockSpec((1, d), lambda i, j: (0, 0)),
                  pl.BlockSpec((d, tn), lambda i, j: (0, j))],
        out_specs=out_specs,
        out_shape=[jax.ShapeDtypeStruct((m, wd), F32) for wd in widths],
        scratch_shapes=[pltpu.VMEM((tm, d), BF16)],
        compiler_params=_params("arbitrary", "arbitrary"),
        name="norm_matmul",
    )(x, g, w)


def _in_proj_src_tile(j):
    return jnp.where(j < 3, (j + 1) % 3, j)


def _in_proj_kernel(x_ref, g_ref, w_ref, *rest, emit):
    if emit:
        k_ref, v_ref, qu_ref, wt_ref, xn_ref = rest
    else:
        k_ref, v_ref, qu_ref, xn_ref = rest
    j = pl.program_id(1)

    if emit:
        wt_ref[...] = w_ref[...].astype(BF16)
        w_ref = wt_ref

    for o_ref, cond, first in ((k_ref, j == 0, True), (v_ref, j == 1, False), (qu_ref, j >= 2, False)):
        @pl.when(cond)
        def _(o_ref=o_ref, first=first):
            if first:
                xn = _rms(x_ref[...], g_ref[...]).astype(BF16)
                xn_ref[...] = xn
            else:
                xn = xn_ref[...]
            o_ref[...] = jnp.dot(xn, w_ref[...], preferred_element_type=F32)


def _in_proj(x, g, w, emit, tm):
    m, d = x.shape
    tn = ATTN_WIDTH
    nt = (w.shape[1] // tn) if emit else w.shape[0]
    assert m % tm == 0 and nt > 3 and (not emit or (m == tm and w.shape[1] % tn == 0))
    if emit:
        w_spec = pl.BlockSpec((d, tn), lambda i, j: (0, _in_proj_src_tile(j)))
    else:
        w_spec = pl.BlockSpec((None, d, tn), lambda i, j: (j, 0, 0))
    out_specs = [pl.BlockSpec((tm, tn), lambda i, j: (i, 0)),
                 pl.BlockSpec((tm, tn), lambda i, j: (i, 0)),
                 pl.BlockSpec((None, tm, tn), lambda i, j: (jnp.maximum(j - 2, 0), i, 0))]
    out_shape = [jax.ShapeDtypeStruct((m, tn), F32), jax.ShapeDtypeStruct((m, tn), F32),
                 jax.ShapeDtypeStruct((nt - 2, m, tn), F32)]
    if emit:
        out_specs.append(pl.BlockSpec((None, d, tn), lambda i, j: (j, 0, 0)))
        out_shape.append(jax.ShapeDtypeStruct((nt, d, tn), BF16))
    return pl.pallas_call(
        functools.partial(_in_proj_kernel, emit=emit),
        grid=(m // tm, nt),
        in_specs=[pl.BlockSpec((tm, d), lambda i, j: (i, 0)),
                  pl.BlockSpec((1, d), lambda i, j: (0, 0)),
                  w_spec],
        out_specs=out_specs,
        out_shape=out_shape,
        scratch_shapes=[pltpu.VMEM((tm, d), BF16)],
        compiler_params=_params("arbitrary", "arbitrary"),
        name="in_proj",
    )(x, g, w)


def _moba_prompt_body(h, g, q_ref, k_ref, v_ref, o_ref, kb_ref, vt_ref, km_ref, *, nb, tps):
    blk = MOBA_BLOCK

    def stage_keys():
        for n in range(nb):
            kf = k_ref[0, n * blk:(n + 1) * blk, :]
            kb_ref[n * blk:(n + 1) * blk, :] = kf.astype(BF16)
            vt_ref[n, 0:HEAD_DIM, :] = v_ref[0, n * blk:(n + 1) * blk, :].T.astype(BF16)
            vt_ref[n, HEAD_DIM:, :] = jnp.ones((ONES_ROWS, blk), BF16)
            km_ref[n:n + 1, :] = jnp.mean(kf, axis=0, keepdims=True)

    scale = HEAD_DIM ** -0.5 * LOG2_E
    slope = jnp.exp2(jnp.broadcast_to(-8.0 * (h + 1).astype(F32) / N_HEADS, (1, blk))) * LOG2_E

    def attend(i, qi):
        q = q_ref[0, qi * blk:(qi + 1) * blk, :]
        qb = q.astype(BF16)
        kr = lax.broadcasted_iota(jnp.int32, (blk, blk), 0)
        qc = lax.broadcasted_iota(jnp.int32, (blk, blk), 1)
        base = slope * (qc - kr).astype(F32)
        s_own = lax.dot_general(kb_ref[i * blk:(i + 1) * blk, :], qb, _NT, preferred_element_type=F32)
        ts = [jnp.where(kr <= qc, s_own * scale - base, NEG_INF)]
        if i:
            gate = lax.dot_general(km_ref[0:i, :], q, _NT, precision=lax.Precision.HIGHEST,
                                   preferred_element_type=F32)
            idxs, oks = _top_blocks(gate, i, axis=0)
            row = lax.broadcasted_iota(jnp.int32, (i, blk), 0)
            sel = jnp.zeros((i, blk), F32)
            for idx, ok in zip(idxs, oks):
                sel = jnp.where((row == idx) & ok, 1.0, sel)
            for n in range(i):
                s = lax.dot_general(kb_ref[n * blk:(n + 1) * blk, :], qb, _NT, preferred_element_type=F32)
                neg = jnp.where(sel[n:n + 1, :] > 0.0, -slope * float((i - n) * blk), NEG_INF)
                ts.append(s * scale - base + neg)
        m = jnp.max(functools.reduce(jnp.maximum, ts), axis=0, keepdims=True)
        acc = None
        for n, t in enumerate(ts):
            p = jnp.exp2(t - m)
            vt = vt_ref[i] if n == 0 else vt_ref[n - 1]
            pv = jnp.dot(vt, p.astype(BF16), preferred_element_type=F32)
            acc = pv if acc is None else acc + pv
        l = acc[HEAD_DIM:HEAD_DIM + 1, :]
        o_ref[0, qi * blk:(qi + 1) * blk, :] = (acc[0:HEAD_DIM, :] / l).T

    for v in range(nb // tps):
        @pl.when(g == v)
        def _(v=v):
            if v == 0:
                stage_keys()
            for qi in range(tps):
                attend(v * tps + qi, qi)


def _head_page_copy(cache_ref, pt_ref, buf_ref, sem, layer, b, h, page_slot, dst_row, n_pages):
    page = pt_ref[b * n_pages + page_slot]
    ps = cache_ref.shape[2]
    return pltpu.make_async_copy(cache_ref.at[layer, page, :, h, :], buf_ref.at[pl.ds(dst_row, ps), :], sem)


def _start_k_pages(st, pt_ref, kc_ref, kbuf_ref, sem_ref, *, layer, n_pages):
    hps = kbuf_ref.shape[1]
    groups = N_HEADS // hps
    bb, hg = st // groups, st % groups
    sl = st % 2
    ps = kc_ref.shape[2]

    def issue(pair, carry):
        for k in range(DMA_PRIORITIES):
            p = pair * DMA_PRIORITIES + k
            for hh in range(hps):
                _head_page_copy(kc_ref, pt_ref, kbuf_ref.at[sl, hh], sem_ref.at[sl], layer, bb, hg * hps + hh,
                                p, pl.multiple_of(p * ps, ps), n_pages).start(priority=k)
        return carry

    lax.fori_loop(0, n_pages // DMA_PRIORITIES, issue, 0, unroll=4)


def _sample_scores_body(step, n_steps, pt_ref, q_ref, kc_ref, s_ref, sel_ref, kbuf_ref, km_ref, gate_ref, sem_ref,
                        *, layer, n_pages, chunk):
    nh = N_HEADS
    hps = kbuf_ref.shape[1]
    groups = nh // hps
    hg = step % groups
    slot = step % 2
    ps = kc_ref.shape[2]
    past = n_pages * ps
    blk = MOBA_BLOCK
    t = q_ref.shape[1]
    start_pages = functools.partial(_start_k_pages, pt_ref=pt_ref, kc_ref=kc_ref, kbuf_ref=kbuf_ref,
                                    sem_ref=sem_ref, layer=layer, n_pages=n_pages)

    @pl.when(step == 0)
    def _():
        start_pages(step)

    @pl.when(step + 1 < n_steps)
    def _():
        start_pages(step + 1)

    pltpu.make_async_copy(kbuf_ref.at[1 - slot], kbuf_ref.at[slot], sem_ref.at[slot]).wait()

    bpc = chunk // blk
    for hh in range(hps):
        q = q_ref[0, :, hh * HEAD_DIM:(hh + 1) * HEAD_DIM]
        qb = q.astype(BF16)

        def body(ci, carry, hh=hh, qb=qb):
            start = pl.multiple_of(ci * chunk, chunk)
            kf = kbuf_ref[slot, hh, pl.ds(start, chunk), :]
            sc = lax.dot_general(qb, kf.astype(BF16), _NT, preferred_element_type=F32)
            for jj in range(bpc):
                s_ref[0, hh, ci * bpc + jj] = sc[:, jj * blk:(jj + 1) * blk]
            km_ref[pl.ds(pl.multiple_of(ci * bpc, bpc), bpc), :] = jnp.mean(
                kf.reshape(bpc, blk, HEAD_DIM), axis=1)
            return carry

        lax.fori_loop(0, past // chunk, body, 0, unroll=True)

        gate_ref[hg * hps + hh] = lax.dot_general(q, km_ref[...], _NT, precision=lax.Precision.HIGHEST,
                                                  preferred_element_type=F32)

    @pl.when(hg == groups - 1)
    def _():
        gate = gate_ref[...].reshape(nh * t, past // blk)
        idxs, _ = _top_blocks(gate, past // blk, axis=1)
        lane = lax.broadcasted_iota(jnp.int32, (nh * t, 128), 1)
        out = jnp.zeros((nh * t, 128), jnp.int32)
        for k, idx in enumerate(idxs):
            out = jnp.where(lane == k, idx, out)
        sel_ref[0] = out.reshape(nh, t, 128)


def _moba_scores_kernel(pt_ref, qs_ref, kc_ref, qp_ref, kp_ref, vp_ref, s_ref, sel_ref, o_ref,
                        kbuf_ref, km_ref, gate_ref, sem_ref, kb_ref, vt_ref, kmp_ref,
                        *, layer, n_pages, chunk, nb, tps, every):
    step = pl.program_id(0)
    n_steps = pl.num_programs(0)
    _sample_scores_body(step, n_steps, pt_ref, qs_ref, kc_ref, s_ref, sel_ref, kbuf_ref, km_ref,
                        gate_ref, sem_ref, layer=layer, n_pages=n_pages, chunk=chunk)

    @pl.when(step % every == 0)
    def _():
        item = step // every
        groups = nb // tps
        _moba_prompt_body((item // groups) % N_HEADS, item % groups, qp_ref, kp_ref, vp_ref, o_ref,
                          kb_ref, vt_ref, kmp_ref, nb=nb, tps=tps)


def _moba_and_scores(qu_p, k_p, v_p, qu_s, cache_k, page_table, layer, tps=8, hps=2):
    _, bs, ts, _ = qu_s.shape
    bp, tp, _ = k_p.shape
    blk = MOBA_BLOCK
    n_pages = page_table.shape[1]
    ps = cache_k.shape[2]
    past = n_pages * ps
    chunk = 8 * blk
    assert past % chunk == 0 and past // blk >= MOBA_TOPK and n_pages % DMA_PRIORITIES == 0
    tps = min(tps, tp // blk)
    assert tp % (blk * tps) == 0 and N_HEADS % hps == 0
    nb = tp // blk
    groups = nb // tps
    hgs = N_HEADS // hps
    n_steps, items = bs * hgs, bp * N_HEADS * groups
    assert n_steps % items == 0
    every = n_steps // items

    def prompt_idx(step):
        item = step // every
        return item // (N_HEADS * groups), (item // groups) % N_HEADS, item % groups

    def qp_map(step, pt):
        bi, h, g = prompt_idx(step)
        return (0, bi, g, h)

    def kv_map(step, pt):
        bi, h, g = prompt_idx(step)
        return (bi, 0, h)

    def o_map(step, pt):
        bi, h, g = prompt_idx(step)
        return (bi, g, h)

    grid_spec = pltpu.PrefetchScalarGridSpec(
        num_scalar_prefetch=1,
        grid=(n_steps,),
        in_specs=[pl.BlockSpec((None, 1, ts, hps * HEAD_DIM), lambda s, pt: (0, s // hgs, 0, s % hgs)),
                  pl.BlockSpec(memory_space=pl.ANY),
                  pl.BlockSpec((None, 1, blk * tps, HEAD_DIM), qp_map),
                  pl.BlockSpec((1, tp, HEAD_DIM), kv_map),
                  pl.BlockSpec((1, tp, HEAD_DIM), kv_map)],
        out_specs=[pl.BlockSpec((1, hps, past // blk, ts, blk), lambda s, pt: (s // hgs, s % hgs, 0, 0, 0)),
                   pl.BlockSpec((1, N_HEADS, ts, 128), lambda s, pt: (s // hgs, 0, 0, 0)),
                   pl.BlockSpec((1, blk * tps, HEAD_DIM), o_map)],
        scratch_shapes=[pltpu.VMEM((2, hps, past, HEAD_DIM), F32),
                        pltpu.VMEM((past // blk, HEAD_DIM), F32),
                        pltpu.VMEM((N_HEADS, ts, past // blk), F32),
                        pltpu.SemaphoreType.DMA((2,)),
                        pltpu.VMEM((tp, HEAD_DIM), BF16), pltpu.VMEM((nb, HEAD_DIM + ONES_ROWS, blk), BF16),
                        pltpu.VMEM((nb, HEAD_DIM), F32)],
    )
    scores, sel, attn = pl.pallas_call(
        functools.partial(_moba_scores_kernel, layer=layer, n_pages=n_pages, chunk=chunk, nb=nb, tps=tps,
                          every=every),
        grid_spec=grid_spec,
        out_shape=[jax.ShapeDtypeStruct((bs, N_HEADS, past // blk, ts, blk), F32),
                   jax.ShapeDtypeStruct((bs, N_HEADS, ts, 128), jnp.int32),
                   jax.ShapeDtypeStruct((bp, tp, ATTN_WIDTH), F32)],
        compiler_params=_params("arbitrary"),
        name="moba_and_scores",
    )(page_table.reshape(-1), qu_s, cache_k, qu_p, k_p, v_p)
    return attn, scores, sel


def _sample_attend_kernel(pt_ref, sel_ref, s_ref, q_ref, kn_ref, vn_ref, vc_ref, o_ref,
                          vbuf_ref, ssel_ref, sem_ref, *, layer, n_pages, q_start):
    b = pl.program_id(0)
    h = pl.program_id(1)
    nbt, nh = pl.num_programs(0), pl.num_programs(1)
    step = b * nh + h
    slot = step % 2
    t = q_ref.shape[1]
    ps = vc_ref.shape[2]
    blk = MOBA_BLOCK
    ppb = blk // ps
    nsel = t * MOBA_TOPK

    def sel_block(st, e):
        return sel_ref[st * nsel + e]

    def copies(st, sl):
        bb, hh = st // nh, st % nh

        def mk(e, pg):
            return _head_page_copy(vc_ref, pt_ref, vbuf_ref.at[sl], sem_ref.at[sl], layer, bb, hh,
                                   sel_block(st, e) * ppb + pg, e * blk + pg * ps, n_pages)
        return mk

    def start_all(st, sl):
        mk = copies(st, sl)
        for e in range(nsel):
            for pg in range(ppb):
                mk(e, pg).start(priority=(e * ppb + pg) % DMA_PRIORITIES)

    @pl.when(step == 0)
    def _():
        start_all(step, slot)

    @pl.when(step + 1 < nbt * nh)
    def _():
        start_all(step + 1, 1 - slot)

    scale = HEAD_DIM ** -0.5
    slope = jnp.exp2(jnp.broadcast_to(-8.0 * (h + 1).astype(F32) / N_HEADS, (1, 1)))
    lane = lax.broadcasted_iota(jnp.int32, (1, blk), 1)

    ssel_ref[...] = jnp.full(ssel_ref.shape, NEG_INF, F32)
    for e in range(nsel):
        ti = e // MOBA_TOPK
        n = sel_block(step, e)
        raw = s_ref[0, 0, n, ti:ti + 1, :]
        dist = (q_start + ti - n * blk - lane).astype(F32)
        ssel_ref[ti:ti + 1, e * blk:(e + 1) * blk] = raw * scale - slope * dist

    q = q_ref[0].astype(BF16)
    r = lax.broadcasted_iota(jnp.int32, (t, t), 0)
    c = lax.broadcasted_iota(jnp.int32, (t, t), 1)
    s_own = lax.dot_general(q, kn_ref[0].astype(BF16), _NT, preferred_element_type=F32) * scale
    s_own = jnp.where(c <= r, s_own - slope * (r - c).astype(F32), NEG_INF)

    s_sel = ssel_ref[...]
    m = jnp.maximum(jnp.max(s_sel, axis=1, keepdims=True), jnp.max(s_own, axis=1, keepdims=True))
    p_sel = jnp.exp(s_sel - m)
    p_own = jnp.exp(s_own - m)
    l = jnp.sum(p_sel, axis=1, keepdims=True) + jnp.sum(p_own, axis=1, keepdims=True)

    mk = copies(step, slot)
    for e in range(nsel):
        for pg in range(ppb):
            mk(e, pg).wait()

    acc = jnp.dot(p_sel.astype(BF16), vbuf_ref[slot].astype(BF16), preferred_element_type=F32)
    acc = acc + jnp.dot(p_own.astype(BF16), vn_ref[0].astype(BF16), preferred_element_type=F32)
    o_ref[0] = acc / l


def _sample_attend(scores, sel, qu, k_new, v_new, cache_v, page_table, layer):
    b, t, _ = k_new.shape
    n_pages = page_table.shape[1]
    ps = cache_v.shape[2]
    past = n_pages * ps
    assert MOBA_BLOCK % ps == 0 and past % MOBA_BLOCK == 0 and t <= MOBA_BLOCK
    nsel = t * MOBA_TOPK
    grid_spec = pltpu.PrefetchScalarGridSpec(
        num_scalar_prefetch=2,
        grid=(b, N_HEADS),
        in_specs=[pl.BlockSpec((1, 1, past // MOBA_BLOCK, t, MOBA_BLOCK), lambda bi, h, pt, sl: (bi, h, 0, 0, 0)),
                  pl.BlockSpec((None, 1, t, HEAD_DIM), lambda bi, h, pt, sl: (0, bi, 0, h)),
                  pl.BlockSpec((1, t, HEAD_DIM), lambda bi, h, pt, sl: (bi, 0, h)),
                  pl.BlockSpec((1, t, HEAD_DIM), lambda bi, h, pt, sl: (bi, 0, h)),
                  pl.BlockSpec(memory_space=pl.ANY)],
        out_specs=pl.BlockSpec((1, t, HEAD_DIM), lambda bi, h, pt, sl: (bi, 0, h)),
        scratch_shapes=[pltpu.VMEM((2, nsel * MOBA_BLOCK, HEAD_DIM), F32),
                        pltpu.VMEM((t, nsel * MOBA_BLOCK), F32),
                        pltpu.SemaphoreType.DMA((2,))],
    )
    return pl.pallas_call(
        functools.partial(_sample_attend_kernel, layer=layer, n_pages=n_pages, q_start=past),
        grid_spec=grid_spec,
        out_shape=jax.ShapeDtypeStruct((b, t, ATTN_WIDTH), F32),
        compiler_params=_params("arbitrary", "arbitrary"),
        name="sample_attend",
    )(page_table.reshape(-1), sel[..., :MOBA_TOPK].reshape(-1), scores, qu, k_new, v_new, cache_v)


def _conv_taps():
    pad = CONV_HALO - (CONV_WIDTH - 1)
    return [(ph, [(j, (pad + j) // 8) for j in range(CONV_WIDTH) if (pad + j) % 8 == ph]) for ph in range(8)]


def _conv_kernel(ua_ref, ug_ref, prev_ref, w_ref, b_ref, g_ref, bl_ref, y_ref, st_ref, buf_ref, par_ref, z_ref,
                 *, tt, rows, nrows, lanes):
    ti = pl.program_id(1)
    ch = y_ref.shape[2]
    hist = CONV_WIDTH - 1
    pad = CONV_HALO - hist
    i_bias, i_gain, i_beta = CONV_WIDTH, CONV_WIDTH + 1, CONV_WIDTH + 2

    @pl.when(ti == 0)
    def _():
        buf_ref[0:CONV_HALO, :] = jnp.zeros((CONV_HALO, ch), F32)
        buf_ref[pad:CONV_HALO, :] = prev_ref[0]
        for j in range(CONV_WIDTH):
            par_ref[j] = jnp.broadcast_to(w_ref[j:j + 1, :], (8, ch))
        par_ref[i_bias] = jnp.broadcast_to(b_ref[...], (8, ch))
        par_ref[i_gain] = jnp.broadcast_to(g_ref[...], (8, ch))
        par_ref[i_beta] = jnp.broadcast_to(bl_ref[...], (8, ch))

    @pl.when(ti > 0)
    def _():
        buf_ref[0:CONV_HALO, :] = buf_ref[tt:tt + CONV_HALO, :]

    def glu(ci, carry):
        r0 = pl.multiple_of(ci * rows, rows)
        buf_ref[pl.ds(CONV_HALO + r0, rows), :] = (ua_ref[0, pl.ds(r0, rows), :]
                                                   * _sigmoid(ug_ref[0, pl.ds(r0, rows), :]))
        return carry

    lax.fori_loop(0, tt // rows, glu, 0)

    def conv(ci, carry):
        r0 = pl.multiple_of(ci * rows, rows)
        for c0 in range(0, ch, lanes):
            cs = slice(c0, c0 + lanes)
            win = buf_ref.at[pl.ds(r0, rows + CONV_HALO), pl.ds(c0, lanes)]
            acc = jnp.broadcast_to(par_ref[i_bias, :, cs][None], (rows // 8, 8, lanes))
            for ph, taps in _conv_taps():
                a_lo, a_hi = taps[0][1], taps[-1][1]
                x = win[8 * a_lo + ph:8 * a_hi + ph + rows, :]
                for j, a in taps:
                    xs = x[8 * (a - a_lo):8 * (a - a_lo) + rows, :].reshape(rows // 8, 8, lanes)
                    acc = acc + par_ref[j, :, cs][None] * xs
            z_ref[pl.ds(r0, rows), cs] = acc.reshape(rows, lanes)
        return carry

    lax.fori_loop(0, tt // rows, conv, 0)

    def norm(ci, carry):
        r0 = pl.multiple_of(ci * nrows, nrows)
        z = z_ref[pl.ds(r0, nrows), :]
        xc = z - jnp.mean(z, axis=-1, keepdims=True)
        xn = (xc * lax.rsqrt(jnp.mean(xc * xc, axis=-1, keepdims=True) + EPS)).reshape(nrows // 8, 8, ch)
        y = (xn * par_ref[i_gain][None] + par_ref[i_beta][None]).reshape(nrows, ch)
        y_ref[0, pl.ds(r0, nrows), :] = y * _sigmoid(y)
        return carry

    lax.fori_loop(0, tt // nrows, norm, 0, unroll=min(8, tt // nrows))

    @pl.when(ti == pl.num_programs(1) - 1)
    def _():
        st_ref[0] = buf_ref[tt + pad:tt + CONV_HALO, :]


def _conv_specs(qu, prev, layer, w_dw, b_dw, g_ln, b_ln, tt):
    _, b, t, ch = qu.shape
    assert ch == w_dw.shape[1]
    hist = CONV_WIDTH - 1
    assert t % tt == 0 and tt % 8 == 0
    rows = 64 if tt % 64 == 0 else 8
    nrows = 16 if tt % 16 == 0 else 8
    lanes = 256 if ch % 256 == 0 else ch
    row = lambda a: a.reshape(1, ch)
    in_specs = [pl.BlockSpec((None, 1, tt, ch), lambda bi, ti: (1, bi, ti, 0)),
                pl.BlockSpec((None, 1, tt, ch), lambda bi, ti: (2, bi, ti, 0)),
                pl.BlockSpec((None, 1, hist, ch), lambda bi, ti: (layer, bi, 0, 0)),
                pl.BlockSpec((CONV_WIDTH, ch), lambda bi, ti: (0, 0)),
                pl.BlockSpec((1, ch), lambda bi, ti: (0, 0)),
                pl.BlockSpec((1, ch), lambda bi, ti: (0, 0)),
                pl.BlockSpec((1, ch), lambda bi, ti: (0, 0))]
    out_specs = [pl.BlockSpec((1, tt, ch), lambda bi, ti: (bi, ti, 0)),
                 pl.BlockSpec((1, hist, ch), lambda bi, ti: (bi, 0, 0))]
    out_shape = [jax.ShapeDtypeStruct((b, t, ch), F32), jax.ShapeDtypeStruct((b, hist, ch), F32)]
    scratch = [pltpu.VMEM((tt + CONV_HALO, ch), F32), pltpu.VMEM((CONV_WIDTH + 3, 8, ch), F32),
               pltpu.VMEM((tt, ch), F32)]
    args = (qu, qu, prev, w_dw, row(b_dw), row(g_ln), row(b_ln))
    return (b, t // tt), in_specs, out_specs, out_shape, scratch, args, dict(tt=tt, rows=rows, nrows=nrows,
                                                                            lanes=lanes)


def _conformer_conv(qu, prev, layer, w_dw, b_dw, g_ln, b_ln, tt):
    grid, in_specs, out_specs, out_shape, scratch, args, kw = _conv_specs(qu, prev, layer, w_dw, b_dw, g_ln,
                                                                         b_ln, tt)
    return pl.pallas_call(
        functools.partial(_conv_kernel, **kw),
        grid=grid,
        in_specs=in_specs,
        out_specs=out_specs,
        out_shape=out_shape,
        scratch_shapes=scratch,
        compiler_params=_params("arbitrary", "arbitrary"),
        name="conformer_conv",
    )(*args)


def _mix_out_kernel(attn_ref, conv_ref, h_ref, ga_ref, gc_ref, w_ref, gp_ref, o_ref, *rest, emit):
    if emit:
        (wt_ref,) = rest
        wt_ref[...] = w_ref[...].astype(BF16)
        w_ref = wt_ref
    wa = attn_ref.shape[1]
    a = _rms(attn_ref[...], ga_ref[...]).astype(BF16)
    cv = _rms(conv_ref[...], gc_ref[...]).astype(BF16)
    mixed = (jnp.dot(a, w_ref[0:wa, :], preferred_element_type=F32)
             + jnp.dot(cv, w_ref[wa:, :], preferred_element_type=F32))
    o_ref[...] = h_ref[...] + _rms(mixed, gp_ref[...])


def _mix_out(attn, conv, h, g_attn, g_conv, w_out, g_post, tm, emit):
    m, wa = attn.shape
    wc = conv.shape[1]
    d = h.shape[1]
    assert m % tm == 0 and (not emit or m == tm)
    row = lambda a: a.reshape(1, -1)
    out_specs = [pl.BlockSpec((tm, d), lambda i: (i, 0))]
    out_shape = [jax.ShapeDtypeStruct((m, d), F32)]
    if emit:
        out_specs.append(pl.BlockSpec((wa + wc, d), lambda i: (0, 0)))
        out_shape.append(jax.ShapeDtypeStruct((wa + wc, d), BF16))
    outs = pl.pallas_call(
        functools.partial(_mix_out_kernel, emit=emit),
        grid=(m // tm,),
        in_specs=[pl.BlockSpec((tm, wa), lambda i: (i, 0)),
                  pl.BlockSpec((tm, wc), lambda i: (i, 0)),
                  pl.BlockSpec((tm, d), lambda i: (i, 0)),
                  pl.BlockSpec((1, wa), lambda i: (0, 0)),
                  pl.BlockSpec((1, wc), lambda i: (0, 0)),
                  pl.BlockSpec((wa + wc, d), lambda i: (0, 0)),
                  pl.BlockSpec((1, d), lambda i: (0, 0))],
        out_specs=out_specs,
        out_shape=out_shape,
        compiler_params=_params("arbitrary", vmem_limit=VMEM_LIMIT_MAX if emit else VMEM_LIMIT),
        name="mix_out",
    )(attn, conv, h, row(g_attn), row(g_conv), w_out, row(g_post))
    return outs[0], (outs[1] if emit else None)


def _xattn_kernel(h_ref, mk_ref, mv_ref, gpre_ref, wq_ref, wo_ref, gpost_ref, o_ref, *rest, emit):
    if emit:
        wq_out, wo_out = rest
        wq_out[...] = wq_ref[...].astype(BF16)
        wo_out[...] = wo_ref[...].astype(BF16)
        wq_ref, wo_ref = wq_out, wo_out
    h = h_ref[...]
    xn = _rms(h, gpre_ref[...]).astype(BF16)
    xq = jnp.dot(xn, wq_ref[...], preferred_element_type=F32)
    scale = X_HEAD_DIM ** -0.5
    seqs = mk_ref.shape[0]
    rows = h.shape[0] // seqs
    per_seq = []
    for bi in range(seqs):
        outs = []
        for hh in range(X_HEADS):
            cols = slice(hh * X_HEAD_DIM, (hh + 1) * X_HEAD_DIM)
            qh = xq[bi * rows:(bi + 1) * rows, cols].astype(BF16)
            s = lax.dot_general(qh, mk_ref[bi, :, cols].astype(BF16), _NT, preferred_element_type=F32) * scale
            p = jnp.exp(s - jnp.max(s, axis=1, keepdims=True))
            l = jnp.sum(p, axis=1, keepdims=True)
            outs.append(jnp.dot(p.astype(BF16), mv_ref[bi, :, cols].astype(BF16),
                                preferred_element_type=F32) / l)
        per_seq.append(jnp.concatenate(outs, axis=1))
    o = jnp.concatenate(per_seq, axis=0).astype(BF16)
    y = jnp.dot(o, wo_ref[...], preferred_element_type=F32)
    o_ref[...] = h + _rms(y, gpost_ref[...])


def _xattn(h, mem_k, mem_v, layer, g_pre, w_xq, w_xo, g_post, rows_per_batch, tm, emit):
    m, d = h.shape
    seqs = max(1, tm // rows_per_batch)
    tiles = max(1, rows_per_batch // tm)
    assert m % tm == 0 and tm * tiles == rows_per_batch * seqs and (not emit or m == tm)
    mem_len, xw = mem_k.shape[2:]
    row = lambda a: a.reshape(1, -1)
    out_specs = [pl.BlockSpec((tm, d), lambda i: (i, 0))]
    out_shape = [jax.ShapeDtypeStruct((m, d), F32)]
    if emit:
        out_specs += [pl.BlockSpec((d, xw), lambda i: (0, 0)), pl.BlockSpec((xw, d), lambda i: (0, 0))]
        out_shape += [jax.ShapeDtypeStruct((d, xw), BF16), jax.ShapeDtypeStruct((xw, d), BF16)]
    outs = pl.pallas_call(
        functools.partial(_xattn_kernel, emit=emit),
        grid=(m // tm,),
        in_specs=[pl.BlockSpec((tm, d), lambda i: (i, 0)),
                  pl.BlockSpec((None, seqs, mem_len, xw), lambda i: (layer, i // tiles, 0, 0)),
                  pl.BlockSpec((None, seqs, mem_len, xw), lambda i: (layer, i // tiles, 0, 0)),
                  pl.BlockSpec((1, d), lambda i: (0, 0)),
                  pl.BlockSpec((d, xw), lambda i: (0, 0)),
                  pl.BlockSpec((xw, d), lambda i: (0, 0)),
                  pl.BlockSpec((1, d), lambda i: (0, 0))],
        out_specs=out_specs,
        out_shape=out_shape,
        compiler_params=_params("arbitrary"),
        name="xattn",
    )(h, mem_k, mem_v, row(g_pre), w_xq, w_xo, row(g_post))
    return outs[0], (tuple(outs[1:]) if emit else None)


def _ffn_body(f, nf, h_ref, gpre_ref, wg_ref, wu_ref, wo_ref, gpost_ref, o_ref, rest, emit):
    if emit:
        wg_out, wu_out, wo_out, xn_ref, acc_ref = rest
        wg_out[...] = wg_ref[...].astype(BF16)
        wu_out[...] = wu_ref[...].astype(BF16)
        wo_out[...] = wo_ref[...].astype(BF16)
        wg_ref, wu_ref, wo_ref = wg_out, wu_out, wo_out
    else:
        xn_ref, acc_ref = rest

    def tile(first, last):
        if first:
            xn = _rms(h_ref[...], gpre_ref[...]).astype(BF16)
            xn_ref[...] = xn
        else:
            xn = xn_ref[...]
        g = jnp.dot(xn, wg_ref[...], preferred_element_type=F32)
        u = jnp.dot(xn, wu_ref[...], preferred_element_type=F32)
        a = (g * jax.nn.sigmoid(g) * u).astype(BF16)
        y = jnp.dot(a, wo_ref[...], preferred_element_type=F32)
        acc = y if first else acc_ref[...] + y
        if last:
            o_ref[...] = h_ref[...] + _rms(acc, gpost_ref[...])
        else:
            acc_ref[...] = acc

    assert nf >= 2
    pl.when(f == 0)(functools.partial(tile, True, False))
    pl.when((f > 0) & (f < nf - 1))(functools.partial(tile, False, False))
    pl.when(f == nf - 1)(functools.partial(tile, False, True))


def _ffn_kernel(h_ref, gpre_ref, wg_ref, wu_ref, wo_ref, gpost_ref, o_ref, *rest, emit, nf):
    _ffn_body(pl.program_id(1), nf, h_ref, gpre_ref, wg_ref, wu_ref, wo_ref, gpost_ref, o_ref, rest, emit)


def _ffn_specs(h, w, tm, tf, emit, tile):
    m, d = h.shape
    if emit:
        w_in, w_out = w
        hidden = w_out.shape[0]
        nf = hidden // tf
        assert m == tm and hidden % tf == 0 and w_in.shape[1] == 2 * hidden
        w_args = (w_in, w_in, w_out)
        w_specs = [pl.BlockSpec((d, tf), lambda *ids: (0, tile(*ids)[1])),
                   pl.BlockSpec((d, tf), lambda *ids: (0, tile(*ids)[1] + nf)),
                   pl.BlockSpec((tf, d), lambda *ids: (tile(*ids)[1], 0))]
    else:
        w_args = w
        nf, hidden = w[0].shape[0], w[2].shape[0]
        assert m % tm == 0 and w[0].shape == w[1].shape == (nf, d, tf) and hidden == nf * tf
        w_specs = [pl.BlockSpec((None, d, tf), lambda *ids: (tile(*ids)[1], 0, 0)),
                   pl.BlockSpec((None, d, tf), lambda *ids: (tile(*ids)[1], 0, 0)),
                   pl.BlockSpec((tf, d), lambda *ids: (tile(*ids)[1], 0))]
    in_specs = [pl.BlockSpec((tm, d), lambda *ids: (tile(*ids)[0], 0)),
                pl.BlockSpec((1, d), lambda *ids: (0, 0)),
                *w_specs,
                pl.BlockSpec((1, d), lambda *ids: (0, 0))]
    out_specs = [pl.BlockSpec((tm, d), lambda *ids: (tile(*ids)[0], 0))]
    out_shape = [jax.ShapeDtypeStruct((m, d), F32)]
    if emit:
        out_specs += [pl.BlockSpec((None, d, tf), lambda *ids: (tile(*ids)[1], 0, 0)),
                      pl.BlockSpec((None, d, tf), lambda *ids: (tile(*ids)[1], 0, 0)),
                      pl.BlockSpec((tf, d), lambda *ids: (tile(*ids)[1], 0))]
        out_shape += [jax.ShapeDtypeStruct((nf, d, tf), BF16), jax.ShapeDtypeStruct((nf, d, tf), BF16),
                      jax.ShapeDtypeStruct((hidden, d), BF16)]
    scratch = [pltpu.VMEM((tm, d), BF16), pltpu.VMEM((tm, d), F32)]
    return nf, in_specs, out_specs, out_shape, scratch, w_args


FFN_RING = 4


def _ffn_ring_kernel(h_ref, gpre_ref, wg_hbm, wu_hbm, wo_hbm, gpost_ref, o_ref,
                     wg_buf, wu_buf, wo_buf, sem_ref, xn_ref, acc_ref, *, nf, tf):
    i = pl.program_id(0)
    total = pl.num_programs(0) * nf
    base = i * nf

    def copies(c):
        f, s = c % nf, c % FFN_RING
        rows = pl.ds(pl.multiple_of(f * tf, tf), tf)
        return (pltpu.make_async_copy(wg_hbm.at[f], wg_buf.at[s], sem_ref.at[0, s]),
                pltpu.make_async_copy(wu_hbm.at[f], wu_buf.at[s], sem_ref.at[1, s]),
                pltpu.make_async_copy(wo_hbm.at[rows, :], wo_buf.at[s], sem_ref.at[2, s]))

    @pl.when(i == 0)
    def _():
        for c in range(FFN_RING - 1):
            for cp in copies(c):
                cp.start()

    def tile(f, first, last):
        c = base + f
        for cp in copies(c):
            cp.wait()

        @pl.when(c + FFN_RING - 1 < total)
        def _():
            for cp in copies(c + FFN_RING - 1):
                cp.start()

        s = c % FFN_RING
        if first:
            xn = _rms(h_ref[...], gpre_ref[...]).astype(BF16)
            xn_ref[...] = xn
        else:
            xn = xn_ref[...]
        g = jnp.dot(xn, wg_buf[s], preferred_element_type=F32)
        u = jnp.dot(xn, wu_buf[s], preferred_element_type=F32)
        a = (g * jax.nn.sigmoid(g) * u).astype(BF16)
        y = jnp.dot(a, wo_buf[s], preferred_element_type=F32)
        acc = y if first else acc_ref[...] + y
        if last:
            o_ref[...] = h_ref[...] + _rms(acc, gpost_ref[...])
        else:
            acc_ref[...] = acc

    tile(0, True, False)
    lax.fori_loop(1, nf - 1, lambda f, carry: (tile(f, False, False), carry)[1], 0)
    tile(nf - 1, False, True)


def _ffn_ring(h, g_pre, w, g_post, tm, tf):
    m, d = h.shape
    nf = w[0].shape[0]
    assert m % tm == 0 and nf >= FFN_RING and w[0].shape == w[1].shape == (nf, d, tf) and w[2].shape == (nf * tf, d)
    row = lambda a: a.reshape(1, -1)
    any_spec = pl.BlockSpec(memory_space=pl.ANY)
    return pl.pallas_call(
        functools.partial(_ffn_ring_kernel, nf=nf, tf=tf),
        grid=(m // tm,),
        in_specs=[pl.BlockSpec((tm, d), lambda i: (i, 0)), pl.BlockSpec((1, d), lambda i: (0, 0)),
                  any_spec, any_spec, any_spec, pl.BlockSpec((1, d), lambda i: (0, 0))],
        out_specs=pl.BlockSpec((tm, d), lambda i: (i, 0)),
        out_shape=jax.ShapeDtypeStruct((m, d), F32),
        scratch_shapes=[pltpu.VMEM((FFN_RING, d, tf), BF16), pltpu.VMEM((FFN_RING, d, tf), BF16),
                        pltpu.VMEM((FFN_RING, tf, d), BF16), pltpu.SemaphoreType.DMA((3, FFN_RING)),
                        pltpu.VMEM((tm, d), BF16), pltpu.VMEM((tm, d), F32)],
        compiler_params=_params("arbitrary"),
        name="ffn_ring",
    )(h, row(g_pre), *w, row(g_post))


def _ffn(h, g_pre, w, g_post, tm, tf, emit):
    m = h.shape[0]
    row = lambda a: a.reshape(1, -1)
    nf, in_specs, out_specs, out_shape, scratch, w_args = _ffn_specs(h, w, tm, tf, emit, lambda i, f: (i, f))
    outs = pl.pallas_call(
        functools.partial(_ffn_kernel, emit=emit, nf=nf),
        grid=(m // tm, nf),
        in_specs=in_specs,
        out_specs=out_specs,
        out_shape=out_shape,
        scratch_shapes=scratch,
        compiler_params=_params("arbitrary", "arbitrary"),
        name="ffn",
    )(h, row(g_pre), *w_args, row(g_post))
    return outs[0], tuple(outs[1:])


def _conv_ffn_kernel(*refs, conv_kw, nf):
    ci, fi, co, fo, cs, fs = refs[0:7], refs[7:13], refs[13:15], refs[15:19], refs[19:22], refs[22:24]
    step = pl.program_id(0) * pl.num_programs(1) + pl.program_id(1)

    @pl.when(step < nf)
    def _():
        _ffn_body(step, nf, *fi, fo[0], tuple(fo[1:]) + tuple(fs), True)

    _conv_kernel(*ci, *co, *cs, **conv_kw)


def _conv_and_emit_ffn(conv_args, h, g_pre, w, g_post, tf):
    grid, c_in, c_out, c_shape, c_scratch, c_args, conv_kw = _conv_specs(*conv_args)
    nt = grid[1]
    row = lambda a: a.reshape(1, -1)
    tile = lambda bi, ti: (0, jnp.minimum(bi * nt + ti, nf - 1))
    nf = w[1].shape[0] // tf
    if grid[0] * nt < nf:
        return None
    nf, f_in, f_out, f_shape, f_scratch, w_args = _ffn_specs(h, w, h.shape[0], tf, True, tile)
    outs = pl.pallas_call(
        functools.partial(_conv_ffn_kernel, conv_kw=conv_kw, nf=nf),
        grid=grid,
        in_specs=c_in + f_in,
        out_specs=c_out + f_out,
        out_shape=c_shape + f_shape,
        scratch_shapes=c_scratch + f_scratch,
        compiler_params=_params("arbitrary", "arbitrary", vmem_limit=VMEM_LIMIT_MAX),
        name="conv_and_ffn",
    )(*c_args, h, row(g_pre), *w_args, row(g_post))
    return outs[0], outs[1], outs[2], tuple(outs[3:])


def _row_tile(m):
    return 512 if m % 512 == 0 else m


def _project_in(x, wts, w_in, emit):
    b, t, d = x.shape
    m = b * t
    assert wts["w_dw"].shape[1] == ATTN_WIDTH
    k_new, v_new, qu, *w_in_b = _in_proj(x.reshape(m, d), wts["g_pre_mix"].reshape(1, d), w_in, emit,
                                         tm=1024 if m % 1024 == 0 else _row_tile(m))
    k_new, v_new = k_new.reshape(b, t, ATTN_WIDTH), v_new.reshape(b, t, ATTN_WIDTH)
    return k_new, v_new, qu.reshape(qu.shape[0], b, t, ATTN_WIDTH), (w_in_b[0] if emit else None)


def _mix_and_xattn(x, attn, conv, mem, mem_layer, wts, w, emit, x_tile):
    b, t, d = x.shape
    m = b * t
    h, w_out_b = _mix_out(attn.reshape(m, ATTN_WIDTH), conv.reshape(m, -1), x.reshape(m, d), wts["g_attn_grp"],
                          wts["g_conv_grp"], w[0], wts["g_post_mix"], tm=_row_tile(m), emit=emit)
    h, w_x_b = _xattn(h, mem[0], mem[1], mem_layer, wts["g_pre_x"], w[1], w[2], wts["g_post_x"],
                      rows_per_batch=t, tm=x_tile, emit=emit)
    return h, ((w_out_b,) + w_x_b if emit else None)


def kernel(x_prompt, x_sample, cache_k, cache_v, state_conv, cache_mem_k, cache_mem_v, page_table,
           mem_prompt, g_mem, w_mem_k, w_mem_v, g_pre_mix, w_in, w_dw, b_dw, g_ln_conv, b_ln_conv,
           g_attn_grp, g_conv_grp, w_out, g_post_mix, g_pre_x, w_xq, w_xo, g_post_x,
           g_pre_ffn, w_ffn_in, w_ffn_out, g_post_ffn):
    depth = w_in.shape[0]
    bp, tp, d = x_prompt.shape
    bs, ts, _ = x_sample.shape
    mem_len = mem_prompt.shape[1]
    xw = X_HEADS * X_HEAD_DIM
    ch = w_dw.shape[2]
    hp, hs = x_prompt, x_sample
    outs = [[] for _ in range(8)]
    for l in range(depth):
        wts = dict(g_pre_mix=g_pre_mix[l], w_dw=w_dw[l], b_dw=b_dw[l],
                   g_ln_conv=g_ln_conv[l], b_ln_conv=b_ln_conv[l], g_attn_grp=g_attn_grp[l],
                   g_conv_grp=g_conv_grp[l], g_post_mix=g_post_mix[l], g_pre_x=g_pre_x[l],
                   g_post_x=g_post_x[l], g_pre_ffn=g_pre_ffn[l], g_post_ffn=g_post_ffn[l])

        ks, vs, qu_s, w_in_b = _project_in(hs, wts, w_in[l], True)
        kp, vp, qu_p, _ = _project_in(hp, wts, w_in_b, False)
        attn_p, scores, sel = _moba_and_scores(qu_p, kp, vp, qu_s, cache_k, page_table, l)
        attn_s = _sample_attend(scores, sel, qu_s, ks, vs, cache_v, page_table, l)

        conv_w = (wts["w_dw"], wts["b_dw"], wts["g_ln_conv"], wts["b_ln_conv"])
        conv_s, cs = _conformer_conv(qu_s, state_conv, l, *conv_w, ts)
        sample_mem = (cache_mem_k.reshape(depth, bs, mem_len, xw), cache_mem_v.reshape(depth, bs, mem_len, xw))
        hs_mid, w_mid_b = _mix_and_xattn(hs, attn_s, conv_s, sample_mem, l, wts, (w_out[l], w_xq[l], w_xo[l]),
                                         True, x_tile=bs * ts)

        conv_args = (qu_p, jnp.zeros((1, bp, CONV_WIDTH - 1, ch), F32), 0, *conv_w, _row_tile(tp))
        ffn_w = (w_ffn_in[l], w_ffn_out[l])
        fused = _conv_and_emit_ffn(conv_args, hs_mid, wts["g_pre_ffn"], ffn_w, wts["g_post_ffn"], tf=512)
        if fused is None:
            conv_p, cp = _conformer_conv(*conv_args)
            hs_out, w_ffn_b = _ffn(hs_mid, wts["g_pre_ffn"], ffn_w, wts["g_post_ffn"], tm=bs * ts, tf=512,
                                   emit=True)
        else:
            conv_p, cp, hs_out, w_ffn_b = fused
        hs = hs_out.reshape(bs, ts, d)

        w_mem = jnp.concatenate([w_mem_k[l], w_mem_v[l]], axis=1).astype(BF16)
        mk_p, mv_p = _norm_matmul(mem_prompt.reshape(bp * mem_len, d), g_mem[l].reshape(1, d), w_mem,
                                  (xw, xw), tn=xw, tm=_row_tile(bp * mem_len))
        mk_p, mv_p = mk_p.reshape(bp, mem_len, xw), mv_p.reshape(bp, mem_len, xw)
        hp_mid, _ = _mix_and_xattn(hp, attn_p, conv_p, (mk_p[None], mv_p[None]), 0, wts, w_mid_b, False,
                                   x_tile=_row_tile(tp))
        hp_out = _ffn_ring(hp_mid, wts["g_pre_ffn"], w_ffn_b, wts["g_post_ffn"], tm=_row_tile(bp * tp), tf=512)
        hp = hp_out.reshape(bp, tp, d)
        mem_shape = (bp, mem_len, X_HEADS, X_HEAD_DIM)
        heads = lambda a: a.reshape(a.shape[:2] + (N_HEADS, HEAD_DIM))
        for lst, a in zip(outs, (heads(kp), heads(vp), cp, mk_p.reshape(mem_shape), mv_p.reshape(mem_shape),
                                 heads(ks), heads(vs), cs)):
            lst.append(a)
    return (hp, hs) + tuple(jnp.stack(lst, 0) for lst in outs)
```
